```python
import math
import jax
import jax.numpy as jnp
from jax import lax
import numpy as np

D_MODEL = 1024
BATCH = 8
SEQ = 2048
DEPTH = 2
DEC_BATCH = 128
DEC_SEQ = 1
PAST_LEN = 16384
PAGE_SIZE = 128

MIX_WIDTH = D_MODEL
BRANCH = MIX_WIDTH // 4
S5_CH = 16
S5_GROUPS = BRANCH // S5_CH
S5_STATE = 64
HG_HEADS = 4
HG_DK = BRANCH // HG_HEADS
HG_DV = BRANCH // HG_HEADS
GLA_HEADS = 4
GLA_DK = BRANCH // (2 * GLA_HEADS)
GLA_DV = BRANCH // GLA_HEADS
GLA_LOWRANK = 16
GLA_TAU = 16.0
RET_HEADS = 4
RET_DK = BRANCH // (2 * RET_HEADS)
RET_DV = BRANCH // RET_HEADS
GATED_CHUNK = 16
RET_CHUNK = 64
ROPE_BASE = 10000.0
EPS = 1e-6
SPLIT_WIDTHS = (BRANCH, BRANCH,
                BRANCH, BRANCH, BRANCH, BRANCH,
                GLA_HEADS * GLA_DK, GLA_HEADS * GLA_DK, BRANCH, GLA_LOWRANK, BRANCH,
                RET_HEADS * RET_DK, RET_HEADS * RET_DK, BRANCH, BRANCH)
N_IN = sum(SPLIT_WIDTHS)

kernel_name = 'hybrid_s5_hgrn2_gla_retnet_step'


def rmsnorm(x, w):
    xf = x.astype(jnp.float32)
    y = xf * lax.rsqrt(jnp.mean(xf * xf, axis=-1, keepdims=True) + EPS)
    return (y * w.astype(jnp.float32)).astype(x.dtype)


def split_heads(t, n_heads):
    b, l, w = t.shape
    return t.reshape(b, l, n_heads, w // n_heads).transpose(0, 2, 1, 3)


def merge_heads(t):
    b, h, l, d = t.shape
    return t.transpose(0, 2, 1, 3).reshape(b, l, h * d)


def head_rmsnorm(o, g):
    h, d = o.shape[1], o.shape[3]
    y = o * lax.rsqrt(jnp.mean(o * o, axis=-1, keepdims=True) + EPS)
    return y * g.astype(jnp.float32).reshape(h, 1, d)


def head_layernorm(o, g):
    h, d = o.shape[1], o.shape[3]
    c = o - jnp.mean(o, axis=-1, keepdims=True)
    y = c * lax.rsqrt(jnp.mean(c * c, axis=-1, keepdims=True) + EPS)
    return y * g.astype(jnp.float32).reshape(h, 1, d)


def rotary(t, pos):
    half = t.shape[-1] // 2
    inv = ROPE_BASE ** (-jnp.arange(half, dtype=jnp.float32) / half)
    ang = pos.astype(jnp.float32)[:, None] * inv[None, :]
    cos, sin = jnp.cos(ang), jnp.sin(ang)
    t1, t2 = t[..., :half], t[..., half:]
    return jnp.concatenate([t1 * cos - t2 * sin, t1 * sin + t2 * cos], axis=-1)


def s5_branch(u, s0_re, s0_im, lam_re, lam_im, log_step, b_re, b_im, c_re, c_im, d_skip, w_glu):
    bn, l, _ = u.shape
    uf = u.reshape(bn, l, S5_GROUPS, S5_CH)
    lr = lam_re.astype(jnp.float32)
    li = lam_im.astype(jnp.float32)
    step = jnp.exp(log_step.astype(jnp.float32))[:, None]
    mag = jnp.exp(lr * step)
    ab_re = mag * jnp.cos(li * step)
    ab_im = mag * jnp.sin(li * step)
    den = lr * lr + li * li
    nr = ab_re - 1.0
    f_re = (nr * lr + ab_im * li) / den
    f_im = (ab_im * lr - nr * li) / den
    br = b_re.astype(jnp.float32)
    bi = b_im.astype(jnp.float32)
    bb_re = f_re[..., None] * br - f_im[..., None] * bi
    bb_im = f_re[..., None] * bi + f_im[..., None] * br
    bu_re = jnp.einsum('blgc,gpc->blgp', uf, bb_re)
    bu_im = jnp.einsum('blgc,gpc->blgp', uf, bb_im)
    a_re = jnp.broadcast_to(ab_re, bu_re.shape)
    a_im = jnp.broadcast_to(ab_im, bu_im.shape)

    def combine(e1, e2):
        a1r, a1i, b1r, b1i = e1
        a2r, a2i, b2r, b2i = e2
        return (a2r * a1r - a2i * a1i, a2r * a1i + a2i * a1r,
                a2r * b1r - a2i * b1i + b2r, a2r * b1i + a2i * b1r + b2i)

    _, _, s_re, s_im = lax.associative_scan(combine, (a_re, a_im, bu_re, bu_im), axis=1)
    t1 = jnp.arange(1, l + 1, dtype=jnp.float32)[:, None, None]
    pm = jnp.exp(lr * step * t1)
    pa = li * step * t1
    p_re = pm * jnp.cos(pa)
    p_im = pm * jnp.sin(pa)
    s0r = s0_re[:, None]
    s0i = s0_im[:, None]
    s_re = s_re + p_re * s0r - p_im * s0i
    s_im = s_im + p_re * s0i + p_im * s0r
    y = (jnp.einsum('blgp,gcp->blgc', s_re, c_re.astype(jnp.float32))
         - jnp.einsum('blgp,gcp->blgc', s_im, c_im.astype(jnp.float32))
         + uf * d_skip.astype(jnp.float32).reshape(S5_GROUPS, S5_CH))
    y = jax.nn.gelu(y.reshape(bn, l, BRANCH))
    y = y * jax.nn.sigmoid(jnp.einsum('blc,cn->bln', y, w_glu.astype(jnp.float32)))
    return y, s_re[:, -1], s_im[:, -1]


def gated_chunk_scan(q, k, v, g, s0):
    bn, h, l, dk = q.shape
    dv = v.shape[-1]
    c = math.gcd(l, GATED_CHUNK)
    n = l // c
    causal = jnp.tril(jnp.ones((c, c), dtype=bool))[:, :, None]

    def to_chunks(t):
        return jnp.moveaxis(t.reshape(bn, h, n, c, t.shape[-1]), 2, 0)

    def step(S, inp):
        qc, kc, vc, gc = inp
        b = jnp.cumsum(gc, axis=2)
        diff = b[:, :, :, None, :] - b[:, :, None, :, :]
        w = jnp.where(causal, jnp.exp(jnp.where(causal, diff, 0.0)), 0.0)
        att = jnp.einsum('bhtk,bhsk,bhtsk->bhts', qc, kc, w)
        o = (jnp.einsum('bhtk,bhkv->bhtv', qc * jnp.exp(b), S)
             + jnp.einsum('bhts,bhsv->bhtv', att, vc))
        b_last = b[:, :, -1:, :]
        S = (S * jnp.exp(b_last)[:, :, 0, :, None]
             + jnp.einsum('bhsk,bhsv->bhkv', kc * jnp.exp(b_last - b), vc))
        return S, o

    S, o = lax.scan(step, s0, (to_chunks(q), to_chunks(k), to_chunks(v), to_chunks(g)))
    return jnp.moveaxis(o, 0, 2).reshape(bn, h, l, dv), S


def retention_chunk_scan(q, k, v, s0):
    bn, h, l, dk = q.shape
    dv = v.shape[-1]
    c = math.gcd(l, RET_CHUNK)
    n = l // c
    log_gamma = jnp.log1p(-jnp.exp2(-5.0 - jnp.arange(h, dtype=jnp.float32)))
    idx = jnp.arange(c, dtype=jnp.float32)
    rel = idx[:, None] - idx[None, :]
    causal = rel >= 0
    decay = jnp.where(causal, jnp.exp(jnp.where(causal, rel, 0.0)[None] * log_gamma[:, None, None]), 0.0)
    inner = jnp.exp((idx[None, :] + 1.0) * log_gamma[:, None])
    kdec = jnp.exp((c - 1.0 - idx[None, :]) * log_gamma[:, None])
    cdec = jnp.exp(c * log_gamma)

    def to_chunks(t):
        return jnp.moveaxis(t.reshape(bn, h, n, c, t.shape[-1]), 2, 0)

    def step(S, inp):
        qc, kc, vc = inp
        o_inter = jnp.einsum('bhtk,bhkv->bhtv', qc, S) * inner[None, :, :, None]
        scores = jnp.einsum('bhtk,bhsk->bhts', qc, kc) * decay[None]
        o = o_inter + jnp.einsum('bhts,bhsv->bhtv', scores, vc)
        S = (S * cdec[None, :, None, None]
             + jnp.einsum('bhsk,bhsv->bhkv', kc * kdec[None, :, :, None], vc))
        return S, o

    S, o = lax.scan(step, s0, (to_chunks(q), to_chunks(k), to_chunks(v)))
    return jnp.moveaxis(o, 0, 2).reshape(bn, h, l, dv), S


def mixer_layer(x, pos, s5_re0, s5_im0, hg0, gla0, ret0, lb, p):
    f32 = jnp.float32
    h = rmsnorm(x, p['norm_w'])
    proj = jnp.einsum('bld,dn->bln', h, p['w_in']).astype(f32)
    offs = [int(o) for o in np.cumsum(SPLIT_WIDTHS)[:-1]]
    (s5_u, s5_z, hg_q, hg_f, hg_i, hg_z, gla_q, gla_k, gla_v, gla_lr, gla_z,
     ret_q, ret_k, ret_v, ret_z) = jnp.split(proj, offs, axis=-1)

    y_s5, s5_re, s5_im = s5_branch(s5_u, s5_re0.astype(f32), s5_im0.astype(f32), p['s5_lam_re'], p['s5_lam_im'],
                                   p['s5_log_step'], p['s5_b_re'], p['s5_b_im'], p['s5_c_re'], p['s5_c_im'],
                                   p['s5_d'], p['s5_w_glu'])

    lbh = lb.reshape(HG_HEADS, 1, HG_DK)
    log_f = jnp.logaddexp(jnp.log(lbh), jnp.log1p(-lbh) + jax.nn.log_sigmoid(split_heads(hg_f, HG_HEADS)))
    o_hg, hg_new = gated_chunk_scan(split_heads(hg_q, HG_HEADS), -jnp.expm1(log_f),
                                    split_heads(hg_i, HG_HEADS), log_f, hg0.astype(f32))
    y_hg = merge_heads(head_rmsnorm(o_hg, p['hgrn_norm_w']))

    g_gla = jax.nn.log_sigmoid(jnp.einsum('blr,rn->bln', gla_lr, p['gla_w_gate_up'].astype(f32))
                               + p['gla_b_gate'].astype(f32)) / GLA_TAU
    o_gla, gla_new = gated_chunk_scan(split_heads(gla_q, GLA_HEADS) * GLA_DK ** -0.5,
                                      split_heads(gla_k, GLA_HEADS), split_heads(gla_v, GLA_HEADS),
                                      split_heads(g_gla, GLA_HEADS), gla0.astype(f32))
    y_gla = merge_heads(head_rmsnorm(o_gla, p['gla_norm_w']))

    rq = rotary(split_heads(ret_q, RET_HEADS), pos)
    rk = rotary(split_heads(ret_k, RET_HEADS), pos) * RET_DK ** -0.5
    o_ret, ret_new = retention_chunk_scan(rq, rk, split_heads(ret_v, RET_HEADS), ret0.astype(f32))
    y_ret = merge_heads(head_layernorm(o_ret, p['ret_norm_w']))

    mix = jnp.concatenate([y_s5 * jax.nn.silu(s5_z), y_hg * jax.nn.silu(hg_z),
                           y_gla * jax.nn.silu(gla_z), y_ret * jax.nn.silu(ret_z)], axis=-1)
    out = jnp.einsum('bln,nd->bld', mix, p['w_out'].astype(f32))
    return x + out.astype(x.dtype), (s5_re, s5_im, hg_new, gla_new, ret_new)


def setup_inputs(seed: int = 0) -> dict:
    key = jax.random.key(seed)
    ks = jax.random.split(key, 26)
    f32 = jnp.float32
    nrm = lambda k, s, sc: jax.random.normal(k, s, f32) * sc
    lam_re = -0.5 * jnp.exp(nrm(ks[14], (DEPTH, S5_GROUPS, S5_STATE), 0.05))
    lam_im = math.pi * jnp.arange(S5_STATE, dtype=f32)[None, None, :] + nrm(ks[15], (DEPTH, S5_GROUPS, S5_STATE), 0.05)
    log_step = math.log(1e-3) + jax.random.uniform(ks[16], (DEPTH, S5_GROUPS), f32) * (math.log(1e-1) - math.log(1e-3))
    return {
        'x_prompt': nrm(ks[0], (BATCH, SEQ, D_MODEL), 1.0),
        'x_sample': nrm(ks[1], (DEC_BATCH, DEC_SEQ, D_MODEL), 1.0),
        'state_s5_re': nrm(ks[2], (DEPTH, DEC_BATCH, S5_GROUPS, S5_STATE), 0.5),
        'state_s5_im': nrm(ks[3], (DEPTH, DEC_BATCH, S5_GROUPS, S5_STATE), 0.5),
        'state_hgrn': nrm(ks[4], (DEPTH, DEC_BATCH, HG_HEADS, HG_DK, HG_DV), 0.3),
        'state_gla': nrm(ks[5], (DEPTH, DEC_BATCH, GLA_HEADS, GLA_DK, GLA_DV), 0.3),
        'state_ret': nrm(ks[6], (DEPTH, DEC_BATCH, RET_HEADS, RET_DK, RET_DV), 0.3),
        'norm_w': 1.0 + nrm(ks[7], (DEPTH, D_MODEL), 0.02),
        'final_norm_w': 1.0 + nrm(ks[8], (D_MODEL,), 0.02),
        'w_in': nrm(ks[9], (DEPTH, D_MODEL, N_IN), D_MODEL ** -0.5),
        'w_out': nrm(ks[10], (DEPTH, MIX_WIDTH, D_MODEL), MIX_WIDTH ** -0.5),
        's5_lam_re': lam_re,
        's5_lam_im': lam_im,
        's5_log_step': log_step,
        's5_b_re': nrm(ks[11], (DEPTH, S5_GROUPS, S5_STATE, S5_CH), (2 * S5_CH) ** -0.5),
        's5_b_im': nrm(ks[12], (DEPTH, S5_GROUPS, S5_STATE, S5_CH), (2 * S5_CH) ** -0.5),
        's5_c_re': nrm(ks[13], (DEPTH, S5_GROUPS, S5_CH, S5_STATE), (2 * S5_STATE) ** -0.5),
        's5_c_im': nrm(ks[17], (DEPTH, S5_GROUPS, S5_CH, S5_STATE), (2 * S5_STATE) ** -0.5),
        's5_d': nrm(ks[18], (DEPTH, BRANCH), 1.0),
        's5_w_glu': nrm(ks[19], (DEPTH, BRANCH, BRANCH), BRANCH ** -0.5),
        'hgrn_lb_logits': nrm(ks[20], (DEPTH, HG_HEADS * HG_DK), 1.0),
        'hgrn_norm_w': 1.0 + nrm(ks[21], (DEPTH, BRANCH), 0.02),
        'gla_w_gate_up': nrm(ks[22], (DEPTH, GLA_LOWRANK, GLA_HEADS * GLA_DK), GLA_LOWRANK ** -0.5),
        'gla_b_gate': nrm(ks[23], (DEPTH, GLA_HEADS * GLA_DK), 0.1),
        'gla_norm_w': 1.0 + nrm(ks[24], (DEPTH, BRANCH), 0.02),
        'ret_norm_w': 1.0 + nrm(ks[25], (DEPTH, BRANCH), 0.02),
    }


def reference(x_prompt, x_sample, state_s5_re, state_s5_im, state_hgrn, state_gla, state_ret,
              norm_w, final_norm_w, w_in, w_out, s5_lam_re, s5_lam_im, s5_log_step,
              s5_b_re, s5_b_im, s5_c_re, s5_c_im, s5_d, s5_w_glu, hgrn_lb_logits, hgrn_norm_w,
              gla_w_gate_up, gla_b_gate, gla_norm_w, ret_norm_w):
    f32 = jnp.float32
    bp, lp = x_prompt.shape[0], x_prompt.shape[1]
    ls = x_sample.shape[1]
    pos_p = jnp.arange(lp)
    pos_s = PAST_LEN + jnp.arange(ls)
    lb_all = jnp.cumsum(jax.nn.softmax(hgrn_lb_logits.astype(f32), axis=0), axis=0)
    lb_all = lb_all - lb_all[0:1]

    xp, xs = x_prompt, x_sample
    new_p = ([], [], [], [], [])
    new_s = ([], [], [], [], [])
    for l in range(DEPTH):
        p = {
            'norm_w': norm_w[l], 'w_in': w_in[l], 'w_out': w_out[l],
            's5_lam_re': s5_lam_re[l], 's5_lam_im': s5_lam_im[l], 's5_log_step': s5_log_step[l],
            's5_b_re': s5_b_re[l], 's5_b_im': s5_b_im[l], 's5_c_re': s5_c_re[l], 's5_c_im': s5_c_im[l],
            's5_d': s5_d[l], 's5_w_glu': s5_w_glu[l], 'hgrn_norm_w': hgrn_norm_w[l],
            'gla_w_gate_up': gla_w_gate_up[l], 'gla_b_gate': gla_b_gate[l], 'gla_norm_w': gla_norm_w[l],
            'ret_norm_w': ret_norm_w[l],
        }
        xp, st_p = mixer_layer(xp, pos_p,
                               jnp.zeros((bp, S5_GROUPS, S5_STATE), f32), jnp.zeros((bp, S5_GROUPS, S5_STATE), f32),
                               jnp.zeros((bp, HG_HEADS, HG_DK, HG_DV), f32),
                               jnp.zeros((bp, GLA_HEADS, GLA_DK, GLA_DV), f32),
                               jnp.zeros((bp, RET_HEADS, RET_DK, RET_DV), f32),
                               lb_all[l], p)
        xs, st_s = mixer_layer(xs, pos_s, state_s5_re[l], state_s5_im[l], state_hgrn[l], state_gla[l],
                               state_ret[l], lb_all[l], p)
        for i in range(5):
            new_p[i].append(st_p[i])
            new_s[i].append(st_s[i])
    y_prompt = rmsnorm(xp, final_norm_w)
    y_sample = rmsnorm(xs, final_norm_w)
    return (y_prompt, y_sample,
            jnp.stack(new_p[0]), jnp.stack(new_p[1]), jnp.stack(new_p[2]), jnp.stack(new_p[3]), jnp.stack(new_p[4]),
            jnp.stack(new_s[0]), jnp.stack(new_s[1]), jnp.stack(new_s[2]), jnp.stack(new_s[3]), jnp.stack(new_s[4]))
```

```python
import functools
import math

import numpy as np
import jax
import jax.numpy as jnp
from jax import lax
from jax.experimental import pallas as pl
from jax.experimental.pallas import tpu as pltpu

F32 = jnp.float32
BF16 = jnp.bfloat16

D_MODEL = 1024
BRANCH = 256
S5_CH = 16
S5_GROUPS = 16
S5_STATE = 64
S5_N = S5_GROUPS * S5_STATE
HEADS = 4
HG_DK = 64
GLA_DK = 32
RET_DK = 32
DV = 64
GLA_LOWRANK = 16
GLA_TAU = 16.0
ROPE_BASE = 10000.0
PAST_LEN = 16384
EPS = 1e-6
SUB = 16

LANES = 128
T_CHUNK = 64
PITCH = T_CHUNK + 8
SAMPLE_ROWS = 32
VMEM_LIMIT = 56 * 1024 * 1024

C_U, C_SZ, C_HQ, C_HF, C_HI, C_HZ = 0, 256, 512, 768, 1024, 1280
C_GQ, C_GK, C_GV, C_GZ = 1536, 1664, 1792, 2048
C_RQ, C_RK, C_RV, C_RZ = 2304, 2432, 2560, 2816
C_LR = 3072
N_PACK = 3200
C_HK = N_PACK
N_SCR = N_PACK + BRANCH


def _dot(a, b):
    return jnp.dot(a.astype(BF16), b.astype(BF16), preferred_element_type=F32)


def _dot_nt(a, b):
    return lax.dot_general(a.astype(BF16), b.astype(BF16), (((1,), (1,)), ((), ())),
                           preferred_element_type=F32)


def _dot_tn(a, b):
    return lax.dot_general(a.astype(BF16), b.astype(BF16), (((0,), (0,)), ((), ())),
                           preferred_element_type=F32)


def _split3(x):
    x1 = x.astype(BF16)
    r1 = x - x1.astype(F32)
    x2 = r1.astype(BF16)
    x3 = (r1 - x2.astype(F32)).astype(BF16)
    return x1, x2, x3


def _dot_sel_lhs(sel, x):
    x1, x2, x3 = _split3(x)
    d = lambda p: jnp.dot(sel, p, preferred_element_type=F32)
    return d(x1) + d(x2) + d(x3)


def _dot_sel_rhs(x, sel):
    x1, x2, x3 = _split3(x)
    d = lambda p: jnp.dot(p, sel, preferred_element_type=F32)
    return d(x1) + d(x2) + d(x3)


def _dot_sel_rhs2(x, sel):
    x1 = x.astype(BF16)
    x2 = (x - x1.astype(F32)).astype(BF16)
    return (jnp.dot(x1, sel, preferred_element_type=F32)
            + jnp.dot(x2, sel, preferred_element_type=F32))


def _dot_sel_nt2(x, sel):
    x1 = x.astype(BF16)
    x2 = (x - x1.astype(F32)).astype(BF16)
    dn = (((1,), (1,)), ((), ()))
    return (lax.dot_general(x1, sel, dn, preferred_element_type=F32)
            + lax.dot_general(x2, sel, dn, preferred_element_type=F32))


def _dot3(a, b):
    a1 = a.astype(BF16)
    a2 = (a - a1.astype(F32)).astype(BF16)
    b1 = b.astype(BF16)
    b2 = (b - b1.astype(F32)).astype(BF16)
    d = lambda p, q: jnp.dot(p, q, preferred_element_type=F32)
    return d(a1, b1) + d(a1, b2) + d(a2, b1)


def _sigmoid(x):
    return 1.0 / (1.0 + jnp.exp(-x))


def _silu(x):
    return x * _sigmoid(x)


def _log_sigmoid(x):
    return jnp.minimum(x, 0.0) - jnp.log(1.0 + jnp.exp(-jnp.abs(x)))


def _gelu_tanh(x):
    return 0.5 * x * (1.0 + jnp.tanh(math.sqrt(2.0 / math.pi) * (x + 0.044715 * (x * x * x))))


def _rmsnorm_rows(x, w):
    return x * lax.rsqrt(jnp.mean(x * x, axis=-1, keepdims=True) + EPS) * w


def _head_rms(o, ones_h, gain):
    ms = _dot_sel_rhs2(o * o, ones_h) * (1.0 / DV)
    return o * lax.rsqrt(ms + EPS) * gain


def _head_ln(o, ones_h, gain):
    c = o - _dot_sel_rhs2(o, ones_h) * (1.0 / DV)
    var = _dot_sel_rhs2(c * c, ones_h) * (1.0 / DV)
    return c * lax.rsqrt(var + EPS) * gain


def _hgrn_gates(xf, lbp):
    loglb, log1mlb, one_m_lb = lbp[0:1, :], lbp[1:2, :], lbp[2:3, :]
    bterm = log1mlb + _log_sigmoid(xf)
    m = jnp.maximum(loglb, bterm)
    log_f = m + jnp.log(jnp.exp(loglb - m) + jnp.exp(bterm - m))
    return log_f, one_m_lb * _sigmoid(-xf)


def _rotary(t, cos, sin_signed, first_half):
    half = RET_DK // 2
    swapped = jnp.where(first_half, pltpu.roll(t, LANES - half, 1), pltpu.roll(t, half, 1))
    return t * cos + swapped * sin_signed


def _s5_output(y_lin, u, sz, dskip, wglu):
    y = _gelu_tanh(y_lin + u * dskip)
    y = y * _sigmoid(_dot(y, wglu))
    return y * _silu(sz)


def _mix_and_project(x, mix, wout, fnw, apply_final):
    out = x + _dot(mix, wout)
    if apply_final:
        out = _rmsnorm_rows(out, fnw)
    return out


def _gated_chunk(q, k, v, g, s_ref, ones_kv, mask_vk, tril, blk, term_scr, kdim):
    n_sub = T_CHUNK // SUB
    bc = _dot_sel_lhs(tril, g)
    bl = _dot_sel_lhs(blk, g)
    qt = q * jnp.exp(bc)
    kh = k * jnp.exp(bl - bc)
    trow = lax.broadcasted_iota(jnp.int32, (SUB, kdim), 0)
    inter = []
    for j in range(n_sub):
        r0 = SUB * j
        s_t = s_ref[...]
        inter.append(_dot_nt(qt[r0:r0 + SUB], s_t))
        w = _dot_tn(v[r0:r0 + SUB], kh[r0:r0 + SUB]) * mask_vk
        s_ref[...] = s_t * jnp.exp(bl[r0:r0 + 1, :]) + w
        qj, kj, bj = q[r0:r0 + SUB], k[r0:r0 + SUB], bc[r0:r0 + SUB]
        for s in range(SUB):
            ok = trow >= s
            wgt = jnp.where(ok, jnp.exp(jnp.where(ok, bj - bj[s:s + 1, :], 0.0)), 0.0)
            row = (j * SUB + s) * SUB
            term_scr[row:row + SUB, 0:kdim] = qj * kj[s:s + 1, :] * wgt
    att = _dot(term_scr[:, 0:kdim], ones_kv)
    outs = []
    for j in range(n_sub):
        acc = inter[j]
        for s in range(SUB):
            row = (j * SUB + s) * SUB
            acc = acc + att[row:row + SUB] * v[SUB * j + s:SUB * j + s + 1, :]
        outs.append(acc)
    return jnp.concatenate(outs, axis=0)


def _ret_chunk(q, k, v, cos, sin_signed, s_ref, dstack, inner, kdec, cdec, mask_vk):
    lane = lax.broadcasted_iota(jnp.int32, (T_CHUNK, HEADS * RET_DK), 1)
    first_half = (lane % RET_DK) < (RET_DK // 2)
    rq = _rotary(q, cos, sin_signed, first_half)
    rk = _rotary(k, cos, sin_signed, first_half) * (RET_DK ** -0.5)
    lane_v = lax.broadcasted_iota(jnp.int32, (T_CHUNK, HEADS * DV), 1)
    kst = jnp.concatenate([jnp.where(lane // RET_DK == h, rk, 0.0) for h in range(HEADS)], axis=0)
    vst = jnp.concatenate([jnp.where(lane_v // DV == h, v, 0.0) for h in range(HEADS)], axis=0)
    p = _dot_nt(rq, kst) * dstack
    s_t = s_ref[...]
    o = _dot(p, vst) + _dot_nt(rq, s_t) * inner
    s_ref[...] = s_t * cdec + _dot_tn(v, rk * kdec) * mask_vk
    return o


def _prompt_kernel(apply_final,
                   x_ref, normw_ref, win_ref, wout_ref, bbd_ref, cbd_ref, are_ref, aim_ref, dskip_ref,
                   wglu_ref, lbp_ref, hgn_ref, wup_ref, bgate_ref, glan_ref, retn_ref, cos_ref, sin_ref,
                   dstack_ref, inner_ref, kdec_ref, cdec_ref, onesh_ref, onesg_ref, tril_ref, blk_ref,
                   fnw_ref,
                   y_ref, s5_ref, hgs_ref, glas_ref, rets_ref,
                   proj_scr, bu_scr, mix_scr, term_scr):
    nb = x_ref.shape[0]
    rows = nb * T_CHUNK
    n_slab = 2 * S5_N // LANES
    half = n_slab // 2

    @pl.when(pl.program_id(0) == 0)
    def _init():
        s5_ref[...] = jnp.zeros_like(s5_ref)
        hgs_ref[...] = jnp.zeros_like(hgs_ref)
        glas_ref[...] = jnp.zeros_like(glas_ref)
        rets_ref[...] = jnp.zeros_like(rets_ref)
        bu_scr[...] = jnp.zeros_like(bu_scr)

    x = x_ref[...].reshape(rows, D_MODEL)
    h = _rmsnorm_rows(x, normw_ref[...])
    proj_scr[:, 0:N_PACK] = _dot(h, win_ref[...])

    u = proj_scr[:, C_U:C_U + BRANCH]
    bu = _dot(u, bbd_ref[...])
    for c in range(n_slab):
        for b in range(nb):
            bu_scr[c, b * PITCH:b * PITCH + T_CHUNK, :] = bu[b * T_CHUNK:(b + 1) * T_CHUNK,
                                                            c * LANES:(c + 1) * LANES]
    a_re = [jnp.broadcast_to(are_ref[:, c * LANES:(c + 1) * LANES], (nb, LANES)) for c in range(half)]
    a_im = [jnp.broadcast_to(aim_ref[:, c * LANES:(c + 1) * LANES], (nb, LANES)) for c in range(half)]
    s_init = s5_ref[...]
    carry0 = tuple(s_init[:, c * LANES:(c + 1) * LANES] for c in range(n_slab))

    def scan_step(t, carry):
        new = [None] * n_slab
        for c in range(half):
            sr, si = carry[c], carry[half + c]
            br = bu_scr[c, pl.ds(t, nb, stride=PITCH), :]
            bi = bu_scr[half + c, pl.ds(t, nb, stride=PITCH), :]
            nr = a_re[c] * sr - a_im[c] * si + br
            ni = a_re[c] * si + a_im[c] * sr + bi
            bu_scr[c, pl.ds(t, nb, stride=PITCH), :] = nr
            bu_scr[half + c, pl.ds(t, nb, stride=PITCH), :] = ni
            new[c], new[half + c] = nr, ni
        return tuple(new)

    carry = lax.fori_loop(0, T_CHUNK, scan_step, carry0)
    s5_ref[...] = jnp.concatenate(carry, axis=1)
    s_all = jnp.concatenate([bu_scr[c] for c in range(n_slab)], axis=1)
    y_all = _dot(s_all, cbd_ref[...])
    y_lin = jnp.concatenate([y_all[b * PITCH:b * PITCH + T_CHUNK] for b in range(nb)], axis=0)
    mix_scr[:, 0:BRANCH] = _s5_output(y_lin, u, proj_scr[:, C_SZ:C_SZ + BRANCH], dskip_ref[...],
                                      wglu_ref[...])

    log_f, hk = _hgrn_gates(proj_scr[:, C_HF:C_HF + BRANCH], lbp_ref[...])
    proj_scr[:, C_HF:C_HF + BRANCH] = log_f
    proj_scr[:, C_HK:C_HK + BRANCH] = hk
    g_gla = _log_sigmoid(_dot3(proj_scr[:, C_LR:C_LR + LANES], wup_ref[...]) + bgate_ref[...])
    proj_scr[:, C_LR:C_LR + LANES] = g_gla * (1.0 / GLA_TAU)

    ones_h = onesh_ref[...]
    ones_g = onesg_ref[...]
    mask_h = ones_h.astype(F32)
    mask_g = jnp.transpose(ones_g.astype(F32))
    tril = tril_ref[...]
    blk = blk_ref[...]

    def per_batch(b, _):
        rs = pl.ds(pl.multiple_of(b * T_CHUNK, T_CHUNK), T_CHUNK)
        o_hg = _gated_chunk(proj_scr[rs, C_HQ:C_HQ + BRANCH], proj_scr[rs, C_HK:C_HK + BRANCH],
                            proj_scr[rs, C_HI:C_HI + BRANCH], proj_scr[rs, C_HF:C_HF + BRANCH],
                            hgs_ref.at[b], ones_h, mask_h, tril, blk, term_scr, HEADS * HG_DK)
        mix_scr[rs, BRANCH:2 * BRANCH] = o_hg
        o_gla = _gated_chunk(proj_scr[rs, C_GQ:C_GQ + LANES] * (GLA_DK ** -0.5),
                             proj_scr[rs, C_GK:C_GK + LANES], proj_scr[rs, C_GV:C_GV + BRANCH],
                             proj_scr[rs, C_LR:C_LR + LANES],
                             glas_ref.at[b], ones_g, mask_g, tril, blk, term_scr, HEADS * GLA_DK)
        mix_scr[rs, 2 * BRANCH:3 * BRANCH] = o_gla
        o_ret = _ret_chunk(proj_scr[rs, C_RQ:C_RQ + LANES], proj_scr[rs, C_RK:C_RK + LANES],
                           proj_scr[rs, C_RV:C_RV + BRANCH], cos_ref[...], sin_ref[...],
                           rets_ref.at[b], dstack_ref[...], inner_ref[...], kdec_ref[...], cdec_ref[...],
                           mask_g)
        mix_scr[rs, 3 * BRANCH:4 * BRANCH] = o_ret
        return 0

    lax.fori_loop(0, nb, per_batch, 0)

    o_hg = _head_rms(mix_scr[:, BRANCH:2 * BRANCH], ones_h, hgn_ref[...])
    mix_scr[:, BRANCH:2 * BRANCH] = o_hg * _silu(proj_scr[:, C_HZ:C_HZ + BRANCH])
    o_gla = _head_rms(mix_scr[:, 2 * BRANCH:3 * BRANCH], ones_h, glan_ref[...])
    mix_scr[:, 2 * BRANCH:3 * BRANCH] = o_gla * _silu(proj_scr[:, C_GZ:C_GZ + BRANCH])
    o_ret = _head_ln(mix_scr[:, 3 * BRANCH:4 * BRANCH], ones_h, retn_ref[...])
    mix_scr[:, 3 * BRANCH:4 * BRANCH] = o_ret * _silu(proj_scr[:, C_RZ:C_RZ + BRANCH])
    out = _mix_and_project(x, mix_scr[...], wout_ref[...], fnw_ref[...], apply_final)
    y_ref[...] = out.reshape(nb, T_CHUNK, D_MODEL)


def _sample_state_step(q, k, v, decay_exp, s0_ref, s_out_ref, expand, vtile_sel, n_chunk):
    width = LANES * DV
    outs = []
    for c in range(n_chunk):
        fs = slice(c * LANES, (c + 1) * LANES)
        cs = slice(c * width, (c + 1) * width)
        vsel, vin = vtile_sel(c, v)
        k_exp = _dot_sel_rhs2(k[:, fs], expand)
        q_exp = _dot_sel_rhs2(q[:, fs], expand)
        v_til = _dot_sel_rhs2(vin, vsel)
        s_new = s0_ref[:, cs] * decay_exp(c) + k_exp * v_til
        s_out_ref[:, cs] = s_new
        outs.append(_dot_sel_nt2(q_exp * s_new, vsel))
    return outs


def _sample_kernel(apply_final,
                   x_ref, normw_ref, win_ref, wout_ref, bbd_ref, cbd_ref, are_ref, aim_ref, dskip_ref,
                   wglu_ref, lbp_ref, hgn_ref, wup_ref, bgate_ref, glan_ref, retn_ref, cos_ref, sin_ref,
                   dret_ref, exp_ref, vselh_ref, vselg_ref, onesh_ref, fnw_ref,
                   s5re_ref, s5im_ref, hg_ref, gla_ref, ret_ref,
                   y_ref, s5re_o, s5im_o, hg_o, gla_o, ret_o):
    x = x_ref[...]
    rows = x.shape[0]
    h = _rmsnorm_rows(x, normw_ref[...])
    proj = _dot(h, win_ref[...])
    ones_h = onesh_ref[...]
    expand = exp_ref[...]

    u = proj[:, C_U:C_U + BRANCH]
    bu = _dot3(u, bbd_ref[...])
    a_re, a_im = are_ref[...], aim_ref[...]
    s0r, s0i = s5re_ref[...], s5im_ref[...]
    s_re = a_re * s0r - a_im * s0i + bu[:, 0:S5_N]
    s_im = a_re * s0i + a_im * s0r + bu[:, S5_N:2 * S5_N]
    s5re_o[...] = s_re
    s5im_o[...] = s_im
    y_lin = _dot3(jnp.concatenate([s_re, s_im], axis=1), cbd_ref[...])
    mix0 = _s5_output(y_lin, u, proj[:, C_SZ:C_SZ + BRANCH], dskip_ref[...], wglu_ref[...])

    log_f, hk = _hgrn_gates(proj[:, C_HF:C_HF + BRANCH], lbp_ref[...])
    d_hg = jnp.exp(log_f)
    vselh = vselh_ref[...]
    o_parts = _sample_state_step(
        proj[:, C_HQ:C_HQ + BRANCH], hk, proj[:, C_HI:C_HI + BRANCH],
        lambda c: _dot_sel_rhs(d_hg[:, c * LANES:(c + 1) * LANES], expand),
        hg_ref, hg_o, expand,
        lambda c, v: (vselh, v[:, c * LANES:(c + 1) * LANES]), BRANCH // LANES)
    o_hg = jnp.concatenate(o_parts, axis=1)
    mix1 = _head_rms(o_hg, ones_h, hgn_ref[...]) * _silu(proj[:, C_HZ:C_HZ + BRANCH])

    lr = jnp.concatenate([proj[:, C_LR:C_LR + LANES]], axis=1)
    g_gla = _log_sigmoid(_dot3(lr, wup_ref[...]) + bgate_ref[...]) * (1.0 / GLA_TAU)
    d_gla = jnp.exp(g_gla)
    vselg = vselg_ref[...]
    o_gla = _sample_state_step(
        proj[:, C_GQ:C_GQ + LANES] * (GLA_DK ** -0.5), proj[:, C_GK:C_GK + LANES],
        proj[:, C_GV:C_GV + BRANCH],
        lambda c: _dot_sel_rhs(d_gla, expand),
        gla_ref, gla_o, expand, lambda c, v: (vselg, v), 1)[0]
    mix2 = _head_rms(o_gla, ones_h, glan_ref[...]) * _silu(proj[:, C_GZ:C_GZ + BRANCH])

    lane = lax.broadcasted_iota(jnp.int32, (rows, HEADS * RET_DK), 1)
    first_half = (lane % RET_DK) < (RET_DK // 2)
    rq = _rotary(proj[:, C_RQ:C_RQ + LANES], cos_ref[...], sin_ref[...], first_half)
    rk = _rotary(proj[:, C_RK:C_RK + LANES], cos_ref[...], sin_ref[...], first_half) * (RET_DK ** -0.5)
    dret = dret_ref[...]
    o_ret = _sample_state_step(rq, rk, proj[:, C_RV:C_RV + BRANCH], lambda c: dret,
                               ret_ref, ret_o, expand, lambda c, v: (vselg, v), 1)[0]
    mix3 = _head_ln(o_ret, ones_h, retn_ref[...]) * _silu(proj[:, C_RZ:C_RZ + BRANCH])

    mix = jnp.concatenate([mix0, mix1, mix2, mix3], axis=1)
    y_ref[...] = _mix_and_project(x, mix, wout_ref[...], fnw_ref[...], apply_final)


def _ret_log_gamma():
    return jnp.log1p(-jnp.exp2(-5.0 - jnp.arange(HEADS, dtype=F32)))


def _block_diag(blocks):
    g, r, c = blocks.shape
    eye = jnp.eye(g, dtype=blocks.dtype)
    return jnp.einsum('grc,gh->grhc', blocks, eye).reshape(g * r, g * c)


def _constants():
    ones_h = (np.arange(BRANCH)[:, None] // DV == np.arange(BRANCH)[None, :] // DV)
    ones_g = (np.arange(HEADS * GLA_DK)[:, None] // GLA_DK == np.arange(BRANCH)[None, :] // DV)
    r = np.arange(T_CHUNK)
    same_sub = r[:, None] // SUB == r[None, :] // SUB
    tril = same_sub & (r[None, :] <= r[:, None])
    col = np.arange(LANES * DV)
    expand = np.arange(LANES)[:, None] == col[None, :] // DV
    vsel_h = ((np.arange(LANES)[:, None] // DV == col[None, :] // (DV * HG_DK))
              & (np.arange(LANES)[:, None] % DV == col[None, :] % DV))
    vsel_g = ((np.arange(BRANCH)[:, None] // DV == col[None, :] // (DV * GLA_DK))
              & (np.arange(BRANCH)[:, None] % DV == col[None, :] % DV))
    as_bf16 = lambda m: jnp.asarray(m.astype(np.float32), dtype=BF16)
    return dict(ones_h=as_bf16(ones_h), ones_g=as_bf16(ones_g), tril=as_bf16(tril), blk=as_bf16(same_sub),
                expand=as_bf16(expand), vsel_h=as_bf16(vsel_h), vsel_g=as_bf16(vsel_g))


def _ret_tables(seq_len):
    lg = _ret_log_gamma()
    idx = jnp.arange(T_CHUNK, dtype=F32)
    rel = idx[:, None] - idx[None, :]
    causal = rel >= 0
    decay = jnp.where(causal[None], jnp.exp(jnp.where(causal, rel, 0.0)[None] * lg[:, None, None]), 0.0)
    dstack = jnp.transpose(decay, (1, 0, 2)).reshape(T_CHUNK, HEADS * T_CHUNK)
    inner = jnp.repeat(jnp.exp((idx[:, None] + 1.0) * lg[None, :]), DV, axis=1)
    kdec = jnp.repeat(jnp.exp((T_CHUNK - 1.0 - idx[:, None]) * lg[None, :]), RET_DK, axis=1)
    cdec = jnp.repeat(jnp.exp(T_CHUNK * lg)[None, :], RET_DK, axis=1)
    dret = jnp.repeat(jnp.exp(lg), RET_DK * DV)[None, :]
    return dstack, inner, kdec, cdec, dret


def _rope_tables(pos):
    half = RET_DK // 2
    inv = ROPE_BASE ** (-jnp.arange(half, dtype=F32) / half)
    ang = pos.astype(F32)[:, None] * inv[None, :]
    cos, sin = jnp.cos(ang), jnp.sin(ang)
    cos_t = jnp.tile(jnp.concatenate([cos, cos], axis=1), (1, HEADS))
    sin_t = jnp.tile(jnp.concatenate([-sin, sin], axis=1), (1, HEADS))
    return cos_t, sin_t


def _pack_w_in(w):
    offs = np.cumsum([0, 256, 256, 256, 256, 256, 256, 128, 128, 256, 16, 256, 128, 128, 256, 256])
    seg = lambda i: w[:, int(offs[i]):int(offs[i + 1])]
    order = [0, 1, 2, 3, 4, 5, 6, 7, 8, 10, 11, 12, 13, 14]
    pad = jnp.zeros((w.shape[0], LANES - GLA_LOWRANK), w.dtype)
    return jnp.concatenate([seg(i) for i in order] + [seg(9), pad], axis=1).astype(BF16)


def _s5_discretize(lam_re, lam_im, log_step, b_re, b_im, c_re, c_im):
    lr, li = lam_re.astype(F32), lam_im.astype(F32)
    step = jnp.exp(log_step.astype(F32))[:, None]
    mag = jnp.exp(lr * step)
    ab_re = mag * jnp.cos(li * step)
    ab_im = mag * jnp.sin(li * step)
    den = lr * lr + li * li
    nr = ab_re - 1.0
    f_re = (nr * lr + ab_im * li) / den
    f_im = (ab_im * lr - nr * li) / den
    br, bi = b_re.astype(F32), b_im.astype(F32)
    bb_re = f_re[..., None] * br - f_im[..., None] * bi
    bb_im = f_re[..., None] * bi + f_im[..., None] * br
    bbd = jnp.concatenate([_block_diag(jnp.transpose(bb_re, (0, 2, 1))),
                           _block_diag(jnp.transpose(bb_im, (0, 2, 1)))], axis=1)
    cbd = jnp.concatenate([_block_diag(jnp.transpose(c_re.astype(F32), (0, 2, 1))),
                           -_block_diag(jnp.transpose(c_im.astype(F32), (0, 2, 1)))], axis=0)
    return bbd, cbd, ab_re.reshape(1, S5_N), ab_im.reshape(1, S5_N)


def _full(shape):
    return pl.BlockSpec(shape, lambda i: (0,) * len(shape), pipeline_mode=pl.Buffered(1))


def _prompt_layer(x, p, consts, tabs, rope, apply_final):
    nb, seq, _ = x.shape
    n_steps = seq // T_CHUNK
    rows = nb * T_CHUNK
    dstack, inner, kdec, cdec, _ = tabs
    cos_t, sin_t = rope
    inputs = [x, p['norm_w'], p['w_in'], p['w_out'], p['bbd'].astype(BF16), p['cbd'].astype(BF16),
              p['a_re'], p['a_im'], p['d_skip'], p['w_glu'], p['lbp'], p['hg_norm'], p['w_up'], p['b_gate'],
              p['gla_norm'], p['ret_norm'], cos_t, sin_t, dstack, inner, kdec, cdec,
              consts['ones_h'], consts['ones_g'], consts['tril'], consts['blk'], p['final_norm']]
    in_specs = [pl.BlockSpec((nb, T_CHUNK, D_MODEL), lambda i: (0, i, 0))]
    for a in inputs[1:16]:
        in_specs.append(_full(a.shape))
    in_specs.append(pl.BlockSpec((T_CHUNK, LANES), lambda i: (i, 0)))
    in_specs.append(pl.BlockSpec((T_CHUNK, LANES), lambda i: (i, 0)))
    for a in inputs[18:]:
        in_specs.append(_full(a.shape))
    out_shape = (jax.ShapeDtypeStruct((nb, seq, D_MODEL), F32),
                 jax.ShapeDtypeStruct((nb, 2 * S5_N), F32),
                 jax.ShapeDtypeStruct((nb, BRANCH, HEADS * HG_DK), F32),
                 jax.ShapeDtypeStruct((nb, BRANCH, HEADS * GLA_DK), F32),
                 jax.ShapeDtypeStruct((nb, BRANCH, HEADS * RET_DK), F32))
    out_specs = (pl.BlockSpec((nb, T_CHUNK, D_MODEL), lambda i: (0, i, 0)),
                 pl.BlockSpec((nb, 2 * S5_N), lambda i: (0, 0)),
                 pl.BlockSpec((nb, BRANCH, HEADS * HG_DK), lambda i: (0, 0, 0)),
                 pl.BlockSpec((nb, BRANCH, HEADS * GLA_DK), lambda i: (0, 0, 0)),
                 pl.BlockSpec((nb, BRANCH, HEADS * RET_DK), lambda i: (0, 0, 0)))
    scratch = [pltpu.VMEM((rows, N_SCR), F32),
               pltpu.VMEM((2 * S5_N // LANES, nb * PITCH, LANES), F32),
               pltpu.VMEM((rows, D_MODEL), F32),
               pltpu.VMEM((T_CHUNK * SUB, BRANCH), F32)]
    y, s5, hgs, glas, rets = pl.pallas_call(
        functools.partial(_prompt_kernel, apply_final),
        grid=(n_steps,), in_specs=in_specs, out_specs=out_specs, out_shape=out_shape,
        scratch_shapes=scratch, name='prompt_layer',
        compiler_params=pltpu.CompilerParams(dimension_semantics=('arbitrary',),
                                             vmem_limit_bytes=VMEM_LIMIT),
    )(*inputs)
    s5 = s5.reshape(nb, 2, S5_GROUPS, S5_STATE)

    def unstack(st, dk):
        st = st.reshape(nb, HEADS, DV, HEADS, dk)
        diag = jnp.stack([st[:, hh, :, hh, :] for hh in range(HEADS)], axis=1)
        return jnp.transpose(diag, (0, 1, 3, 2))

    return y, (s5[:, 0], s5[:, 1], unstack(hgs, HG_DK), unstack(glas, GLA_DK), unstack(rets, RET_DK))


def _sample_layer(x, states, p, consts, tabs, rope, apply_final):
    nb = x.shape[0]
    n_steps = nb // SAMPLE_ROWS
    s5re, s5im, hg, gla, ret = states
    dret = tabs[4]
    cos_t, sin_t = rope
    consts_in = [p['norm_w'], p['w_in'], p['w_out'], p['bbd'], p['cbd'], p['a_re'], p['a_im'], p['d_skip'],
                 p['w_glu'], p['lbp'], p['hg_norm'], p['w_up'], p['b_gate'], p['gla_norm'], p['ret_norm'],
                 cos_t, sin_t, dret, consts['expand'], consts['vsel_h'], consts['vsel_g'], consts['ones_h'],
                 p['final_norm']]
    rowed = [s5re.reshape(nb, S5_N), s5im.reshape(nb, S5_N), hg.reshape(nb, -1), gla.reshape(nb, -1),
             ret.reshape(nb, -1)]
    row_spec = lambda a: pl.BlockSpec((SAMPLE_ROWS, a.shape[1]), lambda i: (i, 0))
    in_specs = [row_spec(x)] + [_full(a.shape) for a in consts_in] + [row_spec(a) for a in rowed]
    out_arrays = [x] + rowed
    out_shape = tuple(jax.ShapeDtypeStruct(a.shape, F32) for a in out_arrays)
    out_specs = tuple(row_spec(a) for a in out_arrays)
    outs = pl.pallas_call(
        functools.partial(_sample_kernel, apply_final),
        grid=(n_steps,), in_specs=in_specs, out_specs=out_specs, out_shape=out_shape,
        name='sample_layer',
        compiler_params=pltpu.CompilerParams(dimension_semantics=('arbitrary',),
                                             vmem_limit_bytes=VMEM_LIMIT),
    )(x, *consts_in, *rowed)
    y = outs[0]
    new = (outs[1].reshape(s5re.shape), outs[2].reshape(s5im.shape), outs[3].reshape(hg.shape),
           outs[4].reshape(gla.shape), outs[5].reshape(ret.shape))
    return y, new


def kernel(x_prompt, x_sample, state_s5_re, state_s5_im, state_hgrn, state_gla, state_ret, norm_w, final_norm_w, w_in, w_out, s5_lam_re, s5_lam_im, s5_log_step, s5_b_re, s5_b_im, s5_c_re, s5_c_im, s5_d, s5_w_glu, hgrn_lb_logits, hgrn_norm_w, gla_w_gate_up, gla_b_gate, gla_norm_w, ret_norm_w):
    depth = w_in.shape[0]
    seq = x_prompt.shape[1]
    consts = _constants()
    tabs = _ret_tables(seq)
    rope_p = _rope_tables(jnp.arange(seq))
    rope_s = _rope_tables(PAST_LEN + jnp.arange(1))
    lb_all = jnp.cumsum(jax.nn.softmax(hgrn_lb_logits.astype(F32), axis=0), axis=0)
    lb_all = lb_all - lb_all[0:1]

    xp = x_prompt
    xs = x_sample.reshape(x_sample.shape[0], D_MODEL)
    new_p = ([], [], [], [], [])
    new_s = ([], [], [], [], [])
    for l in range(depth):
        bbd, cbd, a_re, a_im = _s5_discretize(s5_lam_re[l], s5_lam_im[l], s5_log_step[l], s5_b_re[l],
                                              s5_b_im[l], s5_c_re[l], s5_c_im[l])
        lb = lb_all[l][None, :]
        w_up = jnp.zeros((LANES, HEADS * GLA_DK), F32).at[:GLA_LOWRANK].set(gla_w_gate_up[l].astype(F32))
        p = dict(norm_w=norm_w[l][None, :].astype(F32), w_in=_pack_w_in(w_in[l]), w_out=w_out[l].astype(BF16),
                 bbd=bbd, cbd=cbd, a_re=a_re, a_im=a_im, d_skip=s5_d[l][None, :].astype(F32),
                 w_glu=s5_w_glu[l].astype(BF16),
                 lbp=jnp.concatenate([jnp.log(lb), jnp.log1p(-lb), 1.0 - lb, lb, jnp.zeros((4, BRANCH), F32)],
                                     axis=0),
                 hg_norm=hgrn_norm_w[l][None, :].astype(F32), w_up=w_up,
                 b_gate=gla_b_gate[l][None, :].astype(F32), gla_norm=gla_norm_w[l][None, :].astype(F32),
                 ret_norm=ret_norm_w[l][None, :].astype(F32), final_norm=final_norm_w[None, :].astype(F32))
        last = l == depth - 1
        xp, st_p = _prompt_layer(xp, p, consts, tabs, rope_p, last)
        xs, st_s = _sample_layer(xs, (state_s5_re[l], state_s5_im[l], state_hgrn[l], state_gla[l],
                                      state_ret[l]), p, consts, tabs, rope_s, last)
        for i in range(5):
            new_p[i].append(st_p[i])
            new_s[i].append(st_s[i])
    y_sample = xs.reshape(x_sample.shape)
    return (xp, y_sample,
            jnp.stack(new_p[0]), jnp.stack(new_p[1]), jnp.stack(new_p[2]), jnp.stack(new_p[3]),
            jnp.stack(new_p[4]),
            jnp.stack(new_s[0]), jnp.stack(new_s[1]), jnp.stack(new_s[2]), jnp.stack(new_s[3]),
            jnp.stack(new_s[4]))
```

```python
import functools
import math

import numpy as np
import jax
import jax.numpy as jnp
from jax import lax
from jax.experimental import pallas as pl
from jax.experimental.pallas import tpu as pltpu

F32 = jnp.float32
BF16 = jnp.bfloat16

D_MODEL = 1024
BRANCH = 256
S5_CH = 16
S5_GROUPS = 16
S5_STATE = 64
S5_N = S5_GROUPS * S5_STATE
HEADS = 4
HG_DK = 64
GLA_DK = 32
RET_DK = 32
DV = 64
GLA_LOWRANK = 16
GLA_TAU = 16.0
ROPE_BASE = 10000.0
PAST_LEN = 16384
EPS = 1e-6
SUB = 16

LANES = 128
T_CHUNK = 64
PITCH = T_CHUNK + 8
TERM_ROWS = T_CHUNK * (SUB + SUB // 2) // 2
SAMPLE_ROWS = 32
SAMPLE_COLS = 2048
LOG2E = math.log2(math.e)
MASK_NEG = -1e30
VMEM_LIMIT = 56 * 1024 * 1024

C_U, C_SZ, C_HQ, C_HF, C_HI, C_HZ = 0, 256, 512, 768, 1024, 1280
C_GQ, C_GK, C_GV, C_GZ = 1536, 1664, 1792, 2048
C_RQ, C_RK, C_RV, C_RZ = 2304, 2432, 2560, 2816
C_LR = 3072
N_PACK = 3200
C_HK = N_PACK
N_SCR = N_PACK + BRANCH


def _dot(a, b):
    return jnp.dot(a.astype(BF16), b.astype(BF16), preferred_element_type=F32)


def _dot_nt(a, b):
    return lax.dot_general(a.astype(BF16), b.astype(BF16), (((1,), (1,)), ((), ())),
                           preferred_element_type=F32)


def _dot_tn(a, b):
    return lax.dot_general(a.astype(BF16), b.astype(BF16), (((0,), (0,)), ((), ())),
                           preferred_element_type=F32)


def _split3(x):
    x1 = x.astype(BF16)
    r1 = x - x1.astype(F32)
    x2 = r1.astype(BF16)
    x3 = (r1 - x2.astype(F32)).astype(BF16)
    return x1, x2, x3


def _dot_sel_lhs(sel, x):
    x1, x2, x3 = _split3(x)
    d = lambda p: jnp.dot(sel, p, preferred_element_type=F32)
    return d(x1) + d(x2) + d(x3)


def _dot_sel_rhs(x, sel):
    x1, x2, x3 = _split3(x)
    d = lambda p: jnp.dot(p, sel, preferred_element_type=F32)
    return d(x1) + d(x2) + d(x3)


def _dot_sel_rhs2(x, sel):
    x1 = x.astype(BF16)
    x2 = (x - x1.astype(F32)).astype(BF16)
    return (jnp.dot(x1, sel, preferred_element_type=F32)
            + jnp.dot(x2, sel, preferred_element_type=F32))


def _dot_sel_nt2(x, sel):
    x1 = x.astype(BF16)
    x2 = (x - x1.astype(F32)).astype(BF16)
    dn = (((1,), (1,)), ((), ()))
    return (lax.dot_general(x1, sel, dn, preferred_element_type=F32)
            + lax.dot_general(x2, sel, dn, preferred_element_type=F32))


def _dot3(a, b):
    a1 = a.astype(BF16)
    a2 = (a - a1.astype(F32)).astype(BF16)
    b1 = b.astype(BF16)
    b2 = (b - b1.astype(F32)).astype(BF16)
    d = lambda p, q: jnp.dot(p, q, preferred_element_type=F32)
    return d(a1, b1) + d(a1, b2) + d(a2, b1)


def _sigmoid(x):
    return 1.0 / (1.0 + jnp.exp(-x))


def _silu(x):
    return x * _sigmoid(x)


def _log_sigmoid(x):
    return jnp.minimum(x, 0.0) - jnp.log(1.0 + jnp.exp(-jnp.abs(x)))


def _gelu_tanh(x):
    return 0.5 * x * (1.0 + jnp.tanh(math.sqrt(2.0 / math.pi) * (x + 0.044715 * (x * x * x))))


def _rmsnorm_rows(x, w):
    return x * lax.rsqrt(jnp.mean(x * x, axis=-1, keepdims=True) + EPS) * w


def _head_rms(o, ones_h, gain):
    ms = _dot_sel_rhs2(o * o, ones_h) * (1.0 / DV)
    return o * lax.rsqrt(ms + EPS) * gain


def _head_ln(o, ones_h, gain):
    c = o - _dot_sel_rhs2(o, ones_h) * (1.0 / DV)
    var = _dot_sel_rhs2(c * c, ones_h) * (1.0 / DV)
    return c * lax.rsqrt(var + EPS) * gain


def _hgrn_gates(xf, lbp):
    loglb, log1mlb, one_m_lb = lbp[0:1, :], lbp[1:2, :], lbp[2:3, :]
    bterm = log1mlb + _log_sigmoid(xf)
    m = jnp.maximum(loglb, bterm)
    log_f = m + jnp.log(jnp.exp(loglb - m) + jnp.exp(bterm - m))
    return log_f, one_m_lb * _sigmoid(-xf)


def _rotary(t, cos, sin_signed, first_half):
    half = RET_DK // 2
    swapped = jnp.where(first_half, pltpu.roll(t, LANES - half, 1), pltpu.roll(t, half, 1))
    return t * cos + swapped * sin_signed


def _s5_output(y_lin, u, sz, dskip, wglu):
    y = _gelu_tanh(y_lin + u * dskip)
    y = y * _sigmoid(_dot(y, wglu))
    return y * _silu(sz)


def _mix_and_project(x, mix, wout, fnw, apply_final):
    out = x + _dot(mix, wout)
    if apply_final:
        out = _rmsnorm_rows(out, fnw)
    return out


def _gated_chunk(q, k, v, g, s_ref, ones_kv, mask_vk, tril, blk, term_scr, kdim):
    n_sub = T_CHUNK // SUB
    hs = SUB // 2
    bc = _dot_sel_lhs(tril, g) * LOG2E
    bl = _dot_sel_lhs(blk, g) * LOG2E
    qt = q * jnp.exp2(bc)
    kh = k * jnp.exp2(bl - bc)
    trow = lax.broadcasted_iota(jnp.int32, (hs, kdim), 0)
    neg = [jnp.where(trow >= d, 0.0, MASK_NEG) for d in range(1, hs)]
    ws = [_dot_tn(v[SUB * j:SUB * (j + 1)], kh[SUB * j:SUB * (j + 1)]) * mask_vk for j in range(n_sub)]
    states = [s_ref[...]]
    for j in range(n_sub):
        states.append(states[j] * jnp.exp2(bl[SUB * j:SUB * j + 1, :]) + ws[j])
    s_ref[...] = states[n_sub]
    inter = [_dot_nt(qt[SUB * j:SUB * (j + 1)], states[j]) for j in range(n_sub)]
    row = 0
    for j in range(n_sub):
        r0 = SUB * j
        q_lo, q_hi = q[r0:r0 + hs], q[r0 + hs:r0 + SUB]
        b_lo, b_hi = bc[r0:r0 + hs], bc[r0 + hs:r0 + SUB]
        for s in range(SUB):
            ks, bs = k[r0 + s:r0 + s + 1, :], bc[r0 + s:r0 + s + 1, :]
            d = s % hs
            q_dg, b_dg = (q_lo, b_lo) if s < hs else (q_hi, b_hi)
            e_dg = b_dg - bs if d == 0 else b_dg - bs + neg[d - 1]
            term_scr[row:row + hs, :] = q_dg * ks * jnp.exp2(e_dg)
            row += hs
            if s < hs:
                term_scr[row:row + hs, :] = q_hi * ks * jnp.exp2(b_hi - bs)
                row += hs
    att = _dot(term_scr[0:row, :], ones_kv)
    outs = []
    row = 0
    for j in range(n_sub):
        r0 = SUB * j
        acc_lo, acc_hi = inter[j][0:hs], inter[j][hs:SUB]
        for s in range(SUB):
            vs = v[r0 + s:r0 + s + 1, :]
            if s < hs:
                acc_lo = acc_lo + att[row:row + hs] * vs
                row += hs
            acc_hi = acc_hi + att[row:row + hs] * vs
            row += hs
        outs += [acc_lo, acc_hi]
    return jnp.concatenate(outs, axis=0)


def _ret_chunk(q, k, v, cos, sin_signed, s_ref, dstack, inner, kdec, cdec, mask_vk):
    lane = lax.broadcasted_iota(jnp.int32, (T_CHUNK, HEADS * RET_DK), 1)
    first_half = (lane % RET_DK) < (RET_DK // 2)
    rq = _rotary(q, cos, sin_signed, first_half)
    rk = _rotary(k, cos, sin_signed, first_half) * (RET_DK ** -0.5)
    lane_v = lax.broadcasted_iota(jnp.int32, (T_CHUNK, HEADS * DV), 1)
    kst = jnp.concatenate([jnp.where(lane // RET_DK == h, rk, 0.0) for h in range(HEADS)], axis=0)
    vst = jnp.concatenate([jnp.where(lane_v // DV == h, v, 0.0) for h in range(HEADS)], axis=0)
    p = _dot_nt(rq, kst) * dstack
    s_t = s_ref[...]
    o = _dot(p, vst) + _dot_nt(rq, s_t) * inner
    s_ref[...] = s_t * cdec + _dot_tn(v, rk * kdec) * mask_vk
    return o


def _prompt_kernel(apply_final,
                   x_ref, normw_ref, win_ref, wout_ref, bbd_ref, cbd_ref, are_ref, aim_ref, dskip_ref,
                   wglu_ref, lbp_ref, hgn_ref, wup_ref, bgate_ref, glan_ref, retn_ref, cos_ref, sin_ref,
                   dstack_ref, inner_ref, kdec_ref, cdec_ref, onesh_ref, onesg_ref, tril_ref, blk_ref,
                   fnw_ref,
                   y_ref, s5_ref, hgs_ref, glas_ref, rets_ref,
                   proj_scr, bu_scr, mix_scr, term_h0, term_g0, term_h1, term_g1):
    nb = x_ref.shape[0]
    rows = nb * T_CHUNK
    n_slab = 2 * S5_N // LANES
    half = n_slab // 2

    @pl.when(pl.program_id(0) == 0)
    def _init():
        s5_ref[...] = jnp.zeros_like(s5_ref)
        hgs_ref[...] = jnp.zeros_like(hgs_ref)
        glas_ref[...] = jnp.zeros_like(glas_ref)
        rets_ref[...] = jnp.zeros_like(rets_ref)
        bu_scr[...] = jnp.zeros_like(bu_scr)

    x = x_ref[...].reshape(rows, D_MODEL)
    h = _rmsnorm_rows(x, normw_ref[...])
    proj_scr[:, 0:N_PACK] = _dot(h, win_ref[...])

    u = proj_scr[:, C_U:C_U + BRANCH]
    for c2 in range(n_slab // 2):
        bu = _dot(u, bbd_ref[:, 2 * c2 * LANES:2 * (c2 + 1) * LANES])
        for cc in range(2):
            for b in range(nb):
                bu_scr[2 * c2 + cc, b * PITCH:b * PITCH + T_CHUNK, :] = bu[b * T_CHUNK:(b + 1) * T_CHUNK,
                                                                           cc * LANES:(cc + 1) * LANES]
    a_re = [jnp.broadcast_to(are_ref[:, c * LANES:(c + 1) * LANES], (nb, LANES)) for c in range(half)]
    a_im = [jnp.broadcast_to(aim_ref[:, c * LANES:(c + 1) * LANES], (nb, LANES)) for c in range(half)]
    s_init = s5_ref[...]
    carry0 = tuple(s_init[:, c * LANES:(c + 1) * LANES] for c in range(n_slab))

    def scan_step(t, carry):
        new = [None] * n_slab
        for c in range(half):
            sr, si = carry[c], carry[half + c]
            br = bu_scr[c, pl.ds(t, nb, stride=PITCH), :]
            bi = bu_scr[half + c, pl.ds(t, nb, stride=PITCH), :]
            nr = a_re[c] * sr - a_im[c] * si + br
            ni = a_re[c] * si + a_im[c] * sr + bi
            bu_scr[c, pl.ds(t, nb, stride=PITCH), :] = nr
            bu_scr[half + c, pl.ds(t, nb, stride=PITCH), :] = ni
            new[c], new[half + c] = nr, ni
        return tuple(new)

    carry = lax.fori_loop(0, T_CHUNK, scan_step, carry0)
    s5_ref[...] = jnp.concatenate(carry, axis=1)
    s_all = jnp.concatenate([bu_scr[c] for c in range(n_slab)], axis=1)
    y_all = _dot(s_all, cbd_ref[...])
    y_lin = jnp.concatenate([y_all[b * PITCH:b * PITCH + T_CHUNK] for b in range(nb)], axis=0)
    mix_scr[:, 0:BRANCH] = _s5_output(y_lin, u, proj_scr[:, C_SZ:C_SZ + BRANCH], dskip_ref[...],
                                      wglu_ref[...])

    log_f, hk = _hgrn_gates(proj_scr[:, C_HF:C_HF + BRANCH], lbp_ref[...])
    proj_scr[:, C_HF:C_HF + BRANCH] = log_f
    proj_scr[:, C_HK:C_HK + BRANCH] = hk
    g_gla = _log_sigmoid(_dot3(proj_scr[:, C_LR:C_LR + LANES], wup_ref[...]) + bgate_ref[...])
    proj_scr[:, C_LR:C_LR + LANES] = g_gla * (1.0 / GLA_TAU)

    ones_h = onesh_ref[...]
    ones_g = onesg_ref[...]
    mask_h = ones_h.astype(F32)
    mask_g = jnp.transpose(ones_g.astype(F32))
    tril = tril_ref[...]
    blk = blk_ref[...]

    def one_batch(b, term_h, term_g):
        rs = pl.ds(pl.multiple_of(b * T_CHUNK, T_CHUNK), T_CHUNK)
        o_hg = _gated_chunk(proj_scr[rs, C_HQ:C_HQ + BRANCH], proj_scr[rs, C_HK:C_HK + BRANCH],
                            proj_scr[rs, C_HI:C_HI + BRANCH], proj_scr[rs, C_HF:C_HF + BRANCH],
                            hgs_ref.at[b], ones_h, mask_h, tril, blk, term_h, HEADS * HG_DK)
        mix_scr[rs, BRANCH:2 * BRANCH] = o_hg
        o_gla = _gated_chunk(proj_scr[rs, C_GQ:C_GQ + LANES] * (GLA_DK ** -0.5),
                             proj_scr[rs, C_GK:C_GK + LANES], proj_scr[rs, C_GV:C_GV + BRANCH],
                             proj_scr[rs, C_LR:C_LR + LANES],
                             glas_ref.at[b], ones_g, mask_g, tril, blk, term_g, HEADS * GLA_DK)
        mix_scr[rs, 2 * BRANCH:3 * BRANCH] = o_gla
        o_ret = _ret_chunk(proj_scr[rs, C_RQ:C_RQ + LANES], proj_scr[rs, C_RK:C_RK + LANES],
                           proj_scr[rs, C_RV:C_RV + BRANCH], cos_ref[...], sin_ref[...],
                           rets_ref.at[b], dstack_ref[...], inner_ref[...], kdec_ref[...], cdec_ref[...],
                           mask_g)
        mix_scr[rs, 3 * BRANCH:4 * BRANCH] = o_ret

    def per_pair(bp, _):
        one_batch(2 * bp, term_h0, term_g0)
        one_batch(2 * bp + 1, term_h1, term_g1)
        return 0

    lax.fori_loop(0, nb // 2, per_pair, 0)

    o_hg = _head_rms(mix_scr[:, BRANCH:2 * BRANCH], ones_h, hgn_ref[...])
    mix_scr[:, BRANCH:2 * BRANCH] = o_hg * _silu(proj_scr[:, C_HZ:C_HZ + BRANCH])
    o_gla = _head_rms(mix_scr[:, 2 * BRANCH:3 * BRANCH], ones_h, glan_ref[...])
    mix_scr[:, 2 * BRANCH:3 * BRANCH] = o_gla * _silu(proj_scr[:, C_GZ:C_GZ + BRANCH])
    o_ret = _head_ln(mix_scr[:, 3 * BRANCH:4 * BRANCH], ones_h, retn_ref[...])
    mix_scr[:, 3 * BRANCH:4 * BRANCH] = o_ret * _silu(proj_scr[:, C_RZ:C_RZ + BRANCH])
    out = _mix_and_project(x, mix_scr[...], wout_ref[...], fnw_ref[...], apply_final)
    y_ref[...] = out.reshape(nb, T_CHUNK, D_MODEL)


def _split2_rows(x):
    x1 = x.astype(BF16)
    return [x1, (x - x1.astype(F32)).astype(BF16)]


def _sample_state_step(q, k, v, decay, dec_row, s0_ref, s_out_ref, expand_ref, vsel_ref, vin_of, n_chunk):
    rows = q.shape[0]
    width = LANES * DV
    outs = []
    for c in range(n_chunk):
        fs = slice(c * LANES, (c + 1) * LANES)
        parts = _split2_rows(k[:, fs]) + _split2_rows(q[:, fs])
        if decay is not None:
            parts += list(_split3(decay[:, fs]))
        lhs = jnp.concatenate(parts, axis=0)
        vlhs = jnp.concatenate(_split2_rows(vin_of(c, v)), axis=0)
        acc = None
        for w in range(width // SAMPLE_COLS):
            lo = w * SAMPLE_COLS
            cs = slice(c * width + lo, c * width + lo + SAMPLE_COLS)
            ex = jnp.dot(lhs, expand_ref[:, lo:lo + SAMPLE_COLS], preferred_element_type=F32)
            k_exp = ex[0:rows] + ex[rows:2 * rows]
            q_exp = ex[2 * rows:3 * rows] + ex[3 * rows:4 * rows]
            if decay is not None:
                d_exp = ex[4 * rows:5 * rows] + ex[5 * rows:6 * rows] + ex[6 * rows:7 * rows]
            else:
                d_exp = dec_row[:, lo:lo + SAMPLE_COLS]
            vsel = vsel_ref[:, lo:lo + SAMPLE_COLS]
            vt = jnp.dot(vlhs, vsel, preferred_element_type=F32)
            s_new = s0_ref[:, cs] * d_exp + k_exp * (vt[0:rows] + vt[rows:2 * rows])
            s_out_ref[:, cs] = s_new
            prod = jnp.concatenate(_split2_rows(q_exp * s_new), axis=0)
            red = lax.dot_general(prod, vsel, (((1,), (1,)), ((), ())), preferred_element_type=F32)
            acc = red if acc is None else acc + red
        outs.append(acc[0:rows] + acc[rows:2 * rows])
    return outs


def _sample_kernel(apply_final,
                   x_ref, normw_ref, win_ref, wout_ref, bbd_ref, cbd_ref, are_ref, aim_ref, dskip_ref,
                   wglu_ref, lbp_ref, hgn_ref, wup_ref, bgate_ref, glan_ref, retn_ref, cos_ref, sin_ref,
                   dret_ref, exp_ref, vselh_ref, vselg_ref, onesh_ref, fnw_ref,
                   s5re_ref, s5im_ref, hg_ref, gla_ref, ret_ref,
                   y_ref, s5re_o, s5im_o, hg_o, gla_o, ret_o):
    x = x_ref[...]
    rows = x.shape[0]
    h = _rmsnorm_rows(x, normw_ref[...])
    proj = _dot(h, win_ref[...])
    ones_h = onesh_ref[...]

    u = proj[:, C_U:C_U + BRANCH]
    bu = _dot3(u, bbd_ref[...])
    a_re, a_im = are_ref[...], aim_ref[...]
    s0r, s0i = s5re_ref[...], s5im_ref[...]
    s_re = a_re * s0r - a_im * s0i + bu[:, 0:S5_N]
    s_im = a_re * s0i + a_im * s0r + bu[:, S5_N:2 * S5_N]
    s5re_o[...] = s_re
    s5im_o[...] = s_im
    y_lin = _dot3(jnp.concatenate([s_re, s_im], axis=1), cbd_ref[...])
    mix0 = _s5_output(y_lin, u, proj[:, C_SZ:C_SZ + BRANCH], dskip_ref[...], wglu_ref[...])

    log_f, hk = _hgrn_gates(proj[:, C_HF:C_HF + BRANCH], lbp_ref[...])
    d_hg = jnp.exp(log_f)
    o_parts = _sample_state_step(
        proj[:, C_HQ:C_HQ + BRANCH], hk, proj[:, C_HI:C_HI + BRANCH], d_hg, None,
        hg_ref, hg_o, exp_ref, vselh_ref, lambda c, v: v[:, c * LANES:(c + 1) * LANES], BRANCH // LANES)
    o_hg = jnp.concatenate(o_parts, axis=1)
    mix1 = _head_rms(o_hg, ones_h, hgn_ref[...]) * _silu(proj[:, C_HZ:C_HZ + BRANCH])

    g_gla = _log_sigmoid(_dot3(proj[:, C_LR:C_LR + LANES], wup_ref[...]) + bgate_ref[...]) * (1.0 / GLA_TAU)
    d_gla = jnp.exp(g_gla)
    o_gla = _sample_state_step(
        proj[:, C_GQ:C_GQ + LANES] * (GLA_DK ** -0.5), proj[:, C_GK:C_GK + LANES],
        proj[:, C_GV:C_GV + BRANCH], d_gla, None,
        gla_ref, gla_o, exp_ref, vselg_ref, lambda c, v: v, 1)[0]
    mix2 = _head_rms(o_gla, ones_h, glan_ref[...]) * _silu(proj[:, C_GZ:C_GZ + BRANCH])

    lane = lax.broadcasted_iota(jnp.int32, (rows, HEADS * RET_DK), 1)
    first_half = (lane % RET_DK) < (RET_DK // 2)
    rq = _rotary(proj[:, C_RQ:C_RQ + LANES], cos_ref[...], sin_ref[...], first_half)
    rk = _rotary(proj[:, C_RK:C_RK + LANES], cos_ref[...], sin_ref[...], first_half) * (RET_DK ** -0.5)
    o_ret = _sample_state_step(rq, rk, proj[:, C_RV:C_RV + BRANCH], None, dret_ref,
                               ret_ref, ret_o, exp_ref, vselg_ref, lambda c, v: v, 1)[0]
    mix3 = _head_ln(o_ret, ones_h, retn_ref[...]) * _silu(proj[:, C_RZ:C_RZ + BRANCH])

    mix = jnp.concatenate([mix0, mix1, mix2, mix3], axis=1)
    y_ref[...] = _mix_and_project(x, mix, wout_ref[...], fnw_ref[...], apply_final)


def _ret_log_gamma():
    return jnp.log1p(-jnp.exp2(-5.0 - jnp.arange(HEADS, dtype=F32)))


def _block_diag(blocks):
    g, r, c = blocks.shape
    eye = jnp.eye(g, dtype=blocks.dtype)
    return jnp.einsum('grc,gh->grhc', blocks, eye).reshape(g * r, g * c)


def _constants():
    ones_h = (np.arange(BRANCH)[:, None] // DV == np.arange(BRANCH)[None, :] // DV)
    ones_g = (np.arange(HEADS * GLA_DK)[:, None] // GLA_DK == np.arange(BRANCH)[None, :] // DV)
    r = np.arange(T_CHUNK)
    same_sub = r[:, None] // SUB == r[None, :] // SUB
    tril = same_sub & (r[None, :] <= r[:, None])
    col = np.arange(LANES * DV)
    expand = np.arange(LANES)[:, None] == col[None, :] // DV
    vsel_h = ((np.arange(LANES)[:, None] // DV == col[None, :] // (DV * HG_DK))
              & (np.arange(LANES)[:, None] % DV == col[None, :] % DV))
    vsel_g = ((np.arange(BRANCH)[:, None] // DV == col[None, :] // (DV * GLA_DK))
              & (np.arange(BRANCH)[:, None] % DV == col[None, :] % DV))
    as_bf16 = lambda m: jnp.asarray(m.astype(np.float32), dtype=BF16)
    return dict(ones_h=as_bf16(ones_h), ones_g=as_bf16(ones_g), tril=as_bf16(tril), blk=as_bf16(same_sub),
                expand=as_bf16(expand), vsel_h=as_bf16(vsel_h), vsel_g=as_bf16(vsel_g))


def _ret_tables(seq_len):
    lg = _ret_log_gamma()
    idx = jnp.arange(T_CHUNK, dtype=F32)
    rel = idx[:, None] - idx[None, :]
    causal = rel >= 0
    decay = jnp.where(causal[None], jnp.exp(jnp.where(causal, rel, 0.0)[None] * lg[:, None, None]), 0.0)
    dstack = jnp.transpose(decay, (1, 0, 2)).reshape(T_CHUNK, HEADS * T_CHUNK)
    inner = jnp.repeat(jnp.exp((idx[:, None] + 1.0) * lg[None, :]), DV, axis=1)
    kdec = jnp.repeat(jnp.exp((T_CHUNK - 1.0 - idx[:, None]) * lg[None, :]), RET_DK, axis=1)
    cdec = jnp.repeat(jnp.exp(T_CHUNK * lg)[None, :], RET_DK, axis=1)
    dret = jnp.repeat(jnp.exp(lg), RET_DK * DV)[None, :]
    return dstack, inner, kdec, cdec, dret


def _rope_tables(pos):
    half = RET_DK // 2
    inv = ROPE_BASE ** (-jnp.arange(half, dtype=F32) / half)
    ang = pos.astype(F32)[:, None] * inv[None, :]
    cos, sin = jnp.cos(ang), jnp.sin(ang)
    cos_t = jnp.tile(jnp.concatenate([cos, cos], axis=1), (1, HEADS))
    sin_t = jnp.tile(jnp.concatenate([-sin, sin], axis=1), (1, HEADS))
    return cos_t, sin_t


def _pack_w_in(w):
    offs = np.cumsum([0, 256, 256, 256, 256, 256, 256, 128, 128, 256, 16, 256, 128, 128, 256, 256])
    seg = lambda i: w[:, int(offs[i]):int(offs[i + 1])]
    order = [0, 1, 2, 3, 4, 5, 6, 7, 8, 10, 11, 12, 13, 14]
    pad = jnp.zeros((w.shape[0], LANES - GLA_LOWRANK), w.dtype)
    return jnp.concatenate([seg(i) for i in order] + [seg(9), pad], axis=1).astype(BF16)


def _s5_discretize(lam_re, lam_im, log_step, b_re, b_im, c_re, c_im):
    lr, li = lam_re.astype(F32), lam_im.astype(F32)
    step = jnp.exp(log_step.astype(F32))[:, None]
    mag = jnp.exp(lr * step)
    ab_re = mag * jnp.cos(li * step)
    ab_im = mag * jnp.sin(li * step)
    den = lr * lr + li * li
    nr = ab_re - 1.0
    f_re = (nr * lr + ab_im * li) / den
    f_im = (ab_im * lr - nr * li) / den
    br, bi = b_re.astype(F32), b_im.astype(F32)
    bb_re = f_re[..., None] * br - f_im[..., None] * bi
    bb_im = f_re[..., None] * bi + f_im[..., None] * br
    bbd = jnp.concatenate([_block_diag(jnp.transpose(bb_re, (0, 2, 1))),
                           _block_diag(jnp.transpose(bb_im, (0, 2, 1)))], axis=1)
    cbd = jnp.concatenate([_block_diag(jnp.transpose(c_re.astype(F32), (0, 2, 1))),
                           -_block_diag(jnp.transpose(c_im.astype(F32), (0, 2, 1)))], axis=0)
    return bbd, cbd, ab_re.reshape(1, S5_N), ab_im.reshape(1, S5_N)


def _full(shape):
    return pl.BlockSpec(shape, lambda i: (0,) * len(shape), pipeline_mode=pl.Buffered(1))


def _prompt_layer(x, p, consts, tabs, rope, apply_final):
    nb, seq, _ = x.shape
    n_steps = seq // T_CHUNK
    rows = nb * T_CHUNK
    dstack, inner, kdec, cdec, _ = tabs
    cos_t, sin_t = rope
    inputs = [x, p['norm_w'], p['w_in'], p['w_out'], p['bbd'].astype(BF16), p['cbd'].astype(BF16),
              p['a_re'], p['a_im'], p['d_skip'], p['w_glu'], p['lbp'], p['hg_norm'], p['w_up'], p['b_gate'],
              p['gla_norm'], p['ret_norm'], cos_t, sin_t, dstack, inner, kdec, cdec,
              consts['ones_h'], consts['ones_g'], consts['tril'], consts['blk'], p['final_norm']]
    in_specs = [pl.BlockSpec((nb, T_CHUNK, D_MODEL), lambda i: (0, i, 0))]
    for a in inputs[1:16]:
        in_specs.append(_full(a.shape))
    in_specs.append(pl.BlockSpec((T_CHUNK, LANES), lambda i: (i, 0)))
    in_specs.append(pl.BlockSpec((T_CHUNK, LANES), lambda i: (i, 0)))
    for a in inputs[18:]:
        in_specs.append(_full(a.shape))
    out_shape = (jax.ShapeDtypeStruct((nb, seq, D_MODEL), F32),
                 jax.ShapeDtypeStruct((nb, 2 * S5_N), F32),
                 jax.ShapeDtypeStruct((nb, BRANCH, HEADS * HG_DK), F32),
                 jax.ShapeDtypeStruct((nb, BRANCH, HEADS * GLA_DK), F32),
                 jax.ShapeDtypeStruct((nb, BRANCH, HEADS * RET_DK), F32))
    out_specs = (pl.BlockSpec((nb, T_CHUNK, D_MODEL), lambda i: (0, i, 0)),
                 pl.BlockSpec((nb, 2 * S5_N), lambda i: (0, 0)),
                 pl.BlockSpec((nb, BRANCH, HEADS * HG_DK), lambda i: (0, 0, 0)),
                 pl.BlockSpec((nb, BRANCH, HEADS * GLA_DK), lambda i: (0, 0, 0)),
                 pl.BlockSpec((nb, BRANCH, HEADS * RET_DK), lambda i: (0, 0, 0)))
    scratch = [pltpu.VMEM((rows, N_SCR), F32),
               pltpu.VMEM((2 * S5_N // LANES, nb * PITCH, LANES), F32),
               pltpu.VMEM((rows, D_MODEL), F32),
               pltpu.VMEM((TERM_ROWS, HEADS * HG_DK), F32), pltpu.VMEM((TERM_ROWS, HEADS * GLA_DK), F32),
               pltpu.VMEM((TERM_ROWS, HEADS * HG_DK), F32), pltpu.VMEM((TERM_ROWS, HEADS * GLA_DK), F32)]
    y, s5, hgs, glas, rets = pl.pallas_call(
        functools.partial(_prompt_kernel, apply_final),
        grid=(n_steps,), in_specs=in_specs, out_specs=out_specs, out_shape=out_shape,
        scratch_shapes=scratch, name='prompt_layer',
        compiler_params=pltpu.CompilerParams(dimension_semantics=('arbitrary',),
                                             vmem_limit_bytes=VMEM_LIMIT),
    )(*inputs)
    s5 = s5.reshape(nb, 2, S5_GROUPS, S5_STATE)

    def unstack(st, dk):
        st = st.reshape(nb, HEADS, DV, HEADS, dk)
        diag = jnp.stack([st[:, hh, :, hh, :] for hh in range(HEADS)], axis=1)
        return jnp.transpose(diag, (0, 1, 3, 2))

    return y, (s5[:, 0], s5[:, 1], unstack(hgs, HG_DK), unstack(glas, GLA_DK), unstack(rets, RET_DK))


def _sample_layer(x, states, p, consts, tabs, rope, apply_final):
    nb = x.shape[0]
    n_steps = nb // SAMPLE_ROWS
    s5re, s5im, hg, gla, ret = states
    dret = tabs[4]
    cos_t, sin_t = rope
    consts_in = [p['norm_w'], p['w_in'], p['w_out'], p['bbd'], p['cbd'], p['a_re'], p['a_im'], p['d_skip'],
                 p['w_glu'], p['lbp'], p['hg_norm'], p['w_up'], p['b_gate'], p['gla_norm'], p['ret_norm'],
                 cos_t, sin_t, dret, consts['expand'], consts['vsel_h'], consts['vsel_g'], consts['ones_h'],
                 p['final_norm']]
    rowed = [s5re.reshape(nb, S5_N), s5im.reshape(nb, S5_N), hg.reshape(nb, -1), gla.reshape(nb, -1),
             ret.reshape(nb, -1)]
    row_spec = lambda a: pl.BlockSpec((SAMPLE_ROWS, a.shape[1]), lambda i: (i, 0))
    in_specs = [row_spec(x)] + [_full(a.shape) for a in consts_in] + [row_spec(a) for a in rowed]
    out_arrays = [x] + rowed
    out_shape = tuple(jax.ShapeDtypeStruct(a.shape, F32) for a in out_arrays)
    out_specs = tuple(row_spec(a) for a in out_arrays)
    outs = pl.pallas_call(
        functools.partial(_sample_kernel, apply_final),
        grid=(n_steps,), in_specs=in_specs, out_specs=out_specs, out_shape=out_shape,
        name='sample_layer',
        compiler_params=pltpu.CompilerParams(dimension_semantics=('arbitrary',),
                                             vmem_limit_bytes=VMEM_LIMIT),
    )(x, *consts_in, *rowed)
    y = outs[0]
    new = (outs[1].reshape(s5re.shape), outs[2].reshape(s5im.shape), outs[3].reshape(hg.shape),
           outs[4].reshape(gla.shape), outs[5].reshape(ret.shape))
    return y, new


def kernel(x_prompt, x_sample, state_s5_re, state_s5_im, state_hgrn, state_gla, state_ret, norm_w, final_norm_w, w_in, w_out, s5_lam_re, s5_lam_im, s5_log_step, s5_b_re, s5_b_im, s5_c_re, s5_c_im, s5_d, s5_w_glu, hgrn_lb_logits, hgrn_norm_w, gla_w_gate_up, gla_b_gate, gla_norm_w, ret_norm_w):
    depth = w_in.shape[0]
    seq = x_prompt.shape[1]
    consts = _constants()
    tabs = _ret_tables(seq)
    rope_p = _rope_tables(jnp.arange(seq))
    rope_s = _rope_tables(PAST_LEN + jnp.arange(1))
    lb_all = jnp.cumsum(jax.nn.softmax(hgrn_lb_logits.astype(F32), axis=0), axis=0)
    lb_all = lb_all - lb_all[0:1]

    xp = x_prompt
    xs = x_sample.reshape(x_sample.shape[0], D_MODEL)
    new_p = ([], [], [], [], [])
    new_s = ([], [], [], [], [])
    for l in range(depth):
        bbd, cbd, a_re, a_im = _s5_discretize(s5_lam_re[l], s5_lam_im[l], s5_log_step[l], s5_b_re[l],
                                              s5_b_im[l], s5_c_re[l], s5_c_im[l])
        lb = lb_all[l][None, :]
        w_up = jnp.zeros((LANES, HEADS * GLA_DK), F32).at[:GLA_LOWRANK].set(gla_w_gate_up[l].astype(F32))
        p = dict(norm_w=norm_w[l][None, :].astype(F32), w_in=_pack_w_in(w_in[l]), w_out=w_out[l].astype(BF16),
                 bbd=bbd, cbd=cbd, a_re=a_re, a_im=a_im, d_skip=s5_d[l][None, :].astype(F32),
                 w_glu=s5_w_glu[l].astype(BF16),
                 lbp=jnp.concatenate([jnp.log(lb), jnp.log1p(-lb), 1.0 - lb, lb, jnp.zeros((4, BRANCH), F32)],
                                     axis=0),
                 hg_norm=hgrn_norm_w[l][None, :].astype(F32), w_up=w_up,
                 b_gate=gla_b_gate[l][None, :].astype(F32), gla_norm=gla_norm_w[l][None, :].astype(F32),
                 ret_norm=ret_norm_w[l][None, :].astype(F32), final_norm=final_norm_w[None, :].astype(F32))
        last = l == depth - 1
        xp, st_p = _prompt_layer(xp, p, consts, tabs, rope_p, last)
        xs, st_s = _sample_layer(xs, (state_s5_re[l], state_s5_im[l], state_hgrn[l], state_gla[l],
                                      state_ret[l]), p, consts, tabs, rope_s, last)
        for i in range(5):
            new_p[i].append(st_p[i])
            new_s[i].append(st_s[i])
    y_sample = xs.reshape(x_sample.shape)
    return (xp, y_sample,
            jnp.stack(new_p[0]), jnp.stack(new_p[1]), jnp.stack(new_p[2]), jnp.stack(new_p[3]),
            jnp.stack(new_p[4]),
            jnp.stack(new_s[0]), jnp.stack(new_s[1]), jnp.stack(new_s[2]), jnp.stack(new_s[3]),
            jnp.stack(new_s[4]))
```

```python
import functools
import math

import numpy as np
import jax
import jax.numpy as jnp
from jax import lax
from jax.experimental import pallas as pl
from jax.experimental.pallas import tpu as pltpu

F32 = jnp.float32
BF16 = jnp.bfloat16

D_MODEL = 1024
BRANCH = 256
S5_CH = 16
S5_GROUPS = 16
S5_STATE = 64
S5_N = S5_GROUPS * S5_STATE
HEADS = 4
HG_DK = 64
GLA_DK = 32
RET_DK = 32
DV = 64
GLA_LOWRANK = 16
GLA_TAU = 16.0
ROPE_BASE = 10000.0
PAST_LEN = 16384
EPS = 1e-6
SUB = 16

LANES = 128
T_CHUNK = 64
PITCH = T_CHUNK + 8
TERM_ROWS = T_CHUNK * (SUB + SUB // 2) // 2
SAMPLE_ROWS = 32
SAMPLE_COLS = 2048
LOG2E = math.log2(math.e)
MASK_NEG = -1e30
VMEM_LIMIT = 60 * 1024 * 1024

C_U, C_SZ, C_HQ, C_HF, C_HI, C_HZ = 0, 256, 512, 768, 1024, 1280
C_GQ, C_GK, C_GV, C_GZ = 1536, 1664, 1792, 2048
C_RQ, C_RK, C_RV, C_RZ = 2304, 2432, 2560, 2816
C_LR = 3072
N_PACK = 3200
N_QUARTER = 4
QUARTER = C_LR // N_QUARTER


def _dot(a, b):
    return jnp.dot(a.astype(BF16), b.astype(BF16), preferred_element_type=F32)


def _dot_nt(a, b):
    return lax.dot_general(a.astype(BF16), b.astype(BF16), (((1,), (1,)), ((), ())),
                           preferred_element_type=F32)


def _dot_tn(a, b):
    return lax.dot_general(a.astype(BF16), b.astype(BF16), (((0,), (0,)), ((), ())),
                           preferred_element_type=F32)


def _split3(x):
    x1 = x.astype(BF16)
    r1 = x - x1.astype(F32)
    x2 = r1.astype(BF16)
    x3 = (r1 - x2.astype(F32)).astype(BF16)
    return x1, x2, x3


def _dot_sel_lhs(sel, x):
    x1, x2, x3 = _split3(x)
    d = lambda p: jnp.dot(sel, p, preferred_element_type=F32)
    return d(x1) + d(x2) + d(x3)


def _dot_sel_rhs(x, sel):
    x1, x2, x3 = _split3(x)
    d = lambda p: jnp.dot(p, sel, preferred_element_type=F32)
    return d(x1) + d(x2) + d(x3)


def _dot_sel_rhs2(x, sel):
    x1 = x.astype(BF16)
    x2 = (x - x1.astype(F32)).astype(BF16)
    return (jnp.dot(x1, sel, preferred_element_type=F32)
            + jnp.dot(x2, sel, preferred_element_type=F32))


def _dot_sel_nt2(x, sel):
    x1 = x.astype(BF16)
    x2 = (x - x1.astype(F32)).astype(BF16)
    dn = (((1,), (1,)), ((), ()))
    return (lax.dot_general(x1, sel, dn, preferred_element_type=F32)
            + lax.dot_general(x2, sel, dn, preferred_element_type=F32))


def _dot3(a, b):
    a1 = a.astype(BF16)
    a2 = (a - a1.astype(F32)).astype(BF16)
    b1 = b.astype(BF16)
    b2 = (b - b1.astype(F32)).astype(BF16)
    d = lambda p, q: jnp.dot(p, q, preferred_element_type=F32)
    return d(a1, b1) + d(a1, b2) + d(a2, b1)


def _sigmoid(x):
    return 1.0 / (1.0 + jnp.exp(-x))


def _silu(x):
    return x * _sigmoid(x)


def _log_sigmoid(x):
    return jnp.minimum(x, 0.0) - jnp.log(1.0 + jnp.exp(-jnp.abs(x)))


def _gelu_tanh(x):
    return 0.5 * x * (1.0 + jnp.tanh(math.sqrt(2.0 / math.pi) * (x + 0.044715 * (x * x * x))))


def _rmsnorm_rows(x, w):
    return x * lax.rsqrt(jnp.mean(x * x, axis=-1, keepdims=True) + EPS) * w


def _head_rms(o, ones_h, gain):
    ms = _dot_sel_rhs2(o * o, ones_h) * (1.0 / DV)
    return o * lax.rsqrt(ms + EPS) * gain


def _head_ln(o, ones_h, gain):
    c = o - _dot_sel_rhs2(o, ones_h) * (1.0 / DV)
    var = _dot_sel_rhs2(c * c, ones_h) * (1.0 / DV)
    return c * lax.rsqrt(var + EPS) * gain


def _hgrn_gates(xf, lbp):
    loglb, log1mlb, one_m_lb = lbp[0:1, :], lbp[1:2, :], lbp[2:3, :]
    bterm = log1mlb + _log_sigmoid(xf)
    m = jnp.maximum(loglb, bterm)
    log_f = m + jnp.log(jnp.exp(loglb - m) + jnp.exp(bterm - m))
    return log_f, one_m_lb * _sigmoid(-xf)


def _rotary(t, cos, sin_signed, first_half):
    half = RET_DK // 2
    swapped = jnp.where(first_half, pltpu.roll(t, LANES - half, 1), pltpu.roll(t, half, 1))
    return t * cos + swapped * sin_signed


def _s5_output(y_lin, u, sz, dskip, wglu):
    y = _gelu_tanh(y_lin + u * dskip)
    y = y * _sigmoid(_dot(y, wglu))
    return y * _silu(sz)


def _mix_and_project(x, mix, wout, fnw, apply_final):
    out = x + _dot(mix, wout)
    if apply_final:
        out = _rmsnorm_rows(out, fnw)
    return out


class _GatedStream:
    def __init__(self, q, k, v, g, s_ref, ones_kv, mask_vk, tril, blk, term_scr, kdim):
        self.q, self.k, self.v, self.g = q, k, v, g
        self.s_ref, self.ones_kv, self.mask_vk = s_ref, ones_kv, mask_vk
        self.tril, self.blk, self.term_scr, self.kdim = tril, blk, term_scr, kdim

    def cumulate(self):
        g = self.g()
        self.bc = _dot_sel_lhs(self.tril, g) * LOG2E
        self.bl = _dot_sel_lhs(self.blk, g) * LOG2E

    def state_updates(self):
        n_sub = T_CHUNK // SUB
        v = self.v()
        kh = self.k() * jnp.exp2(self.bl - self.bc)
        self.ws = [_dot_tn(v[SUB * j:SUB * (j + 1)], kh[SUB * j:SUB * (j + 1)]) for j in range(n_sub)]

    def decay_products(self):
        n_sub, hs, term_scr = T_CHUNK // SUB, SUB // 2, self.term_scr
        q, k, bc = self.q(), self.k(), self.bc
        trow = lax.broadcasted_iota(jnp.int32, (hs, self.kdim), 0)
        neg = [jnp.where(trow >= d, 0.0, MASK_NEG) for d in range(1, hs)]
        row = 0
        for j in range(n_sub):
            r0 = SUB * j
            q_lo, q_hi = q[r0:r0 + hs], q[r0 + hs:r0 + SUB]
            b_lo, b_hi = bc[r0:r0 + hs], bc[r0 + hs:r0 + SUB]
            for s in range(SUB):
                ks, bs = k[r0 + s:r0 + s + 1, :], bc[r0 + s:r0 + s + 1, :]
                d = s % hs
                q_dg, b_dg = (q_lo, b_lo) if s < hs else (q_hi, b_hi)
                e_dg = b_dg - bs if d == 0 else b_dg - bs + neg[d - 1]
                term_scr[row:row + hs, :] = q_dg * ks * jnp.exp2(e_dg)
                row += hs
                if s < hs:
                    term_scr[row:row + hs, :] = q_hi * ks * jnp.exp2(b_hi - bs)
                    row += hs

    def matmuls(self):
        n_sub = T_CHUNK // SUB
        qt = self.q() * jnp.exp2(self.bc)
        states = [self.s_ref[...]]
        for j in range(n_sub):
            states.append(states[j] * jnp.exp2(self.bl[SUB * j:SUB * j + 1, :]) + self.ws[j] * self.mask_vk)
        self.s_ref[...] = states[n_sub]
        self.inter = [_dot_nt(qt[SUB * j:SUB * (j + 1)], states[j]) for j in range(n_sub)]
        self.att = _dot(self.term_scr[...], self.ones_kv)

    def output(self):
        n_sub, hs = T_CHUNK // SUB, SUB // 2
        v, att = self.v(), self.att
        outs = []
        row = 0
        for j in range(n_sub):
            r0 = SUB * j
            acc_lo, acc_hi = self.inter[j][0:hs], self.inter[j][hs:SUB]
            for s in range(SUB):
                vs = v[r0 + s:r0 + s + 1, :]
                if s < hs:
                    acc_lo = acc_lo + att[row:row + hs] * vs
                    row += hs
                acc_hi = acc_hi + att[row:row + hs] * vs
                row += hs
            outs += [acc_lo, acc_hi]
        return jnp.concatenate(outs, axis=0)


def _ret_chunk(q, k, v, cos, sin_signed, s_ref, dstack, inner, kdec, cdec, mask_vk):
    lane = lax.broadcasted_iota(jnp.int32, (T_CHUNK, HEADS * RET_DK), 1)
    first_half = (lane % RET_DK) < (RET_DK // 2)
    rq = _rotary(q, cos, sin_signed, first_half)
    rk = _rotary(k, cos, sin_signed, first_half) * (RET_DK ** -0.5)
    lane_v = lax.broadcasted_iota(jnp.int32, (T_CHUNK, HEADS * DV), 1)
    kst = jnp.concatenate([jnp.where(lane // RET_DK == h, rk, 0.0) for h in range(HEADS)], axis=0)
    vst = jnp.concatenate([jnp.where(lane_v // DV == h, v, 0.0) for h in range(HEADS)], axis=0)
    p = _dot_nt(rq, kst) * dstack
    s_t = s_ref[...]
    o = _dot(p, vst) + _dot_nt(rq, s_t) * inner
    s_ref[...] = s_t * cdec + _dot_tn(v, rk * kdec) * mask_vk
    return o


def _prompt_kernel(apply_final,
                   x_ref, xn_ref, normw_ref, win_ref, wlr_ref, wout_ref, bbd_ref, cbd_ref, are_ref, aim_ref,
                   dskip_ref, wglu_ref, lbp_ref, hgn_ref, wup_ref, bgate_ref, glan_ref, retn_ref, cos_ref,
                   sin_ref, dstack_ref, inner_ref, kdec_ref, cdec_ref, onesh_ref, onesg_ref, tril_ref,
                   blk_ref, fnw_ref,
                   y_ref, s5_ref, hgs_ref, glas_ref, rets_ref,
                   proj_scr, nxt_scr, lr_scr, lrn_scr, hk_scr, hn_scr, bu_scr, mix_scr,
                   term_h0, term_g0, term_h1, term_g1):
    nb = x_ref.shape[0]
    rows = nb * T_CHUNK
    n_slab = 2 * S5_N // LANES
    half = n_slab // 2
    step = pl.program_id(0)

    def proj(col, width, rs=slice(None)):
        return proj_scr[col // QUARTER, rs, col % QUARTER:col % QUARTER + width]

    @pl.when(step == 0)
    def _init():
        s5_ref[...] = jnp.zeros_like(s5_ref)
        hgs_ref[...] = jnp.zeros_like(hgs_ref)
        glas_ref[...] = jnp.zeros_like(glas_ref)
        rets_ref[...] = jnp.zeros_like(rets_ref)
        bu_scr[...] = jnp.zeros_like(bu_scr)
        h0 = _rmsnorm_rows(x_ref[...].reshape(rows, D_MODEL), normw_ref[...]).astype(BF16)
        for qtr in range(N_QUARTER):
            proj_scr[qtr] = jnp.dot(h0, win_ref[qtr], preferred_element_type=F32)
        lr_scr[...] = jnp.dot(h0, wlr_ref[...], preferred_element_type=F32)

    @pl.when(step > 0)
    def _advance():
        proj_scr[...] = nxt_scr[...]
        lr_scr[...] = lrn_scr[...]

    hn_scr[...] = _rmsnorm_rows(xn_ref[...].reshape(rows, D_MODEL), normw_ref[...]).astype(BF16)
    lrn_scr[...] = jnp.dot(hn_scr[...], wlr_ref[...], preferred_element_type=F32)

    u = proj(C_U, BRANCH)
    for c2 in range(n_slab // 2):
        bu = _dot(u, bbd_ref[:, 2 * c2 * LANES:2 * (c2 + 1) * LANES])
        for cc in range(2):
            for b in range(nb):
                bu_scr[2 * c2 + cc, b * PITCH:b * PITCH + T_CHUNK, :] = bu[b * T_CHUNK:(b + 1) * T_CHUNK,
                                                                           cc * LANES:(cc + 1) * LANES]
    a_re = [jnp.broadcast_to(are_ref[:, c * LANES:(c + 1) * LANES], (nb, LANES)) for c in range(half)]
    a_im = [jnp.broadcast_to(aim_ref[:, c * LANES:(c + 1) * LANES], (nb, LANES)) for c in range(half)]
    s_init = s5_ref[...]
    carry0 = tuple(s_init[:, c * LANES:(c + 1) * LANES] for c in range(n_slab))

    def scan_step(t, carry):
        new = [None] * n_slab
        for c in range(half):
            sr, si = carry[c], carry[half + c]
            br = bu_scr[c, pl.ds(t, nb, stride=PITCH), :]
            bi = bu_scr[half + c, pl.ds(t, nb, stride=PITCH), :]
            nr = a_re[c] * sr - a_im[c] * si + br
            ni = a_re[c] * si + a_im[c] * sr + bi
            bu_scr[c, pl.ds(t, nb, stride=PITCH), :] = nr
            bu_scr[half + c, pl.ds(t, nb, stride=PITCH), :] = ni
            new[c], new[half + c] = nr, ni
        return tuple(new)

    carry = lax.fori_loop(0, T_CHUNK, scan_step, carry0)
    s5_ref[...] = jnp.concatenate(carry, axis=1)
    s_all = jnp.concatenate([bu_scr[c] for c in range(n_slab)], axis=1)
    y_all = _dot(s_all, cbd_ref[...])
    y_lin = jnp.concatenate([y_all[b * PITCH:b * PITCH + T_CHUNK] for b in range(nb)], axis=0)
    mix_scr[:, 0:BRANCH] = _s5_output(y_lin, u, proj(C_SZ, BRANCH), dskip_ref[...], wglu_ref[...])

    log_f, hk = _hgrn_gates(proj(C_HF, BRANCH), lbp_ref[...])
    proj_scr[C_HF // QUARTER, :, C_HF % QUARTER:C_HF % QUARTER + BRANCH] = log_f
    hk_scr[...] = hk
    g_gla = _log_sigmoid(_dot3(lr_scr[...], wup_ref[...]) + bgate_ref[...])
    lr_scr[...] = g_gla * (1.0 / GLA_TAU)

    ones_h = onesh_ref[...]
    ones_g = onesg_ref[...]
    mask_h = ones_h.astype(F32)
    mask_g = jnp.transpose(ones_g.astype(F32))
    tril = tril_ref[...]
    blk = blk_ref[...]

    def rows_of(bp, odd):
        return pl.ds(pl.multiple_of((2 * bp + odd) * T_CHUNK, T_CHUNK), T_CHUNK)

    def hgrn_stream(bp, odd, term):
        rs = rows_of(bp, odd)
        return _GatedStream(lambda: proj(C_HQ, BRANCH, rs), lambda: hk_scr[rs, :], lambda: proj(C_HI, BRANCH, rs),
                            lambda: proj(C_HF, BRANCH, rs), hgs_ref.at[bp, odd], ones_h, mask_h, tril, blk,
                            term, HEADS * HG_DK)

    def gla_stream(bp, odd, term):
        rs = rows_of(bp, odd)
        return _GatedStream(lambda: proj(C_GQ, LANES, rs) * (GLA_DK ** -0.5), lambda: proj(C_GK, LANES, rs),
                            lambda: proj(C_GV, BRANCH, rs), lambda: lr_scr[rs, :], glas_ref.at[bp, odd],
                            ones_g, mask_g, tril, blk, term, HEADS * GLA_DK)

    def per_pair(bp, _):
        streams = [hgrn_stream(bp, 0, term_h0), gla_stream(bp, 0, term_g0),
                   hgrn_stream(bp, 1, term_h1), gla_stream(bp, 1, term_g1)]
        slots = [(0, BRANCH), (0, 2 * BRANCH), (1, BRANCH), (1, 2 * BRANCH)]
        for st in streams:
            st.cumulate()
        for st in streams:
            st.state_updates()
        nxt_scr[bp] = jnp.dot(hn_scr[...], win_ref[bp], preferred_element_type=F32)
        for st in streams:
            st.decay_products()
        for st in streams:
            st.matmuls()
        for odd in range(2):
            rs = rows_of(bp, odd)
            mix_scr[rs, 3 * BRANCH:4 * BRANCH] = _ret_chunk(
                proj(C_RQ, LANES, rs), proj(C_RK, LANES, rs), proj(C_RV, BRANCH, rs), cos_ref[...],
                sin_ref[...], rets_ref.at[bp, odd], dstack_ref[...], inner_ref[...], kdec_ref[...],
                cdec_ref[...], mask_g)
        for st, (odd, col) in zip(streams, slots):
            mix_scr[rows_of(bp, odd), col:col + BRANCH] = st.output()
        return 0

    lax.fori_loop(0, N_QUARTER, per_pair, 0)

    o_hg = _head_rms(mix_scr[:, BRANCH:2 * BRANCH], ones_h, hgn_ref[...])
    mix_scr[:, BRANCH:2 * BRANCH] = o_hg * _silu(proj(C_HZ, BRANCH))
    o_gla = _head_rms(mix_scr[:, 2 * BRANCH:3 * BRANCH], ones_h, glan_ref[...])
    mix_scr[:, 2 * BRANCH:3 * BRANCH] = o_gla * _silu(proj(C_GZ, BRANCH))
    o_ret = _head_ln(mix_scr[:, 3 * BRANCH:4 * BRANCH], ones_h, retn_ref[...])
    mix_scr[:, 3 * BRANCH:4 * BRANCH] = o_ret * _silu(proj(C_RZ, BRANCH))
    out = _mix_and_project(x_ref[...].reshape(rows, D_MODEL), mix_scr[...], wout_ref[...], fnw_ref[...],
                           apply_final)
    y_ref[...] = out.reshape(nb, T_CHUNK, D_MODEL)


def _split2_rows(x):
    x1 = x.astype(BF16)
    return [x1, (x - x1.astype(F32)).astype(BF16)]


def _sample_state_step(q, k, v, decay, dec_row, s0_ref, s_out_ref, expand_ref, vsel_ref, vin_of, n_chunk):
    rows = q.shape[0]
    width = LANES * DV
    outs = []
    for c in range(n_chunk):
        fs = slice(c * LANES, (c + 1) * LANES)
        parts = _split2_rows(k[:, fs]) + _split2_rows(q[:, fs])
        if decay is not None:
            parts += list(_split3(decay[:, fs]))
        lhs = jnp.concatenate(parts, axis=0)
        vlhs = jnp.concatenate(_split2_rows(vin_of(c, v)), axis=0)
        acc = None
        for w in range(width // SAMPLE_COLS):
            lo = w * SAMPLE_COLS
            cs = slice(c * width + lo, c * width + lo + SAMPLE_COLS)
            ex = jnp.dot(lhs, expand_ref[:, lo:lo + SAMPLE_COLS], preferred_element_type=F32)
            k_exp = ex[0:rows] + ex[rows:2 * rows]
            q_exp = ex[2 * rows:3 * rows] + ex[3 * rows:4 * rows]
            if decay is not None:
                d_exp = ex[4 * rows:5 * rows] + ex[5 * rows:6 * rows] + ex[6 * rows:7 * rows]
            else:
                d_exp = dec_row[:, lo:lo + SAMPLE_COLS]
            vsel = vsel_ref[:, lo:lo + SAMPLE_COLS]
            vt = jnp.dot(vlhs, vsel, preferred_element_type=F32)
            s_new = s0_ref[:, cs] * d_exp + k_exp * (vt[0:rows] + vt[rows:2 * rows])
            s_out_ref[:, cs] = s_new
            prod = jnp.concatenate(_split2_rows(q_exp * s_new), axis=0)
            red = lax.dot_general(prod, vsel, (((1,), (1,)), ((), ())), preferred_element_type=F32)
            acc = red if acc is None else acc + red
        outs.append(acc[0:rows] + acc[rows:2 * rows])
    return outs


def _sample_kernel(apply_final,
                   x_ref, normw_ref, win_ref, wout_ref, bbd_ref, cbd_ref, are_ref, aim_ref, dskip_ref,
                   wglu_ref, lbp_ref, hgn_ref, wup_ref, bgate_ref, glan_ref, retn_ref, cos_ref, sin_ref,
                   dret_ref, exp_ref, vselh_ref, vselg_ref, onesh_ref, fnw_ref,
                   s5re_ref, s5im_ref, hg_ref, gla_ref, ret_ref,
                   y_ref, s5re_o, s5im_o, hg_o, gla_o, ret_o):
    x = x_ref[...]
    rows = x.shape[0]
    h = _rmsnorm_rows(x, normw_ref[...])
    proj = _dot(h, win_ref[...])
    ones_h = onesh_ref[...]

    u = proj[:, C_U:C_U + BRANCH]
    bu = _dot3(u, bbd_ref[...])
    a_re, a_im = are_ref[...], aim_ref[...]
    s0r, s0i = s5re_ref[...], s5im_ref[...]
    s_re = a_re * s0r - a_im * s0i + bu[:, 0:S5_N]
    s_im = a_re * s0i + a_im * s0r + bu[:, S5_N:2 * S5_N]
    s5re_o[...] = s_re
    s5im_o[...] = s_im
    y_lin = _dot3(jnp.concatenate([s_re, s_im], axis=1), cbd_ref[...])
    mix0 = _s5_output(y_lin, u, proj[:, C_SZ:C_SZ + BRANCH], dskip_ref[...], wglu_ref[...])

    log_f, hk = _hgrn_gates(proj[:, C_HF:C_HF + BRANCH], lbp_ref[...])
    d_hg = jnp.exp(log_f)
    o_parts = _sample_state_step(
        proj[:, C_HQ:C_HQ + BRANCH], hk, proj[:, C_HI:C_HI + BRANCH], d_hg, None,
        hg_ref, hg_o, exp_ref, vselh_ref, lambda c, v: v[:, c * LANES:(c + 1) * LANES], BRANCH // LANES)
    o_hg = jnp.concatenate(o_parts, axis=1)
    mix1 = _head_rms(o_hg, ones_h, hgn_ref[...]) * _silu(proj[:, C_HZ:C_HZ + BRANCH])

    g_gla = _log_sigmoid(_dot3(proj[:, C_LR:C_LR + LANES], wup_ref[...]) + bgate_ref[...]) * (1.0 / GLA_TAU)
    d_gla = jnp.exp(g_gla)
    o_gla = _sample_state_step(
        proj[:, C_GQ:C_GQ + LANES] * (GLA_DK ** -0.5), proj[:, C_GK:C_GK + LANES],
        proj[:, C_GV:C_GV + BRANCH], d_gla, None,
        gla_ref, gla_o, exp_ref, vselg_ref, lambda c, v: v, 1)[0]
    mix2 = _head_rms(o_gla, ones_h, glan_ref[...]) * _silu(proj[:, C_GZ:C_GZ + BRANCH])

    lane = lax.broadcasted_iota(jnp.int32, (rows, HEADS * RET_DK), 1)
    first_half = (lane % RET_DK) < (RET_DK // 2)
    rq = _rotary(proj[:, C_RQ:C_RQ + LANES], cos_ref[...], sin_ref[...], first_half)
    rk = _rotary(proj[:, C_RK:C_RK + LANES], cos_ref[...], sin_ref[...], first_half) * (RET_DK ** -0.5)
    o_ret = _sample_state_step(rq, rk, proj[:, C_RV:C_RV + BRANCH], None, dret_ref,
                               ret_ref, ret_o, exp_ref, vselg_ref, lambda c, v: v, 1)[0]
    mix3 = _head_ln(o_ret, ones_h, retn_ref[...]) * _silu(proj[:, C_RZ:C_RZ + BRANCH])

    mix = jnp.concatenate([mix0, mix1, mix2, mix3], axis=1)
    y_ref[...] = _mix_and_project(x, mix, wout_ref[...], fnw_ref[...], apply_final)


def _ret_log_gamma():
    return jnp.log1p(-jnp.exp2(-5.0 - jnp.arange(HEADS, dtype=F32)))


def _block_diag(blocks):
    g, r, c = blocks.shape
    eye = jnp.eye(g, dtype=blocks.dtype)
    return jnp.einsum('grc,gh->grhc', blocks, eye).reshape(g * r, g * c)


def _constants():
    ones_h = (np.arange(BRANCH)[:, None] // DV == np.arange(BRANCH)[None, :] // DV)
    ones_g = (np.arange(HEADS * GLA_DK)[:, None] // GLA_DK == np.arange(BRANCH)[None, :] // DV)
    r = np.arange(T_CHUNK)
    same_sub = r[:, None] // SUB == r[None, :] // SUB
    tril = same_sub & (r[None, :] <= r[:, None])
    col = np.arange(LANES * DV)
    expand = np.arange(LANES)[:, None] == col[None, :] // DV
    vsel_h = ((np.arange(LANES)[:, None] // DV == col[None, :] // (DV * HG_DK))
              & (np.arange(LANES)[:, None] % DV == col[None, :] % DV))
    vsel_g = ((np.arange(BRANCH)[:, None] // DV == col[None, :] // (DV * GLA_DK))
              & (np.arange(BRANCH)[:, None] % DV == col[None, :] % DV))
    as_bf16 = lambda m: jnp.asarray(m.astype(np.float32), dtype=BF16)
    return dict(ones_h=as_bf16(ones_h), ones_g=as_bf16(ones_g), tril=as_bf16(tril), blk=as_bf16(same_sub),
                expand=as_bf16(expand), vsel_h=as_bf16(vsel_h), vsel_g=as_bf16(vsel_g))


def _ret_tables(seq_len):
    lg = _ret_log_gamma()
    idx = jnp.arange(T_CHUNK, dtype=F32)
    rel = idx[:, None] - idx[None, :]
    causal = rel >= 0
    decay = jnp.where(causal[None], jnp.exp(jnp.where(causal, rel, 0.0)[None] * lg[:, None, None]), 0.0)
    dstack = jnp.transpose(decay, (1, 0, 2)).reshape(T_CHUNK, HEADS * T_CHUNK)
    inner = jnp.repeat(jnp.exp((idx[:, None] + 1.0) * lg[None, :]), DV, axis=1)
    kdec = jnp.repeat(jnp.exp((T_CHUNK - 1.0 - idx[:, None]) * lg[None, :]), RET_DK, axis=1)
    cdec = jnp.repeat(jnp.exp(T_CHUNK * lg)[None, :], RET_DK, axis=1)
    dret = jnp.repeat(jnp.exp(lg), RET_DK * DV)[None, :]
    return dstack, inner, kdec, cdec, dret


def _rope_tables(pos):
    half = RET_DK // 2
    inv = ROPE_BASE ** (-jnp.arange(half, dtype=F32) / half)
    ang = pos.astype(F32)[:, None] * inv[None, :]
    cos, sin = jnp.cos(ang), jnp.sin(ang)
    cos_t = jnp.tile(jnp.concatenate([cos, cos], axis=1), (1, HEADS))
    sin_t = jnp.tile(jnp.concatenate([-sin, sin], axis=1), (1, HEADS))
    return cos_t, sin_t


def _pack_w_in(w):
    offs = np.cumsum([0, 256, 256, 256, 256, 256, 256, 128, 128, 256, 16, 256, 128, 128, 256, 256])
    seg = lambda i: w[:, int(offs[i]):int(offs[i + 1])]
    order = [0, 1, 2, 3, 4, 5, 6, 7, 8, 10, 11, 12, 13, 14]
    pad = jnp.zeros((w.shape[0], LANES - GLA_LOWRANK), w.dtype)
    return jnp.concatenate([seg(i) for i in order] + [seg(9), pad], axis=1).astype(BF16)


def _s5_discretize(lam_re, lam_im, log_step, b_re, b_im, c_re, c_im):
    lr, li = lam_re.astype(F32), lam_im.astype(F32)
    step = jnp.exp(log_step.astype(F32))[:, None]
    mag = jnp.exp(lr * step)
    ab_re = mag * jnp.cos(li * step)
    ab_im = mag * jnp.sin(li * step)
    den = lr * lr + li * li
    nr = ab_re - 1.0
    f_re = (nr * lr + ab_im * li) / den
    f_im = (ab_im * lr - nr * li) / den
    br, bi = b_re.astype(F32), b_im.astype(F32)
    bb_re = f_re[..., None] * br - f_im[..., None] * bi
    bb_im = f_re[..., None] * bi + f_im[..., None] * br
    bbd = jnp.concatenate([_block_diag(jnp.transpose(bb_re, (0, 2, 1))),
                           _block_diag(jnp.transpose(bb_im, (0, 2, 1)))], axis=1)
    cbd = jnp.concatenate([_block_diag(jnp.transpose(c_re.astype(F32), (0, 2, 1))),
                           -_block_diag(jnp.transpose(c_im.astype(F32), (0, 2, 1)))], axis=0)
    return bbd, cbd, ab_re.reshape(1, S5_N), ab_im.reshape(1, S5_N)


def _full(shape):
    return pl.BlockSpec(shape, lambda i: (0,) * len(shape), pipeline_mode=pl.Buffered(1))


def _prompt_layer(x, p, consts, tabs, rope, apply_final):
    nb, seq, _ = x.shape
    n_steps = seq // T_CHUNK
    rows = nb * T_CHUNK
    dstack, inner, kdec, cdec, _ = tabs
    cos_t, sin_t = rope
    w_main = jnp.transpose(p['w_in'][:, :C_LR].reshape(D_MODEL, N_QUARTER, QUARTER), (1, 0, 2))
    inputs = [x, x, p['norm_w'], w_main, p['w_in'][:, C_LR:], p['w_out'], p['bbd'].astype(BF16),
              p['cbd'].astype(BF16),
              p['a_re'], p['a_im'], p['d_skip'], p['w_glu'], p['lbp'], p['hg_norm'], p['w_up'], p['b_gate'],
              p['gla_norm'], p['ret_norm'], cos_t, sin_t, dstack, inner, kdec, cdec,
              consts['ones_h'], consts['ones_g'], consts['tril'], consts['blk'], p['final_norm']]
    in_specs = [pl.BlockSpec((nb, T_CHUNK, D_MODEL), lambda i: (0, i, 0)),
                pl.BlockSpec((nb, T_CHUNK, D_MODEL), lambda i: (0, jnp.minimum(i + 1, n_steps - 1), 0))]
    for a in inputs[2:18]:
        in_specs.append(_full(a.shape))
    in_specs.append(pl.BlockSpec((T_CHUNK, LANES), lambda i: (i, 0)))
    in_specs.append(pl.BlockSpec((T_CHUNK, LANES), lambda i: (i, 0)))
    for a in inputs[20:]:
        in_specs.append(_full(a.shape))
    out_shape = (jax.ShapeDtypeStruct((nb, seq, D_MODEL), F32),
                 jax.ShapeDtypeStruct((nb, 2 * S5_N), F32),
                 jax.ShapeDtypeStruct((nb // 2, 2, BRANCH, HEADS * HG_DK), F32),
                 jax.ShapeDtypeStruct((nb // 2, 2, BRANCH, HEADS * GLA_DK), F32),
                 jax.ShapeDtypeStruct((nb // 2, 2, BRANCH, HEADS * RET_DK), F32))
    out_specs = (pl.BlockSpec((nb, T_CHUNK, D_MODEL), lambda i: (0, i, 0)),
                 pl.BlockSpec((nb, 2 * S5_N), lambda i: (0, 0)),
                 pl.BlockSpec((nb // 2, 2, BRANCH, HEADS * HG_DK), lambda i: (0, 0, 0, 0)),
                 pl.BlockSpec((nb // 2, 2, BRANCH, HEADS * GLA_DK), lambda i: (0, 0, 0, 0)),
                 pl.BlockSpec((nb // 2, 2, BRANCH, HEADS * RET_DK), lambda i: (0, 0, 0, 0)))
    scratch = [pltpu.VMEM((N_QUARTER, rows, QUARTER), F32), pltpu.VMEM((N_QUARTER, rows, QUARTER), F32),
               pltpu.VMEM((rows, LANES), F32), pltpu.VMEM((rows, LANES), F32),
               pltpu.VMEM((rows, BRANCH), F32), pltpu.VMEM((rows, D_MODEL), BF16),
               pltpu.VMEM((2 * S5_N // LANES, nb * PITCH, LANES), F32),
               pltpu.VMEM((rows, D_MODEL), F32),
               pltpu.VMEM((TERM_ROWS, HEADS * HG_DK), F32), pltpu.VMEM((TERM_ROWS, HEADS * GLA_DK), F32),
               pltpu.VMEM((TERM_ROWS, HEADS * HG_DK), F32), pltpu.VMEM((TERM_ROWS, HEADS * GLA_DK), F32)]
    y, s5, hgs, glas, rets = pl.pallas_call(
        functools.partial(_prompt_kernel, apply_final),
        grid=(n_steps,), in_specs=in_specs, out_specs=out_specs, out_shape=out_shape,
        scratch_shapes=scratch, name='prompt_layer',
        compiler_params=pltpu.CompilerParams(dimension_semantics=('arbitrary',),
                                             vmem_limit_bytes=VMEM_LIMIT),
    )(*inputs)
    s5 = s5.reshape(nb, 2, S5_GROUPS, S5_STATE)

    def unstack(st, dk):
        st = st.reshape(nb, HEADS, DV, HEADS, dk)
        diag = jnp.stack([st[:, hh, :, hh, :] for hh in range(HEADS)], axis=1)
        return jnp.transpose(diag, (0, 1, 3, 2))

    return y, (s5[:, 0], s5[:, 1], unstack(hgs, HG_DK), unstack(glas, GLA_DK), unstack(rets, RET_DK))


def _sample_layer(x, states, p, consts, tabs, rope, apply_final):
    nb = x.shape[0]
    n_steps = nb // SAMPLE_ROWS
    s5re, s5im, hg, gla, ret = states
    dret = tabs[4]
    cos_t, sin_t = rope
    consts_in = [p['norm_w'], p['w_in'], p['w_out'], p['bbd'], p['cbd'], p['a_re'], p['a_im'], p['d_skip'],
                 p['w_glu'], p['lbp'], p['hg_norm'], p['w_up'], p['b_gate'], p['gla_norm'], p['ret_norm'],
                 cos_t, sin_t, dret, consts['expand'], consts['vsel_h'], consts['vsel_g'], consts['ones_h'],
                 p['final_norm']]
    rowed = [s5re.reshape(nb, S5_N), s5im.reshape(nb, S5_N), hg.reshape(nb, -1), gla.reshape(nb, -1),
             ret.reshape(nb, -1)]
    row_spec = lambda a: pl.BlockSpec((SAMPLE_ROWS, a.shape[1]), lambda i: (i, 0))
    in_specs = [row_spec(x)] + [_full(a.shape) for a in consts_in] + [row_spec(a) for a in rowed]
    out_arrays = [x] + rowed
    out_shape = tuple(jax.ShapeDtypeStruct(a.shape, F32) for a in out_arrays)
    out_specs = tuple(row_spec(a) for a in out_arrays)
    outs = pl.pallas_call(
        functools.partial(_sample_kernel, apply_final),
        grid=(n_steps,), in_specs=in_specs, out_specs=out_specs, out_shape=out_shape,
        name='sample_layer',
        compiler_params=pltpu.CompilerParams(dimension_semantics=('arbitrary',),
                                             vmem_limit_bytes=VMEM_LIMIT),
    )(x, *consts_in, *rowed)
    y = outs[0]
    new = (outs[1].reshape(s5re.shape), outs[2].reshape(s5im.shape), outs[3].reshape(hg.shape),
           outs[4].reshape(gla.shape), outs[5].reshape(ret.shape))
    return y, new


def kernel(x_prompt, x_sample, state_s5_re, state_s5_im, state_hgrn, state_gla, state_ret, norm_w, final_norm_w, w_in, w_out, s5_lam_re, s5_lam_im, s5_log_step, s5_b_re, s5_b_im, s5_c_re, s5_c_im, s5_d, s5_w_glu, hgrn_lb_logits, hgrn_norm_w, gla_w_gate_up, gla_b_gate, gla_norm_w, ret_norm_w):
    depth = w_in.shape[0]
    seq = x_prompt.shape[1]
    consts = _constants()
    tabs = _ret_tables(seq)
    rope_p = _rope_tables(jnp.arange(seq))
    rope_s = _rope_tables(PAST_LEN + jnp.arange(1))
    lb_all = jnp.cumsum(jax.nn.softmax(hgrn_lb_logits.astype(F32), axis=0), axis=0)
    lb_all = lb_all - lb_all[0:1]

    xp = x_prompt
    xs = x_sample.reshape(x_sample.shape[0], D_MODEL)
    new_p = ([], [], [], [], [])
    new_s = ([], [], [], [], [])
    for l in range(depth):
        bbd, cbd, a_re, a_im = _s5_discretize(s5_lam_re[l], s5_lam_im[l], s5_log_step[l], s5_b_re[l],
                                              s5_b_im[l], s5_c_re[l], s5_c_im[l])
        lb = lb_all[l][None, :]
        w_up = jnp.zeros((LANES, HEADS * GLA_DK), F32).at[:GLA_LOWRANK].set(gla_w_gate_up[l].astype(F32))
        p = dict(norm_w=norm_w[l][None, :].astype(F32), w_in=_pack_w_in(w_in[l]), w_out=w_out[l].astype(BF16),
                 bbd=bbd, cbd=cbd, a_re=a_re, a_im=a_im, d_skip=s5_d[l][None, :].astype(F32),
                 w_glu=s5_w_glu[l].astype(BF16),
                 lbp=jnp.concatenate([jnp.log(lb), jnp.log1p(-lb), 1.0 - lb, lb, jnp.zeros((4, BRANCH), F32)],
                                     axis=0),
                 hg_norm=hgrn_norm_w[l][None, :].astype(F32), w_up=w_up,
                 b_gate=gla_b_gate[l][None, :].astype(F32), gla_norm=gla_norm_w[l][None, :].astype(F32),
                 ret_norm=ret_norm_w[l][None, :].astype(F32), final_norm=final_norm_w[None, :].astype(F32))
        last = l == depth - 1
        xp, st_p = _prompt_layer(xp, p, consts, tabs, rope_p, last)
        xs, st_s = _sample_layer(xs, (state_s5_re[l], state_s5_im[l], state_hgrn[l], state_gla[l],
                                      state_ret[l]), p, consts, tabs, rope_s, last)
        for i in range(5):
            new_p[i].append(st_p[i])
            new_s[i].append(st_s[i])
    y_sample = xs.reshape(x_sample.shape)
    return (xp, y_sample,
            jnp.stack(new_p[0]), jnp.stack(new_p[1]), jnp.stack(new_p[2]), jnp.stack(new_p[3]),
            jnp.stack(new_p[4]),
            jnp.stack(new_s[0]), jnp.stack(new_s[1]), jnp.stack(new_s[2]), jnp.stack(new_s[3]),
            jnp.stack(new_s[4]))
```

```python
import functools
import math

import numpy as np
import jax
import jax.numpy as jnp
from jax import lax
from jax.experimental import pallas as pl
from jax.experimental.pallas import tpu as pltpu

F32 = jnp.float32
BF16 = jnp.bfloat16

D_MODEL = 1024
BRANCH = 256
S5_CH = 16
S5_GROUPS = 16
S5_STATE = 64
S5_N = S5_GROUPS * S5_STATE
HEADS = 4
HG_DK = 64
GLA_DK = 32
RET_DK = 32
DV = 64
GLA_LOWRANK = 16
GLA_TAU = 16.0
ROPE_BASE = 10000.0
PAST_LEN = 16384
EPS = 1e-6
SUB = 16

LANES = 128
T_CHUNK = 64
PITCH = T_CHUNK + 8
TERM_ROWS = T_CHUNK * (SUB + SUB // 2) // 2
LOG2E = math.log2(math.e)
MASK_NEG = -1e30
VMEM_LIMIT = 60 * 1024 * 1024

C_U, C_SZ, C_HQ, C_HF, C_HI, C_HZ = 0, 256, 512, 768, 1024, 1280
C_GQ, C_GK, C_GV, C_GZ = 1536, 1664, 1792, 2048
C_RQ, C_RK, C_RV, C_RZ = 2304, 2432, 2560, 2816
C_LR = 3072
N_PACK = 3200
N_QUARTER = 4
QUARTER = C_LR // N_QUARTER


def _dot(a, b):
    return jnp.dot(a.astype(BF16), b.astype(BF16), preferred_element_type=F32)


def _dot_nt(a, b):
    return lax.dot_general(a.astype(BF16), b.astype(BF16), (((1,), (1,)), ((), ())),
                           preferred_element_type=F32)


def _dot_tn(a, b):
    return lax.dot_general(a.astype(BF16), b.astype(BF16), (((0,), (0,)), ((), ())),
                           preferred_element_type=F32)


def _split3(x):
    x1 = x.astype(BF16)
    r1 = x - x1.astype(F32)
    x2 = r1.astype(BF16)
    x3 = (r1 - x2.astype(F32)).astype(BF16)
    return x1, x2, x3


def _dot_sel_lhs(sel, x):
    x1, x2, x3 = _split3(x)
    d = lambda p: jnp.dot(sel, p, preferred_element_type=F32)
    return d(x1) + d(x2) + d(x3)


def _dot_sel_rhs2(x, sel):
    x1 = x.astype(BF16)
    x2 = (x - x1.astype(F32)).astype(BF16)
    return (jnp.dot(x1, sel, preferred_element_type=F32)
            + jnp.dot(x2, sel, preferred_element_type=F32))


def _dot3(a, b):
    a1 = a.astype(BF16)
    a2 = (a - a1.astype(F32)).astype(BF16)
    b1 = b.astype(BF16)
    b2 = (b - b1.astype(F32)).astype(BF16)
    d = lambda p, q: jnp.dot(p, q, preferred_element_type=F32)
    return d(a1, b1) + d(a1, b2) + d(a2, b1)


def _sigmoid(x):
    return 1.0 / (1.0 + jnp.exp(-x))


def _silu(x):
    return x * _sigmoid(x)


def _log_sigmoid(x):
    return jnp.minimum(x, 0.0) - jnp.log(1.0 + jnp.exp(-jnp.abs(x)))


def _gelu_tanh(x):
    return 0.5 * x * (1.0 + jnp.tanh(math.sqrt(2.0 / math.pi) * (x + 0.044715 * (x * x * x))))


def _rmsnorm_rows(x, w):
    return x * lax.rsqrt(jnp.mean(x * x, axis=-1, keepdims=True) + EPS) * w


def _head_rms(o, ones_h, gain):
    ms = _dot_sel_rhs2(o * o, ones_h) * (1.0 / DV)
    return o * lax.rsqrt(ms + EPS) * gain


def _head_ln(o, ones_h, gain):
    c = o - _dot_sel_rhs2(o, ones_h) * (1.0 / DV)
    var = _dot_sel_rhs2(c * c, ones_h) * (1.0 / DV)
    return c * lax.rsqrt(var + EPS) * gain


def _hgrn_gates(xf, loglb, log1mlb, one_m_lb):
    bterm = log1mlb + _log_sigmoid(xf)
    m = jnp.maximum(loglb, bterm)
    log_f = m + jnp.log(jnp.exp(loglb - m) + jnp.exp(bterm - m))
    return log_f, one_m_lb * _sigmoid(-xf)


def _rotary(t, cos, sin_signed, first_half):
    half = RET_DK // 2
    swapped = jnp.where(first_half, pltpu.roll(t, LANES - half, 1), pltpu.roll(t, half, 1))
    return t * cos + swapped * sin_signed


def _s5_output(y_lin, u, sz, dskip, wglu):
    y = _gelu_tanh(y_lin + u * dskip)
    y = y * _sigmoid(_dot(y, wglu))
    return y * _silu(sz)


def _mix_and_project(x, mix, wout, fnw, apply_final):
    out = x + _dot(mix, wout)
    if apply_final:
        out = _rmsnorm_rows(out, fnw)
    return out


class _GatedStream:
    def __init__(self, q, k, v, g, s_ref, ones_kv, mask_vk, tril, blk, term_scr, kdim):
        self.q, self.k, self.v, self.g = q, k, v, g
        self.s_ref, self.ones_kv, self.mask_vk = s_ref, ones_kv, mask_vk
        self.tril, self.blk, self.term_scr, self.kdim = tril, blk, term_scr, kdim

    def cumulate(self):
        g = self.g()
        self.bc = _dot_sel_lhs(self.tril, g) * LOG2E
        self.bl = _dot_sel_lhs(self.blk, g) * LOG2E

    def state_updates(self):
        n_sub = T_CHUNK // SUB
        v = self.v()
        kh = self.k() * jnp.exp2(self.bl - self.bc)
        self.ws = [_dot_tn(v[SUB * j:SUB * (j + 1)], kh[SUB * j:SUB * (j + 1)]) for j in range(n_sub)]

    def decay_products(self):
        n_sub, hs, term_scr = T_CHUNK // SUB, SUB // 2, self.term_scr
        q, k, bc = self.q(), self.k(), self.bc
        trow = lax.broadcasted_iota(jnp.int32, (hs, self.kdim), 0)
        neg = [jnp.where(trow >= d, 0.0, MASK_NEG) for d in range(1, hs)]
        row = 0
        for j in range(n_sub):
            r0 = SUB * j
            q_lo, q_hi = q[r0:r0 + hs], q[r0 + hs:r0 + SUB]
            b_lo, b_hi = bc[r0:r0 + hs], bc[r0 + hs:r0 + SUB]
            for s in range(SUB):
                ks, bs = k[r0 + s:r0 + s + 1, :], bc[r0 + s:r0 + s + 1, :]
                d = s % hs
                q_dg, b_dg = (q_lo, b_lo) if s < hs else (q_hi, b_hi)
                e_dg = b_dg - bs if d == 0 else b_dg - bs + neg[d - 1]
                term_scr[row:row + hs, :] = q_dg * ks * jnp.exp2(e_dg)
                row += hs
                if s < hs:
                    term_scr[row:row + hs, :] = q_hi * ks * jnp.exp2(b_hi - bs)
                    row += hs

    def matmuls(self):
        n_sub = T_CHUNK // SUB
        qt = self.q() * jnp.exp2(self.bc)
        states = [self.s_ref[...]]
        for j in range(n_sub):
            states.append(states[j] * jnp.exp2(self.bl[SUB * j:SUB * j + 1, :]) + self.ws[j] * self.mask_vk)
        self.s_ref[...] = states[n_sub]
        self.inter = [_dot_nt(qt[SUB * j:SUB * (j + 1)], states[j]) for j in range(n_sub)]
        self.att = _dot(self.term_scr[...], self.ones_kv)

    def output(self):
        n_sub, hs = T_CHUNK // SUB, SUB // 2
        v, att = self.v(), self.att
        outs = []
        row = 0
        for j in range(n_sub):
            r0 = SUB * j
            acc_lo, acc_hi = self.inter[j][0:hs], self.inter[j][hs:SUB]
            for s in range(SUB):
                vs = v[r0 + s:r0 + s + 1, :]
                if s < hs:
                    acc_lo = acc_lo + att[row:row + hs] * vs
                    row += hs
                acc_hi = acc_hi + att[row:row + hs] * vs
                row += hs
            outs += [acc_lo, acc_hi]
        return jnp.concatenate(outs, axis=0)


def _ret_chunk(q, k, v, cos, sin_signed, s_ref, dstack, inner, kdec, cdec, mask_vk):
    lane = lax.broadcasted_iota(jnp.int32, (T_CHUNK, HEADS * RET_DK), 1)
    first_half = (lane % RET_DK) < (RET_DK // 2)
    rq = _rotary(q, cos, sin_signed, first_half)
    rk = _rotary(k, cos, sin_signed, first_half) * (RET_DK ** -0.5)
    lane_v = lax.broadcasted_iota(jnp.int32, (T_CHUNK, HEADS * DV), 1)
    kst = jnp.concatenate([jnp.where(lane // RET_DK == h, rk, 0.0) for h in range(HEADS)], axis=0)
    vst = jnp.concatenate([jnp.where(lane_v // DV == h, v, 0.0) for h in range(HEADS)], axis=0)
    p = _dot_nt(rq, kst) * dstack
    s_t = s_ref[...]
    o = _dot(p, vst) + _dot_nt(rq, s_t) * inner
    s_ref[...] = s_t * cdec + _dot_tn(v, rk * kdec) * mask_vk
    return o


def _prompt_kernel(apply_final,
                   x_ref, xn_ref, normw_ref, win_ref, wlr_ref, wout_ref, bbd_ref, cbd_ref, are_ref, aim_ref,
                   dskip_ref, wglu_ref, lbp_ref, hgn_ref, wup_ref, bgate_ref, glan_ref, retn_ref, cos_ref,
                   sin_ref, dstack_ref, inner_ref, kdec_ref, cdec_ref, onesh_ref, onesg_ref, tril_ref,
                   blk_ref, fnw_ref,
                   y_ref, s5_ref, hgs_ref, glas_ref, rets_ref,
                   proj_scr, nxt_scr, lr_scr, lrn_scr, hk_scr, hn_scr, bu_scr, mix_scr,
                   term_h0, term_g0, term_h1, term_g1):
    nb = x_ref.shape[0]
    rows = nb * T_CHUNK
    n_slab = 2 * S5_N // LANES
    half = n_slab // 2
    step = pl.program_id(0)

    def proj(col, width, rs=slice(None)):
        return proj_scr[col // QUARTER, rs, col % QUARTER:col % QUARTER + width]

    @pl.when(step == 0)
    def _init():
        s5_ref[...] = jnp.zeros_like(s5_ref)
        hgs_ref[...] = jnp.zeros_like(hgs_ref)
        glas_ref[...] = jnp.zeros_like(glas_ref)
        rets_ref[...] = jnp.zeros_like(rets_ref)
        bu_scr[...] = jnp.zeros_like(bu_scr)
        h0 = _rmsnorm_rows(x_ref[...].reshape(rows, D_MODEL), normw_ref[...]).astype(BF16)
        for qtr in range(N_QUARTER):
            proj_scr[qtr] = jnp.dot(h0, win_ref[qtr], preferred_element_type=F32)
        lr_scr[...] = jnp.dot(h0, wlr_ref[...], preferred_element_type=F32)

    @pl.when(step > 0)
    def _advance():
        proj_scr[...] = nxt_scr[...]
        lr_scr[...] = lrn_scr[...]

    hn_scr[...] = _rmsnorm_rows(xn_ref[...].reshape(rows, D_MODEL), normw_ref[...]).astype(BF16)
    lrn_scr[...] = jnp.dot(hn_scr[...], wlr_ref[...], preferred_element_type=F32)

    u = proj(C_U, BRANCH)
    for c2 in range(n_slab // 2):
        bu = _dot(u, bbd_ref[:, 2 * c2 * LANES:2 * (c2 + 1) * LANES])
        for cc in range(2):
            for b in range(nb):
                bu_scr[2 * c2 + cc, b * PITCH:b * PITCH + T_CHUNK, :] = bu[b * T_CHUNK:(b + 1) * T_CHUNK,
                                                                           cc * LANES:(cc + 1) * LANES]
    a_re = [jnp.broadcast_to(are_ref[:, c * LANES:(c + 1) * LANES], (nb, LANES)) for c in range(half)]
    a_im = [jnp.broadcast_to(aim_ref[:, c * LANES:(c + 1) * LANES], (nb, LANES)) for c in range(half)]
    s_init = s5_ref[...]
    carry0 = tuple(s_init[:, c * LANES:(c + 1) * LANES] for c in range(n_slab))

    def scan_step(t, carry):
        new = [None] * n_slab
        for c in range(half):
            sr, si = carry[c], carry[half + c]
            br = bu_scr[c, pl.ds(t, nb, stride=PITCH), :]
            bi = bu_scr[half + c, pl.ds(t, nb, stride=PITCH), :]
            nr = a_re[c] * sr - a_im[c] * si + br
            ni = a_re[c] * si + a_im[c] * sr + bi
            bu_scr[c, pl.ds(t, nb, stride=PITCH), :] = nr
            bu_scr[half + c, pl.ds(t, nb, stride=PITCH), :] = ni
            new[c], new[half + c] = nr, ni
        return tuple(new)

    carry = lax.fori_loop(0, T_CHUNK, scan_step, carry0)
    s5_ref[...] = jnp.concatenate(carry, axis=1)
    s_all = jnp.concatenate([bu_scr[c] for c in range(n_slab)], axis=1)
    y_all = _dot(s_all, cbd_ref[...])
    y_lin = jnp.concatenate([y_all[b * PITCH:b * PITCH + T_CHUNK] for b in range(nb)], axis=0)
    mix_scr[:, 0:BRANCH] = _s5_output(y_lin, u, proj(C_SZ, BRANCH), dskip_ref[...], wglu_ref[...])

    lbp = lbp_ref[...]
    log_f, hk = _hgrn_gates(proj(C_HF, BRANCH), lbp[0:1, :], lbp[1:2, :], lbp[2:3, :])
    proj_scr[C_HF // QUARTER, :, C_HF % QUARTER:C_HF % QUARTER + BRANCH] = log_f
    hk_scr[...] = hk
    g_gla = _log_sigmoid(_dot3(lr_scr[...], wup_ref[...]) + bgate_ref[...])
    lr_scr[...] = g_gla * (1.0 / GLA_TAU)

    ones_h = onesh_ref[...]
    ones_g = onesg_ref[...]
    mask_h = ones_h.astype(F32)
    mask_g = jnp.transpose(ones_g.astype(F32))
    tril = tril_ref[...]
    blk = blk_ref[...]

    def rows_of(bp, odd):
        return pl.ds(pl.multiple_of((2 * bp + odd) * T_CHUNK, T_CHUNK), T_CHUNK)

    def hgrn_stream(bp, odd, term):
        rs = rows_of(bp, odd)
        return _GatedStream(lambda: proj(C_HQ, BRANCH, rs), lambda: hk_scr[rs, :], lambda: proj(C_HI, BRANCH, rs),
                            lambda: proj(C_HF, BRANCH, rs), hgs_ref.at[bp, odd], ones_h, mask_h, tril, blk,
                            term, HEADS * HG_DK)

    def gla_stream(bp, odd, term):
        rs = rows_of(bp, odd)
        return _GatedStream(lambda: proj(C_GQ, LANES, rs) * (GLA_DK ** -0.5), lambda: proj(C_GK, LANES, rs),
                            lambda: proj(C_GV, BRANCH, rs), lambda: lr_scr[rs, :], glas_ref.at[bp, odd],
                            ones_g, mask_g, tril, blk, term, HEADS * GLA_DK)

    def per_pair(bp, _):
        streams = [hgrn_stream(bp, 0, term_h0), gla_stream(bp, 0, term_g0),
                   hgrn_stream(bp, 1, term_h1), gla_stream(bp, 1, term_g1)]
        slots = [(0, BRANCH), (0, 2 * BRANCH), (1, BRANCH), (1, 2 * BRANCH)]
        for st in streams:
            st.cumulate()
        for st in streams:
            st.state_updates()
        nxt_scr[bp] = jnp.dot(hn_scr[...], win_ref[bp], preferred_element_type=F32)
        for st in streams:
            st.decay_products()
        for st in streams:
            st.matmuls()
        for odd in range(2):
            rs = rows_of(bp, odd)
            mix_scr[rs, 3 * BRANCH:4 * BRANCH] = _ret_chunk(
                proj(C_RQ, LANES, rs), proj(C_RK, LANES, rs), proj(C_RV, BRANCH, rs), cos_ref[...],
                sin_ref[...], rets_ref.at[bp, odd], dstack_ref[...], inner_ref[...], kdec_ref[...],
                cdec_ref[...], mask_g)
        for st, (odd, col) in zip(streams, slots):
            mix_scr[rows_of(bp, odd), col:col + BRANCH] = st.output()
        return 0

    lax.fori_loop(0, N_QUARTER, per_pair, 0)

    o_hg = _head_rms(mix_scr[:, BRANCH:2 * BRANCH], ones_h, hgn_ref[...])
    mix_scr[:, BRANCH:2 * BRANCH] = o_hg * _silu(proj(C_HZ, BRANCH))
    o_gla = _head_rms(mix_scr[:, 2 * BRANCH:3 * BRANCH], ones_h, glan_ref[...])
    mix_scr[:, 2 * BRANCH:3 * BRANCH] = o_gla * _silu(proj(C_GZ, BRANCH))
    o_ret = _head_ln(mix_scr[:, 3 * BRANCH:4 * BRANCH], ones_h, retn_ref[...])
    mix_scr[:, 3 * BRANCH:4 * BRANCH] = o_ret * _silu(proj(C_RZ, BRANCH))
    out = _mix_and_project(x_ref[...].reshape(rows, D_MODEL), mix_scr[...], wout_ref[...], fnw_ref[...],
                           apply_final)
    y_ref[...] = out.reshape(nb, T_CHUNK, D_MODEL)


def _dot_sel_lhs2(sel, x):
    x1 = x.astype(BF16)
    x2 = (x - x1.astype(F32)).astype(BF16)
    return (jnp.dot(sel, x1, preferred_element_type=F32) + jnp.dot(sel, x2, preferred_element_type=F32))


def _head_rms_t(o, ones_h, gain):
    ms = _dot_sel_lhs2(ones_h, o * o) * (1.0 / DV)
    return o * lax.rsqrt(ms + EPS) * gain


def _head_ln_t(o, ones_h, gain):
    c = o - _dot_sel_lhs2(ones_h, o) * (1.0 / DV)
    var = _dot_sel_lhs2(ones_h, c * c) * (1.0 / DV)
    return c * lax.rsqrt(var + EPS) * gain


def _rotary_t(t, cos, sin_signed, first_half):
    half = RET_DK // 2
    swapped = jnp.where(first_half, pltpu.roll(t, LANES - half, 0), pltpu.roll(t, half, 0))
    return t * cos + swapped * sin_signed


def _sample_kernel(x_ref, normw_ref, wt_ref, wout_ref, bbdt_ref, cbdt_ref, are_ref, aim_ref, dskip_ref,
                   wglut_ref, lbp_ref, hgn_ref, wupt_ref, bgate_ref, glan_ref, retn_ref, fnw_ref,
                   cos_ref, sin_ref, dret_ref, onesh_ref,
                   s5re_ref, s5im_ref, hg_ref, gla_ref, ret_ref,
                   y_ref, s5re_o, s5im_o, hg_o, gla_o, ret_o,
                   xs_scr, pt_scr, hk_scr, ot_scr, mixt_scr):
    layer, head = pl.program_id(0), pl.program_id(1)
    last_layer, last_head = pl.num_programs(0) - 1, pl.num_programs(1) - 1

    @pl.when((layer == 0) & (head == 0))
    def _load_x():
        xs_scr[...] = x_ref[...]

    @pl.when(head == 0)
    def _dense():
        hh = _rmsnorm_rows(xs_scr[...], normw_ref[...]).astype(BF16)
        pt_scr[...] = lax.dot_general(wt_ref[...], hh, (((1,), (1,)), ((), ())), preferred_element_type=F32)

        u = pt_scr[C_U:C_U + BRANCH, :]
        bu = _dot3(bbdt_ref[...], u)
        a_re, a_im = are_ref[...], aim_ref[...]
        s0r, s0i = s5re_ref[...], s5im_ref[...]
        s_re = a_re * s0r - a_im * s0i + bu[0:S5_N]
        s_im = a_re * s0i + a_im * s0r + bu[S5_N:2 * S5_N]
        s5re_o[...] = s_re
        s5im_o[...] = s_im
        y = _gelu_tanh(_dot3(cbdt_ref[...], jnp.concatenate([s_re, s_im], axis=0)) + u * dskip_ref[...])
        y = y * _sigmoid(jnp.dot(wglut_ref[...], y.astype(BF16), preferred_element_type=F32))
        mixt_scr[0:BRANCH, :] = y * _silu(pt_scr[C_SZ:C_SZ + BRANCH, :])

        lbp = lbp_ref[...]
        log_f, hk = _hgrn_gates(pt_scr[C_HF:C_HF + BRANCH, :], lbp[:, 0:1], lbp[:, 1:2], lbp[:, 2:3])
        pt_scr[C_HF:C_HF + BRANCH, :] = jnp.exp(log_f)
        hk_scr[...] = hk
        g_gla = _log_sigmoid(_dot3(wupt_ref[...], pt_scr[C_LR:C_LR + LANES, :]) + bgate_ref[...])
        pt_scr[C_LR:C_LR + LANES, :] = jnp.exp(g_gla * (1.0 / GLA_TAU))
        pt_scr[C_GQ:C_GQ + LANES, :] = pt_scr[C_GQ:C_GQ + LANES, :] * (GLA_DK ** -0.5)
        row = lax.broadcasted_iota(jnp.int32, (HEADS * RET_DK, LANES), 0)
        first_half = (row % RET_DK) < (RET_DK // 2)
        pt_scr[C_RQ:C_RQ + LANES, :] = _rotary_t(pt_scr[C_RQ:C_RQ + LANES, :], cos_ref[...], sin_ref[...],
                                                 first_half)
        pt_scr[C_RK:C_RK + LANES, :] = _rotary_t(pt_scr[C_RK:C_RK + LANES, :], cos_ref[...], sin_ref[...],
                                                 first_half) * (RET_DK ** -0.5)

    def head_update(s0_ref, s_out_ref, dk, dec_ref, dec_row, key_ref, key_row, q_row, v_row, out_row):
        vt = pt_scr[pl.ds(pl.multiple_of(v_row + head * DV, DV), DV), :]

        def feature(kk, acc):
            r = head * dk + kk
            bcast = lambda ref, r0: jnp.broadcast_to(ref[pl.ds(r0 + r, 1), :], (DV, LANES))
            s_new = s0_ref[kk] * bcast(dec_ref, dec_row) + bcast(key_ref, key_row) * vt
            s_out_ref[kk] = s_new
            return acc + bcast(pt_scr, q_row) * s_new

        acc = lax.fori_loop(0, dk, feature, jnp.zeros((DV, LANES), F32), unroll=4)
        ot_scr[pl.ds(pl.multiple_of(out_row + head * DV, DV), DV), :] = acc

    head_update(hg_ref, hg_o, HG_DK, pt_scr, C_HF, hk_scr, 0, C_HQ, C_HI, 0)
    head_update(gla_ref, gla_o, GLA_DK, pt_scr, C_LR, pt_scr, C_GK, C_GQ, C_GV, BRANCH)
    head_update(ret_ref, ret_o, RET_DK, dret_ref, 0, pt_scr, C_RK, C_RQ, C_RV, 2 * BRANCH)

    @pl.when(head == last_head)
    def _finish():
        ones_h = onesh_ref[...]
        o_hg = _head_rms_t(ot_scr[0:BRANCH, :], ones_h, hgn_ref[...])
        mixt_scr[BRANCH:2 * BRANCH, :] = o_hg * _silu(pt_scr[C_HZ:C_HZ + BRANCH, :])
        o_gla = _head_rms_t(ot_scr[BRANCH:2 * BRANCH, :], ones_h, glan_ref[...])
        mixt_scr[2 * BRANCH:3 * BRANCH, :] = o_gla * _silu(pt_scr[C_GZ:C_GZ + BRANCH, :])
        o_ret = _head_ln_t(ot_scr[2 * BRANCH:3 * BRANCH, :], ones_h, retn_ref[...])
        mixt_scr[3 * BRANCH:4 * BRANCH, :] = o_ret * _silu(pt_scr[C_RZ:C_RZ + BRANCH, :])
        out = xs_scr[...] + lax.dot_general(mixt_scr[...].astype(BF16), wout_ref[...], (((0,), (0,)), ((), ())),
                                            preferred_element_type=F32)
        xs_scr[...] = out

        @pl.when(layer == last_layer)
        def _emit():
            y_ref[...] = _rmsnorm_rows(out, fnw_ref[...])


def _ret_log_gamma():
    return jnp.log1p(-jnp.exp2(-5.0 - jnp.arange(HEADS, dtype=F32)))


def _constants():
    ones_h = (np.arange(BRANCH)[:, None] // DV == np.arange(BRANCH)[None, :] // DV)
    ones_g = (np.arange(HEADS * GLA_DK)[:, None] // GLA_DK == np.arange(BRANCH)[None, :] // DV)
    r = np.arange(T_CHUNK)
    same_sub = r[:, None] // SUB == r[None, :] // SUB
    tril = same_sub & (r[None, :] <= r[:, None])
    same_group_b = np.arange(BRANCH)[:, None] // S5_CH == np.arange(S5_N)[None, :] // S5_STATE
    as_bf16 = lambda m: jnp.asarray(m.astype(np.float32), dtype=BF16)
    return dict(ones_h=as_bf16(ones_h), ones_g=as_bf16(ones_g), tril=as_bf16(tril), blk=as_bf16(same_sub),
                s5_mask=jnp.asarray(same_group_b.astype(np.float32)))


def _ret_tables():
    lg = _ret_log_gamma()
    idx = jnp.arange(T_CHUNK, dtype=F32)
    rel = idx[:, None] - idx[None, :]
    causal = rel >= 0
    decay = jnp.where(causal[None], jnp.exp(jnp.where(causal, rel, 0.0)[None] * lg[:, None, None]), 0.0)
    dstack = jnp.transpose(decay, (1, 0, 2)).reshape(T_CHUNK, HEADS * T_CHUNK)
    inner = jnp.repeat(jnp.exp((idx[:, None] + 1.0) * lg[None, :]), DV, axis=1)
    kdec = jnp.repeat(jnp.exp((T_CHUNK - 1.0 - idx[:, None]) * lg[None, :]), RET_DK, axis=1)
    cdec = jnp.repeat(jnp.exp(T_CHUNK * lg)[None, :], RET_DK, axis=1)
    dret = jnp.broadcast_to(jnp.repeat(jnp.exp(lg), RET_DK)[:, None], (HEADS * RET_DK, LANES))
    return dstack, inner, kdec, cdec, dret


def _rope_tables(pos):
    half = RET_DK // 2
    inv = ROPE_BASE ** (-jnp.arange(half, dtype=F32) / half)
    ang = pos.astype(F32)[:, None] * inv[None, :]
    cos, sin = jnp.cos(ang), jnp.sin(ang)
    cos_t = jnp.tile(jnp.concatenate([cos, cos], axis=1), (1, HEADS))
    sin_t = jnp.tile(jnp.concatenate([-sin, sin], axis=1), (1, HEADS))
    return cos_t, sin_t


def _pack_w_in(w):
    offs = np.cumsum([0, 256, 256, 256, 256, 256, 256, 128, 128, 256, 16, 256, 128, 128, 256, 256])
    seg = lambda i: w[:, :, int(offs[i]):int(offs[i + 1])]
    order = [0, 1, 2, 3, 4, 5, 6, 7, 8, 10, 11, 12, 13, 14]
    pad = jnp.zeros(w.shape[:2] + (LANES - GLA_LOWRANK,), w.dtype)
    return jnp.concatenate([seg(i) for i in order] + [seg(9), pad], axis=2).astype(BF16)


def _s5_discretize(lam_re, lam_im, log_step, b_re, b_im, c_re, c_im, mask):
    lr, li = lam_re.astype(F32), lam_im.astype(F32)
    step = jnp.exp(log_step.astype(F32))[..., None]
    mag = jnp.exp(lr * step)
    ab_re = mag * jnp.cos(li * step)
    ab_im = mag * jnp.sin(li * step)
    den = lr * lr + li * li
    nr = ab_re - 1.0
    f_re = (nr * lr + ab_im * li) / den
    f_im = (ab_im * lr - nr * li) / den
    br, bi = b_re.astype(F32), b_im.astype(F32)
    bb_re = f_re[..., None] * br - f_im[..., None] * bi
    bb_im = f_re[..., None] * bi + f_im[..., None] * br
    nl = lr.shape[0]

    def drive(bb):
        rows = jnp.transpose(bb, (0, 1, 3, 2)).reshape(nl, BRANCH, S5_STATE)
        return jnp.tile(rows, (1, 1, S5_GROUPS)) * mask

    def readout(cc):
        rows = cc.astype(F32).reshape(nl, BRANCH, S5_STATE)
        return jnp.tile(rows, (1, 1, S5_GROUPS)) * mask

    bbd = jnp.concatenate([drive(bb_re), drive(bb_im)], axis=2)
    cbd_t = jnp.concatenate([readout(c_re), -readout(c_im)], axis=2)
    return bbd, cbd_t, ab_re.reshape(nl, 1, S5_N), ab_im.reshape(nl, 1, S5_N)


def _full(shape):
    return pl.BlockSpec(shape, lambda *_: (0,) * len(shape), pipeline_mode=pl.Buffered(1))


def _of_layer(arr, layer):
    nd = arr.ndim - 1
    return pl.BlockSpec((None,) + arr.shape[1:], lambda *_: (layer,) + (0,) * nd,
                        pipeline_mode=pl.Buffered(1))


def _prompt_layer(x, layer, p, consts, tabs, rope, apply_final):
    nb, seq, _ = x.shape
    n_steps = seq // T_CHUNK
    rows = nb * T_CHUNK
    dstack, inner, kdec, cdec, _ = tabs
    cos_t, sin_t = rope
    per_layer = [p['norm_w'], p['w_main'], p['w_lr'], p['w_out'], p['bbd'].astype(BF16), p['cbd'],
                 p['a_re'], p['a_im'],
                 p['d_skip'], p['w_glu'], p['lbp'], p['hg_norm'], p['w_up'], p['b_gate'], p['gla_norm'],
                 p['ret_norm']]
    shared = [dstack, inner, kdec, cdec, consts['ones_h'], consts['ones_g'], consts['tril'], consts['blk'],
              p['final_norm']]
    in_specs = [pl.BlockSpec((nb, T_CHUNK, D_MODEL), lambda i: (0, i, 0)),
                pl.BlockSpec((nb, T_CHUNK, D_MODEL), lambda i: (0, jnp.minimum(i + 1, n_steps - 1), 0))]
    in_specs += [_of_layer(a, layer) for a in per_layer]
    in_specs += [pl.BlockSpec((T_CHUNK, LANES), lambda i: (i, 0)), pl.BlockSpec((T_CHUNK, LANES), lambda i: (i, 0))]
    in_specs += [_full(a.shape) for a in shared]
    out_shape = (jax.ShapeDtypeStruct((nb, seq, D_MODEL), F32),
                 jax.ShapeDtypeStruct((nb, 2 * S5_N), F32),
                 jax.ShapeDtypeStruct((nb // 2, 2, BRANCH, HEADS * HG_DK), F32),
                 jax.ShapeDtypeStruct((nb // 2, 2, BRANCH, HEADS * GLA_DK), F32),
                 jax.ShapeDtypeStruct((nb // 2, 2, BRANCH, HEADS * RET_DK), F32))
    out_specs = (pl.BlockSpec((nb, T_CHUNK, D_MODEL), lambda i: (0, i, 0)),
                 pl.BlockSpec((nb, 2 * S5_N), lambda i: (0, 0)),
                 pl.BlockSpec((nb // 2, 2, BRANCH, HEADS * HG_DK), lambda i: (0, 0, 0, 0)),
                 pl.BlockSpec((nb // 2, 2, BRANCH, HEADS * GLA_DK), lambda i: (0, 0, 0, 0)),
                 pl.BlockSpec((nb // 2, 2, BRANCH, HEADS * RET_DK), lambda i: (0, 0, 0, 0)))
    scratch = [pltpu.VMEM((N_QUARTER, rows, QUARTER), F32), pltpu.VMEM((N_QUARTER, rows, QUARTER), F32),
               pltpu.VMEM((rows, LANES), F32), pltpu.VMEM((rows, LANES), F32),
               pltpu.VMEM((rows, BRANCH), F32), pltpu.VMEM((rows, D_MODEL), BF16),
               pltpu.VMEM((2 * S5_N // LANES, nb * PITCH, LANES), F32),
               pltpu.VMEM((rows, D_MODEL), F32),
               pltpu.VMEM((TERM_ROWS, HEADS * HG_DK), F32), pltpu.VMEM((TERM_ROWS, HEADS * GLA_DK), F32),
               pltpu.VMEM((TERM_ROWS, HEADS * HG_DK), F32), pltpu.VMEM((TERM_ROWS, HEADS * GLA_DK), F32)]
    y, s5, hgs, glas, rets = pl.pallas_call(
        functools.partial(_prompt_kernel, apply_final),
        grid=(n_steps,), in_specs=in_specs, out_specs=out_specs, out_shape=out_shape,
        scratch_shapes=scratch, name='prompt_layer',
        compiler_params=pltpu.CompilerParams(dimension_semantics=('arbitrary',),
                                             vmem_limit_bytes=VMEM_LIMIT),
    )(x, x, *per_layer, cos_t, sin_t, *shared)
    s5 = s5.reshape(nb, 2, S5_GROUPS, S5_STATE)

    def unstack(st, dk):
        st = st.reshape(nb, HEADS, DV, HEADS, dk)
        diag = jnp.stack([st[:, hh, :, hh, :] for hh in range(HEADS)], axis=1)
        return jnp.transpose(diag, (0, 1, 3, 2))

    return y, (s5[:, 0], s5[:, 1], unstack(hgs, HG_DK), unstack(glas, GLA_DK), unstack(rets, RET_DK))


def _sample_step(x, states, p, consts, tabs, rope):
    nb = x.shape[0]
    depth = p['w_t'].shape[0]
    s5re, s5im, hg, gla, ret = states
    to_lanes = lambda s: jnp.moveaxis(s, 1, -1)
    s5re_t = to_lanes(s5re).reshape(depth, S5_N, nb)
    s5im_t = to_lanes(s5im).reshape(depth, S5_N, nb)
    hg_t, gla_t, ret_t = to_lanes(hg), to_lanes(gla), to_lanes(ret)
    cos_t, sin_t = rope
    col = lambda a: jnp.swapaxes(a, -1, -2)
    per_layer = [p['norm_w'], p['w_t'], p['w_out'], col(p['bbd']), p['cbd_t'], col(p['a_re']), col(p['a_im']),
                 col(p['d_skip']), col(p['w_glu']), col(p['lbp']), col(p['hg_norm']), col(p['w_up']),
                 col(p['b_gate']), col(p['gla_norm']), col(p['ret_norm'])]
    shared = [p['final_norm'], col(cos_t), col(sin_t), tabs[4], consts['ones_h']]
    layer_spec = lambda a: pl.BlockSpec((None,) + a.shape[1:], lambda l, h: (l,) + (0,) * (a.ndim - 1))
    head_spec = lambda a: pl.BlockSpec((None, None) + a.shape[2:], lambda l, h: (l, h) + (0,) * (a.ndim - 2))
    state_specs = [layer_spec(s5re_t), layer_spec(s5im_t), head_spec(hg_t), head_spec(gla_t), head_spec(ret_t)]
    in_specs = ([_full(x.shape)] + [layer_spec(a) for a in per_layer] + [_full(a.shape) for a in shared]
                + state_specs)
    state_arrays = [s5re_t, s5im_t, hg_t, gla_t, ret_t]
    out_shape = tuple([jax.ShapeDtypeStruct(x.shape, F32)]
                      + [jax.ShapeDtypeStruct(a.shape, F32) for a in state_arrays])
    out_specs = tuple([pl.BlockSpec(x.shape, lambda l, h: (0, 0))] + state_specs)
    scratch = [pltpu.VMEM((nb, D_MODEL), F32), pltpu.VMEM((N_PACK, nb), F32), pltpu.VMEM((BRANCH, nb), F32),
               pltpu.VMEM((3 * BRANCH, nb), F32), pltpu.VMEM((4 * BRANCH, nb), F32)]
    outs = pl.pallas_call(
        _sample_kernel, grid=(depth, HEADS), in_specs=in_specs, out_specs=out_specs, out_shape=out_shape,
        scratch_shapes=scratch, name='sample_step',
        compiler_params=pltpu.CompilerParams(dimension_semantics=('arbitrary', 'arbitrary'),
                                             vmem_limit_bytes=VMEM_LIMIT),
    )(x, *per_layer, *shared, *state_arrays)
    from_lanes = lambda s: jnp.moveaxis(s, -1, 1)
    new = (from_lanes(outs[1].reshape(depth, S5_GROUPS, S5_STATE, nb)),
           from_lanes(outs[2].reshape(depth, S5_GROUPS, S5_STATE, nb)),
           from_lanes(outs[3]), from_lanes(outs[4]), from_lanes(outs[5]))
    return outs[0], new


def kernel(x_prompt, x_sample, state_s5_re, state_s5_im, state_hgrn, state_gla, state_ret, norm_w, final_norm_w, w_in, w_out, s5_lam_re, s5_lam_im, s5_log_step, s5_b_re, s5_b_im, s5_c_re, s5_c_im, s5_d, s5_w_glu, hgrn_lb_logits, hgrn_norm_w, gla_w_gate_up, gla_b_gate, gla_norm_w, ret_norm_w):
    depth = w_in.shape[0]
    seq = x_prompt.shape[1]
    consts = _constants()
    tabs = _ret_tables()
    rope_p = _rope_tables(jnp.arange(seq))
    rope_s = _rope_tables(PAST_LEN + jnp.arange(1))
    lb = jnp.cumsum(jax.nn.softmax(hgrn_lb_logits.astype(F32), axis=0), axis=0)
    lb = (lb - lb[0:1])[:, None, :]
    bbd, cbd_t, a_re, a_im = _s5_discretize(s5_lam_re, s5_lam_im, s5_log_step, s5_b_re, s5_b_im, s5_c_re,
                                            s5_c_im, consts['s5_mask'])
    w_pack = _pack_w_in(w_in)
    row = lambda a: a[:, None, :].astype(F32)
    w_up = jnp.zeros((depth, LANES, HEADS * GLA_DK), F32).at[:, :GLA_LOWRANK].set(gla_w_gate_up.astype(F32))
    p = dict(norm_w=row(norm_w),
             w_main=jnp.transpose(w_pack[:, :, :C_LR].reshape(depth, D_MODEL, N_QUARTER, QUARTER), (0, 2, 1, 3)),
             w_lr=w_pack[:, :, C_LR:], w_t=jnp.swapaxes(w_pack, 1, 2), w_out=w_out.astype(BF16),
             bbd=bbd, cbd=jnp.swapaxes(cbd_t, 1, 2).astype(BF16), cbd_t=cbd_t, a_re=a_re, a_im=a_im,
             d_skip=row(s5_d), w_glu=s5_w_glu.astype(BF16),
             lbp=jnp.concatenate([jnp.log(lb), jnp.log1p(-lb), 1.0 - lb, jnp.zeros((depth, 5, BRANCH), F32)],
                                 axis=1),
             hg_norm=row(hgrn_norm_w), w_up=w_up, b_gate=row(gla_b_gate), gla_norm=row(gla_norm_w),
             ret_norm=row(ret_norm_w), final_norm=final_norm_w[None, :].astype(F32))

    xp = x_prompt
    new_p = ([], [], [], [], [])
    for l in range(depth):
        xp, st_p = _prompt_layer(xp, l, p, consts, tabs, rope_p, l == depth - 1)
        for i in range(5):
            new_p[i].append(st_p[i])
    xs, new_s = _sample_step(x_sample.reshape(x_sample.shape[0], D_MODEL),
                             (state_s5_re, state_s5_im, state_hgrn, state_gla, state_ret),
                             p, consts, tabs, rope_s)
    return (xp, xs.reshape(x_sample.shape),
            jnp.stack(new_p[0]), jnp.stack(new_p[1]), jnp.stack(new_p[2]), jnp.stack(new_p[3]),
            jnp.stack(new_p[4])) + new_s
```

```python
import functools
import math

import numpy as np
import jax
import jax.numpy as jnp
from jax import lax
from jax.experimental import pallas as pl
from jax.experimental.pallas import tpu as pltpu

F32 = jnp.float32
BF16 = jnp.bfloat16

D_MODEL = 1024
BRANCH = 256
S5_CH = 16
S5_GROUPS = 16
S5_STATE = 64
S5_N = S5_GROUPS * S5_STATE
HEADS = 4
HG_DK = 64
GLA_DK = 32
RET_DK = 32
DV = 64
GLA_LOWRANK = 16
GLA_TAU = 16.0
ROPE_BASE = 10000.0
PAST_LEN = 16384
EPS = 1e-6
SUB = 16

LANES = 128
T_CHUNK = 64
PITCH = T_CHUNK + 8
SUB_TERM_ROWS = SUB * (SUB + SUB // 2) // 2
TERM_ROWS = (T_CHUNK // SUB) * SUB_TERM_ROWS
LOG2E = math.log2(math.e)
MASK_NEG = -1e30
VMEM_LIMIT = 60 * 1024 * 1024

C_U, C_SZ, C_HQ, C_HF, C_HI, C_HZ = 0, 256, 512, 768, 1024, 1280
C_GQ, C_GK, C_GV, C_GZ = 1536, 1664, 1792, 2048
C_RQ, C_RK, C_RV, C_RZ = 2304, 2432, 2560, 2816
C_LR = 3072
N_PACK = 3200
N_QUARTER = 4
QUARTER = C_LR // N_QUARTER


def _dot(a, b):
    return jnp.dot(a.astype(BF16), b.astype(BF16), preferred_element_type=F32)


def _dot_nt(a, b):
    return lax.dot_general(a.astype(BF16), b.astype(BF16), (((1,), (1,)), ((), ())),
                           preferred_element_type=F32)


def _dot_tn(a, b):
    return lax.dot_general(a.astype(BF16), b.astype(BF16), (((0,), (0,)), ((), ())),
                           preferred_element_type=F32)


def _split3(x):
    x1 = x.astype(BF16)
    r1 = x - x1.astype(F32)
    x2 = r1.astype(BF16)
    x3 = (r1 - x2.astype(F32)).astype(BF16)
    return x1, x2, x3


def _dot_sel_lhs(sel, x):
    x1, x2, x3 = _split3(x)
    d = lambda p: jnp.dot(sel, p, preferred_element_type=F32)
    return d(x1) + d(x2) + d(x3)


def _dot_sel_rhs2(x, sel):
    x1 = x.astype(BF16)
    x2 = (x - x1.astype(F32)).astype(BF16)
    return (jnp.dot(x1, sel, preferred_element_type=F32)
            + jnp.dot(x2, sel, preferred_element_type=F32))


def _dot3(a, b):
    a1 = a.astype(BF16)
    a2 = (a - a1.astype(F32)).astype(BF16)
    b1 = b.astype(BF16)
    b2 = (b - b1.astype(F32)).astype(BF16)
    d = lambda p, q: jnp.dot(p, q, preferred_element_type=F32)
    return d(a1, b1) + d(a1, b2) + d(a2, b1)


def _sigmoid(x):
    return 1.0 / (1.0 + jnp.exp(-x))


def _silu(x):
    return x * _sigmoid(x)


def _log_sigmoid(x):
    return jnp.minimum(x, 0.0) - jnp.log(1.0 + jnp.exp(-jnp.abs(x)))


def _gelu_tanh(x):
    return 0.5 * x * (1.0 + jnp.tanh(math.sqrt(2.0 / math.pi) * (x + 0.044715 * (x * x * x))))


def _rmsnorm_rows(x, w):
    return x * lax.rsqrt(jnp.mean(x * x, axis=-1, keepdims=True) + EPS) * w


def _head_rms(o, ones_h, gain):
    ms = _dot_sel_rhs2(o * o, ones_h) * (1.0 / DV)
    return o * lax.rsqrt(ms + EPS) * gain


def _head_ln(o, ones_h, gain):
    c = o - _dot_sel_rhs2(o, ones_h) * (1.0 / DV)
    var = _dot_sel_rhs2(c * c, ones_h) * (1.0 / DV)
    return c * lax.rsqrt(var + EPS) * gain


def _hgrn_gates(xf, loglb, log1mlb, one_m_lb):
    bterm = log1mlb + _log_sigmoid(xf)
    m = jnp.maximum(loglb, bterm)
    log_f = m + jnp.log(jnp.exp(loglb - m) + jnp.exp(bterm - m))
    return log_f, one_m_lb * _sigmoid(-xf)


def _rotary(t, cos, sin_signed, first_half):
    half = RET_DK // 2
    swapped = jnp.where(first_half, pltpu.roll(t, LANES - half, 1), pltpu.roll(t, half, 1))
    return t * cos + swapped * sin_signed


def _s5_output(y_lin, u, sz, dskip, wglu):
    y = _gelu_tanh(y_lin + u * dskip)
    y = y * _sigmoid(_dot(y, wglu))
    return y * _silu(sz)


def _mix_and_project(x, mix, wout, fnw, apply_final):
    out = x + _dot(mix, wout)
    if apply_final:
        out = _rmsnorm_rows(out, fnw)
    return out


class _GatedStream:
    def __init__(self, q, k, v, g, s_ref, ones_kv, mask_vk, tril, term_scr, kdim):
        self.q, self.k, self.v, self.g = q, k, v, g
        self.s_ref, self.ones_kv, self.mask_vk = s_ref, ones_kv, mask_vk
        self.tril, self.term_scr, self.kdim = tril, term_scr, kdim
        self.ws, self.inter, self.att = {}, {}, {}

    def cumulate(self):
        self.bc = _dot_sel_lhs(self.tril, self.g()) * LOG2E

    def _total(self, j):
        return self.bc[SUB * (j + 1) - 1:SUB * (j + 1), :]

    def update_part(self, j):
        sl = slice(SUB * j, SUB * (j + 1))
        self.ws[j] = _dot_tn(self.v()[sl], self.k()[sl] * jnp.exp2(self._total(j) - self.bc[sl]))

    def products_part(self, j):
        hs, term_scr, r0 = SUB // 2, self.term_scr, SUB * j
        q, k, bc = self.q()[r0:r0 + SUB], self.k()[r0:r0 + SUB], self.bc[r0:r0 + SUB]
        trow = lax.broadcasted_iota(jnp.int32, (hs, self.kdim), 0)
        neg = [jnp.where(trow >= d, 0.0, MASK_NEG) for d in range(1, hs)]
        q_lo, q_hi, b_lo, b_hi = q[0:hs], q[hs:SUB], bc[0:hs], bc[hs:SUB]
        row = j * SUB_TERM_ROWS
        for s in range(SUB):
            ks, bs = k[s:s + 1, :], bc[s:s + 1, :]
            d = s % hs
            q_dg, b_dg = (q_lo, b_lo) if s < hs else (q_hi, b_hi)
            e_dg = b_dg - bs if d == 0 else b_dg - bs + neg[d - 1]
            term_scr[row:row + hs, :] = q_dg * ks * jnp.exp2(e_dg)
            row += hs
            if s < hs:
                term_scr[row:row + hs, :] = q_hi * ks * jnp.exp2(b_hi - bs)
                row += hs

    def matmul_part(self, j):
        n_sub = T_CHUNK // SUB
        sl = slice(SUB * j, SUB * (j + 1))
        if j == 0:
            self.state = self.s_ref[...]
        self.inter[j] = _dot_nt(self.q()[sl] * jnp.exp2(self.bc[sl]), self.state)
        self.state = self.state * jnp.exp2(self._total(j)) + self.ws[j] * self.mask_vk
        if j == n_sub - 1:
            self.s_ref[...] = self.state
        rows = slice(j * SUB_TERM_ROWS, (j + 1) * SUB_TERM_ROWS)
        self.att[j] = _dot(self.term_scr[rows, :], self.ones_kv)

    def output_part(self, j):
        hs, r0 = SUB // 2, SUB * j
        v, att = self.v()[r0:r0 + SUB], self.att[j]
        acc_lo, acc_hi = self.inter[j][0:hs], self.inter[j][hs:SUB]
        row = 0
        for s in range(SUB):
            vs = v[s:s + 1, :]
            if s < hs:
                acc_lo = acc_lo + att[row:row + hs] * vs
                row += hs
            acc_hi = acc_hi + att[row:row + hs] * vs
            row += hs
        return jnp.concatenate([acc_lo, acc_hi], axis=0)


def _ret_chunk(q, k, v, cos, sin_signed, s_ref, dstack, inner, kdec, cdec, mask_vk):
    lane = lax.broadcasted_iota(jnp.int32, (T_CHUNK, HEADS * RET_DK), 1)
    first_half = (lane % RET_DK) < (RET_DK // 2)
    rq = _rotary(q, cos, sin_signed, first_half)
    rk = _rotary(k, cos, sin_signed, first_half) * (RET_DK ** -0.5)
    lane_v = lax.broadcasted_iota(jnp.int32, (T_CHUNK, HEADS * DV), 1)
    kst = jnp.concatenate([jnp.where(lane // RET_DK == h, rk, 0.0) for h in range(HEADS)], axis=0)
    vst = jnp.concatenate([jnp.where(lane_v // DV == h, v, 0.0) for h in range(HEADS)], axis=0)
    p = _dot_nt(rq, kst) * dstack
    s_t = s_ref[...]
    o = _dot(p, vst) + _dot_nt(rq, s_t) * inner
    s_ref[...] = s_t * cdec + _dot_tn(v, rk * kdec) * mask_vk
    return o


def _prompt_kernel(apply_final,
                   x_ref, xn_ref, normw_ref, win_ref, wout_ref, bbd_ref, cbd_ref, are_ref, aim_ref,
                   dskip_ref, wglu_ref, lbp_ref, hgn_ref, wup_ref, bgate_ref, glan_ref, retn_ref, cos_ref,
                   sin_ref, dstack_ref, inner_ref, kdec_ref, cdec_ref, onesh_ref, onesg_ref, tril_ref,
                   fnw_ref,
                   y_ref, s5_ref, hgs_ref, glas_ref, rets_ref,
                   proj_scr, nxt_scr, lr_scr, lrn_scr, hk_scr, hn_scr, bu_scr, mix_scr,
                   term_h0, term_g0, term_h1, term_g1):
    nb = x_ref.shape[0]
    rows = nb * T_CHUNK
    n_slab = 2 * S5_N // LANES
    half = n_slab // 2
    step = pl.program_id(0)

    def w_quarter(qtr):
        return win_ref[:, qtr * QUARTER:(qtr + 1) * QUARTER]

    def proj(col, width, rs=slice(None)):
        return proj_scr[col // QUARTER, rs, col % QUARTER:col % QUARTER + width]

    @pl.when(step == 0)
    def _init():
        s5_ref[...] = jnp.zeros_like(s5_ref)
        hgs_ref[...] = jnp.zeros_like(hgs_ref)
        glas_ref[...] = jnp.zeros_like(glas_ref)
        rets_ref[...] = jnp.zeros_like(rets_ref)
        bu_scr[...] = jnp.zeros_like(bu_scr)
        h0 = _rmsnorm_rows(x_ref[...].reshape(rows, D_MODEL), normw_ref[...]).astype(BF16)
        for qtr in range(N_QUARTER):
            proj_scr[qtr] = jnp.dot(h0, w_quarter(qtr), preferred_element_type=F32)
        lr_scr[...] = jnp.dot(h0, win_ref[:, C_LR:N_PACK], preferred_element_type=F32)

    @pl.when(step > 0)
    def _advance():
        proj_scr[...] = nxt_scr[...]
        lr_scr[...] = lrn_scr[...]

    hn_scr[...] = _rmsnorm_rows(xn_ref[...].reshape(rows, D_MODEL), normw_ref[...]).astype(BF16)
    lrn_scr[...] = jnp.dot(hn_scr[...], win_ref[:, C_LR:N_PACK], preferred_element_type=F32)

    def project_next(qtr):
        nxt_scr[qtr] = jnp.dot(hn_scr[...], w_quarter(qtr), preferred_element_type=F32)

    u = proj(C_U, BRANCH)
    for c2 in range(n_slab // 2):
        bu = _dot(u, bbd_ref[:, 2 * c2 * LANES:2 * (c2 + 1) * LANES])
        for cc in range(2):
            for b in range(nb):
                bu_scr[2 * c2 + cc, b * PITCH:b * PITCH + T_CHUNK, :] = bu[b * T_CHUNK:(b + 1) * T_CHUNK,
                                                                           cc * LANES:(cc + 1) * LANES]
    a_re = [jnp.broadcast_to(are_ref[:, c * LANES:(c + 1) * LANES], (nb, LANES)) for c in range(half)]
    a_im = [jnp.broadcast_to(aim_ref[:, c * LANES:(c + 1) * LANES], (nb, LANES)) for c in range(half)]
    s_init = s5_ref[...]
    carry0 = tuple(s_init[:, c * LANES:(c + 1) * LANES] for c in range(n_slab))

    def scan_step(t, carry):
        new = [None] * n_slab
        for c in range(half):
            sr, si = carry[c], carry[half + c]
            br = bu_scr[c, pl.ds(t, nb, stride=PITCH), :]
            bi = bu_scr[half + c, pl.ds(t, nb, stride=PITCH), :]
            nr = a_re[c] * sr - a_im[c] * si + br
            ni = a_re[c] * si + a_im[c] * sr + bi
            bu_scr[c, pl.ds(t, nb, stride=PITCH), :] = nr
            bu_scr[half + c, pl.ds(t, nb, stride=PITCH), :] = ni
            new[c], new[half + c] = nr, ni
        return tuple(new)

    carry = lax.fori_loop(0, T_CHUNK, scan_step, carry0)
    s5_ref[...] = jnp.concatenate(carry, axis=1)
    project_next(0)
    s_all = jnp.concatenate([bu_scr[c] for c in range(n_slab)], axis=1)
    y_all = _dot(s_all, cbd_ref[...])
    y_lin = jnp.concatenate([y_all[b * PITCH:b * PITCH + T_CHUNK] for b in range(nb)], axis=0)
    mix_scr[:, 0:BRANCH] = _s5_output(y_lin, u, proj(C_SZ, BRANCH), dskip_ref[...], wglu_ref[...])

    project_next(1)
    lbp = lbp_ref[...]
    log_f, hk = _hgrn_gates(proj(C_HF, BRANCH), lbp[0:1, :], lbp[1:2, :], lbp[2:3, :])
    proj_scr[C_HF // QUARTER, :, C_HF % QUARTER:C_HF % QUARTER + BRANCH] = log_f
    hk_scr[...] = hk
    g_gla = _log_sigmoid(_dot3(lr_scr[...], wup_ref[...]) + bgate_ref[...])
    lr_scr[...] = g_gla * (1.0 / GLA_TAU)

    ones_h = onesh_ref[...]
    ones_g = onesg_ref[...]
    mask_h = ones_h.astype(F32)
    mask_g = jnp.transpose(ones_g.astype(F32))
    tril = tril_ref[...]

    def rows_of(bp, odd):
        return pl.ds(pl.multiple_of((2 * bp + odd) * T_CHUNK, T_CHUNK), T_CHUNK)

    def hgrn_stream(bp, odd, term):
        rs = rows_of(bp, odd)
        return _GatedStream(lambda: proj(C_HQ, BRANCH, rs), lambda: hk_scr[rs, :], lambda: proj(C_HI, BRANCH, rs),
                            lambda: proj(C_HF, BRANCH, rs), hgs_ref.at[bp, odd], ones_h, mask_h, tril, term,
                            HEADS * HG_DK)

    def gla_stream(bp, odd, term):
        rs = rows_of(bp, odd)
        return _GatedStream(lambda: proj(C_GQ, LANES, rs) * (GLA_DK ** -0.5), lambda: proj(C_GK, LANES, rs),
                            lambda: proj(C_GV, BRANCH, rs), lambda: lr_scr[rs, :], glas_ref.at[bp, odd],
                            ones_g, mask_g, tril, term, HEADS * GLA_DK)

    n_sub = T_CHUNK // SUB

    def retention(bp, odd):
        rs = rows_of(bp, odd)
        mix_scr[rs, 3 * BRANCH:4 * BRANCH] = _ret_chunk(
            proj(C_RQ, LANES, rs), proj(C_RK, LANES, rs), proj(C_RV, BRANCH, rs), cos_ref[...],
            sin_ref[...], rets_ref.at[bp, odd], dstack_ref[...], inner_ref[...], kdec_ref[...],
            cdec_ref[...], mask_g)

    def per_pair(bp, _):
        st = [hgrn_stream(bp, 0, term_h0), gla_stream(bp, 0, term_g0),
              hgrn_stream(bp, 1, term_h1), gla_stream(bp, 1, term_g1)]
        slots = [(0, BRANCH), (0, 2 * BRANCH), (1, BRANCH), (1, 2 * BRANCH)]

        def emit(i, j):
            odd, col = slots[i]
            r0 = pl.multiple_of((2 * bp + odd) * T_CHUNK + SUB * j, SUB)
            mix_scr[pl.ds(r0, SUB), col:col + BRANCH] = st[i].output_part(j)

        for s_ in st:
            s_.cumulate()
        for j in range(n_sub):
            st[0].update_part(j)
        for j in range(n_sub):
            st[0].products_part(j)
            st[1].update_part(j)
        for j in range(n_sub):
            st[1].products_part(j)
            st[2].update_part(j)
        for j in range(n_sub):
            st[2].products_part(j)
            st[3].update_part(j)
            st[0].matmul_part(j)
        for j in range(n_sub):
            st[3].products_part(j)
            st[1].matmul_part(j)
        for j in range(n_sub):
            emit(0, j)
            st[2].matmul_part(j)
        for j in range(n_sub):
            emit(1, j)
            st[3].matmul_part(j)
        for j in range(n_sub):
            emit(2, j)
            if j % 2 == 0:
                retention(bp, j // 2)
        for j in range(n_sub):
            emit(3, j)
        return 0

    lax.fori_loop(0, N_QUARTER, per_pair, 0)

    project_next(2)
    o_hg = _head_rms(mix_scr[:, BRANCH:2 * BRANCH], ones_h, hgn_ref[...])
    mix_scr[:, BRANCH:2 * BRANCH] = o_hg * _silu(proj(C_HZ, BRANCH))
    project_next(3)
    o_gla = _head_rms(mix_scr[:, 2 * BRANCH:3 * BRANCH], ones_h, glan_ref[...])
    mix_scr[:, 2 * BRANCH:3 * BRANCH] = o_gla * _silu(proj(C_GZ, BRANCH))
    o_ret = _head_ln(mix_scr[:, 3 * BRANCH:4 * BRANCH], ones_h, retn_ref[...])
    mix_scr[:, 3 * BRANCH:4 * BRANCH] = o_ret * _silu(proj(C_RZ, BRANCH))
    out = _mix_and_project(x_ref[...].reshape(rows, D_MODEL), mix_scr[...], wout_ref[...], fnw_ref[...],
                           apply_final)
    y_ref[...] = out.reshape(nb, T_CHUNK, D_MODEL)


def _dot_sel_lhs2(sel, x):
    x1 = x.astype(BF16)
    x2 = (x - x1.astype(F32)).astype(BF16)
    return (jnp.dot(sel, x1, preferred_element_type=F32) + jnp.dot(sel, x2, preferred_element_type=F32))


def _head_rms_t(o, ones_h, gain):
    ms = _dot_sel_lhs2(ones_h, o * o) * (1.0 / DV)
    return o * lax.rsqrt(ms + EPS) * gain


def _head_ln_t(o, ones_h, gain):
    c = o - _dot_sel_lhs2(ones_h, o) * (1.0 / DV)
    var = _dot_sel_lhs2(ones_h, c * c) * (1.0 / DV)
    return c * lax.rsqrt(var + EPS) * gain


def _rotary_t(t, cos, sin_signed, first_half):
    half = RET_DK // 2
    swapped = jnp.where(first_half, pltpu.roll(t, LANES - half, 0), pltpu.roll(t, half, 0))
    return t * cos + swapped * sin_signed


def _sample_kernel(x_ref, normw_ref, wt_ref, wout_ref, bbdt_ref, cbdt_ref, are_ref, aim_ref, dskip_ref,
                   wglut_ref, lbp_ref, hgn_ref, wupt_ref, bgate_ref, glan_ref, retn_ref, fnw_ref,
                   cos_ref, sin_ref, dret_ref, onesh_ref,
                   s5re_ref, s5im_ref, hg_ref, gla_ref, ret_ref,
                   y_ref, s5re_o, s5im_o, hg_o, gla_o, ret_o,
                   xs_scr, pt_scr, hk_scr, ot_scr, mixt_scr):
    layer, head = pl.program_id(0), pl.program_id(1)
    last_layer, last_head = pl.num_programs(0) - 1, pl.num_programs(1) - 1

    @pl.when((layer == 0) & (head == 0))
    def _load_x():
        xs_scr[...] = x_ref[...]

    @pl.when(head == 0)
    def _dense():
        hh = _rmsnorm_rows(xs_scr[...], normw_ref[...]).astype(BF16)
        pt_scr[...] = lax.dot_general(wt_ref[...], hh, (((1,), (1,)), ((), ())), preferred_element_type=F32)

        u = pt_scr[C_U:C_U + BRANCH, :]
        bu = _dot3(bbdt_ref[...], u)
        a_re, a_im = are_ref[...], aim_ref[...]
        s0r, s0i = s5re_ref[...], s5im_ref[...]
        s_re = a_re * s0r - a_im * s0i + bu[0:S5_N]
        s_im = a_re * s0i + a_im * s0r + bu[S5_N:2 * S5_N]
        s5re_o[...] = s_re
        s5im_o[...] = s_im
        y = _gelu_tanh(_dot3(cbdt_ref[...], jnp.concatenate([s_re, s_im], axis=0)) + u * dskip_ref[...])
        y = y * _sigmoid(jnp.dot(wglut_ref[...], y.astype(BF16), preferred_element_type=F32))
        mixt_scr[0:BRANCH, :] = y * _silu(pt_scr[C_SZ:C_SZ + BRANCH, :])

        lbp = lbp_ref[...]
        log_f, hk = _hgrn_gates(pt_scr[C_HF:C_HF + BRANCH, :], lbp[:, 0:1], lbp[:, 1:2], lbp[:, 2:3])
        pt_scr[C_HF:C_HF + BRANCH, :] = jnp.exp(log_f)
        hk_scr[...] = hk
        g_gla = _log_sigmoid(_dot3(wupt_ref[...], pt_scr[C_LR:C_LR + LANES, :]) + bgate_ref[...])
        pt_scr[C_LR:C_LR + LANES, :] = jnp.exp(g_gla * (1.0 / GLA_TAU))
        pt_scr[C_GQ:C_GQ + LANES, :] = pt_scr[C_GQ:C_GQ + LANES, :] * (GLA_DK ** -0.5)
        row = lax.broadcasted_iota(jnp.int32, (HEADS * RET_DK, LANES), 0)
        first_half = (row % RET_DK) < (RET_DK // 2)
        pt_scr[C_RQ:C_RQ + LANES, :] = _rotary_t(pt_scr[C_RQ:C_RQ + LANES, :], cos_ref[...], sin_ref[...],
                                                 first_half)
        pt_scr[C_RK:C_RK + LANES, :] = _rotary_t(pt_scr[C_RK:C_RK + LANES, :], cos_ref[...], sin_ref[...],
                                                 first_half) * (RET_DK ** -0.5)

    def head_update(s0_ref, s_out_ref, dk, dec_ref, dec_row, key_ref, key_row, q_row, v_row, out_row):
        vt = pt_scr[pl.ds(pl.multiple_of(v_row + head * DV, DV), DV), :]

        def feature(kk, acc):
            r = head * dk + kk
            bcast = lambda ref, r0: jnp.broadcast_to(ref[pl.ds(r0 + r, 1), :], (DV, LANES))
            s_new = s0_ref[kk] * bcast(dec_ref, dec_row) + bcast(key_ref, key_row) * vt
            s_out_ref[kk] = s_new
            return acc + bcast(pt_scr, q_row) * s_new

        acc = lax.fori_loop(0, dk, feature, jnp.zeros((DV, LANES), F32), unroll=4)
        ot_scr[pl.ds(pl.multiple_of(out_row + head * DV, DV), DV), :] = acc

    head_update(hg_ref, hg_o, HG_DK, pt_scr, C_HF, hk_scr, 0, C_HQ, C_HI, 0)
    head_update(gla_ref, gla_o, GLA_DK, pt_scr, C_LR, pt_scr, C_GK, C_GQ, C_GV, BRANCH)
    head_update(ret_ref, ret_o, RET_DK, dret_ref, 0, pt_scr, C_RK, C_RQ, C_RV, 2 * BRANCH)

    @pl.when(head == last_head)
    def _finish():
        ones_h = onesh_ref[...]
        o_hg = _head_rms_t(ot_scr[0:BRANCH, :], ones_h, hgn_ref[...])
        mixt_scr[BRANCH:2 * BRANCH, :] = o_hg * _silu(pt_scr[C_HZ:C_HZ + BRANCH, :])
        o_gla = _head_rms_t(ot_scr[BRANCH:2 * BRANCH, :], ones_h, glan_ref[...])
        mixt_scr[2 * BRANCH:3 * BRANCH, :] = o_gla * _silu(pt_scr[C_GZ:C_GZ + BRANCH, :])
        o_ret = _head_ln_t(ot_scr[2 * BRANCH:3 * BRANCH, :], ones_h, retn_ref[...])
        mixt_scr[3 * BRANCH:4 * BRANCH, :] = o_ret * _silu(pt_scr[C_RZ:C_RZ + BRANCH, :])
        out = xs_scr[...] + lax.dot_general(mixt_scr[...].astype(BF16), wout_ref[...], (((0,), (0,)), ((), ())),
                                            preferred_element_type=F32)
        xs_scr[...] = out

        @pl.when(layer == last_layer)
        def _emit():
            y_ref[...] = _rmsnorm_rows(out, fnw_ref[...])


def _ret_log_gamma():
    return jnp.log1p(-jnp.exp2(-5.0 - jnp.arange(HEADS, dtype=F32)))


def _constants():
    ones_h = (np.arange(BRANCH)[:, None] // DV == np.arange(BRANCH)[None, :] // DV)
    ones_g = (np.arange(HEADS * GLA_DK)[:, None] // GLA_DK == np.arange(BRANCH)[None, :] // DV)
    r = np.arange(T_CHUNK)
    same_sub = r[:, None] // SUB == r[None, :] // SUB
    tril = same_sub & (r[None, :] <= r[:, None])
    same_group_b = np.arange(BRANCH)[:, None] // S5_CH == np.arange(S5_N)[None, :] // S5_STATE
    as_bf16 = lambda m: jnp.asarray(m.astype(np.float32), dtype=BF16)
    return dict(ones_h=as_bf16(ones_h), ones_g=as_bf16(ones_g), tril=as_bf16(tril),
                s5_mask=jnp.asarray(same_group_b.astype(np.float32)))


def _ret_tables():
    lg = _ret_log_gamma()
    idx = jnp.arange(T_CHUNK, dtype=F32)
    rel = idx[:, None] - idx[None, :]
    causal = rel >= 0
    decay = jnp.where(causal[None], jnp.exp(jnp.where(causal, rel, 0.0)[None] * lg[:, None, None]), 0.0)
    dstack = jnp.transpose(decay, (1, 0, 2)).reshape(T_CHUNK, HEADS * T_CHUNK)
    inner = jnp.repeat(jnp.exp((idx[:, None] + 1.0) * lg[None, :]), DV, axis=1)
    kdec = jnp.repeat(jnp.exp((T_CHUNK - 1.0 - idx[:, None]) * lg[None, :]), RET_DK, axis=1)
    cdec = jnp.repeat(jnp.exp(T_CHUNK * lg)[None, :], RET_DK, axis=1)
    dret = jnp.broadcast_to(jnp.repeat(jnp.exp(lg), RET_DK)[:, None], (HEADS * RET_DK, LANES))
    return dstack, inner, kdec, cdec, dret


def _rope_tables(pos):
    half = RET_DK // 2
    inv = ROPE_BASE ** (-jnp.arange(half, dtype=F32) / half)
    ang = pos.astype(F32)[:, None] * inv[None, :]
    cos, sin = jnp.cos(ang), jnp.sin(ang)
    cos_t = jnp.tile(jnp.concatenate([cos, cos], axis=1), (1, HEADS))
    sin_t = jnp.tile(jnp.concatenate([-sin, sin], axis=1), (1, HEADS))
    return cos_t, sin_t


def _pack_w_in(w):
    offs = np.cumsum([0, 256, 256, 256, 256, 256, 256, 128, 128, 256, 16, 256, 128, 128, 256, 256])
    seg = lambda i: w[:, :, int(offs[i]):int(offs[i + 1])]
    order = [0, 1, 2, 3, 4, 5, 6, 7, 8, 10, 11, 12, 13, 14]
    pad = jnp.zeros(w.shape[:2] + (LANES - GLA_LOWRANK,), w.dtype)
    return jnp.concatenate([seg(i) for i in order] + [seg(9), pad], axis=2).astype(BF16)


def _s5_discretize(lam_re, lam_im, log_step, b_re, b_im, c_re, c_im, mask):
    lr, li = lam_re.astype(F32), lam_im.astype(F32)
    step = jnp.exp(log_step.astype(F32))[..., None]
    mag = jnp.exp(lr * step)
    ab_re = mag * jnp.cos(li * step)
    ab_im = mag * jnp.sin(li * step)
    den = lr * lr + li * li
    nr = ab_re - 1.0
    f_re = (nr * lr + ab_im * li) / den
    f_im = (ab_im * lr - nr * li) / den
    br, bi = b_re.astype(F32), b_im.astype(F32)
    bb_re = f_re[..., None] * br - f_im[..., None] * bi
    bb_im = f_re[..., None] * bi + f_im[..., None] * br
    nl = lr.shape[0]

    def drive(bb):
        rows = jnp.transpose(bb, (0, 1, 3, 2)).reshape(nl, BRANCH, S5_STATE)
        return jnp.tile(rows, (1, 1, S5_GROUPS)) * mask

    def readout(cc):
        rows = cc.astype(F32).reshape(nl, BRANCH, S5_STATE)
        return jnp.tile(rows, (1, 1, S5_GROUPS)) * mask

    bbd = jnp.concatenate([drive(bb_re), drive(bb_im)], axis=2)
    cbd_t = jnp.concatenate([readout(c_re), -readout(c_im)], axis=2)
    return bbd, cbd_t, ab_re.reshape(nl, 1, S5_N), ab_im.reshape(nl, 1, S5_N)


def _full(shape):
    return pl.BlockSpec(shape, lambda *_: (0,) * len(shape), pipeline_mode=pl.Buffered(1))


def _of_layer(arr, layer):
    nd = arr.ndim - 1
    return pl.BlockSpec((None,) + arr.shape[1:], lambda *_: (layer,) + (0,) * nd,
                        pipeline_mode=pl.Buffered(1))


def _prompt_layer(x, layer, p, consts, tabs, rope, apply_final):
    nb, seq, _ = x.shape
    n_steps = seq // T_CHUNK
    rows = nb * T_CHUNK
    dstack, inner, kdec, cdec, _ = tabs
    cos_t, sin_t = rope
    per_layer = [p['norm_w'], p['w_pack'], p['w_out'], p['bbd'].astype(BF16), p['cbd'],
                 p['a_re'], p['a_im'],
                 p['d_skip'], p['w_glu'], p['lbp'], p['hg_norm'], p['w_up'], p['b_gate'], p['gla_norm'],
                 p['ret_norm']]
    shared = [dstack, inner, kdec, cdec, consts['ones_h'], consts['ones_g'], consts['tril'],
              p['final_norm']]
    in_specs = [pl.BlockSpec((nb, T_CHUNK, D_MODEL), lambda i: (0, i, 0)),
                pl.BlockSpec((nb, T_CHUNK, D_MODEL), lambda i: (0, jnp.minimum(i + 1, n_steps - 1), 0))]
    in_specs += [_of_layer(a, layer) for a in per_layer]
    in_specs += [pl.BlockSpec((T_CHUNK, LANES), lambda i: (i, 0)), pl.BlockSpec((T_CHUNK, LANES), lambda i: (i, 0))]
    in_specs += [_full(a.shape) for a in shared]
    out_shape = (jax.ShapeDtypeStruct((nb, seq, D_MODEL), F32),
                 jax.ShapeDtypeStruct((nb, 2 * S5_N), F32),
                 jax.ShapeDtypeStruct((nb // 2, 2, BRANCH, HEADS * HG_DK), F32),
                 jax.ShapeDtypeStruct((nb // 2, 2, BRANCH, HEADS * GLA_DK), F32),
                 jax.ShapeDtypeStruct((nb // 2, 2, BRANCH, HEADS * RET_DK), F32))
    out_specs = (pl.BlockSpec((nb, T_CHUNK, D_MODEL), lambda i: (0, i, 0)),
                 pl.BlockSpec((nb, 2 * S5_N), lambda i: (0, 0)),
                 pl.BlockSpec((nb // 2, 2, BRANCH, HEADS * HG_DK), lambda i: (0, 0, 0, 0)),
                 pl.BlockSpec((nb // 2, 2, BRANCH, HEADS * GLA_DK), lambda i: (0, 0, 0, 0)),
                 pl.BlockSpec((nb // 2, 2, BRANCH, HEADS * RET_DK), lambda i: (0, 0, 0, 0)))
    scratch = [pltpu.VMEM((N_QUARTER, rows, QUARTER), F32), pltpu.VMEM((N_QUARTER, rows, QUARTER), F32),
               pltpu.VMEM((rows, LANES), F32), pltpu.VMEM((rows, LANES), F32),
               pltpu.VMEM((rows, BRANCH), F32), pltpu.VMEM((rows, D_MODEL), BF16),
               pltpu.VMEM((2 * S5_N // LANES, nb * PITCH, LANES), F32),
               pltpu.VMEM((rows, D_MODEL), F32),
               pltpu.VMEM((TERM_ROWS, HEADS * HG_DK), F32), pltpu.VMEM((TERM_ROWS, HEADS * GLA_DK), F32),
               pltpu.VMEM((TERM_ROWS, HEADS * HG_DK), F32), pltpu.VMEM((TERM_ROWS, HEADS * GLA_DK), F32)]
    y, s5, hgs, glas, rets = pl.pallas_call(
        functools.partial(_prompt_kernel, apply_final),
        grid=(n_steps,), in_specs=in_specs, out_specs=out_specs, out_shape=out_shape,
        scratch_shapes=scratch, name='prompt_layer',
        compiler_params=pltpu.CompilerParams(dimension_semantics=('arbitrary',),
                                             vmem_limit_bytes=VMEM_LIMIT),
    )(x, x, *per_layer, cos_t, sin_t, *shared)
    s5 = s5.reshape(nb, 2, S5_GROUPS, S5_STATE)

    def unstack(st, dk):
        st = st.reshape(nb, HEADS, DV, HEADS, dk)
        diag = jnp.stack([st[:, hh, :, hh, :] for hh in range(HEADS)], axis=1)
        return jnp.transpose(diag, (0, 1, 3, 2))

    return y, (s5[:, 0], s5[:, 1], unstack(hgs, HG_DK), unstack(glas, GLA_DK), unstack(rets, RET_DK))


def _sample_step(x, states, p, consts, tabs, rope):
    nb = x.shape[0]
    depth = p['w_t'].shape[0]
    s5re, s5im, hg, gla, ret = states
    to_lanes = lambda s: jnp.moveaxis(s, 1, -1)
    s5re_t = to_lanes(s5re).reshape(depth, S5_N, nb)
    s5im_t = to_lanes(s5im).reshape(depth, S5_N, nb)
    hg_t, gla_t, ret_t = to_lanes(hg), to_lanes(gla), to_lanes(ret)
    cos_t, sin_t = rope
    col = lambda a: jnp.swapaxes(a, -1, -2)
    per_layer = [p['norm_w'], p['w_t'], p['w_out'], col(p['bbd']), p['cbd_t'], col(p['a_re']), col(p['a_im']),
                 col(p['d_skip']), col(p['w_glu']), col(p['lbp']), col(p['hg_norm']), col(p['w_up']),
                 col(p['b_gate']), col(p['gla_norm']), col(p['ret_norm'])]
    shared = [p['final_norm'], col(cos_t), col(sin_t), tabs[4], consts['ones_h']]
    layer_spec = lambda a: pl.BlockSpec((None,) + a.shape[1:], lambda l, h: (l,) + (0,) * (a.ndim - 1))
    head_spec = lambda a: pl.BlockSpec((None, None) + a.shape[2:], lambda l, h: (l, h) + (0,) * (a.ndim - 2))
    state_specs = [layer_spec(s5re_t), layer_spec(s5im_t), head_spec(hg_t), head_spec(gla_t), head_spec(ret_t)]
    in_specs = ([_full(x.shape)] + [layer_spec(a) for a in per_layer] + [_full(a.shape) for a in shared]
                + state_specs)
    state_arrays = [s5re_t, s5im_t, hg_t, gla_t, ret_t]
    out_shape = tuple([jax.ShapeDtypeStruct(x.shape, F32)]
                      + [jax.ShapeDtypeStruct(a.shape, F32) for a in state_arrays])
    out_specs = tuple([pl.BlockSpec(x.shape, lambda l, h: (0, 0))] + state_specs)
    scratch = [pltpu.VMEM((nb, D_MODEL), F32), pltpu.VMEM((N_PACK, nb), F32), pltpu.VMEM((BRANCH, nb), F32),
               pltpu.VMEM((3 * BRANCH, nb), F32), pltpu.VMEM((4 * BRANCH, nb), F32)]
    outs = pl.pallas_call(
        _sample_kernel, grid=(depth, HEADS), in_specs=in_specs, out_specs=out_specs, out_shape=out_shape,
        scratch_shapes=scratch, name='sample_step',
        compiler_params=pltpu.CompilerParams(dimension_semantics=('arbitrary', 'arbitrary'),
                                             vmem_limit_bytes=VMEM_LIMIT),
    )(x, *per_layer, *shared, *state_arrays)
    from_lanes = lambda s: jnp.moveaxis(s, -1, 1)
    new = (from_lanes(outs[1].reshape(depth, S5_GROUPS, S5_STATE, nb)),
           from_lanes(outs[2].reshape(depth, S5_GROUPS, S5_STATE, nb)),
           from_lanes(outs[3]), from_lanes(outs[4]), from_lanes(outs[5]))
    return outs[0], new


def kernel(x_prompt, x_sample, state_s5_re, state_s5_im, state_hgrn, state_gla, state_ret, norm_w, final_norm_w, w_in, w_out, s5_lam_re, s5_lam_im, s5_log_step, s5_b_re, s5_b_im, s5_c_re, s5_c_im, s5_d, s5_w_glu, hgrn_lb_logits, hgrn_norm_w, gla_w_gate_up, gla_b_gate, gla_norm_w, ret_norm_w):
    depth = w_in.shape[0]
    seq = x_prompt.shape[1]
    consts = _constants()
    tabs = _ret_tables()
    rope_p = _rope_tables(jnp.arange(seq))
    rope_s = _rope_tables(PAST_LEN + jnp.arange(1))
    lb = jnp.cumsum(jax.nn.softmax(hgrn_lb_logits.astype(F32), axis=0), axis=0)
    lb = (lb - lb[0:1])[:, None, :]
    bbd, cbd_t, a_re, a_im = _s5_discretize(s5_lam_re, s5_lam_im, s5_log_step, s5_b_re, s5_b_im, s5_c_re,
                                            s5_c_im, consts['s5_mask'])
    w_pack = _pack_w_in(w_in)
    row = lambda a: a[:, None, :].astype(F32)
    w_up = jnp.zeros((depth, LANES, HEADS * GLA_DK), F32).at[:, :GLA_LOWRANK].set(gla_w_gate_up.astype(F32))
    p = dict(norm_w=row(norm_w),
             w_pack=w_pack, w_t=jnp.swapaxes(w_pack, 1, 2), w_out=w_out.astype(BF16),
             bbd=bbd, cbd=jnp.swapaxes(cbd_t, 1, 2).astype(BF16), cbd_t=cbd_t, a_re=a_re, a_im=a_im,
             d_skip=row(s5_d), w_glu=s5_w_glu.astype(BF16),
             lbp=jnp.concatenate([jnp.log(lb), jnp.log1p(-lb), 1.0 - lb, jnp.zeros((depth, 5, BRANCH), F32)],
                                 axis=1),
             hg_norm=row(hgrn_norm_w), w_up=w_up, b_gate=row(gla_b_gate), gla_norm=row(gla_norm_w),
             ret_norm=row(ret_norm_w), final_norm=final_norm_w[None, :].astype(F32))

    xp = x_prompt
    new_p = ([], [], [], [], [])
    for l in range(depth):
        xp, st_p = _prompt_layer(xp, l, p, consts, tabs, rope_p, l == depth - 1)
        for i in range(5):
            new_p[i].append(st_p[i])
    xs, new_s = _sample_step(x_sample.reshape(x_sample.shape[0], D_MODEL),
                             (state_s5_re, state_s5_im, state_hgrn, state_gla, state_ret),
                             p, consts, tabs, rope_s)
    return (xp, xs.reshape(x_sample.shape),
            jnp.stack(new_p[0]), jnp.stack(new_p[1]), jnp.stack(new_p[2]), jnp.stack(new_p[3]),
            jnp.stack(new_p[4])) + new_s
```

```python
import functools
import math

import numpy as np
import jax
import jax.numpy as jnp
from jax import lax
from jax.experimental import pallas as pl
from jax.experimental.pallas import tpu as pltpu

F32 = jnp.float32
BF16 = jnp.bfloat16

D_MODEL = 1024
BRANCH = 256
S5_CH = 16
S5_GROUPS = 16
S5_STATE = 64
S5_N = S5_GROUPS * S5_STATE
HEADS = 4
HG_DK = 64
GLA_DK = 32
RET_DK = 32
DV = 64
GLA_LOWRANK = 16
GLA_TAU = 16.0
ROPE_BASE = 10000.0
PAST_LEN = 16384
EPS = 1e-6
SUB = 16

LANES = 128
T_CHUNK = 64
PITCH = T_CHUNK + 8
SUB_TERM_ROWS = SUB * (SUB + SUB // 2) // 2
TERM_ROWS = (T_CHUNK // SUB) * SUB_TERM_ROWS
LOG2E = math.log2(math.e)
MASK_NEG = -1e30
VMEM_LIMIT = 60 * 1024 * 1024

C_U, C_SZ, C_HQ, C_HF, C_HI, C_HZ = 0, 256, 512, 768, 1024, 1280
C_GQ, C_GK, C_GV, C_GZ = 1536, 1664, 1792, 2048
C_RQ, C_RK, C_RV, C_RZ = 2304, 2432, 2560, 2816
C_LR = 3072
N_PACK = 3200
N_QUARTER = 4
QUARTER = C_LR // N_QUARTER


def _dot(a, b):
    return jnp.dot(a.astype(BF16), b.astype(BF16), preferred_element_type=F32)


def _dot_nt(a, b):
    return lax.dot_general(a.astype(BF16), b.astype(BF16), (((1,), (1,)), ((), ())),
                           preferred_element_type=F32)


def _dot_tn(a, b):
    return lax.dot_general(a.astype(BF16), b.astype(BF16), (((0,), (0,)), ((), ())),
                           preferred_element_type=F32)


def _split3(x):
    x1 = x.astype(BF16)
    r1 = x - x1.astype(F32)
    x2 = r1.astype(BF16)
    x3 = (r1 - x2.astype(F32)).astype(BF16)
    return x1, x2, x3


def _dot_sel_lhs(sel, x):
    x1, x2, x3 = _split3(x)
    d = lambda p: jnp.dot(sel, p, preferred_element_type=F32)
    return d(x1) + d(x2) + d(x3)


def _dot_sel_rhs2(x, sel):
    x1 = x.astype(BF16)
    x2 = (x - x1.astype(F32)).astype(BF16)
    return (jnp.dot(x1, sel, preferred_element_type=F32)
            + jnp.dot(x2, sel, preferred_element_type=F32))


def _dot3(a, b):
    a1 = a.astype(BF16)
    a2 = (a - a1.astype(F32)).astype(BF16)
    b1 = b.astype(BF16)
    b2 = (b - b1.astype(F32)).astype(BF16)
    d = lambda p, q: jnp.dot(p, q, preferred_element_type=F32)
    return d(a1, b1) + d(a1, b2) + d(a2, b1)


def _sigmoid(x):
    return 1.0 / (1.0 + jnp.exp(-x))


def _silu(x):
    return x * _sigmoid(x)


def _log_sigmoid(x):
    return jnp.minimum(x, 0.0) - jnp.log(1.0 + jnp.exp(-jnp.abs(x)))


def _gelu_tanh(x):
    return 0.5 * x * (1.0 + jnp.tanh(math.sqrt(2.0 / math.pi) * (x + 0.044715 * (x * x * x))))


def _rmsnorm_rows(x, w):
    return x * lax.rsqrt(jnp.mean(x * x, axis=-1, keepdims=True) + EPS) * w


def _head_rms(o, ones_h, gain):
    ms = _dot_sel_rhs2(o * o, ones_h) * (1.0 / DV)
    return o * lax.rsqrt(ms + EPS) * gain


def _head_ln(o, ones_h, gain):
    c = o - _dot_sel_rhs2(o, ones_h) * (1.0 / DV)
    var = _dot_sel_rhs2(c * c, ones_h) * (1.0 / DV)
    return c * lax.rsqrt(var + EPS) * gain


def _hgrn_gates(xf, loglb, log1mlb, one_m_lb):
    bterm = log1mlb + _log_sigmoid(xf)
    m = jnp.maximum(loglb, bterm)
    log_f = m + jnp.log(jnp.exp(loglb - m) + jnp.exp(bterm - m))
    return log_f, one_m_lb * _sigmoid(-xf)


def _rotary(t, cos, sin_signed, first_half):
    half = RET_DK // 2
    swapped = jnp.where(first_half, pltpu.roll(t, LANES - half, 1), pltpu.roll(t, half, 1))
    return t * cos + swapped * sin_signed


def _s5_output(y_lin, u, sz, dskip, wglu):
    y = _gelu_tanh(y_lin + u * dskip)
    y = y * _sigmoid(_dot(y, wglu))
    return y * _silu(sz)


def _mix_and_project(x, mix, wout, fnw, apply_final):
    out = x + _dot(mix, wout)
    if apply_final:
        out = _rmsnorm_rows(out, fnw)
    return out


class _GatedStream:
    def __init__(self, q, k, v, g, s_ref, ones_kv, mask_vk, tril, term_scr, kdim):
        self.q, self.k, self.v, self.g = q, k, v, g
        self.s_ref, self.ones_kv, self.mask_vk = s_ref, ones_kv, mask_vk
        self.tril, self.term_scr, self.kdim = tril, term_scr, kdim
        self.ws, self.inter, self.att = {}, {}, {}

    def cumulate(self):
        self.bc = _dot_sel_lhs(self.tril, self.g()) * LOG2E

    def _total(self, j):
        return self.bc[SUB * (j + 1) - 1:SUB * (j + 1), :]

    def update_part(self, j):
        sl = slice(SUB * j, SUB * (j + 1))
        self.ws[j] = _dot_tn(self.v()[sl], self.k()[sl] * jnp.exp2(self._total(j) - self.bc[sl]))

    def products_part(self, j):
        hs, term_scr, r0 = SUB // 2, self.term_scr, SUB * j
        q, k, bc = self.q()[r0:r0 + SUB], self.k()[r0:r0 + SUB], self.bc[r0:r0 + SUB]
        trow = lax.broadcasted_iota(jnp.int32, (hs, self.kdim), 0)
        neg = [jnp.where(trow >= d, 0.0, MASK_NEG) for d in range(1, hs)]
        q_lo, q_hi, b_lo, b_hi = q[0:hs], q[hs:SUB], bc[0:hs], bc[hs:SUB]
        row = j * SUB_TERM_ROWS
        for s in range(SUB):
            ks, bs = k[s:s + 1, :], bc[s:s + 1, :]
            d = s % hs
            q_dg, b_dg = (q_lo, b_lo) if s < hs else (q_hi, b_hi)
            e_dg = b_dg - bs if d == 0 else b_dg - bs + neg[d - 1]
            term_scr[row:row + hs, :] = q_dg * ks * jnp.exp2(e_dg)
            row += hs
            if s < hs:
                term_scr[row:row + hs, :] = q_hi * ks * jnp.exp2(b_hi - bs)
                row += hs

    def matmul_part(self, j):
        n_sub = T_CHUNK // SUB
        sl = slice(SUB * j, SUB * (j + 1))
        if j == 0:
            self.state = self.s_ref[...]
        self.inter[j] = _dot_nt(self.q()[sl] * jnp.exp2(self.bc[sl]), self.state)
        self.state = self.state * jnp.exp2(self._total(j)) + self.ws[j] * self.mask_vk
        if j == n_sub - 1:
            self.s_ref[...] = self.state
        rows = slice(j * SUB_TERM_ROWS, (j + 1) * SUB_TERM_ROWS)
        self.att[j] = _dot(self.term_scr[rows, :], self.ones_kv)

    def output_part(self, j):
        hs, r0 = SUB // 2, SUB * j
        v, att = self.v()[r0:r0 + SUB], self.att[j]
        acc_lo, acc_hi = self.inter[j][0:hs], self.inter[j][hs:SUB]
        row = 0
        for s in range(SUB):
            vs = v[s:s + 1, :]
            if s < hs:
                acc_lo = acc_lo + att[row:row + hs] * vs
                row += hs
            acc_hi = acc_hi + att[row:row + hs] * vs
            row += hs
        return jnp.concatenate([acc_lo, acc_hi], axis=0)


def _ret_chunk(q, k, v, cos, sin_signed, s_ref, dstack, inner, kdec, cdec, mask_vk):
    lane = lax.broadcasted_iota(jnp.int32, (T_CHUNK, HEADS * RET_DK), 1)
    first_half = (lane % RET_DK) < (RET_DK // 2)
    rq = _rotary(q, cos, sin_signed, first_half)
    rk = _rotary(k, cos, sin_signed, first_half) * (RET_DK ** -0.5)
    lane_v = lax.broadcasted_iota(jnp.int32, (T_CHUNK, HEADS * DV), 1)
    kst = jnp.concatenate([jnp.where(lane // RET_DK == h, rk, 0.0) for h in range(HEADS)], axis=0)
    vst = jnp.concatenate([jnp.where(lane_v // DV == h, v, 0.0) for h in range(HEADS)], axis=0)
    p = _dot_nt(rq, kst) * dstack
    s_t = s_ref[...]
    o = _dot(p, vst) + _dot_nt(rq, s_t) * inner
    s_ref[...] = s_t * cdec + _dot_tn(v, rk * kdec) * mask_vk
    return o


def _prompt_kernel(apply_final,
                   x_ref, xn_ref, normw_ref, win_ref, wout_ref, bbd_ref, cbd_ref, are_ref, aim_ref,
                   dskip_ref, wglu_ref, lbp_ref, hgn_ref, wup_ref, bgate_ref, glan_ref, retn_ref, cos_ref,
                   sin_ref, dstack_ref, inner_ref, kdec_ref, cdec_ref, onesh_ref, onesg_ref, tril_ref,
                   fnw_ref,
                   y_ref, s5_ref, hgs_ref, glas_ref, rets_ref,
                   proj_scr, lr_scr, hk_scr, bu_scr, mix_scr, term_h0, term_g0, term_h1, term_g1):
    nb = x_ref.shape[0]
    rows = nb * T_CHUNK
    n_slab = 2 * S5_N // LANES
    half = n_slab // 2
    step = pl.program_id(0)

    def w_quarter(qtr):
        return win_ref[:, qtr * QUARTER:(qtr + 1) * QUARTER]

    def proj(col, width, rs=slice(None)):
        return proj_scr[col // QUARTER, rs, col % QUARTER:col % QUARTER + width]

    @pl.when(step == 0)
    def _init():
        s5_ref[...] = jnp.zeros_like(s5_ref)
        hgs_ref[...] = jnp.zeros_like(hgs_ref)
        glas_ref[...] = jnp.zeros_like(glas_ref)
        rets_ref[...] = jnp.zeros_like(rets_ref)
        bu_scr[...] = jnp.zeros_like(bu_scr)
        h0 = _rmsnorm_rows(x_ref[...].reshape(rows, D_MODEL), normw_ref[...]).astype(BF16)
        for qtr in range(N_QUARTER):
            proj_scr[qtr] = jnp.dot(h0, w_quarter(qtr), preferred_element_type=F32)
        lr_scr[...] = jnp.dot(h0, win_ref[:, C_LR:N_PACK], preferred_element_type=F32)

    u = proj(C_U, BRANCH)
    for c2 in range(n_slab // 2):
        bu = _dot(u, bbd_ref[:, 2 * c2 * LANES:2 * (c2 + 1) * LANES])
        for cc in range(2):
            for b in range(nb):
                bu_scr[2 * c2 + cc, b * PITCH:b * PITCH + T_CHUNK, :] = bu[b * T_CHUNK:(b + 1) * T_CHUNK,
                                                                           cc * LANES:(cc + 1) * LANES]
    a_re = [jnp.broadcast_to(are_ref[:, c * LANES:(c + 1) * LANES], (nb, LANES)) for c in range(half)]
    a_im = [jnp.broadcast_to(aim_ref[:, c * LANES:(c + 1) * LANES], (nb, LANES)) for c in range(half)]
    s_init = s5_ref[...]
    carry0 = tuple(s_init[:, c * LANES:(c + 1) * LANES] for c in range(n_slab))

    def scan_step(t, carry):
        new = [None] * n_slab
        for c in range(half):
            sr, si = carry[c], carry[half + c]
            br = bu_scr[c, pl.ds(t, nb, stride=PITCH), :]
            bi = bu_scr[half + c, pl.ds(t, nb, stride=PITCH), :]
            nr = a_re[c] * sr - a_im[c] * si + br
            ni = a_re[c] * si + a_im[c] * sr + bi
            bu_scr[c, pl.ds(t, nb, stride=PITCH), :] = nr
            bu_scr[half + c, pl.ds(t, nb, stride=PITCH), :] = ni
            new[c], new[half + c] = nr, ni
        return tuple(new)

    carry = lax.fori_loop(0, T_CHUNK, scan_step, carry0)
    s5_ref[...] = jnp.concatenate(carry, axis=1)
    s_all = jnp.concatenate([bu_scr[c] for c in range(n_slab)], axis=1)
    y_all = _dot(s_all, cbd_ref[...])
    y_lin = jnp.concatenate([y_all[b * PITCH:b * PITCH + T_CHUNK] for b in range(nb)], axis=0)
    mix_scr[:, 0:BRANCH] = _s5_output(y_lin, u, proj(C_SZ, BRANCH), dskip_ref[...], wglu_ref[...])

    lbp = lbp_ref[...]
    log_f, hk = _hgrn_gates(proj(C_HF, BRANCH), lbp[0:1, :], lbp[1:2, :], lbp[2:3, :])
    proj_scr[C_HF // QUARTER, :, C_HF % QUARTER:C_HF % QUARTER + BRANCH] = log_f
    hk_scr[...] = hk
    g_gla = _log_sigmoid(_dot3(lr_scr[...], wup_ref[...]) + bgate_ref[...])
    lr_scr[...] = g_gla * (1.0 / GLA_TAU)

    ones_h = onesh_ref[...]
    ones_g = onesg_ref[...]
    mask_h = ones_h.astype(F32)
    mask_g = jnp.transpose(ones_g.astype(F32))
    tril = tril_ref[...]

    def rows_of(bp, odd):
        return pl.ds(pl.multiple_of((2 * bp + odd) * T_CHUNK, T_CHUNK), T_CHUNK)

    def hgrn_stream(bp, odd, term):
        rs = rows_of(bp, odd)
        return _GatedStream(lambda: proj(C_HQ, BRANCH, rs), lambda: hk_scr[rs, :], lambda: proj(C_HI, BRANCH, rs),
                            lambda: proj(C_HF, BRANCH, rs), hgs_ref.at[bp, odd], ones_h, mask_h, tril, term,
                            HEADS * HG_DK)

    def gla_stream(bp, odd, term):
        rs = rows_of(bp, odd)
        return _GatedStream(lambda: proj(C_GQ, LANES, rs) * (GLA_DK ** -0.5), lambda: proj(C_GK, LANES, rs),
                            lambda: proj(C_GV, BRANCH, rs), lambda: lr_scr[rs, :], glas_ref.at[bp, odd],
                            ones_g, mask_g, tril, term, HEADS * GLA_DK)

    n_sub = T_CHUNK // SUB

    def retention(bp, odd):
        rs = rows_of(bp, odd)
        mix_scr[rs, 3 * BRANCH:4 * BRANCH] = _ret_chunk(
            proj(C_RQ, LANES, rs), proj(C_RK, LANES, rs), proj(C_RV, BRANCH, rs), cos_ref[...],
            sin_ref[...], rets_ref.at[bp, odd], dstack_ref[...], inner_ref[...], kdec_ref[...],
            cdec_ref[...], mask_g)

    def per_pair(bp, _):
        st = [hgrn_stream(bp, 0, term_h0), gla_stream(bp, 0, term_g0),
              hgrn_stream(bp, 1, term_h1), gla_stream(bp, 1, term_g1)]
        slots = [(0, BRANCH), (0, 2 * BRANCH), (1, BRANCH), (1, 2 * BRANCH)]

        def emit(i, j):
            odd, col = slots[i]
            r0 = pl.multiple_of((2 * bp + odd) * T_CHUNK + SUB * j, SUB)
            mix_scr[pl.ds(r0, SUB), col:col + BRANCH] = st[i].output_part(j)

        for s_ in st:
            s_.cumulate()
        for j in range(n_sub):
            st[0].update_part(j)
        for j in range(n_sub):
            st[0].products_part(j)
            st[1].update_part(j)
        for j in range(n_sub):
            st[1].products_part(j)
            st[2].update_part(j)
        for j in range(n_sub):
            st[2].products_part(j)
            st[3].update_part(j)
            st[0].matmul_part(j)
        for j in range(n_sub):
            st[3].products_part(j)
            st[1].matmul_part(j)
        for j in range(n_sub):
            emit(0, j)
            st[2].matmul_part(j)
        for j in range(n_sub):
            emit(1, j)
            st[3].matmul_part(j)
        for j in range(n_sub):
            emit(2, j)
            if j % 2 == 0:
                retention(bp, j // 2)
        for j in range(n_sub):
            emit(3, j)
        return 0

    lax.fori_loop(0, N_QUARTER, per_pair, 0)

    o_hg = _head_rms(mix_scr[:, BRANCH:2 * BRANCH], ones_h, hgn_ref[...])
    mix_scr[:, BRANCH:2 * BRANCH] = o_hg * _silu(proj(C_HZ, BRANCH))
    o_gla = _head_rms(mix_scr[:, 2 * BRANCH:3 * BRANCH], ones_h, glan_ref[...])
    mix_scr[:, 2 * BRANCH:3 * BRANCH] = o_gla * _silu(proj(C_GZ, BRANCH))
    o_ret = _head_ln(mix_scr[:, 3 * BRANCH:4 * BRANCH], ones_h, retn_ref[...])
    mix_scr[:, 3 * BRANCH:4 * BRANCH] = o_ret * _silu(proj(C_RZ, BRANCH))
    hn = _rmsnorm_rows(xn_ref[...].reshape(rows, D_MODEL), normw_ref[...]).astype(BF16)
    for qtr in range(N_QUARTER):
        proj_scr[qtr] = jnp.dot(hn, w_quarter(qtr), preferred_element_type=F32)
    lr_scr[...] = jnp.dot(hn, win_ref[:, C_LR:N_PACK], preferred_element_type=F32)
    out = _mix_and_project(x_ref[...].reshape(rows, D_MODEL), mix_scr[...], wout_ref[...], fnw_ref[...],
                           apply_final)
    y_ref[...] = out.reshape(nb, T_CHUNK, D_MODEL)


def _dot_sel_lhs2(sel, x):
    x1 = x.astype(BF16)
    x2 = (x - x1.astype(F32)).astype(BF16)
    return (jnp.dot(sel, x1, preferred_element_type=F32) + jnp.dot(sel, x2, preferred_element_type=F32))


def _head_rms_t(o, ones_h, gain):
    ms = _dot_sel_lhs2(ones_h, o * o) * (1.0 / DV)
    return o * lax.rsqrt(ms + EPS) * gain


def _head_ln_t(o, ones_h, gain):
    c = o - _dot_sel_lhs2(ones_h, o) * (1.0 / DV)
    var = _dot_sel_lhs2(ones_h, c * c) * (1.0 / DV)
    return c * lax.rsqrt(var + EPS) * gain


def _rotary_t(t, cos, sin_signed, first_half):
    half = RET_DK // 2
    swapped = jnp.where(first_half, pltpu.roll(t, LANES - half, 0), pltpu.roll(t, half, 0))
    return t * cos + swapped * sin_signed


def _sample_kernel(x_ref, normw_ref, wt_ref, wout_ref, bbdt_ref, cbdt_ref, are_ref, aim_ref, dskip_ref,
                   wglut_ref, lbp_ref, hgn_ref, wupt_ref, bgate_ref, glan_ref, retn_ref, fnw_ref,
                   cos_ref, sin_ref, dret_ref, onesh_ref,
                   s5re_ref, s5im_ref, hg_ref, gla_ref, ret_ref,
                   y_ref, s5re_o, s5im_o, hg_o, gla_o, ret_o,
                   xs_scr, pt_scr, hk_scr, ot_scr, mixt_scr):
    layer, head = pl.program_id(0), pl.program_id(1)
    last_layer, last_head = pl.num_programs(0) - 1, pl.num_programs(1) - 1

    @pl.when((layer == 0) & (head == 0))
    def _load_x():
        xs_scr[...] = x_ref[...]

    @pl.when(head == 0)
    def _dense():
        hh = _rmsnorm_rows(xs_scr[...], normw_ref[...]).astype(BF16)
        pt_scr[...] = lax.dot_general(wt_ref[...], hh, (((1,), (1,)), ((), ())), preferred_element_type=F32)

        u = pt_scr[C_U:C_U + BRANCH, :]
        bu = _dot3(bbdt_ref[...], u)
        a_re, a_im = are_ref[...], aim_ref[...]
        s0r, s0i = s5re_ref[...], s5im_ref[...]
        s_re = a_re * s0r - a_im * s0i + bu[0:S5_N]
        s_im = a_re * s0i + a_im * s0r + bu[S5_N:2 * S5_N]
        s5re_o[...] = s_re
        s5im_o[...] = s_im
        y = _gelu_tanh(_dot3(cbdt_ref[...], jnp.concatenate([s_re, s_im], axis=0)) + u * dskip_ref[...])
        y = y * _sigmoid(jnp.dot(wglut_ref[...], y.astype(BF16), preferred_element_type=F32))
        mixt_scr[0:BRANCH, :] = y * _silu(pt_scr[C_SZ:C_SZ + BRANCH, :])

        lbp = lbp_ref[...]
        log_f, hk = _hgrn_gates(pt_scr[C_HF:C_HF + BRANCH, :], lbp[:, 0:1], lbp[:, 1:2], lbp[:, 2:3])
        pt_scr[C_HF:C_HF + BRANCH, :] = jnp.exp(log_f)
        hk_scr[...] = hk
        g_gla = _log_sigmoid(_dot3(wupt_ref[...], pt_scr[C_LR:C_LR + LANES, :]) + bgate_ref[...])
        pt_scr[C_LR:C_LR + LANES, :] = jnp.exp(g_gla * (1.0 / GLA_TAU))
        pt_scr[C_GQ:C_GQ + LANES, :] = pt_scr[C_GQ:C_GQ + LANES, :] * (GLA_DK ** -0.5)
        row = lax.broadcasted_iota(jnp.int32, (HEADS * RET_DK, LANES), 0)
        first_half = (row % RET_DK) < (RET_DK // 2)
        pt_scr[C_RQ:C_RQ + LANES, :] = _rotary_t(pt_scr[C_RQ:C_RQ + LANES, :], cos_ref[...], sin_ref[...],
                                                 first_half)
        pt_scr[C_RK:C_RK + LANES, :] = _rotary_t(pt_scr[C_RK:C_RK + LANES, :], cos_ref[...], sin_ref[...],
                                                 first_half) * (RET_DK ** -0.5)

    def head_update(s0_ref, s_out_ref, dk, dec_ref, dec_row, key_ref, key_row, q_row, v_row, out_row):
        vt = pt_scr[pl.ds(pl.multiple_of(v_row + head * DV, DV), DV), :]

        def feature(kk, acc):
            r = head * dk + kk
            bcast = lambda ref, r0: jnp.broadcast_to(ref[pl.ds(r0 + r, 1), :], (DV, LANES))
            s_new = s0_ref[kk] * bcast(dec_ref, dec_row) + bcast(key_ref, key_row) * vt
            s_out_ref[kk] = s_new
            return acc + bcast(pt_scr, q_row) * s_new

        acc = lax.fori_loop(0, dk, feature, jnp.zeros((DV, LANES), F32), unroll=4)
        ot_scr[pl.ds(pl.multiple_of(out_row + head * DV, DV), DV), :] = acc

    head_update(hg_ref, hg_o, HG_DK, pt_scr, C_HF, hk_scr, 0, C_HQ, C_HI, 0)
    head_update(gla_ref, gla_o, GLA_DK, pt_scr, C_LR, pt_scr, C_GK, C_GQ, C_GV, BRANCH)
    head_update(ret_ref, ret_o, RET_DK, dret_ref, 0, pt_scr, C_RK, C_RQ, C_RV, 2 * BRANCH)

    @pl.when(head == last_head)
    def _finish():
        ones_h = onesh_ref[...]
        o_hg = _head_rms_t(ot_scr[0:BRANCH, :], ones_h, hgn_ref[...])
        mixt_scr[BRANCH:2 * BRANCH, :] = o_hg * _silu(pt_scr[C_HZ:C_HZ + BRANCH, :])
        o_gla = _head_rms_t(ot_scr[BRANCH:2 * BRANCH, :], ones_h, glan_ref[...])
        mixt_scr[2 * BRANCH:3 * BRANCH, :] = o_gla * _silu(pt_scr[C_GZ:C_GZ + BRANCH, :])
        o_ret = _head_ln_t(ot_scr[2 * BRANCH:3 * BRANCH, :], ones_h, retn_ref[...])
        mixt_scr[3 * BRANCH:4 * BRANCH, :] = o_ret * _silu(pt_scr[C_RZ:C_RZ + BRANCH, :])
        out = xs_scr[...] + lax.dot_general(mixt_scr[...].astype(BF16), wout_ref[...], (((0,), (0,)), ((), ())),
                                            preferred_element_type=F32)
        xs_scr[...] = out

        @pl.when(layer == last_layer)
        def _emit():
            y_ref[...] = _rmsnorm_rows(out, fnw_ref[...])


def _ret_log_gamma():
    return jnp.log1p(-jnp.exp2(-5.0 - jnp.arange(HEADS, dtype=F32)))


def _constants():
    ones_h = (np.arange(BRANCH)[:, None] // DV == np.arange(BRANCH)[None, :] // DV)
    ones_g = (np.arange(HEADS * GLA_DK)[:, None] // GLA_DK == np.arange(BRANCH)[None, :] // DV)
    r = np.arange(T_CHUNK)
    same_sub = r[:, None] // SUB == r[None, :] // SUB
    tril = same_sub & (r[None, :] <= r[:, None])
    same_group_b = np.arange(BRANCH)[:, None] // S5_CH == np.arange(S5_N)[None, :] // S5_STATE
    as_bf16 = lambda m: jnp.asarray(m.astype(np.float32), dtype=BF16)
    return dict(ones_h=as_bf16(ones_h), ones_g=as_bf16(ones_g), tril=as_bf16(tril),
                s5_mask=jnp.asarray(same_group_b.astype(np.float32)))


def _ret_tables():
    lg = _ret_log_gamma()
    idx = jnp.arange(T_CHUNK, dtype=F32)
    rel = idx[:, None] - idx[None, :]
    causal = rel >= 0
    decay = jnp.where(causal[None], jnp.exp(jnp.where(causal, rel, 0.0)[None] * lg[:, None, None]), 0.0)
    dstack = jnp.transpose(decay, (1, 0, 2)).reshape(T_CHUNK, HEADS * T_CHUNK)
    inner = jnp.repeat(jnp.exp((idx[:, None] + 1.0) * lg[None, :]), DV, axis=1)
    kdec = jnp.repeat(jnp.exp((T_CHUNK - 1.0 - idx[:, None]) * lg[None, :]), RET_DK, axis=1)
    cdec = jnp.repeat(jnp.exp(T_CHUNK * lg)[None, :], RET_DK, axis=1)
    dret = jnp.broadcast_to(jnp.repeat(jnp.exp(lg), RET_DK)[:, None], (HEADS * RET_DK, LANES))
    return dstack, inner, kdec, cdec, dret


def _rope_tables(pos):
    half = RET_DK // 2
    inv = ROPE_BASE ** (-jnp.arange(half, dtype=F32) / half)
    ang = pos.astype(F32)[:, None] * inv[None, :]
    cos, sin = jnp.cos(ang), jnp.sin(ang)
    cos_t = jnp.tile(jnp.concatenate([cos, cos], axis=1), (1, HEADS))
    sin_t = jnp.tile(jnp.concatenate([-sin, sin], axis=1), (1, HEADS))
    return cos_t, sin_t


def _pack_w_in_t(w):
    wt = jnp.swapaxes(w, 1, 2)
    offs = np.cumsum([0, 256, 256, 256, 256, 256, 256, 128, 128, 256, 16, 256, 128, 128, 256, 256])
    seg = lambda i: wt[:, int(offs[i]):int(offs[i + 1]), :]
    order = [0, 1, 2, 3, 4, 5, 6, 7, 8, 10, 11, 12, 13, 14]
    pad = jnp.zeros((w.shape[0], LANES - GLA_LOWRANK, w.shape[1]), w.dtype)
    return jnp.concatenate([seg(i) for i in order] + [seg(9), pad], axis=1).astype(BF16)


def _s5_discretize(lam_re, lam_im, log_step, b_re, b_im, c_re, c_im, mask):
    lr, li = lam_re.astype(F32), lam_im.astype(F32)
    step = jnp.exp(log_step.astype(F32))[..., None]
    mag = jnp.exp(lr * step)
    ab_re = mag * jnp.cos(li * step)
    ab_im = mag * jnp.sin(li * step)
    den = lr * lr + li * li
    nr = ab_re - 1.0
    f_re = (nr * lr + ab_im * li) / den
    f_im = (ab_im * lr - nr * li) / den
    br, bi = b_re.astype(F32), b_im.astype(F32)
    bb_re = f_re[..., None] * br - f_im[..., None] * bi
    bb_im = f_re[..., None] * bi + f_im[..., None] * br
    nl = lr.shape[0]

    def drive(bb):
        rows = jnp.transpose(bb, (0, 1, 3, 2)).reshape(nl, BRANCH, S5_STATE)
        return jnp.tile(rows, (1, 1, S5_GROUPS)) * mask

    def readout(cc):
        rows = cc.astype(F32).reshape(nl, BRANCH, S5_STATE)
        return jnp.tile(rows, (1, 1, S5_GROUPS)) * mask

    bbd = jnp.concatenate([drive(bb_re), drive(bb_im)], axis=2)
    cbd_t = jnp.concatenate([readout(c_re), -readout(c_im)], axis=2)
    return bbd, cbd_t, ab_re.reshape(nl, 1, S5_N), ab_im.reshape(nl, 1, S5_N)


def _full(shape):
    return pl.BlockSpec(shape, lambda *_: (0,) * len(shape), pipeline_mode=pl.Buffered(1))


def _of_layer(arr, layer):
    nd = arr.ndim - 1
    return pl.BlockSpec((None,) + arr.shape[1:], lambda *_: (layer,) + (0,) * nd,
                        pipeline_mode=pl.Buffered(1))


def _prompt_layer(x, layer, p, consts, tabs, rope, apply_final):
    nb, seq, _ = x.shape
    n_steps = seq // T_CHUNK
    rows = nb * T_CHUNK
    dstack, inner, kdec, cdec, _ = tabs
    cos_t, sin_t = rope
    per_layer = [p['norm_w'], p['w_pack'], p['w_out'], p['bbd'].astype(BF16), p['cbd'],
                 p['a_re'], p['a_im'],
                 p['d_skip'], p['w_glu'], p['lbp'], p['hg_norm'], p['w_up'], p['b_gate'], p['gla_norm'],
                 p['ret_norm']]
    shared = [dstack, inner, kdec, cdec, consts['ones_h'], consts['ones_g'], consts['tril'],
              p['final_norm']]
    in_specs = [pl.BlockSpec((nb, T_CHUNK, D_MODEL), lambda i: (0, i, 0)),
                pl.BlockSpec((nb, T_CHUNK, D_MODEL), lambda i: (0, jnp.minimum(i + 1, n_steps - 1), 0))]
    in_specs += [_of_layer(a, layer) for a in per_layer]
    in_specs += [pl.BlockSpec((T_CHUNK, LANES), lambda i: (i, 0)), pl.BlockSpec((T_CHUNK, LANES), lambda i: (i, 0))]
    in_specs += [_full(a.shape) for a in shared]
    out_shape = (jax.ShapeDtypeStruct((nb, seq, D_MODEL), F32),
                 jax.ShapeDtypeStruct((nb, 2 * S5_N), F32),
                 jax.ShapeDtypeStruct((nb // 2, 2, BRANCH, HEADS * HG_DK), F32),
                 jax.ShapeDtypeStruct((nb // 2, 2, BRANCH, HEADS * GLA_DK), F32),
                 jax.ShapeDtypeStruct((nb // 2, 2, BRANCH, HEADS * RET_DK), F32))
    out_specs = (pl.BlockSpec((nb, T_CHUNK, D_MODEL), lambda i: (0, i, 0)),
                 pl.BlockSpec((nb, 2 * S5_N), lambda i: (0, 0)),
                 pl.BlockSpec((nb // 2, 2, BRANCH, HEADS * HG_DK), lambda i: (0, 0, 0, 0)),
                 pl.BlockSpec((nb // 2, 2, BRANCH, HEADS * GLA_DK), lambda i: (0, 0, 0, 0)),
                 pl.BlockSpec((nb // 2, 2, BRANCH, HEADS * RET_DK), lambda i: (0, 0, 0, 0)))
    scratch = [pltpu.VMEM((N_QUARTER, rows, QUARTER), F32), pltpu.VMEM((rows, LANES), F32),
               pltpu.VMEM((rows, BRANCH), F32),
               pltpu.VMEM((2 * S5_N // LANES, nb * PITCH, LANES), F32),
               pltpu.VMEM((rows, D_MODEL), F32),
               pltpu.VMEM((TERM_ROWS, HEADS * HG_DK), F32), pltpu.VMEM((TERM_ROWS, HEADS * GLA_DK), F32),
               pltpu.VMEM((TERM_ROWS, HEADS * HG_DK), F32), pltpu.VMEM((TERM_ROWS, HEADS * GLA_DK), F32)]
    y, s5, hgs, glas, rets = pl.pallas_call(
        functools.partial(_prompt_kernel, apply_final),
        grid=(n_steps,), in_specs=in_specs, out_specs=out_specs, out_shape=out_shape,
        scratch_shapes=scratch, name='prompt_layer',
        compiler_params=pltpu.CompilerParams(dimension_semantics=('arbitrary',),
                                             vmem_limit_bytes=VMEM_LIMIT),
    )(x, x, *per_layer, cos_t, sin_t, *shared)
    s5 = s5.reshape(nb, 2, S5_GROUPS, S5_STATE)

    def unstack(st, dk):
        st = st.reshape(nb, HEADS, DV, HEADS, dk)
        diag = jnp.stack([st[:, hh, :, hh, :] for hh in range(HEADS)], axis=1)
        return jnp.transpose(diag, (0, 1, 3, 2))

    return y, (s5[:, 0], s5[:, 1], unstack(hgs, HG_DK), unstack(glas, GLA_DK), unstack(rets, RET_DK))


def _sample_step(x, states, p, consts, tabs, rope):
    nb = x.shape[0]
    depth = p['w_t'].shape[0]
    s5re, s5im, hg, gla, ret = states
    to_lanes = lambda s: jnp.moveaxis(s, 1, -1)
    s5re_t = to_lanes(s5re).reshape(depth, S5_N, nb)
    s5im_t = to_lanes(s5im).reshape(depth, S5_N, nb)
    hg_t, gla_t, ret_t = to_lanes(hg), to_lanes(gla), to_lanes(ret)
    cos_t, sin_t = rope
    col = lambda a: jnp.swapaxes(a, -1, -2)
    per_layer = [p['norm_w'], p['w_t'], p['w_out'], col(p['bbd']), p['cbd_t'], col(p['a_re']), col(p['a_im']),
                 col(p['d_skip']), col(p['w_glu']), col(p['lbp']), col(p['hg_norm']), col(p['w_up']),
                 col(p['b_gate']), col(p['gla_norm']), col(p['ret_norm'])]
    shared = [p['final_norm'], col(cos_t), col(sin_t), tabs[4], consts['ones_h']]
    layer_spec = lambda a: pl.BlockSpec((None,) + a.shape[1:], lambda l, h: (l,) + (0,) * (a.ndim - 1))
    head_spec = lambda a: pl.BlockSpec((None, None) + a.shape[2:], lambda l, h: (l, h) + (0,) * (a.ndim - 2))
    state_specs = [layer_spec(s5re_t), layer_spec(s5im_t), head_spec(hg_t), head_spec(gla_t), head_spec(ret_t)]
    in_specs = ([_full(x.shape)] + [layer_spec(a) for a in per_layer] + [_full(a.shape) for a in shared]
                + state_specs)
    state_arrays = [s5re_t, s5im_t, hg_t, gla_t, ret_t]
    out_shape = tuple([jax.ShapeDtypeStruct(x.shape, F32)]
                      + [jax.ShapeDtypeStruct(a.shape, F32) for a in state_arrays])
    out_specs = tuple([pl.BlockSpec(x.shape, lambda l, h: (0, 0))] + state_specs)
    scratch = [pltpu.VMEM((nb, D_MODEL), F32), pltpu.VMEM((N_PACK, nb), F32), pltpu.VMEM((BRANCH, nb), F32),
               pltpu.VMEM((3 * BRANCH, nb), F32), pltpu.VMEM((4 * BRANCH, nb), F32)]
    outs = pl.pallas_call(
        _sample_kernel, grid=(depth, HEADS), in_specs=in_specs, out_specs=out_specs, out_shape=out_shape,
        scratch_shapes=scratch, name='sample_step',
        compiler_params=pltpu.CompilerParams(dimension_semantics=('arbitrary', 'arbitrary'),
                                             vmem_limit_bytes=VMEM_LIMIT),
    )(x, *per_layer, *shared, *state_arrays)
    from_lanes = lambda s: jnp.moveaxis(s, -1, 1)
    new = (from_lanes(outs[1].reshape(depth, S5_GROUPS, S5_STATE, nb)),
           from_lanes(outs[2].reshape(depth, S5_GROUPS, S5_STATE, nb)),
           from_lanes(outs[3]), from_lanes(outs[4]), from_lanes(outs[5]))
    return outs[0], new


def kernel(x_prompt, x_sample, state_s5_re, state_s5_im, state_hgrn, state_gla, state_ret, norm_w, final_norm_w, w_in, w_out, s5_lam_re, s5_lam_im, s5_log_step, s5_b_re, s5_b_im, s5_c_re, s5_c_im, s5_d, s5_w_glu, hgrn_lb_logits, hgrn_norm_w, gla_w_gate_up, gla_b_gate, gla_norm_w, ret_norm_w):
    depth = w_in.shape[0]
    seq = x_prompt.shape[1]
    consts = _constants()
    tabs = _ret_tables()
    rope_p = _rope_tables(jnp.arange(seq))
    rope_s = _rope_tables(PAST_LEN + jnp.arange(1))
    lb = jnp.cumsum(jax.nn.softmax(hgrn_lb_logits.astype(F32), axis=0), axis=0)
    lb = (lb - lb[0:1])[:, None, :]
    bbd, cbd_t, a_re, a_im = _s5_discretize(s5_lam_re, s5_lam_im, s5_log_step, s5_b_re, s5_b_im, s5_c_re,
                                            s5_c_im, consts['s5_mask'])
    w_t = _pack_w_in_t(w_in)
    row = lambda a: a[:, None, :].astype(F32)
    w_up = jnp.zeros((depth, LANES, HEADS * GLA_DK), F32).at[:, :GLA_LOWRANK].set(gla_w_gate_up.astype(F32))
    p = dict(norm_w=row(norm_w),
             w_pack=jnp.swapaxes(w_t, 1, 2), w_t=w_t, w_out=w_out.astype(BF16),
             bbd=bbd, cbd=jnp.swapaxes(cbd_t, 1, 2).astype(BF16), cbd_t=cbd_t, a_re=a_re, a_im=a_im,
             d_skip=row(s5_d), w_glu=s5_w_glu.astype(BF16),
             lbp=jnp.concatenate([jnp.log(lb), jnp.log1p(-lb), 1.0 - lb, jnp.zeros((depth, 5, BRANCH), F32)],
                                 axis=1),
             hg_norm=row(hgrn_norm_w), w_up=w_up, b_gate=row(gla_b_gate), gla_norm=row(gla_norm_w),
             ret_norm=row(ret_norm_w), final_norm=final_norm_w[None, :].astype(F32))

    xp = x_prompt
    new_p = ([], [], [], [], [])
    for l in range(depth):
        xp, st_p = _prompt_layer(xp, l, p, consts, tabs, rope_p, l == depth - 1)
        for i in range(5):
            new_p[i].append(st_p[i])
    xs, new_s = _sample_step(x_sample.reshape(x_sample.shape[0], D_MODEL),
                             (state_s5_re, state_s5_im, state_hgrn, state_gla, state_ret),
                             p, consts, tabs, rope_s)
    return (xp, xs.reshape(x_sample.shape),
            jnp.stack(new_p[0]), jnp.stack(new_p[1]), jnp.stack(new_p[2]), jnp.stack(new_p[3]),
            jnp.stack(new_p[4])) + new_s
```

```python
import functools
import math

import numpy as np
import jax
import jax.numpy as jnp
from jax import lax
from jax.experimental import pallas as pl
from jax.experimental.pallas import tpu as pltpu

F32 = jnp.float32
BF16 = jnp.bfloat16

D_MODEL = 1024
BRANCH = 256
S5_CH = 16
S5_GROUPS = 16
S5_STATE = 64
S5_N = S5_GROUPS * S5_STATE
HEADS = 4
HG_DK = 64
GLA_DK = 32
RET_DK = 32
DV = 64
GLA_LOWRANK = 16
GLA_TAU = 16.0
ROPE_BASE = 10000.0
PAST_LEN = 16384
EPS = 1e-6
SUB = 16

LANES = 128
T_CHUNK = 64
PITCH = T_CHUNK + 8
SUB_TERM_ROWS = SUB * (SUB + SUB // 2) // 2
TERM_ROWS = (T_CHUNK // SUB) * SUB_TERM_ROWS
LOG2E = math.log2(math.e)
MASK_NEG = -1e30
VMEM_LIMIT = 60 * 1024 * 1024

C_U, C_SZ, C_HQ, C_HF, C_HI, C_HZ = 0, 256, 512, 768, 1024, 1280
C_GQ, C_GK, C_GV, C_GZ = 1536, 1664, 1792, 2048
C_RQ, C_RK, C_RV, C_RZ = 2304, 2432, 2560, 2816
C_LR = 3072
N_PACK = 3200
N_QUARTER = 4
QUARTER = C_LR // N_QUARTER


def _dot(a, b):
    return jnp.dot(a.astype(BF16), b.astype(BF16), preferred_element_type=F32)


def _dot_nt(a, b):
    return lax.dot_general(a.astype(BF16), b.astype(BF16), (((1,), (1,)), ((), ())),
                           preferred_element_type=F32)


def _dot_tn(a, b):
    return lax.dot_general(a.astype(BF16), b.astype(BF16), (((0,), (0,)), ((), ())),
                           preferred_element_type=F32)


def _split3(x):
    x1 = x.astype(BF16)
    r1 = x - x1.astype(F32)
    x2 = r1.astype(BF16)
    x3 = (r1 - x2.astype(F32)).astype(BF16)
    return x1, x2, x3


def _dot_sel_lhs(sel, x):
    x1, x2, x3 = _split3(x)
    d = lambda p: jnp.dot(sel, p, preferred_element_type=F32)
    return d(x1) + d(x2) + d(x3)


def _dot_sel_rhs2(x, sel):
    x1 = x.astype(BF16)
    x2 = (x - x1.astype(F32)).astype(BF16)
    return (jnp.dot(x1, sel, preferred_element_type=F32)
            + jnp.dot(x2, sel, preferred_element_type=F32))


def _dot3(a, b):
    a1 = a.astype(BF16)
    a2 = (a - a1.astype(F32)).astype(BF16)
    b1 = b.astype(BF16)
    b2 = (b - b1.astype(F32)).astype(BF16)
    d = lambda p, q: jnp.dot(p, q, preferred_element_type=F32)
    return d(a1, b1) + d(a1, b2) + d(a2, b1)


def _sigmoid(x):
    return 0.5 * jnp.tanh(0.5 * x) + 0.5


def _silu(x):
    return x * _sigmoid(x)


def _log_sigmoid(x):
    return jnp.minimum(x, 0.0) - jnp.log(1.0 + jnp.exp(-jnp.abs(x)))


def _gelu_tanh(x):
    return 0.5 * x * (1.0 + jnp.tanh(math.sqrt(2.0 / math.pi) * (x + 0.044715 * (x * x * x))))


def _rmsnorm_rows(x, w):
    return x * lax.rsqrt(jnp.mean(x * x, axis=-1, keepdims=True) + EPS) * w


def _head_rms(o, ones_h, gain):
    ms = _dot_sel_rhs2(o * o, ones_h) * (1.0 / DV)
    return o * lax.rsqrt(ms + EPS) * gain


def _head_ln(o, ones_h, gain):
    c = o - _dot_sel_rhs2(o, ones_h) * (1.0 / DV)
    var = _dot_sel_rhs2(c * c, ones_h) * (1.0 / DV)
    return c * lax.rsqrt(var + EPS) * gain


def _hgrn_gates(xf, loglb, log1mlb, one_m_lb):
    bterm = log1mlb + _log_sigmoid(xf)
    m = jnp.maximum(loglb, bterm)
    log_f = m + jnp.log(jnp.exp(loglb - m) + jnp.exp(bterm - m))
    return log_f, one_m_lb * _sigmoid(-xf)


def _rotary(t, cos, sin_signed, first_half):
    half = RET_DK // 2
    swapped = jnp.where(first_half, pltpu.roll(t, LANES - half, 1), pltpu.roll(t, half, 1))
    return t * cos + swapped * sin_signed


def _s5_output(y_lin, u, sz, dskip, wglu):
    y = _gelu_tanh(y_lin + u * dskip)
    y = y * _sigmoid(_dot(y, wglu))
    return y * _silu(sz)


def _mix_and_project(x, mix, wout, fnw, apply_final):
    out = x + _dot(mix, wout)
    if apply_final:
        out = _rmsnorm_rows(out, fnw)
    return out


class _GatedStream:
    def __init__(self, q, k, v, g, s_ref, ones_kv, mask_vk, tril, term_scr, kdim):
        self.q, self.k, self.v, self.g = q, k, v, g
        self.s_ref, self.ones_kv, self.mask_vk = s_ref, ones_kv, mask_vk
        self.tril, self.term_scr, self.kdim = tril, term_scr, kdim
        self.ws, self.inter, self.att = {}, {}, {}

    def cumulate(self):
        self.bc = _dot_sel_lhs(self.tril, self.g()) * LOG2E

    def _total(self, j):
        return self.bc[SUB * (j + 1) - 1:SUB * (j + 1), :]

    def update_part(self, j):
        sl = slice(SUB * j, SUB * (j + 1))
        self.ws[j] = _dot_tn(self.v()[sl], self.k()[sl] * jnp.exp2(self._total(j) - self.bc[sl]))

    def products_part(self, j):
        hs, term_scr, r0 = SUB // 2, self.term_scr, SUB * j
        q, k, bc = self.q()[r0:r0 + SUB], self.k()[r0:r0 + SUB], self.bc[r0:r0 + SUB]
        trow = lax.broadcasted_iota(jnp.int32, (hs, self.kdim), 0)
        neg = [jnp.where(trow >= d, 0.0, MASK_NEG) for d in range(1, hs)]
        q_lo, q_hi, b_lo, b_hi = q[0:hs], q[hs:SUB], bc[0:hs], bc[hs:SUB]
        row = j * SUB_TERM_ROWS
        for s in range(SUB):
            ks, bs = k[s:s + 1, :], bc[s:s + 1, :]
            d = s % hs
            q_dg, b_dg = (q_lo, b_lo) if s < hs else (q_hi, b_hi)
            e_dg = b_dg - bs if d == 0 else b_dg - bs + neg[d - 1]
            term_scr[row:row + hs, :] = q_dg * ks * jnp.exp2(e_dg)
            row += hs
            if s < hs:
                term_scr[row:row + hs, :] = q_hi * ks * jnp.exp2(b_hi - bs)
                row += hs

    def matmul_part(self, j):
        n_sub = T_CHUNK // SUB
        sl = slice(SUB * j, SUB * (j + 1))
        if j == 0:
            self.state = self.s_ref[...]
        self.inter[j] = _dot_nt(self.q()[sl] * jnp.exp2(self.bc[sl]), self.state)
        self.state = self.state * jnp.exp2(self._total(j)) + self.ws[j] * self.mask_vk
        if j == n_sub - 1:
            self.s_ref[...] = self.state
        rows = slice(j * SUB_TERM_ROWS, (j + 1) * SUB_TERM_ROWS)
        self.att[j] = _dot(self.term_scr[rows, :], self.ones_kv)

    def output_part(self, j):
        hs, r0 = SUB // 2, SUB * j
        v, att = self.v()[r0:r0 + SUB], self.att[j]
        acc_lo, acc_hi = self.inter[j][0:hs], self.inter[j][hs:SUB]
        row = 0
        for s in range(SUB):
            vs = v[s:s + 1, :]
            if s < hs:
                acc_lo = acc_lo + att[row:row + hs] * vs
                row += hs
            acc_hi = acc_hi + att[row:row + hs] * vs
            row += hs
        return jnp.concatenate([acc_lo, acc_hi], axis=0)


def _ret_chunk(q, k, v, cos, sin_signed, s_ref, dstack, inner, kdec, cdec, mask_vk):
    lane = lax.broadcasted_iota(jnp.int32, (T_CHUNK, HEADS * RET_DK), 1)
    first_half = (lane % RET_DK) < (RET_DK // 2)
    rq = _rotary(q, cos, sin_signed, first_half)
    rk = _rotary(k, cos, sin_signed, first_half) * (RET_DK ** -0.5)
    lane_v = lax.broadcasted_iota(jnp.int32, (T_CHUNK, HEADS * DV), 1)
    kst = jnp.concatenate([jnp.where(lane // RET_DK == h, rk, 0.0) for h in range(HEADS)], axis=0)
    vst = jnp.concatenate([jnp.where(lane_v // DV == h, v, 0.0) for h in range(HEADS)], axis=0)
    p = _dot_nt(rq, kst) * dstack
    s_t = s_ref[...]
    o = _dot(p, vst) + _dot_nt(rq, s_t) * inner
    s_ref[...] = s_t * cdec + _dot_tn(v, rk * kdec) * mask_vk
    return o


def _prompt_kernel(apply_final,
                   x_ref, xn_ref, normw_ref, win_ref, wout_ref, bbd_ref, cbd_ref, are_ref, aim_ref,
                   dskip_ref, wglu_ref, lbp_ref, hgn_ref, wup_ref, bgate_ref, glan_ref, retn_ref, cos_ref,
                   sin_ref, dstack_ref, inner_ref, kdec_ref, cdec_ref, onesh_ref, onesg_ref, tril_ref,
                   fnw_ref,
                   y_ref, s5_ref, hgs_ref, glas_ref, rets_ref,
                   proj_scr, lr_scr, hk_scr, bu_scr, mix_scr, term_h0, term_g0, term_h1, term_g1):
    nb = x_ref.shape[0]
    rows = nb * T_CHUNK
    n_slab = 2 * S5_N // LANES
    half = n_slab // 2
    step = pl.program_id(0)

    def w_quarter(qtr):
        return win_ref[:, qtr * QUARTER:(qtr + 1) * QUARTER]

    def proj(col, width, rs=slice(None)):
        return proj_scr[col // QUARTER, rs, col % QUARTER:col % QUARTER + width]

    @pl.when(step == 0)
    def _init():
        s5_ref[...] = jnp.zeros_like(s5_ref)
        hgs_ref[...] = jnp.zeros_like(hgs_ref)
        glas_ref[...] = jnp.zeros_like(glas_ref)
        rets_ref[...] = jnp.zeros_like(rets_ref)
        bu_scr[...] = jnp.zeros_like(bu_scr)
        h0 = _rmsnorm_rows(x_ref[...].reshape(rows, D_MODEL), normw_ref[...]).astype(BF16)
        for qtr in range(N_QUARTER):
            proj_scr[qtr] = jnp.dot(h0, w_quarter(qtr), preferred_element_type=F32)
        lr_scr[...] = jnp.dot(h0, win_ref[:, C_LR:N_PACK], preferred_element_type=F32)

    def drive(c2):
        bu = _dot(proj(C_U, BRANCH), bbd_ref[:, 2 * c2 * LANES:2 * (c2 + 1) * LANES])
        for cc in range(2):
            for b in range(nb):
                bu_scr[2 * c2 + cc, b * PITCH:b * PITCH + T_CHUNK, :] = bu[b * T_CHUNK:(b + 1) * T_CHUNK,
                                                                           cc * LANES:(cc + 1) * LANES]

    lbp = lbp_ref[...]
    log_f, hk = _hgrn_gates(proj(C_HF, BRANCH), lbp[0:1, :], lbp[1:2, :], lbp[2:3, :])
    proj_scr[C_HF // QUARTER, :, C_HF % QUARTER:C_HF % QUARTER + BRANCH] = log_f
    hk_scr[...] = hk
    g_gla = _log_sigmoid(_dot3(lr_scr[...], wup_ref[...]) + bgate_ref[...])
    lr_scr[...] = g_gla * (1.0 / GLA_TAU)
    for c2 in range(n_slab // 2):
        drive(c2)
    a_re = [jnp.broadcast_to(are_ref[:, c * LANES:(c + 1) * LANES], (nb, LANES)) for c in range(half)]
    a_im = [jnp.broadcast_to(aim_ref[:, c * LANES:(c + 1) * LANES], (nb, LANES)) for c in range(half)]
    s_init = s5_ref[...]
    carry0 = tuple(s_init[:, c * LANES:(c + 1) * LANES] for c in range(n_slab))

    def scan_step(t, carry):
        new = [None] * n_slab
        for c in range(half):
            sr, si = carry[c], carry[half + c]
            br = bu_scr[c, pl.ds(t, nb, stride=PITCH), :]
            bi = bu_scr[half + c, pl.ds(t, nb, stride=PITCH), :]
            nr = a_re[c] * sr - a_im[c] * si + br
            ni = a_re[c] * si + a_im[c] * sr + bi
            bu_scr[c, pl.ds(t, nb, stride=PITCH), :] = nr
            bu_scr[half + c, pl.ds(t, nb, stride=PITCH), :] = ni
            new[c], new[half + c] = nr, ni
        return tuple(new)

    carry = lax.fori_loop(0, T_CHUNK, scan_step, carry0)
    s5_ref[...] = jnp.concatenate(carry, axis=1)
    s_all = jnp.concatenate([bu_scr[c] for c in range(n_slab)], axis=1)
    y_all = _dot(s_all, cbd_ref[...])
    y_lin = jnp.concatenate([y_all[b * PITCH:b * PITCH + T_CHUNK] for b in range(nb)], axis=0)
    mix_scr[:, 0:BRANCH] = _s5_output(y_lin, proj(C_U, BRANCH), proj(C_SZ, BRANCH), dskip_ref[...],
                                      wglu_ref[...])

    ones_h = onesh_ref[...]
    ones_g = onesg_ref[...]
    mask_h = ones_h.astype(F32)
    mask_g = jnp.transpose(ones_g.astype(F32))
    tril = tril_ref[...]

    def rows_of(bp, odd):
        return pl.ds(pl.multiple_of((2 * bp + odd) * T_CHUNK, T_CHUNK), T_CHUNK)

    def hgrn_stream(bp, odd, term):
        rs = rows_of(bp, odd)
        return _GatedStream(lambda: proj(C_HQ, BRANCH, rs), lambda: hk_scr[rs, :], lambda: proj(C_HI, BRANCH, rs),
                            lambda: proj(C_HF, BRANCH, rs), hgs_ref.at[bp, odd], ones_h, mask_h, tril, term,
                            HEADS * HG_DK)

    def gla_stream(bp, odd, term):
        rs = rows_of(bp, odd)
        return _GatedStream(lambda: proj(C_GQ, LANES, rs) * (GLA_DK ** -0.5), lambda: proj(C_GK, LANES, rs),
                            lambda: proj(C_GV, BRANCH, rs), lambda: lr_scr[rs, :], glas_ref.at[bp, odd],
                            ones_g, mask_g, tril, term, HEADS * GLA_DK)

    n_sub = T_CHUNK // SUB

    def retention(bp, odd):
        rs = rows_of(bp, odd)
        mix_scr[rs, 3 * BRANCH:4 * BRANCH] = _ret_chunk(
            proj(C_RQ, LANES, rs), proj(C_RK, LANES, rs), proj(C_RV, BRANCH, rs), cos_ref[...],
            sin_ref[...], rets_ref.at[bp, odd], dstack_ref[...], inner_ref[...], kdec_ref[...],
            cdec_ref[...], mask_g)

    def per_pair(bp, _):
        st = [hgrn_stream(bp, 0, term_h0), gla_stream(bp, 0, term_g0),
              hgrn_stream(bp, 1, term_h1), gla_stream(bp, 1, term_g1)]
        slots = [(0, BRANCH), (0, 2 * BRANCH), (1, BRANCH), (1, 2 * BRANCH)]

        def emit(i, j):
            odd, col = slots[i]
            r0 = pl.multiple_of((2 * bp + odd) * T_CHUNK + SUB * j, SUB)
            mix_scr[pl.ds(r0, SUB), col:col + BRANCH] = st[i].output_part(j)

        for s_ in st:
            s_.cumulate()
        for j in range(n_sub):
            st[0].update_part(j)
        for j in range(n_sub):
            st[0].products_part(j)
            st[1].update_part(j)
        for j in range(n_sub):
            st[1].products_part(j)
            st[2].update_part(j)
        for j in range(n_sub):
            st[2].products_part(j)
            st[3].update_part(j)
            st[0].matmul_part(j)
        for j in range(n_sub):
            st[3].products_part(j)
            st[1].matmul_part(j)
        for j in range(n_sub):
            emit(0, j)
            st[2].matmul_part(j)
        for j in range(n_sub):
            emit(1, j)
            st[3].matmul_part(j)
        for j in range(n_sub):
            emit(2, j)
            if j % 2 == 0:
                retention(bp, j // 2)
        for j in range(n_sub):
            emit(3, j)
        return 0

    lax.fori_loop(0, N_QUARTER, per_pair, 0)

    o_hg = _head_rms(mix_scr[:, BRANCH:2 * BRANCH], ones_h, hgn_ref[...])
    mix_scr[:, BRANCH:2 * BRANCH] = o_hg * _silu(proj(C_HZ, BRANCH))
    o_gla = _head_rms(mix_scr[:, 2 * BRANCH:3 * BRANCH], ones_h, glan_ref[...])
    mix_scr[:, 2 * BRANCH:3 * BRANCH] = o_gla * _silu(proj(C_GZ, BRANCH))
    o_ret = _head_ln(mix_scr[:, 3 * BRANCH:4 * BRANCH], ones_h, retn_ref[...])
    mix_scr[:, 3 * BRANCH:4 * BRANCH] = o_ret * _silu(proj(C_RZ, BRANCH))
    hn = _rmsnorm_rows(xn_ref[...].reshape(rows, D_MODEL), normw_ref[...]).astype(BF16)
    for qtr in range(N_QUARTER):
        proj_scr[qtr] = jnp.dot(hn, w_quarter(qtr), preferred_element_type=F32)
    lr_scr[...] = jnp.dot(hn, win_ref[:, C_LR:N_PACK], preferred_element_type=F32)
    out = _mix_and_project(x_ref[...].reshape(rows, D_MODEL), mix_scr[...], wout_ref[...], fnw_ref[...],
                           apply_final)
    y_ref[...] = out.reshape(nb, T_CHUNK, D_MODEL)


def _dot_sel_lhs2(sel, x):
    x1 = x.astype(BF16)
    x2 = (x - x1.astype(F32)).astype(BF16)
    return (jnp.dot(sel, x1, preferred_element_type=F32) + jnp.dot(sel, x2, preferred_element_type=F32))


def _head_rms_t(o, ones_h, gain):
    ms = _dot_sel_lhs2(ones_h, o * o) * (1.0 / DV)
    return o * lax.rsqrt(ms + EPS) * gain


def _head_ln_t(o, ones_h, gain):
    c = o - _dot_sel_lhs2(ones_h, o) * (1.0 / DV)
    var = _dot_sel_lhs2(ones_h, c * c) * (1.0 / DV)
    return c * lax.rsqrt(var + EPS) * gain


def _rotary_t(t, cos, sin_signed, first_half):
    half = RET_DK // 2
    swapped = jnp.where(first_half, pltpu.roll(t, LANES - half, 0), pltpu.roll(t, half, 0))
    return t * cos + swapped * sin_signed


def _sample_kernel(x_ref, normw_ref, wt_ref, wout_ref, bbdt_ref, cbdt_ref, are_ref, aim_ref, dskip_ref,
                   wglut_ref, lbp_ref, hgn_ref, wupt_ref, bgate_ref, glan_ref, retn_ref, fnw_ref,
                   cos_ref, sin_ref, dret_ref, onesh_ref,
                   s5re_ref, s5im_ref, hg_ref, gla_ref, ret_ref,
                   y_ref, s5re_o, s5im_o, hg_o, gla_o, ret_o,
                   xs_scr, pt_scr, hk_scr, ot_scr, mixt_scr):
    layer, head = pl.program_id(0), pl.program_id(1)
    last_layer, last_head = pl.num_programs(0) - 1, pl.num_programs(1) - 1

    @pl.when((layer == 0) & (head == 0))
    def _load_x():
        xs_scr[...] = x_ref[...]

    @pl.when(head == 0)
    def _dense():
        hh = _rmsnorm_rows(xs_scr[...], normw_ref[...]).astype(BF16)
        pt_scr[...] = lax.dot_general(wt_ref[...], hh, (((1,), (1,)), ((), ())), preferred_element_type=F32)

        u = pt_scr[C_U:C_U + BRANCH, :]
        bu = _dot3(bbdt_ref[...], u)
        a_re, a_im = are_ref[...], aim_ref[...]
        s0r, s0i = s5re_ref[...], s5im_ref[...]
        s_re = a_re * s0r - a_im * s0i + bu[0:S5_N]
        s_im = a_re * s0i + a_im * s0r + bu[S5_N:2 * S5_N]
        s5re_o[...] = s_re
        s5im_o[...] = s_im
        y = _gelu_tanh(_dot3(cbdt_ref[...], jnp.concatenate([s_re, s_im], axis=0)) + u * dskip_ref[...])
        y = y * _sigmoid(jnp.dot(wglut_ref[...], y.astype(BF16), preferred_element_type=F32))
        mixt_scr[0:BRANCH, :] = y * _silu(pt_scr[C_SZ:C_SZ + BRANCH, :])

        lbp = lbp_ref[...]
        log_f, hk = _hgrn_gates(pt_scr[C_HF:C_HF + BRANCH, :], lbp[:, 0:1], lbp[:, 1:2], lbp[:, 2:3])
        pt_scr[C_HF:C_HF + BRANCH, :] = jnp.exp(log_f)
        hk_scr[...] = hk
        g_gla = _log_sigmoid(_dot3(wupt_ref[...], pt_scr[C_LR:C_LR + LANES, :]) + bgate_ref[...])
        pt_scr[C_LR:C_LR + LANES, :] = jnp.exp(g_gla * (1.0 / GLA_TAU))
        pt_scr[C_GQ:C_GQ + LANES, :] = pt_scr[C_GQ:C_GQ + LANES, :] * (GLA_DK ** -0.5)
        row = lax.broadcasted_iota(jnp.int32, (HEADS * RET_DK, LANES), 0)
        first_half = (row % RET_DK) < (RET_DK // 2)
        pt_scr[C_RQ:C_RQ + LANES, :] = _rotary_t(pt_scr[C_RQ:C_RQ + LANES, :], cos_ref[...], sin_ref[...],
                                                 first_half)
        pt_scr[C_RK:C_RK + LANES, :] = _rotary_t(pt_scr[C_RK:C_RK + LANES, :], cos_ref[...], sin_ref[...],
                                                 first_half) * (RET_DK ** -0.5)

    def head_update(s0_ref, s_out_ref, dk, dec_ref, dec_row, key_ref, key_row, q_row, v_row, out_row):
        vt = pt_scr[pl.ds(pl.multiple_of(v_row + head * DV, DV), DV), :]

        def feature(kk, acc):
            r = head * dk + kk
            bcast = lambda ref, r0: jnp.broadcast_to(ref[pl.ds(r0 + r, 1), :], (DV, LANES))
            s_new = s0_ref[kk] * bcast(dec_ref, dec_row) + bcast(key_ref, key_row) * vt
            s_out_ref[kk] = s_new
            return acc + bcast(pt_scr, q_row) * s_new

        acc = lax.fori_loop(0, dk, feature, jnp.zeros((DV, LANES), F32), unroll=4)
        ot_scr[pl.ds(pl.multiple_of(out_row + head * DV, DV), DV), :] = acc

    head_update(hg_ref, hg_o, HG_DK, pt_scr, C_HF, hk_scr, 0, C_HQ, C_HI, 0)
    head_update(gla_ref, gla_o, GLA_DK, pt_scr, C_LR, pt_scr, C_GK, C_GQ, C_GV, BRANCH)
    head_update(ret_ref, ret_o, RET_DK, dret_ref, 0, pt_scr, C_RK, C_RQ, C_RV, 2 * BRANCH)

    @pl.when(head == last_head)
    def _finish():
        ones_h = onesh_ref[...]
        o_hg = _head_rms_t(ot_scr[0:BRANCH, :], ones_h, hgn_ref[...])
        mixt_scr[BRANCH:2 * BRANCH, :] = o_hg * _silu(pt_scr[C_HZ:C_HZ + BRANCH, :])
        o_gla = _head_rms_t(ot_scr[BRANCH:2 * BRANCH, :], ones_h, glan_ref[...])
        mixt_scr[2 * BRANCH:3 * BRANCH, :] = o_gla * _silu(pt_scr[C_GZ:C_GZ + BRANCH, :])
        o_ret = _head_ln_t(ot_scr[2 * BRANCH:3 * BRANCH, :], ones_h, retn_ref[...])
        mixt_scr[3 * BRANCH:4 * BRANCH, :] = o_ret * _silu(pt_scr[C_RZ:C_RZ + BRANCH, :])
        out = xs_scr[...] + lax.dot_general(mixt_scr[...].astype(BF16), wout_ref[...], (((0,), (0,)), ((), ())),
                                            preferred_element_type=F32)
        xs_scr[...] = out

        @pl.when(layer == last_layer)
        def _emit():
            y_ref[...] = _rmsnorm_rows(out, fnw_ref[...])


def _ret_log_gamma():
    return jnp.log1p(-jnp.exp2(-5.0 - jnp.arange(HEADS, dtype=F32)))


def _constants():
    ones_h = (np.arange(BRANCH)[:, None] // DV == np.arange(BRANCH)[None, :] // DV)
    ones_g = (np.arange(HEADS * GLA_DK)[:, None] // GLA_DK == np.arange(BRANCH)[None, :] // DV)
    r = np.arange(T_CHUNK)
    same_sub = r[:, None] // SUB == r[None, :] // SUB
    tril = same_sub & (r[None, :] <= r[:, None])
    same_group_b = np.arange(BRANCH)[:, None] // S5_CH == np.arange(S5_N)[None, :] // S5_STATE
    as_bf16 = lambda m: jnp.asarray(m.astype(np.float32), dtype=BF16)
    return dict(ones_h=as_bf16(ones_h), ones_g=as_bf16(ones_g), tril=as_bf16(tril),
                s5_mask=jnp.asarray(same_group_b.astype(np.float32)))


def _ret_tables():
    lg = _ret_log_gamma()
    idx = jnp.arange(T_CHUNK, dtype=F32)
    rel = idx[:, None] - idx[None, :]
    causal = rel >= 0
    decay = jnp.where(causal[None], jnp.exp(jnp.where(causal, rel, 0.0)[None] * lg[:, None, None]), 0.0)
    dstack = jnp.transpose(decay, (1, 0, 2)).reshape(T_CHUNK, HEADS * T_CHUNK)
    inner = jnp.repeat(jnp.exp((idx[:, None] + 1.0) * lg[None, :]), DV, axis=1)
    kdec = jnp.repeat(jnp.exp((T_CHUNK - 1.0 - idx[:, None]) * lg[None, :]), RET_DK, axis=1)
    cdec = jnp.repeat(jnp.exp(T_CHUNK * lg)[None, :], RET_DK, axis=1)
    dret = jnp.broadcast_to(jnp.repeat(jnp.exp(lg), RET_DK)[:, None], (HEADS * RET_DK, LANES))
    return dstack, inner, kdec, cdec, dret


def _rope_tables(pos):
    half = RET_DK // 2
    inv = ROPE_BASE ** (-jnp.arange(half, dtype=F32) / half)
    ang = pos.astype(F32)[:, None] * inv[None, :]
    cos, sin = jnp.cos(ang), jnp.sin(ang)
    cos_t = jnp.tile(jnp.concatenate([cos, cos], axis=1), (1, HEADS))
    sin_t = jnp.tile(jnp.concatenate([-sin, sin], axis=1), (1, HEADS))
    return cos_t, sin_t


def _pack_w_in_t(w):
    wt = jnp.swapaxes(w, 1, 2)
    offs = np.cumsum([0, 256, 256, 256, 256, 256, 256, 128, 128, 256, 16, 256, 128, 128, 256, 256])
    seg = lambda i: wt[:, int(offs[i]):int(offs[i + 1]), :]
    order = [0, 1, 2, 3, 4, 5, 6, 7, 8, 10, 11, 12, 13, 14]
    pad = jnp.zeros((w.shape[0], LANES - GLA_LOWRANK, w.shape[1]), w.dtype)
    return jnp.concatenate([seg(i) for i in order] + [seg(9), pad], axis=1).astype(BF16)


def _s5_discretize(lam_re, lam_im, log_step, b_re, b_im, c_re, c_im, mask):
    lr, li = lam_re.astype(F32), lam_im.astype(F32)
    step = jnp.exp(log_step.astype(F32))[..., None]
    mag = jnp.exp(lr * step)
    ab_re = mag * jnp.cos(li * step)
    ab_im = mag * jnp.sin(li * step)
    den = lr * lr + li * li
    nr = ab_re - 1.0
    f_re = (nr * lr + ab_im * li) / den
    f_im = (ab_im * lr - nr * li) / den
    br, bi = b_re.astype(F32), b_im.astype(F32)
    bb_re = f_re[..., None] * br - f_im[..., None] * bi
    bb_im = f_re[..., None] * bi + f_im[..., None] * br
    nl = lr.shape[0]

    def drive(bb):
        rows = jnp.transpose(bb, (0, 1, 3, 2)).reshape(nl, BRANCH, S5_STATE)
        return jnp.tile(rows, (1, 1, S5_GROUPS)) * mask

    def readout(cc):
        rows = cc.astype(F32).reshape(nl, BRANCH, S5_STATE)
        return jnp.tile(rows, (1, 1, S5_GROUPS)) * mask

    bbd = jnp.concatenate([drive(bb_re), drive(bb_im)], axis=2)
    cbd_t = jnp.concatenate([readout(c_re), -readout(c_im)], axis=2)
    return bbd, cbd_t, ab_re.reshape(nl, 1, S5_N), ab_im.reshape(nl, 1, S5_N)


def _full(shape):
    return pl.BlockSpec(shape, lambda *_: (0,) * len(shape), pipeline_mode=pl.Buffered(1))


def _of_layer(arr, layer):
    nd = arr.ndim - 1
    return pl.BlockSpec((None,) + arr.shape[1:], lambda *_: (layer,) + (0,) * nd,
                        pipeline_mode=pl.Buffered(1))


def _prompt_layer(x, layer, p, consts, tabs, rope, apply_final):
    nb, seq, _ = x.shape
    n_steps = seq // T_CHUNK
    rows = nb * T_CHUNK
    dstack, inner, kdec, cdec, _ = tabs
    cos_t, sin_t = rope
    per_layer = [p['norm_w'], p['w_pack'], p['w_out'], p['bbd'].astype(BF16), p['cbd'],
                 p['a_re'], p['a_im'],
                 p['d_skip'], p['w_glu'], p['lbp'], p['hg_norm'], p['w_up'], p['b_gate'], p['gla_norm'],
                 p['ret_norm']]
    shared = [dstack, inner, kdec, cdec, consts['ones_h'], consts['ones_g'], consts['tril'],
              p['final_norm']]
    in_specs = [pl.BlockSpec((nb, T_CHUNK, D_MODEL), lambda i: (0, i, 0)),
                pl.BlockSpec((nb, T_CHUNK, D_MODEL), lambda i: (0, jnp.minimum(i + 1, n_steps - 1), 0))]
    in_specs += [_of_layer(a, layer) for a in per_layer]
    in_specs += [pl.BlockSpec((T_CHUNK, LANES), lambda i: (i, 0)), pl.BlockSpec((T_CHUNK, LANES), lambda i: (i, 0))]
    in_specs += [_full(a.shape) for a in shared]
    out_shape = (jax.ShapeDtypeStruct((nb, seq, D_MODEL), F32),
                 jax.ShapeDtypeStruct((nb, 2 * S5_N), F32),
                 jax.ShapeDtypeStruct((nb // 2, 2, BRANCH, HEADS * HG_DK), F32),
                 jax.ShapeDtypeStruct((nb // 2, 2, BRANCH, HEADS * GLA_DK), F32),
                 jax.ShapeDtypeStruct((nb // 2, 2, BRANCH, HEADS * RET_DK), F32))
    out_specs = (pl.BlockSpec((nb, T_CHUNK, D_MODEL), lambda i: (0, i, 0)),
                 pl.BlockSpec((nb, 2 * S5_N), lambda i: (0, 0)),
                 pl.BlockSpec((nb // 2, 2, BRANCH, HEADS * HG_DK), lambda i: (0, 0, 0, 0)),
                 pl.BlockSpec((nb // 2, 2, BRANCH, HEADS * GLA_DK), lambda i: (0, 0, 0, 0)),
                 pl.BlockSpec((nb // 2, 2, BRANCH, HEADS * RET_DK), lambda i: (0, 0, 0, 0)))
    scratch = [pltpu.VMEM((N_QUARTER, rows, QUARTER), F32), pltpu.VMEM((rows, LANES), F32),
               pltpu.VMEM((rows, BRANCH), F32),
               pltpu.VMEM((2 * S5_N // LANES, nb * PITCH, LANES), F32),
               pltpu.VMEM((rows, D_MODEL), F32),
               pltpu.VMEM((TERM_ROWS, HEADS * HG_DK), F32), pltpu.VMEM((TERM_ROWS, HEADS * GLA_DK), F32),
               pltpu.VMEM((TERM_ROWS, HEADS * HG_DK), F32), pltpu.VMEM((TERM_ROWS, HEADS * GLA_DK), F32)]
    y, s5, hgs, glas, rets = pl.pallas_call(
        functools.partial(_prompt_kernel, apply_final),
        grid=(n_steps,), in_specs=in_specs, out_specs=out_specs, out_shape=out_shape,
        scratch_shapes=scratch, name='prompt_layer',
        compiler_params=pltpu.CompilerParams(dimension_semantics=('arbitrary',),
                                             vmem_limit_bytes=VMEM_LIMIT),
    )(x, x, *per_layer, cos_t, sin_t, *shared)
    s5 = s5.reshape(nb, 2, S5_GROUPS, S5_STATE)

    def unstack(st, dk):
        st = st.reshape(nb, HEADS, DV, HEADS, dk)
        diag = jnp.stack([st[:, hh, :, hh, :] for hh in range(HEADS)], axis=1)
        return jnp.transpose(diag, (0, 1, 3, 2))

    return y, (s5[:, 0], s5[:, 1], unstack(hgs, HG_DK), unstack(glas, GLA_DK), unstack(rets, RET_DK))


def _sample_step(x, states, p, consts, tabs, rope):
    nb = x.shape[0]
    depth = p['w_t'].shape[0]
    s5re, s5im, hg, gla, ret = states
    to_lanes = lambda s: jnp.moveaxis(s, 1, -1)
    s5re_t = to_lanes(s5re).reshape(depth, S5_N, nb)
    s5im_t = to_lanes(s5im).reshape(depth, S5_N, nb)
    hg_t, gla_t, ret_t = to_lanes(hg), to_lanes(gla), to_lanes(ret)
    cos_t, sin_t = rope
    col = lambda a: jnp.swapaxes(a, -1, -2)
    per_layer = [p['norm_w'], p['w_t'], p['w_out'], col(p['bbd']), p['cbd_t'], col(p['a_re']), col(p['a_im']),
                 col(p['d_skip']), col(p['w_glu']), col(p['lbp']), col(p['hg_norm']), col(p['w_up']),
                 col(p['b_gate']), col(p['gla_norm']), col(p['ret_norm'])]
    shared = [p['final_norm'], col(cos_t), col(sin_t), tabs[4], consts['ones_h']]
    layer_spec = lambda a: pl.BlockSpec((None,) + a.shape[1:], lambda l, h: (l,) + (0,) * (a.ndim - 1))
    head_spec = lambda a: pl.BlockSpec((None, None) + a.shape[2:], lambda l, h: (l, h) + (0,) * (a.ndim - 2))
    state_specs = [layer_spec(s5re_t), layer_spec(s5im_t), head_spec(hg_t), head_spec(gla_t), head_spec(ret_t)]
    in_specs = ([_full(x.shape)] + [layer_spec(a) for a in per_layer] + [_full(a.shape) for a in shared]
                + state_specs)
    state_arrays = [s5re_t, s5im_t, hg_t, gla_t, ret_t]
    out_shape = tuple([jax.ShapeDtypeStruct(x.shape, F32)]
                      + [jax.ShapeDtypeStruct(a.shape, F32) for a in state_arrays])
    out_specs = tuple([pl.BlockSpec(x.shape, lambda l, h: (0, 0))] + state_specs)
    scratch = [pltpu.VMEM((nb, D_MODEL), F32), pltpu.VMEM((N_PACK, nb), F32), pltpu.VMEM((BRANCH, nb), F32),
               pltpu.VMEM((3 * BRANCH, nb), F32), pltpu.VMEM((4 * BRANCH, nb), F32)]
    outs = pl.pallas_call(
        _sample_kernel, grid=(depth, HEADS), in_specs=in_specs, out_specs=out_specs, out_shape=out_shape,
        scratch_shapes=scratch, name='sample_step',
        compiler_params=pltpu.CompilerParams(dimension_semantics=('arbitrary', 'arbitrary'),
                                             vmem_limit_bytes=VMEM_LIMIT),
    )(x, *per_layer, *shared, *state_arrays)
    from_lanes = lambda s: jnp.moveaxis(s, -1, 1)
    new = (from_lanes(outs[1].reshape(depth, S5_GROUPS, S5_STATE, nb)),
           from_lanes(outs[2].reshape(depth, S5_GROUPS, S5_STATE, nb)),
           from_lanes(outs[3]), from_lanes(outs[4]), from_lanes(outs[5]))
    return outs[0], new


def kernel(x_prompt, x_sample, state_s5_re, state_s5_im, state_hgrn, state_gla, state_ret, norm_w, final_norm_w, w_in, w_out, s5_lam_re, s5_lam_im, s5_log_step, s5_b_re, s5_b_im, s5_c_re, s5_c_im, s5_d, s5_w_glu, hgrn_lb_logits, hgrn_norm_w, gla_w_gate_up, gla_b_gate, gla_norm_w, ret_norm_w):
    depth = w_in.shape[0]
    seq = x_prompt.shape[1]
    consts = _constants()
    tabs = _ret_tables()
    rope_p = _rope_tables(jnp.arange(seq))
    rope_s = _rope_tables(PAST_LEN + jnp.arange(1))
    lb = jnp.cumsum(jax.nn.softmax(hgrn_lb_logits.astype(F32), axis=0), axis=0)
    lb = (lb - lb[0:1])[:, None, :]
    bbd, cbd_t, a_re, a_im = _s5_discretize(s5_lam_re, s5_lam_im, s5_log_step, s5_b_re, s5_b_im, s5_c_re,
                                            s5_c_im, consts['s5_mask'])
    w_t = lax.optimization_barrier(_pack_w_in_t(w_in))
    row = lambda a: a[:, None, :].astype(F32)
    w_up = jnp.zeros((depth, LANES, HEADS * GLA_DK), F32).at[:, :GLA_LOWRANK].set(gla_w_gate_up.astype(F32))
    p = dict(norm_w=row(norm_w),
             w_pack=jnp.swapaxes(w_t, 1, 2), w_t=w_t, w_out=w_out.astype(BF16),
             bbd=bbd, cbd=jnp.swapaxes(cbd_t, 1, 2).astype(BF16), cbd_t=cbd_t, a_re=a_re, a_im=a_im,
             d_skip=row(s5_d), w_glu=s5_w_glu.astype(BF16),
             lbp=jnp.concatenate([jnp.log(lb), jnp.log1p(-lb), 1.0 - lb, jnp.zeros((depth, 5, BRANCH), F32)],
                                 axis=1),
             hg_norm=row(hgrn_norm_w), w_up=w_up, b_gate=row(gla_b_gate), gla_norm=row(gla_norm_w),
             ret_norm=row(ret_norm_w), final_norm=final_norm_w[None, :].astype(F32))

    xp = x_prompt
    new_p = ([], [], [], [], [])
    for l in range(depth):
        xp, st_p = _prompt_layer(xp, l, p, consts, tabs, rope_p, l == depth - 1)
        for i in range(5):
            new_p[i].append(st_p[i])
    xs, new_s = _sample_step(x_sample.reshape(x_sample.shape[0], D_MODEL),
                             (state_s5_re, state_s5_im, state_hgrn, state_gla, state_ret),
                             p, consts, tabs, rope_s)
    return (xp, xs.reshape(x_sample.shape),
            jnp.stack(new_p[0]), jnp.stack(new_p[1]), jnp.stack(new_p[2]), jnp.stack(new_p[3]),
            jnp.stack(new_p[4])) + new_s
```

```python
import functools
import math

import numpy as np
import jax
import jax.numpy as jnp
from jax import lax
from jax.experimental import pallas as pl
from jax.experimental.pallas import tpu as pltpu

F32 = jnp.float32
BF16 = jnp.bfloat16

D_MODEL = 1024
BRANCH = 256
S5_CH = 16
S5_GROUPS = 16
S5_STATE = 64
S5_N = S5_GROUPS * S5_STATE
HEADS = 4
HG_DK = 64
GLA_DK = 32
RET_DK = 32
DV = 64
GLA_LOWRANK = 16
GLA_TAU = 16.0
ROPE_BASE = 10000.0
PAST_LEN = 16384
EPS = 1e-6
SUB = 16
PAIR = 2 * SUB

LANES = 128
T_CHUNK = 64
PITCH = T_CHUNK + 8
SUB_TERM_ROWS = SUB * (SUB + SUB // 2) // 2
TERM_ROWS = (T_CHUNK // SUB) * SUB_TERM_ROWS
LOG2E = math.log2(math.e)
MASK_NEG = -1e30
VMEM_LIMIT = 60 * 1024 * 1024

C_U, C_SZ, C_HQ, C_HF, C_HI, C_HZ = 0, 256, 512, 768, 1024, 1280
C_GQ, C_GK, C_GV, C_GZ = 1536, 1664, 1792, 2048
C_RQ, C_RK, C_RV, C_RZ = 2304, 2432, 2560, 2816
C_LR = 3072
N_PACK = 3200
N_QUARTER = 4
QUARTER = C_LR // N_QUARTER


def _dot(a, b):
    return jnp.dot(a.astype(BF16), b.astype(BF16), preferred_element_type=F32)


def _dot_nt(a, b):
    return lax.dot_general(a.astype(BF16), b.astype(BF16), (((1,), (1,)), ((), ())),
                           preferred_element_type=F32)


def _dot_tn(a, b):
    return lax.dot_general(a.astype(BF16), b.astype(BF16), (((0,), (0,)), ((), ())),
                           preferred_element_type=F32)


def _split3(x):
    x1 = x.astype(BF16)
    r1 = x - x1.astype(F32)
    x2 = r1.astype(BF16)
    x3 = (r1 - x2.astype(F32)).astype(BF16)
    return x1, x2, x3


def _dot_sel_lhs(sel, x):
    x1, x2, x3 = _split3(x)
    d = lambda p: jnp.dot(sel, p, preferred_element_type=F32)
    return d(x1) + d(x2) + d(x3)


def _dot_sel_rhs2(x, sel):
    x1 = x.astype(BF16)
    x2 = (x - x1.astype(F32)).astype(BF16)
    return (jnp.dot(x1, sel, preferred_element_type=F32)
            + jnp.dot(x2, sel, preferred_element_type=F32))


def _dot3(a, b):
    a1 = a.astype(BF16)
    a2 = (a - a1.astype(F32)).astype(BF16)
    b1 = b.astype(BF16)
    b2 = (b - b1.astype(F32)).astype(BF16)
    d = lambda p, q: jnp.dot(p, q, preferred_element_type=F32)
    return d(a1, b1) + d(a1, b2) + d(a2, b1)


def _sigmoid(x):
    return 0.5 * jnp.tanh(0.5 * x) + 0.5


def _silu(x):
    return x * _sigmoid(x)


def _log_sigmoid(x):
    return jnp.minimum(x, 0.0) - jnp.log(1.0 + jnp.exp(-jnp.abs(x)))


def _gelu_tanh(x):
    return 0.5 * x * (1.0 + jnp.tanh(math.sqrt(2.0 / math.pi) * (x + 0.044715 * (x * x * x))))


def _rmsnorm_rows(x, w):
    return x * lax.rsqrt(jnp.mean(x * x, axis=-1, keepdims=True) + EPS) * w


def _head_rms(o, ones_h, gain):
    ms = _dot_sel_rhs2(o * o, ones_h) * (1.0 / DV)
    return o * lax.rsqrt(ms + EPS) * gain


def _head_ln(o, ones_h, gain):
    c = o - _dot_sel_rhs2(o, ones_h) * (1.0 / DV)
    var = _dot_sel_rhs2(c * c, ones_h) * (1.0 / DV)
    return c * lax.rsqrt(var + EPS) * gain


def _hgrn_gates(xf, loglb, log1mlb, one_m_lb):
    bterm = log1mlb + _log_sigmoid(xf)
    m = jnp.maximum(loglb, bterm)
    log_f = m + jnp.log(jnp.exp(loglb - m) + jnp.exp(bterm - m))
    return log_f, one_m_lb * _sigmoid(-xf)


def _rotary(t, cos, sin_signed, first_half):
    half = RET_DK // 2
    swapped = jnp.where(first_half, pltpu.roll(t, LANES - half, 1), pltpu.roll(t, half, 1))
    return t * cos + swapped * sin_signed


def _s5_output(y_lin, u, sz, dskip, wglu):
    y = _gelu_tanh(y_lin + u * dskip)
    y = y * _sigmoid(_dot(y, wglu))
    return y * _silu(sz)


def _mix_and_project(x, mix, wout, fnw, apply_final):
    out = x + _dot(mix, wout)
    if apply_final:
        out = _rmsnorm_rows(out, fnw)
    return out


class _GatedStream:
    def __init__(self, q, k, v, g, s_ref, ones_kv, mask_vk, tril, term_scr, row_scr, kdim):
        self.q, self.k, self.v, self.g = q, k, v, g
        self.bc, self.kc, self.vc = row_scr
        self.s_ref, self.ones_kv, self.mask_vk = s_ref, ones_kv, mask_vk
        self.tril, self.term_scr, self.kdim = tril, term_scr, kdim
        self.ws, self.inter, self.att, self.cross = {}, {}, {}, {}

    def cumulate(self):
        self._keep(self.bc, _dot_sel_lhs(self.tril, self.g()) * LOG2E)
        self._keep(self.kc, self.k())
        self._keep(self.vc, self.v())

    @staticmethod
    def _keep(slabs, value):
        for c in range(slabs.shape[0]):
            slabs[c] = value[:, c * LANES:(c + 1) * LANES]

    @staticmethod
    def _tile(slabs, r0, n):
        return jnp.concatenate([slabs[c, r0:r0 + n, :] for c in range(slabs.shape[0])], axis=1)

    @staticmethod
    def _row(slabs, r, n):
        return jnp.concatenate([slabs[c, pl.ds(r, n, stride=0), :] for c in range(slabs.shape[0])], axis=1)

    def _total(self, p):
        return self._tile(self.bc, PAIR * (p + 1) - 1, 1)

    def update_part(self, p):
        sl = slice(PAIR * p, PAIR * (p + 1))
        self.ws[p] = _dot_tn(self.v()[sl],
                             self.k()[sl] * jnp.exp2(self._total(p) - self._tile(self.bc, PAIR * p, PAIR)))

    def cross_part(self, p):
        r_a, r_b = PAIR * p, PAIR * p + SUB
        edge = self._row(self.bc, r_b - 1, SUB)
        q_b = self.q()[r_b:r_b + SUB] * jnp.exp2(self._tile(self.bc, r_b, SUB) - edge)
        k_a = self._tile(self.kc, r_a, SUB) * jnp.exp2(edge - self._tile(self.bc, r_a, SUB))
        lane = lax.broadcasted_iota(jnp.int32, (SUB, self.kdim), 1) // (self.kdim // HEADS)
        q_heads = jnp.concatenate([jnp.where(lane == h, q_b, 0.0) for h in range(HEADS)], axis=0)
        scores = _dot_nt(q_heads, k_a)
        res = _dot(scores, self._tile(self.vc, r_a, SUB))
        lane_v = lax.broadcasted_iota(jnp.int32, (SUB, BRANCH), 1) // DV
        out = jnp.where(lane_v == 0, res[0:SUB], 0.0)
        for h in range(1, HEADS):
            out = out + jnp.where(lane_v == h, res[SUB * h:SUB * (h + 1)], 0.0)
        self.cross[p] = out

    def products_part(self, j):
        hs, term_scr, r0 = SUB // 2, self.term_scr, SUB * j
        q, bc = self.q()[r0:r0 + SUB], self._tile(self.bc, r0, SUB)
        trow = lax.broadcasted_iota(jnp.int32, (hs, self.kdim), 0)
        neg = [jnp.where(trow >= d, 0.0, MASK_NEG) for d in range(1, hs)]
        q_lo, q_hi, b_lo, b_hi = q[0:hs], q[hs:SUB], bc[0:hs], bc[hs:SUB]
        row = j * SUB_TERM_ROWS
        for s in range(SUB):
            ks, bs = self._row(self.kc, r0 + s, hs), self._row(self.bc, r0 + s, hs)
            d = s % hs
            q_dg, b_dg = (q_lo, b_lo) if s < hs else (q_hi, b_hi)
            e_dg = b_dg - bs if d == 0 else b_dg - bs + neg[d - 1]
            term_scr[row:row + hs, :] = q_dg * ks * jnp.exp2(e_dg)
            row += hs
            if s < hs:
                term_scr[row:row + hs, :] = q_hi * ks * jnp.exp2(b_hi - bs)
                row += hs

    def matmul_part(self, p):
        sl = slice(PAIR * p, PAIR * (p + 1))
        if p == 0:
            self.state = self.s_ref[...]
        self.inter[p] = _dot_nt(self.q()[sl] * jnp.exp2(self._tile(self.bc, PAIR * p, PAIR)), self.state)
        self.state = self.state * jnp.exp2(self._total(p)) + self.ws[p] * self.mask_vk
        if p == T_CHUNK // PAIR - 1:
            self.s_ref[...] = self.state
        rows = slice(2 * p * SUB_TERM_ROWS, 2 * (p + 1) * SUB_TERM_ROWS)
        self.att[p] = _dot(self.term_scr[rows, :], self.ones_kv)

    def output_part(self, j):
        hs, r0, p, second = SUB // 2, SUB * j, j // 2, j % 2
        att = self.att[p][second * SUB_TERM_ROWS:(second + 1) * SUB_TERM_ROWS]
        start = self.inter[p][second * SUB:(second + 1) * SUB]
        if second:
            start = start + self.cross[p]
        acc_lo, acc_hi = start[0:hs], start[hs:SUB]
        row = 0
        for s in range(SUB):
            vs = self._row(self.vc, r0 + s, hs)
            if s < hs:
                acc_lo = acc_lo + att[row:row + hs] * vs
                row += hs
            acc_hi = acc_hi + att[row:row + hs] * vs
            row += hs
        return jnp.concatenate([acc_lo, acc_hi], axis=0)


def _ret_chunk(q, k, v, cos, sin_signed, s_ref, dstack, inner, kdec, cdec, mask_vk):
    lane = lax.broadcasted_iota(jnp.int32, (T_CHUNK, HEADS * RET_DK), 1)
    first_half = (lane % RET_DK) < (RET_DK // 2)
    rq = _rotary(q, cos, sin_signed, first_half)
    rk = _rotary(k, cos, sin_signed, first_half) * (RET_DK ** -0.5)
    lane_v = lax.broadcasted_iota(jnp.int32, (T_CHUNK, HEADS * DV), 1)
    kst = jnp.concatenate([jnp.where(lane // RET_DK == h, rk, 0.0) for h in range(HEADS)], axis=0)
    vst = jnp.concatenate([jnp.where(lane_v // DV == h, v, 0.0) for h in range(HEADS)], axis=0)
    p = _dot_nt(rq, kst) * dstack
    s_t = s_ref[...]
    o = _dot(p, vst) + _dot_nt(rq, s_t) * inner
    s_ref[...] = s_t * cdec + _dot_tn(v, rk * kdec) * mask_vk
    return o


def _prompt_kernel(apply_final,
                   x_ref, xn_ref, normw_ref, win_ref, wout_ref, bbd_ref, cbd_ref, are_ref, aim_ref,
                   dskip_ref, wglu_ref, lbp_ref, hgn_ref, wup_ref, bgate_ref, glan_ref, retn_ref, cos_ref,
                   sin_ref, dstack_ref, inner_ref, kdec_ref, cdec_ref, onesh_ref, onesg_ref, tril_ref,
                   fnw_ref,
                   y_ref, s5_ref, hgs_ref, glas_ref, rets_ref,
                   proj_scr, lr_scr, hk_scr, bu_scr, mix_scr, term_h0, term_g0, term_h1, term_g1,
                   *row_scratch):
    nb = x_ref.shape[0]
    rows = nb * T_CHUNK
    n_slab = 2 * S5_N // LANES
    half = n_slab // 2
    step = pl.program_id(0)

    def w_quarter(qtr):
        return win_ref[:, qtr * QUARTER:(qtr + 1) * QUARTER]

    def proj(col, width, rs=slice(None)):
        return proj_scr[col // QUARTER, rs, col % QUARTER:col % QUARTER + width]

    @pl.when(step == 0)
    def _init():
        s5_ref[...] = jnp.zeros_like(s5_ref)
        hgs_ref[...] = jnp.zeros_like(hgs_ref)
        glas_ref[...] = jnp.zeros_like(glas_ref)
        rets_ref[...] = jnp.zeros_like(rets_ref)
        bu_scr[...] = jnp.zeros_like(bu_scr)
        h0 = _rmsnorm_rows(x_ref[...].reshape(rows, D_MODEL), normw_ref[...]).astype(BF16)
        for qtr in range(N_QUARTER):
            proj_scr[qtr] = jnp.dot(h0, w_quarter(qtr), preferred_element_type=F32)
        lr_scr[...] = jnp.dot(h0, win_ref[:, C_LR:N_PACK], preferred_element_type=F32)

    def drive(c2):
        bu = _dot(proj(C_U, BRANCH), bbd_ref[:, 2 * c2 * LANES:2 * (c2 + 1) * LANES])
        for cc in range(2):
            for b in range(nb):
                bu_scr[2 * c2 + cc, b * PITCH:b * PITCH + T_CHUNK, :] = bu[b * T_CHUNK:(b + 1) * T_CHUNK,
                                                                           cc * LANES:(cc + 1) * LANES]

    lbp = lbp_ref[...]
    log_f, hk = _hgrn_gates(proj(C_HF, BRANCH), lbp[0:1, :], lbp[1:2, :], lbp[2:3, :])
    proj_scr[C_HF // QUARTER, :, C_HF % QUARTER:C_HF % QUARTER + BRANCH] = log_f
    hk_scr[...] = hk
    g_gla = _log_sigmoid(_dot3(lr_scr[...], wup_ref[...]) + bgate_ref[...])
    lr_scr[...] = g_gla * (1.0 / GLA_TAU)
    for c2 in range(n_slab // 2):
        drive(c2)
    a_re = [jnp.broadcast_to(are_ref[:, c * LANES:(c + 1) * LANES], (nb, LANES)) for c in range(half)]
    a_im = [jnp.broadcast_to(aim_ref[:, c * LANES:(c + 1) * LANES], (nb, LANES)) for c in range(half)]
    s_init = s5_ref[...]
    carry0 = tuple(s_init[:, c * LANES:(c + 1) * LANES] for c in range(n_slab))

    def scan_step(t, carry):
        new = [None] * n_slab
        for c in range(half):
            sr, si = carry[c], carry[half + c]
            br = bu_scr[c, pl.ds(t, nb, stride=PITCH), :]
            bi = bu_scr[half + c, pl.ds(t, nb, stride=PITCH), :]
            nr = a_re[c] * sr - a_im[c] * si + br
            ni = a_re[c] * si + a_im[c] * sr + bi
            bu_scr[c, pl.ds(t, nb, stride=PITCH), :] = nr
            bu_scr[half + c, pl.ds(t, nb, stride=PITCH), :] = ni
            new[c], new[half + c] = nr, ni
        return tuple(new)

    carry = lax.fori_loop(0, T_CHUNK, scan_step, carry0)
    s5_ref[...] = jnp.concatenate(carry, axis=1)
    s_all = jnp.concatenate([bu_scr[c] for c in range(n_slab)], axis=1)
    y_all = _dot(s_all, cbd_ref[...])
    y_lin = jnp.concatenate([y_all[b * PITCH:b * PITCH + T_CHUNK] for b in range(nb)], axis=0)
    mix_scr[:, 0:BRANCH] = _s5_output(y_lin, proj(C_U, BRANCH), proj(C_SZ, BRANCH), dskip_ref[...],
                                      wglu_ref[...])

    ones_h = onesh_ref[...]
    ones_g = onesg_ref[...]
    mask_h = ones_h.astype(F32)
    mask_g = jnp.transpose(ones_g.astype(F32))
    tril = tril_ref[...]

    def rows_of(bp, odd):
        return pl.ds(pl.multiple_of((2 * bp + odd) * T_CHUNK, T_CHUNK), T_CHUNK)

    def hgrn_stream(bp, odd, term, row_scr):
        rs = rows_of(bp, odd)
        return _GatedStream(lambda: proj(C_HQ, BRANCH, rs), lambda: hk_scr[rs, :], lambda: proj(C_HI, BRANCH, rs),
                            lambda: proj(C_HF, BRANCH, rs), hgs_ref.at[bp, odd], ones_h, mask_h, tril, term,
                            row_scr, HEADS * HG_DK)

    def gla_stream(bp, odd, term, row_scr):
        rs = rows_of(bp, odd)
        return _GatedStream(lambda: proj(C_GQ, LANES, rs) * (GLA_DK ** -0.5), lambda: proj(C_GK, LANES, rs),
                            lambda: proj(C_GV, BRANCH, rs), lambda: lr_scr[rs, :], glas_ref.at[bp, odd],
                            ones_g, mask_g, tril, term, row_scr, HEADS * GLA_DK)

    n_sub = T_CHUNK // SUB
    rows_h0, rows_g0, rows_h1, rows_g1 = (row_scratch[3 * i:3 * i + 3] for i in range(4))

    def retention(bp, odd):
        rs = rows_of(bp, odd)
        mix_scr[rs, 3 * BRANCH:4 * BRANCH] = _ret_chunk(
            proj(C_RQ, LANES, rs), proj(C_RK, LANES, rs), proj(C_RV, BRANCH, rs), cos_ref[...],
            sin_ref[...], rets_ref.at[bp, odd], dstack_ref[...], inner_ref[...], kdec_ref[...],
            cdec_ref[...], mask_g)

    def per_pair(bp, _):
        st = [hgrn_stream(bp, 0, term_h0, rows_h0), gla_stream(bp, 0, term_g0, rows_g0),
              hgrn_stream(bp, 1, term_h1, rows_h1), gla_stream(bp, 1, term_g1, rows_g1)]
        slots = [(0, BRANCH), (0, 2 * BRANCH), (1, BRANCH), (1, 2 * BRANCH)]

        def emit(i, j):
            odd, col = slots[i]
            r0 = pl.multiple_of((2 * bp + odd) * T_CHUNK + SUB * j, SUB)
            mix_scr[pl.ds(r0, SUB), col:col + BRANCH] = st[i].output_part(j)

        def fill(j, updates, crosses, matmuls):
            if updates is not None and j < n_sub // 2:
                st[updates].update_part(j)
            if crosses is not None and j >= n_sub // 2:
                st[crosses].cross_part(j - n_sub // 2)
            if matmuls is not None and j % 2 == 1:
                st[matmuls].matmul_part(j // 2)

        for s_ in st:
            s_.cumulate()
        for p in range(n_sub // 2):
            st[0].update_part(p)
        for j in range(n_sub):
            st[0].products_part(j)
            fill(j, 1, 0, None)
        for j in range(n_sub):
            st[1].products_part(j)
            fill(j, 2, 1, None)
        for j in range(n_sub):
            st[2].products_part(j)
            fill(j, 3, 2, 0)
        for j in range(n_sub):
            st[3].products_part(j)
            fill(j, None, 3, 1)
        for j in range(n_sub):
            emit(0, j)
            fill(j, None, None, 2)
        for j in range(n_sub):
            emit(1, j)
            fill(j, None, None, 3)
        for j in range(n_sub):
            emit(2, j)
            if j % 2 == 0:
                retention(bp, j // 2)
        for j in range(n_sub):
            emit(3, j)
        return 0

    lax.fori_loop(0, N_QUARTER, per_pair, 0)

    o_hg = _head_rms(mix_scr[:, BRANCH:2 * BRANCH], ones_h, hgn_ref[...])
    mix_scr[:, BRANCH:2 * BRANCH] = o_hg * _silu(proj(C_HZ, BRANCH))
    o_gla = _head_rms(mix_scr[:, 2 * BRANCH:3 * BRANCH], ones_h, glan_ref[...])
    mix_scr[:, 2 * BRANCH:3 * BRANCH] = o_gla * _silu(proj(C_GZ, BRANCH))
    o_ret = _head_ln(mix_scr[:, 3 * BRANCH:4 * BRANCH], ones_h, retn_ref[...])
    mix_scr[:, 3 * BRANCH:4 * BRANCH] = o_ret * _silu(proj(C_RZ, BRANCH))
    hn = _rmsnorm_rows(xn_ref[...].reshape(rows, D_MODEL), normw_ref[...]).astype(BF16)
    for qtr in range(N_QUARTER):
        proj_scr[qtr] = jnp.dot(hn, w_quarter(qtr), preferred_element_type=F32)
    lr_scr[...] = jnp.dot(hn, win_ref[:, C_LR:N_PACK], preferred_element_type=F32)
    out = _mix_and_project(x_ref[...].reshape(rows, D_MODEL), mix_scr[...], wout_ref[...], fnw_ref[...],
                           apply_final)
    y_ref[...] = out.reshape(nb, T_CHUNK, D_MODEL)


def _dot_sel_lhs2(sel, x):
    x1 = x.astype(BF16)
    x2 = (x - x1.astype(F32)).astype(BF16)
    return (jnp.dot(sel, x1, preferred_element_type=F32) + jnp.dot(sel, x2, preferred_element_type=F32))


def _head_rms_t(o, ones_h, gain):
    ms = _dot_sel_lhs2(ones_h, o * o) * (1.0 / DV)
    return o * lax.rsqrt(ms + EPS) * gain


def _head_ln_t(o, ones_h, gain):
    c = o - _dot_sel_lhs2(ones_h, o) * (1.0 / DV)
    var = _dot_sel_lhs2(ones_h, c * c) * (1.0 / DV)
    return c * lax.rsqrt(var + EPS) * gain


def _rotary_t(t, cos, sin_signed, first_half):
    half = RET_DK // 2
    swapped = jnp.where(first_half, pltpu.roll(t, LANES - half, 0), pltpu.roll(t, half, 0))
    return t * cos + swapped * sin_signed


def _sample_kernel(x_ref, normw_ref, wt_ref, wout_ref, bbdt_ref, cbdt_ref, are_ref, aim_ref, dskip_ref,
                   wglut_ref, lbp_ref, hgn_ref, wupt_ref, bgate_ref, glan_ref, retn_ref, fnw_ref,
                   cos_ref, sin_ref, dret_ref, onesh_ref,
                   s5re_ref, s5im_ref, hg_ref, gla_ref, ret_ref,
                   y_ref, s5re_o, s5im_o, hg_o, gla_o, ret_o,
                   xs_scr, pt_scr, hk_scr, ot_scr, mixt_scr):
    layer, head = pl.program_id(0), pl.program_id(1)
    last_layer, last_head = pl.num_programs(0) - 1, pl.num_programs(1) - 1

    @pl.when((layer == 0) & (head == 0))
    def _load_x():
        xs_scr[...] = x_ref[...]

    @pl.when(head == 0)
    def _dense():
        hh = _rmsnorm_rows(xs_scr[...], normw_ref[...]).astype(BF16)
        pt_scr[...] = lax.dot_general(wt_ref[...], hh, (((1,), (1,)), ((), ())), preferred_element_type=F32)

        u = pt_scr[C_U:C_U + BRANCH, :]
        bu = _dot3(bbdt_ref[...], u)
        a_re, a_im = are_ref[...], aim_ref[...]
        s0r, s0i = s5re_ref[...], s5im_ref[...]
        s_re = a_re * s0r - a_im * s0i + bu[0:S5_N]
        s_im = a_re * s0i + a_im * s0r + bu[S5_N:2 * S5_N]
        s5re_o[...] = s_re
        s5im_o[...] = s_im
        y = _gelu_tanh(_dot3(cbdt_ref[...], jnp.concatenate([s_re, s_im], axis=0)) + u * dskip_ref[...])
        y = y * _sigmoid(jnp.dot(wglut_ref[...], y.astype(BF16), preferred_element_type=F32))
        mixt_scr[0:BRANCH, :] = y * _silu(pt_scr[C_SZ:C_SZ + BRANCH, :])

        lbp = lbp_ref[...]
        log_f, hk = _hgrn_gates(pt_scr[C_HF:C_HF + BRANCH, :], lbp[:, 0:1], lbp[:, 1:2], lbp[:, 2:3])
        pt_scr[C_HF:C_HF + BRANCH, :] = jnp.exp(log_f)
        hk_scr[...] = hk
        g_gla = _log_sigmoid(_dot3(wupt_ref[...], pt_scr[C_LR:C_LR + LANES, :]) + bgate_ref[...])
        pt_scr[C_LR:C_LR + LANES, :] = jnp.exp(g_gla * (1.0 / GLA_TAU))
        pt_scr[C_GQ:C_GQ + LANES, :] = pt_scr[C_GQ:C_GQ + LANES, :] * (GLA_DK ** -0.5)
        row = lax.broadcasted_iota(jnp.int32, (HEADS * RET_DK, LANES), 0)
        first_half = (row % RET_DK) < (RET_DK // 2)
        pt_scr[C_RQ:C_RQ + LANES, :] = _rotary_t(pt_scr[C_RQ:C_RQ + LANES, :], cos_ref[...], sin_ref[...],
                                                 first_half)
        pt_scr[C_RK:C_RK + LANES, :] = _rotary_t(pt_scr[C_RK:C_RK + LANES, :], cos_ref[...], sin_ref[...],
                                                 first_half) * (RET_DK ** -0.5)

    def head_update(s0_ref, s_out_ref, dk, dec_ref, dec_row, key_ref, key_row, q_row, v_row, out_row):
        vt = pt_scr[pl.ds(pl.multiple_of(v_row + head * DV, DV), DV), :]

        def feature(kk, acc):
            r = head * dk + kk
            bcast = lambda ref, r0: jnp.broadcast_to(ref[pl.ds(r0 + r, 1), :], (DV, LANES))
            s_new = s0_ref[kk] * bcast(dec_ref, dec_row) + bcast(key_ref, key_row) * vt
            s_out_ref[kk] = s_new
            return acc + bcast(pt_scr, q_row) * s_new

        acc = lax.fori_loop(0, dk, feature, jnp.zeros((DV, LANES), F32), unroll=4)
        ot_scr[pl.ds(pl.multiple_of(out_row + head * DV, DV), DV), :] = acc

    head_update(hg_ref, hg_o, HG_DK, pt_scr, C_HF, hk_scr, 0, C_HQ, C_HI, 0)
    head_update(gla_ref, gla_o, GLA_DK, pt_scr, C_LR, pt_scr, C_GK, C_GQ, C_GV, BRANCH)
    head_update(ret_ref, ret_o, RET_DK, dret_ref, 0, pt_scr, C_RK, C_RQ, C_RV, 2 * BRANCH)

    @pl.when(head == last_head)
    def _finish():
        ones_h = onesh_ref[...]
        o_hg = _head_rms_t(ot_scr[0:BRANCH, :], ones_h, hgn_ref[...])
        mixt_scr[BRANCH:2 * BRANCH, :] = o_hg * _silu(pt_scr[C_HZ:C_HZ + BRANCH, :])
        o_gla = _head_rms_t(ot_scr[BRANCH:2 * BRANCH, :], ones_h, glan_ref[...])
        mixt_scr[2 * BRANCH:3 * BRANCH, :] = o_gla * _silu(pt_scr[C_GZ:C_GZ + BRANCH, :])
        o_ret = _head_ln_t(ot_scr[2 * BRANCH:3 * BRANCH, :], ones_h, retn_ref[...])
        mixt_scr[3 * BRANCH:4 * BRANCH, :] = o_ret * _silu(pt_scr[C_RZ:C_RZ + BRANCH, :])
        out = xs_scr[...] + lax.dot_general(mixt_scr[...].astype(BF16), wout_ref[...], (((0,), (0,)), ((), ())),
                                            preferred_element_type=F32)
        xs_scr[...] = out

        @pl.when(layer == last_layer)
        def _emit():
            y_ref[...] = _rmsnorm_rows(out, fnw_ref[...])


def _ret_log_gamma():
    return jnp.log1p(-jnp.exp2(-5.0 - jnp.arange(HEADS, dtype=F32)))


def _constants():
    ones_h = (np.arange(BRANCH)[:, None] // DV == np.arange(BRANCH)[None, :] // DV)
    ones_g = (np.arange(HEADS * GLA_DK)[:, None] // GLA_DK == np.arange(BRANCH)[None, :] // DV)
    r = np.arange(T_CHUNK)
    tril = (r[:, None] // PAIR == r[None, :] // PAIR) & (r[None, :] <= r[:, None])
    same_group_b = np.arange(BRANCH)[:, None] // S5_CH == np.arange(S5_N)[None, :] // S5_STATE
    as_bf16 = lambda m: jnp.asarray(m.astype(np.float32), dtype=BF16)
    return dict(ones_h=as_bf16(ones_h), ones_g=as_bf16(ones_g), tril=as_bf16(tril),
                s5_mask=jnp.asarray(same_group_b.astype(np.float32)))


def _ret_tables():
    lg = _ret_log_gamma()
    idx = jnp.arange(T_CHUNK, dtype=F32)
    rel = idx[:, None] - idx[None, :]
    causal = rel >= 0
    decay = jnp.where(causal[None], jnp.exp(jnp.where(causal, rel, 0.0)[None] * lg[:, None, None]), 0.0)
    dstack = jnp.transpose(decay, (1, 0, 2)).reshape(T_CHUNK, HEADS * T_CHUNK)
    inner = jnp.repeat(jnp.exp((idx[:, None] + 1.0) * lg[None, :]), DV, axis=1)
    kdec = jnp.repeat(jnp.exp((T_CHUNK - 1.0 - idx[:, None]) * lg[None, :]), RET_DK, axis=1)
    cdec = jnp.repeat(jnp.exp(T_CHUNK * lg)[None, :], RET_DK, axis=1)
    dret = jnp.broadcast_to(jnp.repeat(jnp.exp(lg), RET_DK)[:, None], (HEADS * RET_DK, LANES))
    return dstack, inner, kdec, cdec, dret


def _rope_tables(pos):
    half = RET_DK // 2
    inv = ROPE_BASE ** (-jnp.arange(half, dtype=F32) / half)
    ang = pos.astype(F32)[:, None] * inv[None, :]
    cos, sin = jnp.cos(ang), jnp.sin(ang)
    cos_t = jnp.tile(jnp.concatenate([cos, cos], axis=1), (1, HEADS))
    sin_t = jnp.tile(jnp.concatenate([-sin, sin], axis=1), (1, HEADS))
    return cos_t, sin_t


def _pack_w_in_t(w):
    wt = jnp.swapaxes(w, 1, 2)
    offs = np.cumsum([0, 256, 256, 256, 256, 256, 256, 128, 128, 256, 16, 256, 128, 128, 256, 256])
    seg = lambda i: wt[:, int(offs[i]):int(offs[i + 1]), :]
    order = [0, 1, 2, 3, 4, 5, 6, 7, 8, 10, 11, 12, 13, 14]
    pad = jnp.zeros((w.shape[0], LANES - GLA_LOWRANK, w.shape[1]), w.dtype)
    return jnp.concatenate([seg(i) for i in order] + [seg(9), pad], axis=1).astype(BF16)


def _s5_discretize(lam_re, lam_im, log_step, b_re, b_im, c_re, c_im, mask):
    lr, li = lam_re.astype(F32), lam_im.astype(F32)
    step = jnp.exp(log_step.astype(F32))[..., None]
    mag = jnp.exp(lr * step)
    ab_re = mag * jnp.cos(li * step)
    ab_im = mag * jnp.sin(li * step)
    den = lr * lr + li * li
    nr = ab_re - 1.0
    f_re = (nr * lr + ab_im * li) / den
    f_im = (ab_im * lr - nr * li) / den
    br, bi = b_re.astype(F32), b_im.astype(F32)
    bb_re = f_re[..., None] * br - f_im[..., None] * bi
    bb_im = f_re[..., None] * bi + f_im[..., None] * br
    nl = lr.shape[0]

    def drive(bb):
        rows = jnp.transpose(bb, (0, 1, 3, 2)).reshape(nl, BRANCH, S5_STATE)
        return jnp.tile(rows, (1, 1, S5_GROUPS)) * mask

    def readout(cc):
        rows = cc.astype(F32).reshape(nl, BRANCH, S5_STATE)
        return jnp.tile(rows, (1, 1, S5_GROUPS)) * mask

    bbd = jnp.concatenate([drive(bb_re), drive(bb_im)], axis=2)
    cbd_t = jnp.concatenate([readout(c_re), -readout(c_im)], axis=2)
    return bbd, cbd_t, ab_re.reshape(nl, 1, S5_N), ab_im.reshape(nl, 1, S5_N)


def _full(shape):
    return pl.BlockSpec(shape, lambda *_: (0,) * len(shape), pipeline_mode=pl.Buffered(1))


def _of_layer(arr, layer):
    nd = arr.ndim - 1
    return pl.BlockSpec((None,) + arr.shape[1:], lambda *_: (layer,) + (0,) * nd,
                        pipeline_mode=pl.Buffered(1))


def _prompt_layer(x, layer, p, consts, tabs, rope, apply_final):
    nb, seq, _ = x.shape
    n_steps = seq // T_CHUNK
    rows = nb * T_CHUNK
    dstack, inner, kdec, cdec, _ = tabs
    cos_t, sin_t = rope
    per_layer = [p['norm_w'], p['w_pack'], p['w_out'], p['bbd'].astype(BF16), p['cbd'],
                 p['a_re'], p['a_im'],
                 p['d_skip'], p['w_glu'], p['lbp'], p['hg_norm'], p['w_up'], p['b_gate'], p['gla_norm'],
                 p['ret_norm']]
    shared = [dstack, inner, kdec, cdec, consts['ones_h'], consts['ones_g'], consts['tril'],
              p['final_norm']]
    in_specs = [pl.BlockSpec((nb, T_CHUNK, D_MODEL), lambda i: (0, i, 0)),
                pl.BlockSpec((nb, T_CHUNK, D_MODEL), lambda i: (0, jnp.minimum(i + 1, n_steps - 1), 0))]
    in_specs += [_of_layer(a, layer) for a in per_layer]
    in_specs += [pl.BlockSpec((T_CHUNK, LANES), lambda i: (i, 0)), pl.BlockSpec((T_CHUNK, LANES), lambda i: (i, 0))]
    in_specs += [_full(a.shape) for a in shared]
    out_shape = (jax.ShapeDtypeStruct((nb, seq, D_MODEL), F32),
                 jax.ShapeDtypeStruct((nb, 2 * S5_N), F32),
                 jax.ShapeDtypeStruct((nb // 2, 2, BRANCH, HEADS * HG_DK), F32),
                 jax.ShapeDtypeStruct((nb // 2, 2, BRANCH, HEADS * GLA_DK), F32),
                 jax.ShapeDtypeStruct((nb // 2, 2, BRANCH, HEADS * RET_DK), F32))
    out_specs = (pl.BlockSpec((nb, T_CHUNK, D_MODEL), lambda i: (0, i, 0)),
                 pl.BlockSpec((nb, 2 * S5_N), lambda i: (0, 0)),
                 pl.BlockSpec((nb // 2, 2, BRANCH, HEADS * HG_DK), lambda i: (0, 0, 0, 0)),
                 pl.BlockSpec((nb // 2, 2, BRANCH, HEADS * GLA_DK), lambda i: (0, 0, 0, 0)),
                 pl.BlockSpec((nb // 2, 2, BRANCH, HEADS * RET_DK), lambda i: (0, 0, 0, 0)))
    scratch = [pltpu.VMEM((N_QUARTER, rows, QUARTER), F32), pltpu.VMEM((rows, LANES), F32),
               pltpu.VMEM((rows, BRANCH), F32),
               pltpu.VMEM((2 * S5_N // LANES, nb * PITCH, LANES), F32),
               pltpu.VMEM((rows, D_MODEL), F32),
               pltpu.VMEM((TERM_ROWS, HEADS * HG_DK), F32), pltpu.VMEM((TERM_ROWS, HEADS * GLA_DK), F32),
               pltpu.VMEM((TERM_ROWS, HEADS * HG_DK), F32), pltpu.VMEM((TERM_ROWS, HEADS * GLA_DK), F32),
               ] + 2 * [pltpu.VMEM((w // LANES, T_CHUNK, LANES), F32) for kd in (HEADS * HG_DK, HEADS * GLA_DK)
                        for w in (kd, kd, BRANCH)]
    y, s5, hgs, glas, rets = pl.pallas_call(
        functools.partial(_prompt_kernel, apply_final),
        grid=(n_steps,), in_specs=in_specs, out_specs=out_specs, out_shape=out_shape,
        scratch_shapes=scratch, name='prompt_layer',
        compiler_params=pltpu.CompilerParams(dimension_semantics=('arbitrary',),
                                             vmem_limit_bytes=VMEM_LIMIT),
    )(x, x, *per_layer, cos_t, sin_t, *shared)
    s5 = s5.reshape(nb, 2, S5_GROUPS, S5_STATE)

    def unstack(st, dk):
        st = st.reshape(nb, HEADS, DV, HEADS, dk)
        diag = jnp.stack([st[:, hh, :, hh, :] for hh in range(HEADS)], axis=1)
        return jnp.transpose(diag, (0, 1, 3, 2))

    return y, (s5[:, 0], s5[:, 1], unstack(hgs, HG_DK), unstack(glas, GLA_DK), unstack(rets, RET_DK))


def _sample_step(x, states, p, consts, tabs, rope):
    nb = x.shape[0]
    depth = p['w_t'].shape[0]
    s5re, s5im, hg, gla, ret = states
    to_lanes = lambda s: jnp.moveaxis(s, 1, -1)
    s5re_t = to_lanes(s5re).reshape(depth, S5_N, nb)
    s5im_t = to_lanes(s5im).reshape(depth, S5_N, nb)
    hg_t, gla_t, ret_t = to_lanes(hg), to_lanes(gla), to_lanes(ret)
    cos_t, sin_t = rope
    col = lambda a: jnp.swapaxes(a, -1, -2)
    per_layer = [p['norm_w'], p['w_t'], p['w_out'], col(p['bbd']), p['cbd_t'], col(p['a_re']), col(p['a_im']),
                 col(p['d_skip']), col(p['w_glu']), col(p['lbp']), col(p['hg_norm']), col(p['w_up']),
                 col(p['b_gate']), col(p['gla_norm']), col(p['ret_norm'])]
    shared = [p['final_norm'], col(cos_t), col(sin_t), tabs[4], consts['ones_h']]
    layer_spec = lambda a: pl.BlockSpec((None,) + a.shape[1:], lambda l, h: (l,) + (0,) * (a.ndim - 1))
    head_spec = lambda a: pl.BlockSpec((None, None) + a.shape[2:], lambda l, h: (l, h) + (0,) * (a.ndim - 2))
    state_specs = [layer_spec(s5re_t), layer_spec(s5im_t), head_spec(hg_t), head_spec(gla_t), head_spec(ret_t)]
    in_specs = ([_full(x.shape)] + [layer_spec(a) for a in per_layer] + [_full(a.shape) for a in shared]
                + state_specs)
    state_arrays = [s5re_t, s5im_t, hg_t, gla_t, ret_t]
    out_shape = tuple([jax.ShapeDtypeStruct(x.shape, F32)]
                      + [jax.ShapeDtypeStruct(a.shape, F32) for a in state_arrays])
    out_specs = tuple([pl.BlockSpec(x.shape, lambda l, h: (0, 0))] + state_specs)
    scratch = [pltpu.VMEM((nb, D_MODEL), F32), pltpu.VMEM((N_PACK, nb), F32), pltpu.VMEM((BRANCH, nb), F32),
               pltpu.VMEM((3 * BRANCH, nb), F32), pltpu.VMEM((4 * BRANCH, nb), F32)]
    outs = pl.pallas_call(
        _sample_kernel, grid=(depth, HEADS), in_specs=in_specs, out_specs=out_specs, out_shape=out_shape,
        scratch_shapes=scratch, name='sample_step',
        compiler_params=pltpu.CompilerParams(dimension_semantics=('arbitrary', 'arbitrary'),
                                             vmem_limit_bytes=VMEM_LIMIT),
    )(x, *per_layer, *shared, *state_arrays)
    from_lanes = lambda s: jnp.moveaxis(s, -1, 1)
    new = (from_lanes(outs[1].reshape(depth, S5_GROUPS, S5_STATE, nb)),
           from_lanes(outs[2].reshape(depth, S5_GROUPS, S5_STATE, nb)),
           from_lanes(outs[3]), from_lanes(outs[4]), from_lanes(outs[5]))
    return outs[0], new


def kernel(x_prompt, x_sample, state_s5_re, state_s5_im, state_hgrn, state_gla, state_ret, norm_w, final_norm_w, w_in, w_out, s5_lam_re, s5_lam_im, s5_log_step, s5_b_re, s5_b_im, s5_c_re, s5_c_im, s5_d, s5_w_glu, hgrn_lb_logits, hgrn_norm_w, gla_w_gate_up, gla_b_gate, gla_norm_w, ret_norm_w):
    depth = w_in.shape[0]
    seq = x_prompt.shape[1]
    consts = _constants()
    tabs = _ret_tables()
    rope_p = _rope_tables(jnp.arange(seq))
    rope_s = _rope_tables(PAST_LEN + jnp.arange(1))
    lb = jnp.cumsum(jax.nn.softmax(hgrn_lb_logits.astype(F32), axis=0), axis=0)
    lb = (lb - lb[0:1])[:, None, :]
    bbd, cbd_t, a_re, a_im = _s5_discretize(s5_lam_re, s5_lam_im, s5_log_step, s5_b_re, s5_b_im, s5_c_re,
                                            s5_c_im, consts['s5_mask'])
    w_t = lax.optimization_barrier(_pack_w_in_t(w_in))
    row = lambda a: a[:, None, :].astype(F32)
    w_up = jnp.zeros((depth, LANES, HEADS * GLA_DK), F32).at[:, :GLA_LOWRANK].set(gla_w_gate_up.astype(F32))
    p = dict(norm_w=row(norm_w),
             w_pack=jnp.swapaxes(w_t, 1, 2), w_t=w_t, w_out=w_out.astype(BF16),
             bbd=bbd, cbd=jnp.swapaxes(cbd_t, 1, 2).astype(BF16), cbd_t=cbd_t, a_re=a_re, a_im=a_im,
             d_skip=row(s5_d), w_glu=s5_w_glu.astype(BF16),
             lbp=jnp.concatenate([jnp.log(lb), jnp.log1p(-lb), 1.0 - lb, jnp.zeros((depth, 5, BRANCH), F32)],
                                 axis=1),
             hg_norm=row(hgrn_norm_w), w_up=w_up, b_gate=row(gla_b_gate), gla_norm=row(gla_norm_w),
             ret_norm=row(ret_norm_w), final_norm=final_norm_w[None, :].astype(F32))

    xp = x_prompt
    new_p = ([], [], [], [], [])
    for l in range(depth):
        xp, st_p = _prompt_layer(xp, l, p, consts, tabs, rope_p, l == depth - 1)
        for i in range(5):
            new_p[i].append(st_p[i])
    xs, new_s = _sample_step(x_sample.reshape(x_sample.shape[0], D_MODEL),
                             (state_s5_re, state_s5_im, state_hgrn, state_gla, state_ret),
                             p, consts, tabs, rope_s)
    return (xp, xs.reshape(x_sample.shape),
            jnp.stack(new_p[0]), jnp.stack(new_p[1]), jnp.stack(new_p[2]), jnp.stack(new_p[3]),
            jnp.stack(new_p[4])) + new_s
```

```python
import functools
import math

import numpy as np
import jax
import jax.numpy as jnp
from jax import lax
from jax.experimental import pallas as pl
from jax.experimental.pallas import tpu as pltpu

F32 = jnp.float32
BF16 = jnp.bfloat16

D_MODEL = 1024
BRANCH = 256
S5_CH = 16
S5_GROUPS = 16
S5_STATE = 64
S5_N = S5_GROUPS * S5_STATE
HEADS = 4
HG_DK = 64
GLA_DK = 32
RET_DK = 32
DV = 64
GLA_LOWRANK = 16
GLA_TAU = 16.0
ROPE_BASE = 10000.0
PAST_LEN = 16384
EPS = 1e-6
SUB = 16
PAIR = 2 * SUB

LANES = 128
T_CHUNK = 64
PITCH = T_CHUNK + 8
SUB_TERM_ROWS = SUB * (SUB + SUB // 2) // 2
TERM_ROWS = (T_CHUNK // SUB) * SUB_TERM_ROWS
LOG2E = math.log2(math.e)
MASK_NEG = -1e30
VMEM_LIMIT = 60 * 1024 * 1024

C_U, C_SZ, C_HQ, C_HF, C_HI, C_HZ = 0, 256, 512, 768, 1024, 1280
C_GQ, C_GK, C_GV, C_GZ = 1536, 1664, 1792, 2048
C_RQ, C_RK, C_RV, C_RZ = 2304, 2432, 2560, 2816
C_LR = 3072
N_PACK = 3200
N_QUARTER = 4
QUARTER = C_LR // N_QUARTER


def _dot(a, b):
    return jnp.dot(a.astype(BF16), b.astype(BF16), preferred_element_type=F32)


def _dot_nt(a, b):
    return lax.dot_general(a.astype(BF16), b.astype(BF16), (((1,), (1,)), ((), ())),
                           preferred_element_type=F32)


def _dot_tn(a, b):
    return lax.dot_general(a.astype(BF16), b.astype(BF16), (((0,), (0,)), ((), ())),
                           preferred_element_type=F32)


def _split3(x):
    x1 = x.astype(BF16)
    r1 = x - x1.astype(F32)
    x2 = r1.astype(BF16)
    x3 = (r1 - x2.astype(F32)).astype(BF16)
    return x1, x2, x3


def _dot_sel_lhs(sel, x):
    x1, x2, x3 = _split3(x)
    d = lambda p: jnp.dot(sel, p, preferred_element_type=F32)
    return d(x1) + d(x2) + d(x3)


def _dot_sel_rhs2(x, sel):
    x1 = x.astype(BF16)
    x2 = (x - x1.astype(F32)).astype(BF16)
    return (jnp.dot(x1, sel, preferred_element_type=F32)
            + jnp.dot(x2, sel, preferred_element_type=F32))


def _dot3(a, b):
    a1 = a.astype(BF16)
    a2 = (a - a1.astype(F32)).astype(BF16)
    b1 = b.astype(BF16)
    b2 = (b - b1.astype(F32)).astype(BF16)
    d = lambda p, q: jnp.dot(p, q, preferred_element_type=F32)
    return d(a1, b1) + d(a1, b2) + d(a2, b1)


def _sigmoid(x):
    return 0.5 * jnp.tanh(0.5 * x) + 0.5


def _silu(x):
    return x * _sigmoid(x)


def _log_sigmoid(x):
    return jnp.minimum(x, 0.0) - jnp.log(1.0 + jnp.exp(-jnp.abs(x)))


def _gelu_tanh(x):
    return 0.5 * x * (1.0 + jnp.tanh(math.sqrt(2.0 / math.pi) * (x + 0.044715 * (x * x * x))))


def _rmsnorm_rows(x, w):
    return x * lax.rsqrt(jnp.mean(x * x, axis=-1, keepdims=True) + EPS) * w


def _head_rms(o, ones_h, gain):
    ms = _dot_sel_rhs2(o * o, ones_h) * (1.0 / DV)
    return o * lax.rsqrt(ms + EPS) * gain


def _head_ln(o, ones_h, gain):
    c = o - _dot_sel_rhs2(o, ones_h) * (1.0 / DV)
    var = _dot_sel_rhs2(c * c, ones_h) * (1.0 / DV)
    return c * lax.rsqrt(var + EPS) * gain


def _hgrn_gates(xf, loglb, log1mlb, one_m_lb):
    bterm = log1mlb + _log_sigmoid(xf)
    m = jnp.maximum(loglb, bterm)
    log_f = m + jnp.log(jnp.exp(loglb - m) + jnp.exp(bterm - m))
    return log_f, one_m_lb * _sigmoid(-xf)


def _rotary(t, cos, sin_signed, first_half):
    half = RET_DK // 2
    swapped = jnp.where(first_half, pltpu.roll(t, LANES - half, 1), pltpu.roll(t, half, 1))
    return t * cos + swapped * sin_signed


def _s5_output(y_lin, u, sz, dskip, wglu):
    y = _gelu_tanh(y_lin + u * dskip)
    y = y * _sigmoid(_dot(y, wglu))
    return y * _silu(sz)


def _mix_and_project(x, mix, wout, fnw, apply_final):
    out = x + _dot(mix, wout)
    if apply_final:
        out = _rmsnorm_rows(out, fnw)
    return out


class _Slabs:
    def __init__(self, ref, base=0):
        self.ref, self.base = ref, base

    def keep(self, value):
        for c in range(self.ref.shape[0]):
            self.ref[c, pl.ds(self.base, value.shape[0]), :] = value[:, c * LANES:(c + 1) * LANES]

    def tile(self, r0, n):
        return jnp.concatenate([self.ref[c, pl.ds(self.base + r0, n), :] for c in range(self.ref.shape[0])],
                               axis=1)

    def row(self, r, n):
        return jnp.concatenate([self.ref[c, pl.ds(self.base + r, n, stride=0), :]
                                for c in range(self.ref.shape[0])], axis=1)


class _GatedStream:
    def __init__(self, q, k, v, g, s_ref, ones_kv, mask_vk, tril, term_scr, row_scr, kdim):
        self.q, self.k, self.v, self.g = q, k, v, g
        self.bc, self.kc, self.vc = (_Slabs(r) for r in row_scr)
        self.s_ref, self.ones_kv, self.mask_vk = s_ref, ones_kv, mask_vk
        self.tril, self.term_scr, self.kdim = tril, term_scr, kdim
        self.ws, self.inter, self.att, self.cross = {}, {}, {}, {}

    def cumulate(self):
        self.bc.keep(_dot_sel_lhs(self.tril, self.g()) * LOG2E)
        self.kc.keep(self.k())
        self.vc.keep(self.v())

    def _total(self, p):
        return self.bc.tile(PAIR * (p + 1) - 1, 1)

    def update_part(self, p):
        sl = slice(PAIR * p, PAIR * (p + 1))
        self.ws[p] = _dot_tn(self.v()[sl],
                             self.k()[sl] * jnp.exp2(self._total(p) - self.bc.tile(PAIR * p, PAIR)))

    def cross_part(self, p):
        r_a, r_b = PAIR * p, PAIR * p + SUB
        edge = self.bc.row(r_b - 1, SUB)
        q_b = self.q()[r_b:r_b + SUB] * jnp.exp2(self.bc.tile(r_b, SUB) - edge)
        k_a = self.kc.tile(r_a, SUB) * jnp.exp2(edge - self.bc.tile(r_a, SUB))
        lane = lax.broadcasted_iota(jnp.int32, (SUB, self.kdim), 1) // (self.kdim // HEADS)
        q_heads = jnp.concatenate([jnp.where(lane == h, q_b, 0.0) for h in range(HEADS)], axis=0)
        scores = _dot_nt(q_heads, k_a)
        res = _dot(scores, self.vc.tile(r_a, SUB))
        lane_v = lax.broadcasted_iota(jnp.int32, (SUB, BRANCH), 1) // DV
        out = jnp.where(lane_v == 0, res[0:SUB], 0.0)
        for h in range(1, HEADS):
            out = out + jnp.where(lane_v == h, res[SUB * h:SUB * (h + 1)], 0.0)
        self.cross[p] = out

    def products_part(self, j):
        hs, term_scr, r0 = SUB // 2, self.term_scr, SUB * j
        q, bc = self.q()[r0:r0 + SUB], self.bc.tile(r0, SUB)
        trow = lax.broadcasted_iota(jnp.int32, (hs, self.kdim), 0)
        neg = [jnp.where(trow >= d, 0.0, MASK_NEG) for d in range(1, hs)]
        q_lo, q_hi, b_lo, b_hi = q[0:hs], q[hs:SUB], bc[0:hs], bc[hs:SUB]
        row = j * SUB_TERM_ROWS
        for s in range(SUB):
            ks, bs = self.kc.row(r0 + s, hs), self.bc.row(r0 + s, hs)
            d = s % hs
            q_dg, b_dg = (q_lo, b_lo) if s < hs else (q_hi, b_hi)
            e_dg = b_dg - bs if d == 0 else b_dg - bs + neg[d - 1]
            term_scr[row:row + hs, :] = q_dg * ks * jnp.exp2(e_dg)
            row += hs
            if s < hs:
                term_scr[row:row + hs, :] = q_hi * ks * jnp.exp2(b_hi - bs)
                row += hs

    def matmul_part(self, p):
        sl = slice(PAIR * p, PAIR * (p + 1))
        if p == 0:
            self.state = self.s_ref[...]
        self.inter[p] = _dot_nt(self.q()[sl] * jnp.exp2(self.bc.tile(PAIR * p, PAIR)), self.state)
        self.state = self.state * jnp.exp2(self._total(p)) + self.ws[p] * self.mask_vk
        if p == T_CHUNK // PAIR - 1:
            self.s_ref[...] = self.state
        rows = slice(2 * p * SUB_TERM_ROWS, 2 * (p + 1) * SUB_TERM_ROWS)
        self.att[p] = _dot(self.term_scr[rows, :], self.ones_kv)

    def output_part(self, j):
        hs, r0, p, second = SUB // 2, SUB * j, j // 2, j % 2
        att = self.att[p][second * SUB_TERM_ROWS:(second + 1) * SUB_TERM_ROWS]
        start = self.inter[p][second * SUB:(second + 1) * SUB]
        if second:
            start = start + self.cross[p]
        acc_lo, acc_hi = start[0:hs], start[hs:SUB]
        row = 0
        for s in range(SUB):
            vs = self.vc.row(r0 + s, hs)
            if s < hs:
                acc_lo = acc_lo + att[row:row + hs] * vs
                row += hs
            acc_hi = acc_hi + att[row:row + hs] * vs
            row += hs
        return jnp.concatenate([acc_lo, acc_hi], axis=0)


def _ret_chunk(q, k, v, cos, sin_signed, s_ref, dstack, inner, kdec, cdec, mask_vk):
    lane = lax.broadcasted_iota(jnp.int32, (T_CHUNK, HEADS * RET_DK), 1)
    first_half = (lane % RET_DK) < (RET_DK // 2)
    rq = _rotary(q, cos, sin_signed, first_half)
    rk = _rotary(k, cos, sin_signed, first_half) * (RET_DK ** -0.5)
    lane_v = lax.broadcasted_iota(jnp.int32, (T_CHUNK, HEADS * DV), 1)
    kst = jnp.concatenate([jnp.where(lane // RET_DK == h, rk, 0.0) for h in range(HEADS)], axis=0)
    vst = jnp.concatenate([jnp.where(lane_v // DV == h, v, 0.0) for h in range(HEADS)], axis=0)
    p = _dot_nt(rq, kst) * dstack
    s_t = s_ref[...]
    o = _dot(p, vst) + _dot_nt(rq, s_t) * inner
    s_ref[...] = s_t * cdec + _dot_tn(v, rk * kdec) * mask_vk
    return o


def _prompt_kernel(apply_final,
                   x_ref, xn_ref, normw_ref, win_ref, wout_ref, bbd_ref, cbd_ref, are_ref, aim_ref,
                   dskip_ref, wglu_ref, lbp_ref, hgn_ref, wup_ref, bgate_ref, glan_ref, retn_ref, cos_ref,
                   sin_ref, dstack_ref, inner_ref, kdec_ref, cdec_ref, onesh_ref, onesg_ref, tril_ref,
                   fnw_ref,
                   y_ref, s5_ref, hgs_ref, glas_ref, rets_ref,
                   proj_scr, lr_scr, hk_scr, bu_scr, mix_scr, term_h0, term_g0, term_h1, term_g1,
                   *row_scratch):
    nb = x_ref.shape[0]
    rows = nb * T_CHUNK
    n_slab = 2 * S5_N // LANES
    half = n_slab // 2
    step = pl.program_id(0)

    def w_quarter(qtr):
        return win_ref[:, qtr * QUARTER:(qtr + 1) * QUARTER]

    def proj(col, width, rs=slice(None)):
        return proj_scr[col // QUARTER, rs, col % QUARTER:col % QUARTER + width]

    @pl.when(step == 0)
    def _init():
        s5_ref[...] = jnp.zeros_like(s5_ref)
        hgs_ref[...] = jnp.zeros_like(hgs_ref)
        glas_ref[...] = jnp.zeros_like(glas_ref)
        rets_ref[...] = jnp.zeros_like(rets_ref)
        bu_scr[...] = jnp.zeros_like(bu_scr)
        h0 = _rmsnorm_rows(x_ref[...].reshape(rows, D_MODEL), normw_ref[...]).astype(BF16)
        for qtr in range(N_QUARTER):
            proj_scr[qtr] = jnp.dot(h0, w_quarter(qtr), preferred_element_type=F32)
        lr_scr[...] = jnp.dot(h0, win_ref[:, C_LR:N_PACK], preferred_element_type=F32)

    def drive(c2):
        bu = _dot(proj(C_U, BRANCH), bbd_ref[:, 2 * c2 * LANES:2 * (c2 + 1) * LANES])
        for cc in range(2):
            for b in range(nb):
                bu_scr[2 * c2 + cc, b * PITCH:b * PITCH + T_CHUNK, :] = bu[b * T_CHUNK:(b + 1) * T_CHUNK,
                                                                           cc * LANES:(cc + 1) * LANES]

    lbp = lbp_ref[...]
    log_f, hk = _hgrn_gates(proj(C_HF, BRANCH), lbp[0:1, :], lbp[1:2, :], lbp[2:3, :])
    proj_scr[C_HF // QUARTER, :, C_HF % QUARTER:C_HF % QUARTER + BRANCH] = log_f
    hk_scr[...] = hk
    g_gla = _log_sigmoid(_dot3(lr_scr[...], wup_ref[...]) + bgate_ref[...])
    lr_scr[...] = g_gla * (1.0 / GLA_TAU)
    for c2 in range(n_slab // 2):
        drive(c2)
    a_re = [jnp.broadcast_to(are_ref[:, c * LANES:(c + 1) * LANES], (nb, LANES)) for c in range(half)]
    a_im = [jnp.broadcast_to(aim_ref[:, c * LANES:(c + 1) * LANES], (nb, LANES)) for c in range(half)]
    s_init = s5_ref[...]
    carry0 = tuple(s_init[:, c * LANES:(c + 1) * LANES] for c in range(n_slab))

    def scan_step(t, carry):
        new = [None] * n_slab
        for c in range(half):
            sr, si = carry[c], carry[half + c]
            br = bu_scr[c, pl.ds(t, nb, stride=PITCH), :]
            bi = bu_scr[half + c, pl.ds(t, nb, stride=PITCH), :]
            nr = a_re[c] * sr - a_im[c] * si + br
            ni = a_re[c] * si + a_im[c] * sr + bi
            bu_scr[c, pl.ds(t, nb, stride=PITCH), :] = nr
            bu_scr[half + c, pl.ds(t, nb, stride=PITCH), :] = ni
            new[c], new[half + c] = nr, ni
        return tuple(new)

    carry = lax.fori_loop(0, T_CHUNK, scan_step, carry0)
    s5_ref[...] = jnp.concatenate(carry, axis=1)
    s_all = jnp.concatenate([bu_scr[c] for c in range(n_slab)], axis=1)
    y_all = _dot(s_all, cbd_ref[...])
    y_lin = jnp.concatenate([y_all[b * PITCH:b * PITCH + T_CHUNK] for b in range(nb)], axis=0)
    mix_scr[:, 0:BRANCH] = _s5_output(y_lin, proj(C_U, BRANCH), proj(C_SZ, BRANCH), dskip_ref[...],
                                      wglu_ref[...])

    ones_h = onesh_ref[...]
    ones_g = onesg_ref[...]
    mask_h = ones_h.astype(F32)
    mask_g = jnp.transpose(ones_g.astype(F32))
    tril = tril_ref[...]

    def rows_of(bp, odd):
        return pl.ds(pl.multiple_of((2 * bp + odd) * T_CHUNK, T_CHUNK), T_CHUNK)

    def hgrn_stream(bp, odd, term, row_scr):
        rs = rows_of(bp, odd)
        return _GatedStream(lambda: proj(C_HQ, BRANCH, rs), lambda: hk_scr[rs, :], lambda: proj(C_HI, BRANCH, rs),
                            lambda: proj(C_HF, BRANCH, rs), hgs_ref.at[bp, odd], ones_h, mask_h, tril, term,
                            row_scr, HEADS * HG_DK)

    def gla_stream(bp, odd, term, row_scr):
        rs = rows_of(bp, odd)
        return _GatedStream(lambda: proj(C_GQ, LANES, rs) * (GLA_DK ** -0.5), lambda: proj(C_GK, LANES, rs),
                            lambda: proj(C_GV, BRANCH, rs), lambda: lr_scr[rs, :],
                            glas_ref.at[bp, odd], ones_g, mask_g, tril, term, row_scr, HEADS * GLA_DK)

    n_sub = T_CHUNK // SUB
    rows_h0, rows_g0, rows_h1, rows_g1 = (row_scratch[3 * i:3 * i + 3] for i in range(4))

    def retention(bp, odd):
        rs = rows_of(bp, odd)
        mix_scr[rs, 3 * BRANCH:4 * BRANCH] = _ret_chunk(
            proj(C_RQ, LANES, rs), proj(C_RK, LANES, rs), proj(C_RV, BRANCH, rs), cos_ref[...],
            sin_ref[...], rets_ref.at[bp, odd], dstack_ref[...], inner_ref[...], kdec_ref[...],
            cdec_ref[...], mask_g)

    def per_pair(bp, _):
        st = [hgrn_stream(bp, 0, term_h0, rows_h0), gla_stream(bp, 0, term_g0, rows_g0),
              hgrn_stream(bp, 1, term_h1, rows_h1), gla_stream(bp, 1, term_g1, rows_g1)]
        slots = [(0, BRANCH), (0, 2 * BRANCH), (1, BRANCH), (1, 2 * BRANCH)]

        def emit(i, j):
            odd, col = slots[i]
            r0 = pl.multiple_of((2 * bp + odd) * T_CHUNK + SUB * j, SUB)
            mix_scr[pl.ds(r0, SUB), col:col + BRANCH] = st[i].output_part(j)

        def fill(j, updates, crosses, matmuls):
            if updates is not None and j < n_sub // 2:
                st[updates].update_part(j)
            if crosses is not None and j >= n_sub // 2:
                st[crosses].cross_part(j - n_sub // 2)
            if matmuls is not None and j % 2 == 1:
                st[matmuls].matmul_part(j // 2)

        for s_ in st:
            s_.cumulate()
        for odd in range(2):
            retention(bp, odd)
        for s_ in st:
            for p in range(n_sub // 2):
                s_.update_part(p)
        for s_ in st:
            for j in range(n_sub):
                s_.products_part(j)
        for s_ in st:
            for p in range(n_sub // 2):
                s_.cross_part(p)
        for s_ in st:
            for p in range(n_sub // 2):
                s_.matmul_part(p)
        for i in range(len(st)):
            for j in range(n_sub):
                emit(i, j)
        return 0

    lax.fori_loop(0, N_QUARTER, per_pair, 0)

    o_hg = _head_rms(mix_scr[:, BRANCH:2 * BRANCH], ones_h, hgn_ref[...])
    mix_scr[:, BRANCH:2 * BRANCH] = o_hg * _silu(proj(C_HZ, BRANCH))
    o_gla = _head_rms(mix_scr[:, 2 * BRANCH:3 * BRANCH], ones_h, glan_ref[...])
    mix_scr[:, 2 * BRANCH:3 * BRANCH] = o_gla * _silu(proj(C_GZ, BRANCH))
    o_ret = _head_ln(mix_scr[:, 3 * BRANCH:4 * BRANCH], ones_h, retn_ref[...])
    mix_scr[:, 3 * BRANCH:4 * BRANCH] = o_ret * _silu(proj(C_RZ, BRANCH))
    hn = _rmsnorm_rows(xn_ref[...].reshape(rows, D_MODEL), normw_ref[...]).astype(BF16)
    for qtr in range(N_QUARTER):
        proj_scr[qtr] = jnp.dot(hn, w_quarter(qtr), preferred_element_type=F32)
    lr_scr[...] = jnp.dot(hn, win_ref[:, C_LR:N_PACK], preferred_element_type=F32)
    out = _mix_and_project(x_ref[...].reshape(rows, D_MODEL), mix_scr[...], wout_ref[...], fnw_ref[...],
                           apply_final)
    y_ref[...] = out.reshape(nb, T_CHUNK, D_MODEL)


def _dot_sel_lhs2(sel, x):
    x1 = x.astype(BF16)
    x2 = (x - x1.astype(F32)).astype(BF16)
    return (jnp.dot(sel, x1, preferred_element_type=F32) + jnp.dot(sel, x2, preferred_element_type=F32))


def _head_rms_t(o, ones_h, gain):
    ms = _dot_sel_lhs2(ones_h, o * o) * (1.0 / DV)
    return o * lax.rsqrt(ms + EPS) * gain


def _head_ln_t(o, ones_h, gain):
    c = o - _dot_sel_lhs2(ones_h, o) * (1.0 / DV)
    var = _dot_sel_lhs2(ones_h, c * c) * (1.0 / DV)
    return c * lax.rsqrt(var + EPS) * gain


def _rotary_t(t, cos, sin_signed, first_half):
    half = RET_DK // 2
    swapped = jnp.where(first_half, pltpu.roll(t, LANES - half, 0), pltpu.roll(t, half, 0))
    return t * cos + swapped * sin_signed


def _sample_kernel(x_ref, normw_ref, wt_ref, wout_ref, bbdt_ref, cbdt_ref, are_ref, aim_ref, dskip_ref,
                   wglut_ref, lbp_ref, hgn_ref, wupt_ref, bgate_ref, glan_ref, retn_ref, fnw_ref,
                   cos_ref, sin_ref, dret_ref, onesh_ref,
                   s5re_ref, s5im_ref, hg_ref, gla_ref, ret_ref,
                   y_ref, s5re_o, s5im_o, hg_o, gla_o, ret_o,
                   xs_scr, pt_scr, hk_scr, ot_scr, mixt_scr):
    layer, head = pl.program_id(0), pl.program_id(1)
    last_layer, last_head = pl.num_programs(0) - 1, pl.num_programs(1) - 1

    @pl.when((layer == 0) & (head == 0))
    def _load_x():
        xs_scr[...] = x_ref[...]

    @pl.when(head == 0)
    def _dense():
        hh = _rmsnorm_rows(xs_scr[...], normw_ref[...]).astype(BF16)
        pt_scr[...] = lax.dot_general(wt_ref[...], hh, (((1,), (1,)), ((), ())), preferred_element_type=F32)

        u = pt_scr[C_U:C_U + BRANCH, :]
        bu = _dot3(bbdt_ref[...], u)
        a_re, a_im = are_ref[...], aim_ref[...]
        s0r, s0i = s5re_ref[...], s5im_ref[...]
        s_re = a_re * s0r - a_im * s0i + bu[0:S5_N]
        s_im = a_re * s0i + a_im * s0r + bu[S5_N:2 * S5_N]
        s5re_o[...] = s_re
        s5im_o[...] = s_im
        y = _gelu_tanh(_dot3(cbdt_ref[...], jnp.concatenate([s_re, s_im], axis=0)) + u * dskip_ref[...])
        y = y * _sigmoid(jnp.dot(wglut_ref[...], y.astype(BF16), preferred_element_type=F32))
        mixt_scr[0:BRANCH, :] = y * _silu(pt_scr[C_SZ:C_SZ + BRANCH, :])

        lbp = lbp_ref[...]
        log_f, hk = _hgrn_gates(pt_scr[C_HF:C_HF + BRANCH, :], lbp[:, 0:1], lbp[:, 1:2], lbp[:, 2:3])
        pt_scr[C_HF:C_HF + BRANCH, :] = jnp.exp(log_f)
        hk_scr[...] = hk
        g_gla = _log_sigmoid(_dot3(wupt_ref[...], pt_scr[C_LR:C_LR + LANES, :]) + bgate_ref[...])
        pt_scr[C_LR:C_LR + LANES, :] = jnp.exp(g_gla * (1.0 / GLA_TAU))
        pt_scr[C_GQ:C_GQ + LANES, :] = pt_scr[C_GQ:C_GQ + LANES, :] * (GLA_DK ** -0.5)
        row = lax.broadcasted_iota(jnp.int32, (HEADS * RET_DK, LANES), 0)
        first_half = (row % RET_DK) < (RET_DK // 2)
        pt_scr[C_RQ:C_RQ + LANES, :] = _rotary_t(pt_scr[C_RQ:C_RQ + LANES, :], cos_ref[...], sin_ref[...],
                                                 first_half)
        pt_scr[C_RK:C_RK + LANES, :] = _rotary_t(pt_scr[C_RK:C_RK + LANES, :], cos_ref[...], sin_ref[...],
                                                 first_half) * (RET_DK ** -0.5)

    def head_update(s0_ref, s_out_ref, dk, dec_ref, dec_row, key_ref, key_row, q_row, v_row, out_row):
        vt = pt_scr[pl.ds(pl.multiple_of(v_row + head * DV, DV), DV), :]

        def feature(kk, acc):
            r = head * dk + kk
            bcast = lambda ref, r0: jnp.broadcast_to(ref[pl.ds(r0 + r, 1), :], (DV, LANES))
            s_new = s0_ref[kk] * bcast(dec_ref, dec_row) + bcast(key_ref, key_row) * vt
            s_out_ref[kk] = s_new
            return acc + bcast(pt_scr, q_row) * s_new

        acc = lax.fori_loop(0, dk, feature, jnp.zeros((DV, LANES), F32), unroll=4)
        ot_scr[pl.ds(pl.multiple_of(out_row + head * DV, DV), DV), :] = acc

    head_update(hg_ref, hg_o, HG_DK, pt_scr, C_HF, hk_scr, 0, C_HQ, C_HI, 0)
    head_update(gla_ref, gla_o, GLA_DK, pt_scr, C_LR, pt_scr, C_GK, C_GQ, C_GV, BRANCH)
    head_update(ret_ref, ret_o, RET_DK, dret_ref, 0, pt_scr, C_RK, C_RQ, C_RV, 2 * BRANCH)

    @pl.when(head == last_head)
    def _finish():
        ones_h = onesh_ref[...]
        o_hg = _head_rms_t(ot_scr[0:BRANCH, :], ones_h, hgn_ref[...])
        mixt_scr[BRANCH:2 * BRANCH, :] = o_hg * _silu(pt_scr[C_HZ:C_HZ + BRANCH, :])
        o_gla = _head_rms_t(ot_scr[BRANCH:2 * BRANCH, :], ones_h, glan_ref[...])
        mixt_scr[2 * BRANCH:3 * BRANCH, :] = o_gla * _silu(pt_scr[C_GZ:C_GZ + BRANCH, :])
        o_ret = _head_ln_t(ot_scr[2 * BRANCH:3 * BRANCH, :], ones_h, retn_ref[...])
        mixt_scr[3 * BRANCH:4 * BRANCH, :] = o_ret * _silu(pt_scr[C_RZ:C_RZ + BRANCH, :])
        out = xs_scr[...] + lax.dot_general(mixt_scr[...].astype(BF16), wout_ref[...], (((0,), (0,)), ((), ())),
                                            preferred_element_type=F32)
        xs_scr[...] = out

        @pl.when(layer == last_layer)
        def _emit():
            y_ref[...] = _rmsnorm_rows(out, fnw_ref[...])


def _ret_log_gamma():
    return jnp.log1p(-jnp.exp2(-5.0 - jnp.arange(HEADS, dtype=F32)))


def _constants():
    ones_h = (np.arange(BRANCH)[:, None] // DV == np.arange(BRANCH)[None, :] // DV)
    ones_g = (np.arange(HEADS * GLA_DK)[:, None] // GLA_DK == np.arange(BRANCH)[None, :] // DV)
    r = np.arange(T_CHUNK)
    tril = (r[:, None] // PAIR == r[None, :] // PAIR) & (r[None, :] <= r[:, None])
    same_group_b = np.arange(BRANCH)[:, None] // S5_CH == np.arange(S5_N)[None, :] // S5_STATE
    as_bf16 = lambda m: jnp.asarray(m.astype(np.float32), dtype=BF16)
    return dict(ones_h=as_bf16(ones_h), ones_g=as_bf16(ones_g), tril=as_bf16(tril),
                s5_mask=jnp.asarray(same_group_b.astype(np.float32)))


def _ret_tables():
    lg = _ret_log_gamma()
    idx = jnp.arange(T_CHUNK, dtype=F32)
    rel = idx[:, None] - idx[None, :]
    causal = rel >= 0
    decay = jnp.where(causal[None], jnp.exp(jnp.where(causal, rel, 0.0)[None] * lg[:, None, None]), 0.0)
    dstack = jnp.transpose(decay, (1, 0, 2)).reshape(T_CHUNK, HEADS * T_CHUNK)
    inner = jnp.repeat(jnp.exp((idx[:, None] + 1.0) * lg[None, :]), DV, axis=1)
    kdec = jnp.repeat(jnp.exp((T_CHUNK - 1.0 - idx[:, None]) * lg[None, :]), RET_DK, axis=1)
    cdec = jnp.repeat(jnp.exp(T_CHUNK * lg)[None, :], RET_DK, axis=1)
    dret = jnp.broadcast_to(jnp.repeat(jnp.exp(lg), RET_DK)[:, None], (HEADS * RET_DK, LANES))
    return dstack, inner, kdec, cdec, dret


def _rope_tables(pos):
    half = RET_DK // 2
    inv = ROPE_BASE ** (-jnp.arange(half, dtype=F32) / half)
    ang = pos.astype(F32)[:, None] * inv[None, :]
    cos, sin = jnp.cos(ang), jnp.sin(ang)
    cos_t = jnp.tile(jnp.concatenate([cos, cos], axis=1), (1, HEADS))
    sin_t = jnp.tile(jnp.concatenate([-sin, sin], axis=1), (1, HEADS))
    return cos_t, sin_t


def _pack_w_in_t(w):
    wt = jnp.swapaxes(w, 1, 2)
    offs = np.cumsum([0, 256, 256, 256, 256, 256, 256, 128, 128, 256, 16, 256, 128, 128, 256, 256])
    seg = lambda i: wt[:, int(offs[i]):int(offs[i + 1]), :]
    order = [0, 1, 2, 3, 4, 5, 6, 7, 8, 10, 11, 12, 13, 14]
    pad = jnp.zeros((w.shape[0], LANES - GLA_LOWRANK, w.shape[1]), w.dtype)
    return jnp.concatenate([seg(i) for i in order] + [seg(9), pad], axis=1).astype(BF16)


def _s5_discretize(lam_re, lam_im, log_step, b_re, b_im, c_re, c_im, mask):
    lr, li = lam_re.astype(F32), lam_im.astype(F32)
    step = jnp.exp(log_step.astype(F32))[..., None]
    mag = jnp.exp(lr * step)
    ab_re = mag * jnp.cos(li * step)
    ab_im = mag * jnp.sin(li * step)
    den = lr * lr + li * li
    nr = ab_re - 1.0
    f_re = (nr * lr + ab_im * li) / den
    f_im = (ab_im * lr - nr * li) / den
    br, bi = b_re.astype(F32), b_im.astype(F32)
    bb_re = f_re[..., None] * br - f_im[..., None] * bi
    bb_im = f_re[..., None] * bi + f_im[..., None] * br
    nl = lr.shape[0]

    def drive(bb):
        rows = jnp.transpose(bb, (0, 1, 3, 2)).reshape(nl, BRANCH, S5_STATE)
        return jnp.tile(rows, (1, 1, S5_GROUPS)) * mask

    def readout(cc):
        rows = cc.astype(F32).reshape(nl, BRANCH, S5_STATE)
        return jnp.tile(rows, (1, 1, S5_GROUPS)) * mask

    bbd = jnp.concatenate([drive(bb_re), drive(bb_im)], axis=2)
    cbd_t = jnp.concatenate([readout(c_re), -readout(c_im)], axis=2)
    return bbd, cbd_t, ab_re.reshape(nl, 1, S5_N), ab_im.reshape(nl, 1, S5_N)


def _full(shape):
    return pl.BlockSpec(shape, lambda *_: (0,) * len(shape), pipeline_mode=pl.Buffered(1))


def _of_layer(arr, layer):
    nd = arr.ndim - 1
    return pl.BlockSpec((None,) + arr.shape[1:], lambda *_: (layer,) + (0,) * nd,
                        pipeline_mode=pl.Buffered(1))


def _prompt_layer(x, layer, p, consts, tabs, rope, apply_final):
    nb, seq, _ = x.shape
    n_steps = seq // T_CHUNK
    rows = nb * T_CHUNK
    dstack, inner, kdec, cdec, _ = tabs
    cos_t, sin_t = rope
    per_layer = [p['norm_w'], p['w_pack'], p['w_out'], p['bbd'].astype(BF16), p['cbd'],
                 p['a_re'], p['a_im'],
                 p['d_skip'], p['w_glu'], p['lbp'], p['hg_norm'], p['w_up'], p['b_gate'], p['gla_norm'],
                 p['ret_norm']]
    shared = [dstack, inner, kdec, cdec, consts['ones_h'], consts['ones_g'], consts['tril'],
              p['final_norm']]
    in_specs = [pl.BlockSpec((nb, T_CHUNK, D_MODEL), lambda i: (0, i, 0)),
                pl.BlockSpec((nb, T_CHUNK, D_MODEL), lambda i: (0, jnp.minimum(i + 1, n_steps - 1), 0))]
    in_specs += [_of_layer(a, layer) for a in per_layer]
    in_specs += [pl.BlockSpec((T_CHUNK, LANES), lambda i: (i, 0)), pl.BlockSpec((T_CHUNK, LANES), lambda i: (i, 0))]
    in_specs += [_full(a.shape) for a in shared]
    out_shape = (jax.ShapeDtypeStruct((nb, seq, D_MODEL), F32),
                 jax.ShapeDtypeStruct((nb, 2 * S5_N), F32),
                 jax.ShapeDtypeStruct((nb // 2, 2, BRANCH, HEADS * HG_DK), F32),
                 jax.ShapeDtypeStruct((nb // 2, 2, BRANCH, HEADS * GLA_DK), F32),
                 jax.ShapeDtypeStruct((nb // 2, 2, BRANCH, HEADS * RET_DK), F32))
    out_specs = (pl.BlockSpec((nb, T_CHUNK, D_MODEL), lambda i: (0, i, 0)),
                 pl.BlockSpec((nb, 2 * S5_N), lambda i: (0, 0)),
                 pl.BlockSpec((nb // 2, 2, BRANCH, HEADS * HG_DK), lambda i: (0, 0, 0, 0)),
                 pl.BlockSpec((nb // 2, 2, BRANCH, HEADS * GLA_DK), lambda i: (0, 0, 0, 0)),
                 pl.BlockSpec((nb // 2, 2, BRANCH, HEADS * RET_DK), lambda i: (0, 0, 0, 0)))
    scratch = [pltpu.VMEM((N_QUARTER, rows, QUARTER), F32), pltpu.VMEM((rows, LANES), F32),
               pltpu.VMEM((rows, BRANCH), F32),
               pltpu.VMEM((2 * S5_N // LANES, nb * PITCH, LANES), F32),
               pltpu.VMEM((rows, D_MODEL), F32),
               pltpu.VMEM((TERM_ROWS, HEADS * HG_DK), F32), pltpu.VMEM((TERM_ROWS, HEADS * GLA_DK), F32),
               pltpu.VMEM((TERM_ROWS, HEADS * HG_DK), F32), pltpu.VMEM((TERM_ROWS, HEADS * GLA_DK), F32),
               ] + 2 * [pltpu.VMEM((w // LANES, T_CHUNK, LANES), F32) for kd in (HEADS * HG_DK, HEADS * GLA_DK)
                        for w in (kd, kd, BRANCH)]
    y, s5, hgs, glas, rets = pl.pallas_call(
        functools.partial(_prompt_kernel, apply_final),
        grid=(n_steps,), in_specs=in_specs, out_specs=out_specs, out_shape=out_shape,
        scratch_shapes=scratch, name='prompt_layer',
        compiler_params=pltpu.CompilerParams(dimension_semantics=('arbitrary',),
                                             vmem_limit_bytes=VMEM_LIMIT),
    )(x, x, *per_layer, cos_t, sin_t, *shared)
    s5 = s5.reshape(nb, 2, S5_GROUPS, S5_STATE)

    def unstack(st, dk):
        st = st.reshape(nb, HEADS, DV, HEADS, dk)
        diag = jnp.stack([st[:, hh, :, hh, :] for hh in range(HEADS)], axis=1)
        return jnp.transpose(diag, (0, 1, 3, 2))

    return y, (s5[:, 0], s5[:, 1], unstack(hgs, HG_DK), unstack(glas, GLA_DK), unstack(rets, RET_DK))


def _sample_step(x, states, p, consts, tabs, rope):
    nb = x.shape[0]
    depth = p['w_t'].shape[0]
    s5re, s5im, hg, gla, ret = states
    to_lanes = lambda s: jnp.moveaxis(s, 1, -1)
    s5re_t = to_lanes(s5re).reshape(depth, S5_N, nb)
    s5im_t = to_lanes(s5im).reshape(depth, S5_N, nb)
    hg_t, gla_t, ret_t = to_lanes(hg), to_lanes(gla), to_lanes(ret)
    cos_t, sin_t = rope
    col = lambda a: jnp.swapaxes(a, -1, -2)
    per_layer = [p['norm_w'], p['w_t'], p['w_out'], col(p['bbd']), p['cbd_t'], col(p['a_re']), col(p['a_im']),
                 col(p['d_skip']), col(p['w_glu']), col(p['lbp']), col(p['hg_norm']), col(p['w_up']),
                 col(p['b_gate']), col(p['gla_norm']), col(p['ret_norm'])]
    shared = [p['final_norm'], col(cos_t), col(sin_t), tabs[4], consts['ones_h']]
    layer_spec = lambda a: pl.BlockSpec((None,) + a.shape[1:], lambda l, h: (l,) + (0,) * (a.ndim - 1))
    head_spec = lambda a: pl.BlockSpec((None, None) + a.shape[2:], lambda l, h: (l, h) + (0,) * (a.ndim - 2))
    state_specs = [layer_spec(s5re_t), layer_spec(s5im_t), head_spec(hg_t), head_spec(gla_t), head_spec(ret_t)]
    in_specs = ([_full(x.shape)] + [layer_spec(a) for a in per_layer] + [_full(a.shape) for a in shared]
                + state_specs)
    state_arrays = [s5re_t, s5im_t, hg_t, gla_t, ret_t]
    out_shape = tuple([jax.ShapeDtypeStruct(x.shape, F32)]
                      + [jax.ShapeDtypeStruct(a.shape, F32) for a in state_arrays])
    out_specs = tuple([pl.BlockSpec(x.shape, lambda l, h: (0, 0))] + state_specs)
    scratch = [pltpu.VMEM((nb, D_MODEL), F32), pltpu.VMEM((N_PACK, nb), F32), pltpu.VMEM((BRANCH, nb), F32),
               pltpu.VMEM((3 * BRANCH, nb), F32), pltpu.VMEM((4 * BRANCH, nb), F32)]
    outs = pl.pallas_call(
        _sample_kernel, grid=(depth, HEADS), in_specs=in_specs, out_specs=out_specs, out_shape=out_shape,
        scratch_shapes=scratch, name='sample_step',
        compiler_params=pltpu.CompilerParams(dimension_semantics=('arbitrary', 'arbitrary'),
                                             vmem_limit_bytes=VMEM_LIMIT),
    )(x, *per_layer, *shared, *state_arrays)
    from_lanes = lambda s: jnp.moveaxis(s, -1, 1)
    new = (from_lanes(outs[1].reshape(depth, S5_GROUPS, S5_STATE, nb)),
           from_lanes(outs[2].reshape(depth, S5_GROUPS, S5_STATE, nb)),
           from_lanes(outs[3]), from_lanes(outs[4]), from_lanes(outs[5]))
    return outs[0], new


def kernel(x_prompt, x_sample, state_s5_re, state_s5_im, state_hgrn, state_gla, state_ret, norm_w, final_norm_w, w_in, w_out, s5_lam_re, s5_lam_im, s5_log_step, s5_b_re, s5_b_im, s5_c_re, s5_c_im, s5_d, s5_w_glu, hgrn_lb_logits, hgrn_norm_w, gla_w_gate_up, gla_b_gate, gla_norm_w, ret_norm_w):
    depth = w_in.shape[0]
    seq = x_prompt.shape[1]
    consts = _constants()
    tabs = _ret_tables()
    rope_p = _rope_tables(jnp.arange(seq))
    rope_s = _rope_tables(PAST_LEN + jnp.arange(1))
    lb = jnp.cumsum(jax.nn.softmax(hgrn_lb_logits.astype(F32), axis=0), axis=0)
    lb = (lb - lb[0:1])[:, None, :]
    bbd, cbd_t, a_re, a_im = _s5_discretize(s5_lam_re, s5_lam_im, s5_log_step, s5_b_re, s5_b_im, s5_c_re,
                                            s5_c_im, consts['s5_mask'])
    w_t = lax.optimization_barrier(_pack_w_in_t(w_in))
    row = lambda a: a[:, None, :].astype(F32)
    w_up = jnp.zeros((depth, LANES, HEADS * GLA_DK), F32).at[:, :GLA_LOWRANK].set(gla_w_gate_up.astype(F32))
    p = dict(norm_w=row(norm_w),
             w_pack=jnp.swapaxes(w_t, 1, 2), w_t=w_t, w_out=w_out.astype(BF16),
             bbd=bbd, cbd=jnp.swapaxes(cbd_t, 1, 2).astype(BF16), cbd_t=cbd_t, a_re=a_re, a_im=a_im,
             d_skip=row(s5_d), w_glu=s5_w_glu.astype(BF16),
             lbp=jnp.concatenate([jnp.log(lb), jnp.log1p(-lb), 1.0 - lb, jnp.zeros((depth, 5, BRANCH), F32)],
                                 axis=1),
             hg_norm=row(hgrn_norm_w), w_up=w_up, b_gate=row(gla_b_gate), gla_norm=row(gla_norm_w),
             ret_norm=row(ret_norm_w), final_norm=final_norm_w[None, :].astype(F32))

    xp = x_prompt
    new_p = ([], [], [], [], [])
    for l in range(depth):
        xp, st_p = _prompt_layer(xp, l, p, consts, tabs, rope_p, l == depth - 1)
        for i in range(5):
            new_p[i].append(st_p[i])
    xs, new_s = _sample_step(x_sample.reshape(x_sample.shape[0], D_MODEL),
                             (state_s5_re, state_s5_im, state_hgrn, state_gla, state_ret),
                             p, consts, tabs, rope_s)
    return (xp, xs.reshape(x_sample.shape),
            jnp.stack(new_p[0]), jnp.stack(new_p[1]), jnp.stack(new_p[2]), jnp.stack(new_p[3]),
            jnp.stack(new_p[4])) + new_s
```

```python
import functools
import math

import numpy as np
import jax
import jax.numpy as jnp
from jax import lax
from jax.experimental import pallas as pl
from jax.experimental.pallas import tpu as pltpu

F32 = jnp.float32
BF16 = jnp.bfloat16

D_MODEL = 1024
BRANCH = 256
S5_CH = 16
S5_GROUPS = 16
S5_STATE = 64
S5_N = S5_GROUPS * S5_STATE
HEADS = 4
HG_DK = 64
GLA_DK = 32
RET_DK = 32
DV = 64
GLA_LOWRANK = 16
GLA_TAU = 16.0
ROPE_BASE = 10000.0
PAST_LEN = 16384
EPS = 1e-6
SUB = 16
PAIR = 2 * SUB
GROUP_ROWS = 4

LANES = 128
T_CHUNK = 64
PITCH = T_CHUNK + 8
SUB_TERM_ROWS = SUB * (SUB + SUB // 2) // 2
TERM_ROWS = (T_CHUNK // SUB) * SUB_TERM_ROWS
LOG2E = math.log2(math.e)
MASK_NEG = -1e30
VMEM_LIMIT = 60 * 1024 * 1024

C_U, C_SZ, C_HQ, C_HF, C_HI, C_HZ = 0, 256, 512, 768, 1024, 1280
C_GQ, C_GK, C_GV, C_GZ = 1536, 1664, 1792, 2048
C_RQ, C_RK, C_RV, C_RZ = 2304, 2432, 2560, 2816
C_LR = 3072
N_PACK = 3200
N_QUARTER = 4
QUARTER = C_LR // N_QUARTER


def _dot(a, b):
    return jnp.dot(a.astype(BF16), b.astype(BF16), preferred_element_type=F32)


def _dot_nt(a, b):
    return lax.dot_general(a.astype(BF16), b.astype(BF16), (((1,), (1,)), ((), ())),
                           preferred_element_type=F32)


def _dot_tn(a, b):
    return lax.dot_general(a.astype(BF16), b.astype(BF16), (((0,), (0,)), ((), ())),
                           preferred_element_type=F32)


def _split3(x):
    x1 = x.astype(BF16)
    r1 = x - x1.astype(F32)
    x2 = r1.astype(BF16)
    x3 = (r1 - x2.astype(F32)).astype(BF16)
    return x1, x2, x3


def _dot_sel_lhs(sel, x):
    x1, x2, x3 = _split3(x)
    d = lambda p: jnp.dot(sel, p, preferred_element_type=F32)
    return d(x1) + d(x2) + d(x3)


def _dot_sel_rhs2(x, sel):
    x1 = x.astype(BF16)
    x2 = (x - x1.astype(F32)).astype(BF16)
    return (jnp.dot(x1, sel, preferred_element_type=F32)
            + jnp.dot(x2, sel, preferred_element_type=F32))


def _dot3(a, b):
    a1 = a.astype(BF16)
    a2 = (a - a1.astype(F32)).astype(BF16)
    b1 = b.astype(BF16)
    b2 = (b - b1.astype(F32)).astype(BF16)
    d = lambda p, q: jnp.dot(p, q, preferred_element_type=F32)
    return d(a1, b1) + d(a1, b2) + d(a2, b1)


def _sigmoid(x):
    return 0.5 * jnp.tanh(0.5 * x) + 0.5


def _silu(x):
    return x * _sigmoid(x)


def _log_sigmoid(x):
    return jnp.minimum(x, 0.0) - jnp.log(1.0 + jnp.exp(-jnp.abs(x)))


def _gelu_tanh(x):
    return 0.5 * x * (1.0 + jnp.tanh(math.sqrt(2.0 / math.pi) * (x + 0.044715 * (x * x * x))))


def _rmsnorm_rows(x, w):
    return x * lax.rsqrt(jnp.mean(x * x, axis=-1, keepdims=True) + EPS) * w


def _head_rms(o, ones_h, gain):
    ms = _dot_sel_rhs2(o * o, ones_h) * (1.0 / DV)
    return o * lax.rsqrt(ms + EPS) * gain


def _head_ln(o, ones_h, gain):
    c = o - _dot_sel_rhs2(o, ones_h) * (1.0 / DV)
    var = _dot_sel_rhs2(c * c, ones_h) * (1.0 / DV)
    return c * lax.rsqrt(var + EPS) * gain


def _hgrn_gates(xf, loglb, log1mlb, one_m_lb):
    bterm = log1mlb + _log_sigmoid(xf)
    m = jnp.maximum(loglb, bterm)
    log_f = m + jnp.log(jnp.exp(loglb - m) + jnp.exp(bterm - m))
    return log_f, one_m_lb * _sigmoid(-xf)


def _rotary(t, cos, sin_signed, first_half):
    half = RET_DK // 2
    swapped = jnp.where(first_half, pltpu.roll(t, LANES - half, 1), pltpu.roll(t, half, 1))
    return t * cos + swapped * sin_signed


def _s5_output(y_lin, u, sz, dskip, wglu):
    y = _gelu_tanh(y_lin + u * dskip)
    y = y * _sigmoid(_dot(y, wglu))
    return y * _silu(sz)


def _mix_and_project(x, mix, wout, fnw, apply_final):
    out = x + _dot(mix, wout)
    if apply_final:
        out = _rmsnorm_rows(out, fnw)
    return out


class _Slabs:
    def __init__(self, ref, base=0):
        self.ref, self.base = ref, base

    def keep(self, value):
        for c in range(self.ref.shape[0]):
            self.ref[c, pl.ds(self.base, value.shape[0]), :] = value[:, c * LANES:(c + 1) * LANES]

    def tile(self, r0, n):
        return jnp.concatenate([self.ref[c, pl.ds(self.base + r0, n), :] for c in range(self.ref.shape[0])],
                               axis=1)

    def row(self, r, n):
        return jnp.concatenate([self.ref[c, pl.ds(self.base + r, n, stride=0), :]
                                for c in range(self.ref.shape[0])], axis=1)


class _GatedStream:
    def __init__(self, q, k, v, g, s_ref, ones_kv, mask_vk, tril, term_scr, row_scr, kdim):
        self.q, self.k, self.v, self.g = q, k, v, g
        self.bc, self.kc, self.vc = (_Slabs(r) for r in row_scr)
        self.s_ref, self.ones_kv, self.mask_vk = s_ref, ones_kv, mask_vk
        self.tril, self.term_scr, self.kdim = tril, term_scr, kdim
        self.ws, self.inter, self.att, self.cross = {}, {}, {}, {}

    def cumulate(self):
        self.bc.keep(_dot_sel_lhs(self.tril, self.g()) * LOG2E)
        self.kc.keep(self.k())
        self.vc.keep(self.v())

    def _total(self, p):
        return self.bc.tile(PAIR * (p + 1) - 1, 1)

    def update_part(self, p):
        sl = slice(PAIR * p, PAIR * (p + 1))
        self.ws[p] = _dot_tn(self.v()[sl],
                             self.k()[sl] * jnp.exp2(self._total(p) - self.bc.tile(PAIR * p, PAIR)))

    def cross_part(self, p):
        r_a, r_b = PAIR * p, PAIR * p + SUB
        edge = self.bc.row(r_b - 1, SUB)
        q_b = self.q()[r_b:r_b + SUB] * jnp.exp2(self.bc.tile(r_b, SUB) - edge)
        k_a = self.kc.tile(r_a, SUB) * jnp.exp2(edge - self.bc.tile(r_a, SUB))
        lane = lax.broadcasted_iota(jnp.int32, (SUB, self.kdim), 1) // (self.kdim // HEADS)
        q_heads = jnp.concatenate([jnp.where(lane == h, q_b, 0.0) for h in range(HEADS)], axis=0)
        scores = _dot_nt(q_heads, k_a)
        res = _dot(scores, self.vc.tile(r_a, SUB))
        lane_v = lax.broadcasted_iota(jnp.int32, (SUB, BRANCH), 1) // DV
        out = jnp.where(lane_v == 0, res[0:SUB], 0.0)
        for h in range(1, HEADS):
            out = out + jnp.where(lane_v == h, res[SUB * h:SUB * (h + 1)], 0.0)
        self.cross[p] = out

    def products_part(self, j):
        hs, term_scr, r0 = SUB // 2, self.term_scr, SUB * j
        q, bc = self.q()[r0:r0 + SUB], self.bc.tile(r0, SUB)
        trow = lax.broadcasted_iota(jnp.int32, (hs, self.kdim), 0)
        neg = [jnp.where(trow >= d, 0.0, MASK_NEG) for d in range(1, hs)]
        q_lo, q_hi, b_lo, b_hi = q[0:hs], q[hs:SUB], bc[0:hs], bc[hs:SUB]
        row = j * SUB_TERM_ROWS
        for s in range(SUB):
            ks, bs = self.kc.row(r0 + s, hs), self.bc.row(r0 + s, hs)
            d = s % hs
            q_dg, b_dg = (q_lo, b_lo) if s < hs else (q_hi, b_hi)
            e_dg = b_dg - bs if d == 0 else b_dg - bs + neg[d - 1]
            term_scr[row:row + hs, :] = q_dg * ks * jnp.exp2(e_dg)
            row += hs
            if s < hs:
                term_scr[row:row + hs, :] = q_hi * ks * jnp.exp2(b_hi - bs)
                row += hs

    def matmul_part(self, p):
        sl = slice(PAIR * p, PAIR * (p + 1))
        if p == 0:
            self.state = self.s_ref[...]
        self.inter[p] = _dot_nt(self.q()[sl] * jnp.exp2(self.bc.tile(PAIR * p, PAIR)), self.state)
        self.state = self.state * jnp.exp2(self._total(p)) + self.ws[p] * self.mask_vk
        if p == T_CHUNK // PAIR - 1:
            self.s_ref[...] = self.state
        rows = slice(2 * p * SUB_TERM_ROWS, 2 * (p + 1) * SUB_TERM_ROWS)
        self.att[p] = _dot(self.term_scr[rows, :], self.ones_kv)

    def output_part(self, j):
        hs, r0, p, second = SUB // 2, SUB * j, j // 2, j % 2
        att = self.att[p][second * SUB_TERM_ROWS:(second + 1) * SUB_TERM_ROWS]
        start = self.inter[p][second * SUB:(second + 1) * SUB]
        if second:
            start = start + self.cross[p]
        acc_lo, acc_hi = start[0:hs], start[hs:SUB]
        row = 0
        for s in range(SUB):
            vs = self.vc.row(r0 + s, hs)
            if s < hs:
                acc_lo = acc_lo + att[row:row + hs] * vs
                row += hs
            acc_hi = acc_hi + att[row:row + hs] * vs
            row += hs
        return jnp.concatenate([acc_lo, acc_hi], axis=0)


def _ret_chunk(q, k, v, cos, sin_signed, s_ref, dstack, inner, kdec, cdec, mask_vk):
    lane = lax.broadcasted_iota(jnp.int32, (T_CHUNK, HEADS * RET_DK), 1)
    first_half = (lane % RET_DK) < (RET_DK // 2)
    rq = _rotary(q, cos, sin_signed, first_half)
    rk = _rotary(k, cos, sin_signed, first_half) * (RET_DK ** -0.5)
    lane_v = lax.broadcasted_iota(jnp.int32, (T_CHUNK, HEADS * DV), 1)
    kst = jnp.concatenate([jnp.where(lane // RET_DK == h, rk, 0.0) for h in range(HEADS)], axis=0)
    vst = jnp.concatenate([jnp.where(lane_v // DV == h, v, 0.0) for h in range(HEADS)], axis=0)
    p = _dot_nt(rq, kst) * dstack
    s_t = s_ref[...]
    o = _dot(p, vst) + _dot_nt(rq, s_t) * inner
    s_ref[...] = s_t * cdec + _dot_tn(v, rk * kdec) * mask_vk
    return o


def _prompt_kernel(apply_final,
                   x_ref, xn_ref, normw_ref, win_ref, wout_ref, bbd_ref, cbd_ref, are_ref, aim_ref,
                   dskip_ref, wglu_ref, lbp_ref, hgn_ref, wup_ref, bgate_ref, glan_ref, retn_ref, cos_ref,
                   sin_ref, dstack_ref, inner_ref, kdec_ref, cdec_ref, onesh_ref, onesg_ref, tril_ref,
                   fnw_ref,
                   y_ref, s5_ref, hgs_ref, glas_ref, rets_ref,
                   proj_scr, lr_scr, hk_scr, bu_scr, mix_scr, *stream_scratch):
    nb = x_ref.shape[0]
    rows = nb * T_CHUNK
    n_slab = 2 * S5_N // LANES
    half = n_slab // 2
    step = pl.program_id(0)

    def w_quarter(qtr):
        return win_ref[:, qtr * QUARTER:(qtr + 1) * QUARTER]

    def proj(col, width, rs=slice(None)):
        return proj_scr[col // QUARTER, rs, col % QUARTER:col % QUARTER + width]

    @pl.when(step == 0)
    def _init():
        s5_ref[...] = jnp.zeros_like(s5_ref)
        hgs_ref[...] = jnp.zeros_like(hgs_ref)
        glas_ref[...] = jnp.zeros_like(glas_ref)
        rets_ref[...] = jnp.zeros_like(rets_ref)
        bu_scr[...] = jnp.zeros_like(bu_scr)
        h0 = _rmsnorm_rows(x_ref[...].reshape(rows, D_MODEL), normw_ref[...]).astype(BF16)
        for qtr in range(N_QUARTER):
            proj_scr[qtr] = jnp.dot(h0, w_quarter(qtr), preferred_element_type=F32)
        lr_scr[...] = jnp.dot(h0, win_ref[:, C_LR:N_PACK], preferred_element_type=F32)

    u16 = proj(C_U, BRANCH).astype(BF16)

    def drive(c2):
        bu = jnp.dot(u16, bbd_ref[:, 2 * c2 * LANES:2 * (c2 + 1) * LANES], preferred_element_type=F32)
        for cc in range(2):
            for b in range(nb):
                bu_scr[2 * c2 + cc, b * PITCH:b * PITCH + T_CHUNK, :] = bu[b * T_CHUNK:(b + 1) * T_CHUNK,
                                                                           cc * LANES:(cc + 1) * LANES]

    lbp = lbp_ref[...]
    log_f, hk = _hgrn_gates(proj(C_HF, BRANCH), lbp[0:1, :], lbp[1:2, :], lbp[2:3, :])
    proj_scr[C_HF // QUARTER, :, C_HF % QUARTER:C_HF % QUARTER + BRANCH] = log_f
    hk_scr[...] = hk
    g_gla = _log_sigmoid(_dot3(lr_scr[...], wup_ref[...]) + bgate_ref[...])
    lr_scr[...] = g_gla * (1.0 / GLA_TAU)
    for c2 in range(n_slab // 2):
        drive(c2)
    a_re = [jnp.broadcast_to(are_ref[:, c * LANES:(c + 1) * LANES], (nb, LANES)) for c in range(half)]
    a_im = [jnp.broadcast_to(aim_ref[:, c * LANES:(c + 1) * LANES], (nb, LANES)) for c in range(half)]
    s_init = s5_ref[...]
    carry0 = tuple(s_init[:, c * LANES:(c + 1) * LANES] for c in range(n_slab))

    def scan_step(t, carry):
        new = [None] * n_slab
        for c in range(half):
            sr, si = carry[c], carry[half + c]
            br = bu_scr[c, pl.ds(t, nb, stride=PITCH), :]
            bi = bu_scr[half + c, pl.ds(t, nb, stride=PITCH), :]
            nr = a_re[c] * sr - a_im[c] * si + br
            ni = a_re[c] * si + a_im[c] * sr + bi
            bu_scr[c, pl.ds(t, nb, stride=PITCH), :] = nr
            bu_scr[half + c, pl.ds(t, nb, stride=PITCH), :] = ni
            new[c], new[half + c] = nr, ni
        return tuple(new)

    carry = lax.fori_loop(0, T_CHUNK, scan_step, carry0)
    s5_ref[...] = jnp.concatenate(carry, axis=1)
    s_all = jnp.concatenate([bu_scr[c] for c in range(n_slab)], axis=1)
    y_all = _dot(s_all, cbd_ref[...])
    y_lin = jnp.concatenate([y_all[b * PITCH:b * PITCH + T_CHUNK] for b in range(nb)], axis=0)
    mix_scr[:, 0:BRANCH] = _s5_output(y_lin, proj(C_U, BRANCH), proj(C_SZ, BRANCH), dskip_ref[...],
                                      wglu_ref[...])

    ones_h = onesh_ref[...]
    ones_g = onesg_ref[...]
    mask_h = ones_h.astype(F32)
    mask_g = jnp.transpose(ones_g.astype(F32))
    tril = tril_ref[...]

    def rows_of(bp, odd):
        return pl.ds(pl.multiple_of((2 * bp + odd) * T_CHUNK, T_CHUNK), T_CHUNK)

    def hgrn_stream(bp, odd, term, row_scr):
        rs = rows_of(bp, odd)
        return _GatedStream(lambda: proj(C_HQ, BRANCH, rs), lambda: hk_scr[rs, :], lambda: proj(C_HI, BRANCH, rs),
                            lambda: proj(C_HF, BRANCH, rs), hgs_ref.at[bp, odd], ones_h, mask_h, tril, term,
                            row_scr, HEADS * HG_DK)

    def gla_stream(bp, odd, term, row_scr):
        rs = rows_of(bp, odd)
        return _GatedStream(lambda: proj(C_GQ, LANES, rs) * (GLA_DK ** -0.5), lambda: proj(C_GK, LANES, rs),
                            lambda: proj(C_GV, BRANCH, rs), lambda: lr_scr[rs, :],
                            glas_ref.at[bp, odd], ones_g, mask_g, tril, term, row_scr, HEADS * GLA_DK)

    n_sub = T_CHUNK // SUB
    def retention(bp, odd):
        rs = rows_of(bp, odd)
        mix_scr[rs, 3 * BRANCH:4 * BRANCH] = _ret_chunk(
            proj(C_RQ, LANES, rs), proj(C_RK, LANES, rs), proj(C_RV, BRANCH, rs), cos_ref[...],
            sin_ref[...], rets_ref.at[bp, odd], dstack_ref[...], inner_ref[...], kdec_ref[...],
            cdec_ref[...], mask_g)

    def per_group(grp, _):
        st, slots = [], []
        for i in range(GROUP_ROWS):
            bp, odd = (GROUP_ROWS // 2) * grp + i // 2, i % 2
            scr = stream_scratch[8 * i:8 * (i + 1)]
            st += [hgrn_stream(bp, odd, scr[0], scr[2:5]), gla_stream(bp, odd, scr[1], scr[5:8])]
            slots += [(bp, odd, BRANCH), (bp, odd, 2 * BRANCH)]

        def emit(i, j):
            bp, odd, col = slots[i]
            r0 = pl.multiple_of((2 * bp + odd) * T_CHUNK + SUB * j, SUB)
            mix_scr[pl.ds(r0, SUB), col:col + BRANCH] = st[i].output_part(j)

        for s_ in st:
            s_.cumulate()
        for i in range(GROUP_ROWS):
            retention((GROUP_ROWS // 2) * grp + i // 2, i % 2)
        for s_ in st:
            for p in range(n_sub // 2):
                s_.update_part(p)
        for s_ in st:
            for j in range(n_sub):
                s_.products_part(j)
        for s_ in st:
            for p in range(n_sub // 2):
                s_.cross_part(p)
        for s_ in st:
            for p in range(n_sub // 2):
                s_.matmul_part(p)
        for i in range(len(st)):
            for j in range(n_sub):
                emit(i, j)
        return 0

    lax.fori_loop(0, nb // GROUP_ROWS, per_group, 0)

    o_hg = _head_rms(mix_scr[:, BRANCH:2 * BRANCH], ones_h, hgn_ref[...])
    mix_scr[:, BRANCH:2 * BRANCH] = o_hg * _silu(proj(C_HZ, BRANCH))
    o_gla = _head_rms(mix_scr[:, 2 * BRANCH:3 * BRANCH], ones_h, glan_ref[...])
    mix_scr[:, 2 * BRANCH:3 * BRANCH] = o_gla * _silu(proj(C_GZ, BRANCH))
    o_ret = _head_ln(mix_scr[:, 3 * BRANCH:4 * BRANCH], ones_h, retn_ref[...])
    mix_scr[:, 3 * BRANCH:4 * BRANCH] = o_ret * _silu(proj(C_RZ, BRANCH))
    hn = _rmsnorm_rows(xn_ref[...].reshape(rows, D_MODEL), normw_ref[...]).astype(BF16)
    for qtr in range(N_QUARTER):
        proj_scr[qtr] = jnp.dot(hn, w_quarter(qtr), preferred_element_type=F32)
    lr_scr[...] = jnp.dot(hn, win_ref[:, C_LR:N_PACK], preferred_element_type=F32)
    out = _mix_and_project(x_ref[...].reshape(rows, D_MODEL), mix_scr[...], wout_ref[...], fnw_ref[...],
                           apply_final)
    y_ref[...] = out.reshape(nb, T_CHUNK, D_MODEL)


def _dot_sel_lhs2(sel, x):
    x1 = x.astype(BF16)
    x2 = (x - x1.astype(F32)).astype(BF16)
    return (jnp.dot(sel, x1, preferred_element_type=F32) + jnp.dot(sel, x2, preferred_element_type=F32))


def _head_rms_t(o, ones_h, gain):
    ms = _dot_sel_lhs2(ones_h, o * o) * (1.0 / DV)
    return o * lax.rsqrt(ms + EPS) * gain


def _head_ln_t(o, ones_h, gain):
    c = o - _dot_sel_lhs2(ones_h, o) * (1.0 / DV)
    var = _dot_sel_lhs2(ones_h, c * c) * (1.0 / DV)
    return c * lax.rsqrt(var + EPS) * gain


def _rotary_t(t, cos, sin_signed, first_half):
    half = RET_DK // 2
    swapped = jnp.where(first_half, pltpu.roll(t, LANES - half, 0), pltpu.roll(t, half, 0))
    return t * cos + swapped * sin_signed


def _sample_kernel(x_ref, normw_ref, wt_ref, wout_ref, bbdt_ref, cbdt_ref, are_ref, aim_ref, dskip_ref,
                   wglut_ref, lbp_ref, hgn_ref, wupt_ref, bgate_ref, glan_ref, retn_ref, fnw_ref,
                   cos_ref, sin_ref, dret_ref, onesh_ref,
                   s5re_ref, s5im_ref, hg_ref, gla_ref, ret_ref,
                   y_ref, s5re_o, s5im_o, hg_o, gla_o, ret_o,
                   xs_scr, pt_scr, hk_scr, ot_scr, mixt_scr):
    layer, head = pl.program_id(0), pl.program_id(1)
    last_layer, last_head = pl.num_programs(0) - 1, pl.num_programs(1) - 1

    @pl.when((layer == 0) & (head == 0))
    def _load_x():
        xs_scr[...] = x_ref[...]

    @pl.when(head == 0)
    def _dense():
        hh = _rmsnorm_rows(xs_scr[...], normw_ref[...]).astype(BF16)
        pt_scr[...] = lax.dot_general(wt_ref[...], hh, (((1,), (1,)), ((), ())), preferred_element_type=F32)

        u = pt_scr[C_U:C_U + BRANCH, :]
        bu = _dot3(bbdt_ref[...], u)
        a_re, a_im = are_ref[...], aim_ref[...]
        s0r, s0i = s5re_ref[...], s5im_ref[...]
        s_re = a_re * s0r - a_im * s0i + bu[0:S5_N]
        s_im = a_re * s0i + a_im * s0r + bu[S5_N:2 * S5_N]
        s5re_o[...] = s_re
        s5im_o[...] = s_im
        y = _gelu_tanh(_dot3(cbdt_ref[...], jnp.concatenate([s_re, s_im], axis=0)) + u * dskip_ref[...])
        y = y * _sigmoid(jnp.dot(wglut_ref[...], y.astype(BF16), preferred_element_type=F32))
        mixt_scr[0:BRANCH, :] = y * _silu(pt_scr[C_SZ:C_SZ + BRANCH, :])

        lbp = lbp_ref[...]
        log_f, hk = _hgrn_gates(pt_scr[C_HF:C_HF + BRANCH, :], lbp[:, 0:1], lbp[:, 1:2], lbp[:, 2:3])
        pt_scr[C_HF:C_HF + BRANCH, :] = jnp.exp(log_f)
        hk_scr[...] = hk
        g_gla = _log_sigmoid(_dot3(wupt_ref[...], pt_scr[C_LR:C_LR + LANES, :]) + bgate_ref[...])
        pt_scr[C_LR:C_LR + LANES, :] = jnp.exp(g_gla * (1.0 / GLA_TAU))
        pt_scr[C_GQ:C_GQ + LANES, :] = pt_scr[C_GQ:C_GQ + LANES, :] * (GLA_DK ** -0.5)
        row = lax.broadcasted_iota(jnp.int32, (HEADS * RET_DK, LANES), 0)
        first_half = (row % RET_DK) < (RET_DK // 2)
        pt_scr[C_RQ:C_RQ + LANES, :] = _rotary_t(pt_scr[C_RQ:C_RQ + LANES, :], cos_ref[...], sin_ref[...],
                                                 first_half)
        pt_scr[C_RK:C_RK + LANES, :] = _rotary_t(pt_scr[C_RK:C_RK + LANES, :], cos_ref[...], sin_ref[...],
                                                 first_half) * (RET_DK ** -0.5)

    def head_update(s0_ref, s_out_ref, dk, dec_ref, dec_row, key_ref, key_row, q_row, v_row, out_row):
        vt = pt_scr[pl.ds(pl.multiple_of(v_row + head * DV, DV), DV), :]

        def feature(kk, acc):
            r = head * dk + kk
            bcast = lambda ref, r0: jnp.broadcast_to(ref[pl.ds(r0 + r, 1), :], (DV, LANES))
            s_new = s0_ref[kk] * bcast(dec_ref, dec_row) + bcast(key_ref, key_row) * vt
            s_out_ref[kk] = s_new
            return acc + bcast(pt_scr, q_row) * s_new

        acc = lax.fori_loop(0, dk, feature, jnp.zeros((DV, LANES), F32), unroll=4)
        ot_scr[pl.ds(pl.multiple_of(out_row + head * DV, DV), DV), :] = acc

    head_update(hg_ref, hg_o, HG_DK, pt_scr, C_HF, hk_scr, 0, C_HQ, C_HI, 0)
    head_update(gla_ref, gla_o, GLA_DK, pt_scr, C_LR, pt_scr, C_GK, C_GQ, C_GV, BRANCH)
    head_update(ret_ref, ret_o, RET_DK, dret_ref, 0, pt_scr, C_RK, C_RQ, C_RV, 2 * BRANCH)

    @pl.when(head == last_head)
    def _finish():
        ones_h = onesh_ref[...]
        o_hg = _head_rms_t(ot_scr[0:BRANCH, :], ones_h, hgn_ref[...])
        mixt_scr[BRANCH:2 * BRANCH, :] = o_hg * _silu(pt_scr[C_HZ:C_HZ + BRANCH, :])
        o_gla = _head_rms_t(ot_scr[BRANCH:2 * BRANCH, :], ones_h, glan_ref[...])
        mixt_scr[2 * BRANCH:3 * BRANCH, :] = o_gla * _silu(pt_scr[C_GZ:C_GZ + BRANCH, :])
        o_ret = _head_ln_t(ot_scr[2 * BRANCH:3 * BRANCH, :], ones_h, retn_ref[...])
        mixt_scr[3 * BRANCH:4 * BRANCH, :] = o_ret * _silu(pt_scr[C_RZ:C_RZ + BRANCH, :])
        out = xs_scr[...] + lax.dot_general(mixt_scr[...].astype(BF16), wout_ref[...], (((0,), (0,)), ((), ())),
                                            preferred_element_type=F32)
        xs_scr[...] = out

        @pl.when(layer == last_layer)
        def _emit():
            y_ref[...] = _rmsnorm_rows(out, fnw_ref[...])


def _ret_log_gamma():
    return jnp.log1p(-jnp.exp2(-5.0 - jnp.arange(HEADS, dtype=F32)))


def _constants():
    ones_h = (np.arange(BRANCH)[:, None] // DV == np.arange(BRANCH)[None, :] // DV)
    ones_g = (np.arange(HEADS * GLA_DK)[:, None] // GLA_DK == np.arange(BRANCH)[None, :] // DV)
    r = np.arange(T_CHUNK)
    tril = (r[:, None] // PAIR == r[None, :] // PAIR) & (r[None, :] <= r[:, None])
    same_group_b = np.arange(BRANCH)[:, None] // S5_CH == np.arange(S5_N)[None, :] // S5_STATE
    as_bf16 = lambda m: jnp.asarray(m.astype(np.float32), dtype=BF16)
    return dict(ones_h=as_bf16(ones_h), ones_g=as_bf16(ones_g), tril=as_bf16(tril),
                s5_mask=jnp.asarray(same_group_b.astype(np.float32)))


def _ret_tables():
    lg = _ret_log_gamma()
    idx = jnp.arange(T_CHUNK, dtype=F32)
    rel = idx[:, None] - idx[None, :]
    causal = rel >= 0
    decay = jnp.where(causal[None], jnp.exp(jnp.where(causal, rel, 0.0)[None] * lg[:, None, None]), 0.0)
    dstack = jnp.transpose(decay, (1, 0, 2)).reshape(T_CHUNK, HEADS * T_CHUNK)
    inner = jnp.repeat(jnp.exp((idx[:, None] + 1.0) * lg[None, :]), DV, axis=1)
    kdec = jnp.repeat(jnp.exp((T_CHUNK - 1.0 - idx[:, None]) * lg[None, :]), RET_DK, axis=1)
    cdec = jnp.repeat(jnp.exp(T_CHUNK * lg)[None, :], RET_DK, axis=1)
    dret = jnp.broadcast_to(jnp.repeat(jnp.exp(lg), RET_DK)[:, None], (HEADS * RET_DK, LANES))
    return dstack, inner, kdec, cdec, dret


def _rope_tables(pos):
    half = RET_DK // 2
    inv = ROPE_BASE ** (-jnp.arange(half, dtype=F32) / half)
    ang = pos.astype(F32)[:, None] * inv[None, :]
    cos, sin = jnp.cos(ang), jnp.sin(ang)
    cos_t = jnp.tile(jnp.concatenate([cos, cos], axis=1), (1, HEADS))
    sin_t = jnp.tile(jnp.concatenate([-sin, sin], axis=1), (1, HEADS))
    return cos_t, sin_t


def _pack_w_in_t(w):
    wt = jnp.swapaxes(w, 1, 2)
    offs = np.cumsum([0, 256, 256, 256, 256, 256, 256, 128, 128, 256, 16, 256, 128, 128, 256, 256])
    seg = lambda i: wt[:, int(offs[i]):int(offs[i + 1]), :]
    order = [0, 1, 2, 3, 4, 5, 6, 7, 8, 10, 11, 12, 13, 14]
    pad = jnp.zeros((w.shape[0], LANES - GLA_LOWRANK, w.shape[1]), w.dtype)
    return jnp.concatenate([seg(i) for i in order] + [seg(9), pad], axis=1).astype(BF16)


def _s5_discretize(lam_re, lam_im, log_step, b_re, b_im, c_re, c_im, mask):
    lr, li = lam_re.astype(F32), lam_im.astype(F32)
    step = jnp.exp(log_step.astype(F32))[..., None]
    mag = jnp.exp(lr * step)
    ab_re = mag * jnp.cos(li * step)
    ab_im = mag * jnp.sin(li * step)
    den = lr * lr + li * li
    nr = ab_re - 1.0
    f_re = (nr * lr + ab_im * li) / den
    f_im = (ab_im * lr - nr * li) / den
    br, bi = b_re.astype(F32), b_im.astype(F32)
    bb_re = f_re[..., None] * br - f_im[..., None] * bi
    bb_im = f_re[..., None] * bi + f_im[..., None] * br
    nl = lr.shape[0]

    def drive(bb):
        rows = jnp.transpose(bb, (0, 1, 3, 2)).reshape(nl, BRANCH, S5_STATE)
        return jnp.tile(rows, (1, 1, S5_GROUPS)) * mask

    def readout(cc):
        rows = cc.astype(F32).reshape(nl, BRANCH, S5_STATE)
        return jnp.tile(rows, (1, 1, S5_GROUPS)) * mask

    bbd = jnp.concatenate([drive(bb_re), drive(bb_im)], axis=2)
    cbd_t = jnp.concatenate([readout(c_re), -readout(c_im)], axis=2)
    return bbd, cbd_t, ab_re.reshape(nl, 1, S5_N), ab_im.reshape(nl, 1, S5_N)


def _full(shape):
    return pl.BlockSpec(shape, lambda *_: (0,) * len(shape), pipeline_mode=pl.Buffered(1))


def _of_layer(arr, layer):
    nd = arr.ndim - 1
    return pl.BlockSpec((None,) + arr.shape[1:], lambda *_: (layer,) + (0,) * nd,
                        pipeline_mode=pl.Buffered(1))


def _prompt_layer(x, layer, p, consts, tabs, rope, apply_final):
    nb, seq, _ = x.shape
    n_steps = seq // T_CHUNK
    rows = nb * T_CHUNK
    dstack, inner, kdec, cdec, _ = tabs
    cos_t, sin_t = rope
    per_layer = [p['norm_w'], p['w_pack'], p['w_out'], p['bbd'].astype(BF16), p['cbd'],
                 p['a_re'], p['a_im'],
                 p['d_skip'], p['w_glu'], p['lbp'], p['hg_norm'], p['w_up'], p['b_gate'], p['gla_norm'],
                 p['ret_norm']]
    shared = [dstack, inner, kdec, cdec, consts['ones_h'], consts['ones_g'], consts['tril'],
              p['final_norm']]
    in_specs = [pl.BlockSpec((nb, T_CHUNK, D_MODEL), lambda i: (0, i, 0)),
                pl.BlockSpec((nb, T_CHUNK, D_MODEL), lambda i: (0, jnp.minimum(i + 1, n_steps - 1), 0))]
    in_specs += [_of_layer(a, layer) for a in per_layer]
    in_specs += [pl.BlockSpec((T_CHUNK, LANES), lambda i: (i, 0)), pl.BlockSpec((T_CHUNK, LANES), lambda i: (i, 0))]
    in_specs += [_full(a.shape) for a in shared]
    out_shape = (jax.ShapeDtypeStruct((nb, seq, D_MODEL), F32),
                 jax.ShapeDtypeStruct((nb, 2 * S5_N), F32),
                 jax.ShapeDtypeStruct((nb // 2, 2, BRANCH, HEADS * HG_DK), F32),
                 jax.ShapeDtypeStruct((nb // 2, 2, BRANCH, HEADS * GLA_DK), F32),
                 jax.ShapeDtypeStruct((nb // 2, 2, BRANCH, HEADS * RET_DK), F32))
    out_specs = (pl.BlockSpec((nb, T_CHUNK, D_MODEL), lambda i: (0, i, 0)),
                 pl.BlockSpec((nb, 2 * S5_N), lambda i: (0, 0)),
                 pl.BlockSpec((nb // 2, 2, BRANCH, HEADS * HG_DK), lambda i: (0, 0, 0, 0)),
                 pl.BlockSpec((nb // 2, 2, BRANCH, HEADS * GLA_DK), lambda i: (0, 0, 0, 0)),
                 pl.BlockSpec((nb // 2, 2, BRANCH, HEADS * RET_DK), lambda i: (0, 0, 0, 0)))
    scratch = [pltpu.VMEM((N_QUARTER, rows, QUARTER), F32), pltpu.VMEM((rows, LANES), F32),
               pltpu.VMEM((rows, BRANCH), F32),
               pltpu.VMEM((2 * S5_N // LANES, nb * PITCH, LANES), F32),
               pltpu.VMEM((rows, D_MODEL), F32),
               ] + GROUP_ROWS * (
                   [pltpu.VMEM((TERM_ROWS, kd), F32) for kd in (HEADS * HG_DK, HEADS * GLA_DK)]
                   + [pltpu.VMEM((w // LANES, T_CHUNK, LANES), F32) for kd in (HEADS * HG_DK, HEADS * GLA_DK)
                      for w in (kd, kd, BRANCH)])
    y, s5, hgs, glas, rets = pl.pallas_call(
        functools.partial(_prompt_kernel, apply_final),
        grid=(n_steps,), in_specs=in_specs, out_specs=out_specs, out_shape=out_shape,
        scratch_shapes=scratch, name='prompt_layer',
        compiler_params=pltpu.CompilerParams(dimension_semantics=('arbitrary',),
                                             vmem_limit_bytes=VMEM_LIMIT),
    )(x, x, *per_layer, cos_t, sin_t, *shared)
    s5 = s5.reshape(nb, 2, S5_GROUPS, S5_STATE)

    def unstack(st, dk):
        st = st.reshape(nb, HEADS, DV, HEADS, dk)
        diag = jnp.stack([st[:, hh, :, hh, :] for hh in range(HEADS)], axis=1)
        return jnp.transpose(diag, (0, 1, 3, 2))

    return y, (s5[:, 0], s5[:, 1], unstack(hgs, HG_DK), unstack(glas, GLA_DK), unstack(rets, RET_DK))


def _sample_step(x, states, p, consts, tabs, rope):
    nb = x.shape[0]
    depth = p['w_t'].shape[0]
    s5re, s5im, hg, gla, ret = states
    to_lanes = lambda s: jnp.moveaxis(s, 1, -1)
    s5re_t = to_lanes(s5re).reshape(depth, S5_N, nb)
    s5im_t = to_lanes(s5im).reshape(depth, S5_N, nb)
    hg_t, gla_t, ret_t = to_lanes(hg), to_lanes(gla), to_lanes(ret)
    cos_t, sin_t = rope
    col = lambda a: jnp.swapaxes(a, -1, -2)
    per_layer = [p['norm_w'], p['w_t'], p['w_out'], col(p['bbd']), p['cbd_t'], col(p['a_re']), col(p['a_im']),
                 col(p['d_skip']), col(p['w_glu']), col(p['lbp']), col(p['hg_norm']), col(p['w_up']),
                 col(p['b_gate']), col(p['gla_norm']), col(p['ret_norm'])]
    shared = [p['final_norm'], col(cos_t), col(sin_t), tabs[4], consts['ones_h']]
    layer_spec = lambda a: pl.BlockSpec((None,) + a.shape[1:], lambda l, h: (l,) + (0,) * (a.ndim - 1))
    head_spec = lambda a: pl.BlockSpec((None, None) + a.shape[2:], lambda l, h: (l, h) + (0,) * (a.ndim - 2))
    state_specs = [layer_spec(s5re_t), layer_spec(s5im_t), head_spec(hg_t), head_spec(gla_t), head_spec(ret_t)]
    in_specs = ([_full(x.shape)] + [layer_spec(a) for a in per_layer] + [_full(a.shape) for a in shared]
                + state_specs)
    state_arrays = [s5re_t, s5im_t, hg_t, gla_t, ret_t]
    out_shape = tuple([jax.ShapeDtypeStruct(x.shape, F32)]
                      + [jax.ShapeDtypeStruct(a.shape, F32) for a in state_arrays])
    out_specs = tuple([pl.BlockSpec(x.shape, lambda l, h: (0, 0))] + state_specs)
    scratch = [pltpu.VMEM((nb, D_MODEL), F32), pltpu.VMEM((N_PACK, nb), F32), pltpu.VMEM((BRANCH, nb), F32),
               pltpu.VMEM((3 * BRANCH, nb), F32), pltpu.VMEM((4 * BRANCH, nb), F32)]
    outs = pl.pallas_call(
        _sample_kernel, grid=(depth, HEADS), in_specs=in_specs, out_specs=out_specs, out_shape=out_shape,
        scratch_shapes=scratch, name='sample_step',
        compiler_params=pltpu.CompilerParams(dimension_semantics=('arbitrary', 'arbitrary'),
                                             vmem_limit_bytes=VMEM_LIMIT),
    )(x, *per_layer, *shared, *state_arrays)
    from_lanes = lambda s: jnp.moveaxis(s, -1, 1)
    new = (from_lanes(outs[1].reshape(depth, S5_GROUPS, S5_STATE, nb)),
           from_lanes(outs[2].reshape(depth, S5_GROUPS, S5_STATE, nb)),
           from_lanes(outs[3]), from_lanes(outs[4]), from_lanes(outs[5]))
    return outs[0], new


def kernel(x_prompt, x_sample, state_s5_re, state_s5_im, state_hgrn, state_gla, state_ret, norm_w, final_norm_w, w_in, w_out, s5_lam_re, s5_lam_im, s5_log_step, s5_b_re, s5_b_im, s5_c_re, s5_c_im, s5_d, s5_w_glu, hgrn_lb_logits, hgrn_norm_w, gla_w_gate_up, gla_b_gate, gla_norm_w, ret_norm_w):
    depth = w_in.shape[0]
    seq = x_prompt.shape[1]
    consts = _constants()
    tabs = _ret_tables()
    rope_p = _rope_tables(jnp.arange(seq))
    rope_s = _rope_tables(PAST_LEN + jnp.arange(1))
    lb = jnp.cumsum(jax.nn.softmax(hgrn_lb_logits.astype(F32), axis=0), axis=0)
    lb = (lb - lb[0:1])[:, None, :]
    bbd, cbd_t, a_re, a_im = _s5_discretize(s5_lam_re, s5_lam_im, s5_log_step, s5_b_re, s5_b_im, s5_c_re,
                                            s5_c_im, consts['s5_mask'])
    w_t = lax.optimization_barrier(_pack_w_in_t(w_in))
    row = lambda a: a[:, None, :].astype(F32)
    w_up = jnp.zeros((depth, LANES, HEADS * GLA_DK), F32).at[:, :GLA_LOWRANK].set(gla_w_gate_up.astype(F32))
    p = dict(norm_w=row(norm_w),
             w_pack=jnp.swapaxes(w_t, 1, 2), w_t=w_t, w_out=w_out.astype(BF16),
             bbd=bbd, cbd=jnp.swapaxes(cbd_t, 1, 2).astype(BF16), cbd_t=cbd_t, a_re=a_re, a_im=a_im,
             d_skip=row(s5_d), w_glu=s5_w_glu.astype(BF16),
             lbp=jnp.concatenate([jnp.log(lb), jnp.log1p(-lb), 1.0 - lb, jnp.zeros((depth, 5, BRANCH), F32)],
                                 axis=1),
             hg_norm=row(hgrn_norm_w), w_up=w_up, b_gate=row(gla_b_gate), gla_norm=row(gla_norm_w),
             ret_norm=row(ret_norm_w), final_norm=final_norm_w[None, :].astype(F32))

    xp = x_prompt
    new_p = ([], [], [], [], [])
    for l in range(depth):
        xp, st_p = _prompt_layer(xp, l, p, consts, tabs, rope_p, l == depth - 1)
        for i in range(5):
            new_p[i].append(st_p[i])
    xs, new_s = _sample_step(x_sample.reshape(x_sample.shape[0], D_MODEL),
                             (state_s5_re, state_s5_im, state_hgrn, state_gla, state_ret),
                             p, consts, tabs, rope_s)
    return (xp, xs.reshape(x_sample.shape),
            jnp.stack(new_p[0]), jnp.stack(new_p[1]), jnp.stack(new_p[2]), jnp.stack(new_p[3]),
            jnp.stack(new_p[4])) + new_s
```

```python
import functools
import math

import numpy as np
import jax
import jax.numpy as jnp
from jax import lax
from jax.experimental import pallas as pl
from jax.experimental.pallas import tpu as pltpu

F32 = jnp.float32
BF16 = jnp.bfloat16

D_MODEL = 1024
BRANCH = 256
S5_CH = 16
S5_GROUPS = 16
S5_STATE = 64
S5_N = S5_GROUPS * S5_STATE
HEADS = 4
HG_DK = 64
GLA_DK = 32
RET_DK = 32
DV = 64
GLA_LOWRANK = 16
GLA_TAU = 16.0
ROPE_BASE = 10000.0
PAST_LEN = 16384
EPS = 1e-6
SUB = 16
PAIR = 2 * SUB
GROUP_ROWS = 4

LANES = 128
T_CHUNK = 64
PITCH = T_CHUNK + 8
SUB_TERM_ROWS = SUB * (SUB + SUB // 2) // 2
TERM_ROWS = (T_CHUNK // SUB) * SUB_TERM_ROWS
LOG2E = math.log2(math.e)
MASK_NEG = -1e30
VMEM_LIMIT = 60 * 1024 * 1024

C_U, C_SZ, C_HQ, C_HF, C_HI, C_HZ = 0, 256, 512, 768, 1024, 1280
C_GQ, C_GK, C_GV, C_GZ = 1536, 1664, 1792, 2048
C_RQ, C_RK, C_RV, C_RZ = 2304, 2432, 2560, 2816
C_LR = 3072
N_PACK = 3200
N_QUARTER = 4
QUARTER = C_LR // N_QUARTER


def _dot(a, b):
    return jnp.dot(a.astype(BF16), b.astype(BF16), preferred_element_type=F32)


def _dot_nt(a, b):
    return lax.dot_general(a.astype(BF16), b.astype(BF16), (((1,), (1,)), ((), ())),
                           preferred_element_type=F32)


def _dot_tn(a, b):
    return lax.dot_general(a.astype(BF16), b.astype(BF16), (((0,), (0,)), ((), ())),
                           preferred_element_type=F32)


def _split3(x):
    x1 = x.astype(BF16)
    r1 = x - x1.astype(F32)
    x2 = r1.astype(BF16)
    x3 = (r1 - x2.astype(F32)).astype(BF16)
    return x1, x2, x3


def _dot_sel_lhs(sel, x):
    x1, x2, x3 = _split3(x)
    d = lambda p: jnp.dot(sel, p, preferred_element_type=F32)
    return d(x1) + d(x2) + d(x3)


def _dot_sel_rhs2(x, sel):
    x1 = x.astype(BF16)
    x2 = (x - x1.astype(F32)).astype(BF16)
    return (jnp.dot(x1, sel, preferred_element_type=F32)
            + jnp.dot(x2, sel, preferred_element_type=F32))


def _dot3(a, b):
    a1 = a.astype(BF16)
    a2 = (a - a1.astype(F32)).astype(BF16)
    b1 = b.astype(BF16)
    b2 = (b - b1.astype(F32)).astype(BF16)
    d = lambda p, q: jnp.dot(p, q, preferred_element_type=F32)
    return d(a1, b1) + d(a1, b2) + d(a2, b1)


def _sigmoid(x):
    return 0.5 * jnp.tanh(0.5 * x) + 0.5


def _silu(x):
    return x * _sigmoid(x)


def _log_sigmoid(x):
    return jnp.minimum(x, 0.0) - jnp.log(1.0 + jnp.exp(-jnp.abs(x)))


def _gelu_tanh(x):
    return 0.5 * x * (1.0 + jnp.tanh(math.sqrt(2.0 / math.pi) * (x + 0.044715 * (x * x * x))))


def _rmsnorm_rows(x, w):
    return x * lax.rsqrt(jnp.mean(x * x, axis=-1, keepdims=True) + EPS) * w


def _head_rms(o, ones_h, gain):
    ms = _dot_sel_rhs2(o * o, ones_h) * (1.0 / DV)
    return o * lax.rsqrt(ms + EPS) * gain


def _head_ln(o, ones_h, gain):
    c = o - _dot_sel_rhs2(o, ones_h) * (1.0 / DV)
    var = _dot_sel_rhs2(c * c, ones_h) * (1.0 / DV)
    return c * lax.rsqrt(var + EPS) * gain


def _hgrn_gates(xf, loglb, log1mlb, one_m_lb):
    bterm = log1mlb + _log_sigmoid(xf)
    m = jnp.maximum(loglb, bterm)
    log_f = m + jnp.log(jnp.exp(loglb - m) + jnp.exp(bterm - m))
    return log_f, one_m_lb * _sigmoid(-xf)


def _rotary(t, cos, sin_signed, first_half):
    half = RET_DK // 2
    swapped = jnp.where(first_half, pltpu.roll(t, LANES - half, 1), pltpu.roll(t, half, 1))
    return t * cos + swapped * sin_signed


def _s5_output(y_lin, u, sz, dskip, wglu):
    y = _gelu_tanh(y_lin + u * dskip)
    y = y * _sigmoid(_dot(y, wglu))
    return y * _silu(sz)


def _mix_and_project(x, mix, wout, fnw, apply_final):
    out = x + _dot(mix, wout)
    if apply_final:
        out = _rmsnorm_rows(out, fnw)
    return out


class _Slabs:
    def __init__(self, ref, base=0):
        self.ref, self.base = ref, base

    def keep(self, value):
        for c in range(self.ref.shape[0]):
            self.ref[c, pl.ds(self.base, value.shape[0]), :] = value[:, c * LANES:(c + 1) * LANES]

    def tile(self, r0, n):
        return jnp.concatenate([self.ref[c, pl.ds(self.base + r0, n), :] for c in range(self.ref.shape[0])],
                               axis=1)

    def row(self, r, n):
        return jnp.concatenate([self.ref[c, pl.ds(self.base + r, n, stride=0), :]
                                for c in range(self.ref.shape[0])], axis=1)


class _GatedStream:
    def __init__(self, q, k, v, g, s_ref, ones_kv, mask_vk, tril, term_scr, row_scr, kdim):
        self.q, self.k, self.v, self.g = q, k, v, g
        self.bc, self.kc, self.vc = (_Slabs(r) for r in row_scr)
        self.s_ref, self.ones_kv, self.mask_vk = s_ref, ones_kv, mask_vk
        self.tril, self.term_scr, self.kdim = tril, term_scr, kdim
        self.ws, self.inter, self.att, self.cross = {}, {}, {}, {}

    def cumulate(self):
        self.bc.keep(_dot_sel_lhs(self.tril, self.g()) * LOG2E)
        self.kc.keep(self.k())
        self.vc.keep(self.v())

    def _total(self, p):
        return self.bc.tile(PAIR * (p + 1) - 1, 1)

    def update_part(self, p):
        sl = slice(PAIR * p, PAIR * (p + 1))
        self.ws[p] = _dot_tn(self.v()[sl],
                             self.k()[sl] * jnp.exp2(self._total(p) - self.bc.tile(PAIR * p, PAIR)))

    def cross_part(self, p):
        r_a, r_b = PAIR * p, PAIR * p + SUB
        edge = self.bc.row(r_b - 1, SUB)
        q_b = self.q()[r_b:r_b + SUB] * jnp.exp2(self.bc.tile(r_b, SUB) - edge)
        k_a = self.kc.tile(r_a, SUB) * jnp.exp2(edge - self.bc.tile(r_a, SUB))
        lane = lax.broadcasted_iota(jnp.int32, (SUB, self.kdim), 1) // (self.kdim // HEADS)
        q_heads = jnp.concatenate([jnp.where(lane == h, q_b, 0.0) for h in range(HEADS)], axis=0)
        scores = _dot_nt(q_heads, k_a)
        res = _dot(scores, self.vc.tile(r_a, SUB))
        lane_v = lax.broadcasted_iota(jnp.int32, (SUB, BRANCH), 1) // DV
        out = jnp.where(lane_v == 0, res[0:SUB], 0.0)
        for h in range(1, HEADS):
            out = out + jnp.where(lane_v == h, res[SUB * h:SUB * (h + 1)], 0.0)
        self.cross[p] = out

    def products_part(self, j):
        hs, term_scr, r0 = SUB // 2, self.term_scr, SUB * j
        q, bc = self.q()[r0:r0 + SUB], self.bc.tile(r0, SUB)
        trow = lax.broadcasted_iota(jnp.int32, (hs, self.kdim), 0)
        neg = [jnp.where(trow >= d, 0.0, MASK_NEG) for d in range(1, hs)]
        q_lo, q_hi, b_lo, b_hi = q[0:hs], q[hs:SUB], bc[0:hs], bc[hs:SUB]
        row = j * SUB_TERM_ROWS
        for s in range(SUB):
            ks, bs = self.kc.row(r0 + s, hs), self.bc.row(r0 + s, hs)
            d = s % hs
            q_dg, b_dg = (q_lo, b_lo) if s < hs else (q_hi, b_hi)
            e_dg = b_dg - bs if d == 0 else b_dg - bs + neg[d - 1]
            term_scr[row:row + hs, :] = q_dg * ks * jnp.exp2(e_dg)
            row += hs
            if s < hs:
                term_scr[row:row + hs, :] = q_hi * ks * jnp.exp2(b_hi - bs)
                row += hs

    def matmul_part(self, p):
        sl = slice(PAIR * p, PAIR * (p + 1))
        if p == 0:
            self.state = self.s_ref[...]
        self.inter[p] = _dot_nt(self.q()[sl] * jnp.exp2(self.bc.tile(PAIR * p, PAIR)), self.state)
        self.state = self.state * jnp.exp2(self._total(p)) + self.ws[p] * self.mask_vk
        if p == T_CHUNK // PAIR - 1:
            self.s_ref[...] = self.state
        rows = slice(2 * p * SUB_TERM_ROWS, 2 * (p + 1) * SUB_TERM_ROWS)
        self.att[p] = _dot(self.term_scr[rows, :], self.ones_kv)

    def output_part(self, j):
        hs, r0, p, second = SUB // 2, SUB * j, j // 2, j % 2
        att = self.att[p][second * SUB_TERM_ROWS:(second + 1) * SUB_TERM_ROWS]
        start = self.inter[p][second * SUB:(second + 1) * SUB]
        if second:
            start = start + self.cross[p]
        acc_lo, acc_hi = start[0:hs], start[hs:SUB]
        row = 0
        for s in range(SUB):
            vs = self.vc.row(r0 + s, hs)
            if s < hs:
                acc_lo = acc_lo + att[row:row + hs] * vs
                row += hs
            acc_hi = acc_hi + att[row:row + hs] * vs
            row += hs
        return jnp.concatenate([acc_lo, acc_hi], axis=0)


def _ret_chunk(q, k, v, cos, sin_signed, s_ref, dstack, inner, kdec, cdec, mask_vk):
    lane = lax.broadcasted_iota(jnp.int32, (T_CHUNK, HEADS * RET_DK), 1)
    first_half = (lane % RET_DK) < (RET_DK // 2)
    rq = _rotary(q, cos, sin_signed, first_half)
    rk = _rotary(k, cos, sin_signed, first_half) * (RET_DK ** -0.5)
    lane_v = lax.broadcasted_iota(jnp.int32, (T_CHUNK, HEADS * DV), 1)
    kst = jnp.concatenate([jnp.where(lane // RET_DK == h, rk, 0.0) for h in range(HEADS)], axis=0)
    vst = jnp.concatenate([jnp.where(lane_v // DV == h, v, 0.0) for h in range(HEADS)], axis=0)
    p = _dot_nt(rq, kst) * dstack
    s_t = s_ref[...]
    o = _dot(p, vst) + _dot_nt(rq, s_t) * inner
    s_ref[...] = s_t * cdec + _dot_tn(v, rk * kdec) * mask_vk
    return o


def _prompt_kernel(apply_final,
                   x_ref, xn_ref, normw_ref, win_ref, wout_ref, bbd_ref, cbd_ref, are_ref, aim_ref,
                   dskip_ref, wglu_ref, lbp_ref, hgn_ref, wup_ref, bgate_ref, glan_ref, retn_ref, cos_ref,
                   sin_ref, dstack_ref, inner_ref, kdec_ref, cdec_ref, onesh_ref, onesg_ref, tril_ref,
                   fnw_ref,
                   y_ref, s5_ref, hgc_ref, glac_ref, retc_ref,
                   hgs_ref, glas_ref, rets_ref, proj_scr, lr_scr, hk_scr, bu_scr, mix_scr, *stream_scratch):
    nb = x_ref.shape[0]
    rows = nb * T_CHUNK
    n_slab = 2 * S5_N // LANES
    half = n_slab // 2
    step = pl.program_id(0)

    def w_quarter(qtr):
        return win_ref[:, qtr * QUARTER:(qtr + 1) * QUARTER]

    def proj(col, width, rs=slice(None)):
        return proj_scr[col // QUARTER, rs, col % QUARTER:col % QUARTER + width]

    @pl.when(step == 0)
    def _init():
        s5_ref[...] = jnp.zeros_like(s5_ref)
        hgs_ref[...] = jnp.zeros_like(hgs_ref)
        glas_ref[...] = jnp.zeros_like(glas_ref)
        rets_ref[...] = jnp.zeros_like(rets_ref)
        bu_scr[...] = jnp.zeros_like(bu_scr)
        h0 = _rmsnorm_rows(x_ref[...].reshape(rows, D_MODEL), normw_ref[...]).astype(BF16)
        for qtr in range(N_QUARTER):
            proj_scr[qtr] = jnp.dot(h0, w_quarter(qtr), preferred_element_type=F32)
        lr_scr[...] = jnp.dot(h0, win_ref[:, C_LR:N_PACK], preferred_element_type=F32)

    u16 = proj(C_U, BRANCH).astype(BF16)

    def drive(c2):
        bu = jnp.dot(u16, bbd_ref[:, 2 * c2 * LANES:2 * (c2 + 1) * LANES], preferred_element_type=F32)
        for cc in range(2):
            for b in range(nb):
                bu_scr[2 * c2 + cc, b * PITCH:b * PITCH + T_CHUNK, :] = bu[b * T_CHUNK:(b + 1) * T_CHUNK,
                                                                           cc * LANES:(cc + 1) * LANES]

    lbp = lbp_ref[...]
    log_f, hk = _hgrn_gates(proj(C_HF, BRANCH), lbp[0:1, :], lbp[1:2, :], lbp[2:3, :])
    proj_scr[C_HF // QUARTER, :, C_HF % QUARTER:C_HF % QUARTER + BRANCH] = log_f
    hk_scr[...] = hk
    g_gla = _log_sigmoid(_dot3(lr_scr[...], wup_ref[...]) + bgate_ref[...])
    lr_scr[...] = g_gla * (1.0 / GLA_TAU)
    for c2 in range(n_slab // 2):
        drive(c2)
    a_re = [jnp.broadcast_to(are_ref[:, c * LANES:(c + 1) * LANES], (nb, LANES)) for c in range(half)]
    a_im = [jnp.broadcast_to(aim_ref[:, c * LANES:(c + 1) * LANES], (nb, LANES)) for c in range(half)]
    s_init = s5_ref[...]
    carry0 = tuple(s_init[:, c * LANES:(c + 1) * LANES] for c in range(n_slab))

    def scan_step(t, carry):
        new = [None] * n_slab
        for c in range(half):
            sr, si = carry[c], carry[half + c]
            br = bu_scr[c, pl.ds(t, nb, stride=PITCH), :]
            bi = bu_scr[half + c, pl.ds(t, nb, stride=PITCH), :]
            nr = a_re[c] * sr - a_im[c] * si + br
            ni = a_re[c] * si + a_im[c] * sr + bi
            bu_scr[c, pl.ds(t, nb, stride=PITCH), :] = nr
            bu_scr[half + c, pl.ds(t, nb, stride=PITCH), :] = ni
            new[c], new[half + c] = nr, ni
        return tuple(new)

    carry = lax.fori_loop(0, T_CHUNK, scan_step, carry0)
    s5_ref[...] = jnp.concatenate(carry, axis=1)
    s_all = jnp.concatenate([bu_scr[c] for c in range(n_slab)], axis=1)
    y_all = _dot(s_all, cbd_ref[...])
    y_lin = jnp.concatenate([y_all[b * PITCH:b * PITCH + T_CHUNK] for b in range(nb)], axis=0)
    mix_scr[:, 0:BRANCH] = _s5_output(y_lin, proj(C_U, BRANCH), proj(C_SZ, BRANCH), dskip_ref[...],
                                      wglu_ref[...])

    ones_h = onesh_ref[...]
    ones_g = onesg_ref[...]
    mask_h = ones_h.astype(F32)
    mask_g = jnp.transpose(ones_g.astype(F32))
    tril = tril_ref[...]

    def rows_of(bp, odd):
        return pl.ds(pl.multiple_of((2 * bp + odd) * T_CHUNK, T_CHUNK), T_CHUNK)

    def hgrn_stream(bp, odd, term, row_scr):
        rs = rows_of(bp, odd)
        return _GatedStream(lambda: proj(C_HQ, BRANCH, rs), lambda: hk_scr[rs, :], lambda: proj(C_HI, BRANCH, rs),
                            lambda: proj(C_HF, BRANCH, rs), hgs_ref.at[bp, odd], ones_h, mask_h, tril, term,
                            row_scr, HEADS * HG_DK)

    def gla_stream(bp, odd, term, row_scr):
        rs = rows_of(bp, odd)
        return _GatedStream(lambda: proj(C_GQ, LANES, rs) * (GLA_DK ** -0.5), lambda: proj(C_GK, LANES, rs),
                            lambda: proj(C_GV, BRANCH, rs), lambda: lr_scr[rs, :],
                            glas_ref.at[bp, odd], ones_g, mask_g, tril, term, row_scr, HEADS * GLA_DK)

    n_sub = T_CHUNK // SUB
    def retention(bp, odd):
        rs = rows_of(bp, odd)
        mix_scr[rs, 3 * BRANCH:4 * BRANCH] = _ret_chunk(
            proj(C_RQ, LANES, rs), proj(C_RK, LANES, rs), proj(C_RV, BRANCH, rs), cos_ref[...],
            sin_ref[...], rets_ref.at[bp, odd], dstack_ref[...], inner_ref[...], kdec_ref[...],
            cdec_ref[...], mask_g)

    def per_group(grp, _):
        st, slots = [], []
        for i in range(GROUP_ROWS):
            bp, odd = (GROUP_ROWS // 2) * grp + i // 2, i % 2
            scr = stream_scratch[8 * i:8 * (i + 1)]
            st += [hgrn_stream(bp, odd, scr[0], scr[2:5]), gla_stream(bp, odd, scr[1], scr[5:8])]
            slots += [(bp, odd, BRANCH), (bp, odd, 2 * BRANCH)]

        def emit(i, j):
            bp, odd, col = slots[i]
            r0 = pl.multiple_of((2 * bp + odd) * T_CHUNK + SUB * j, SUB)
            mix_scr[pl.ds(r0, SUB), col:col + BRANCH] = st[i].output_part(j)

        for s_ in st:
            s_.cumulate()
        for i in range(GROUP_ROWS):
            retention((GROUP_ROWS // 2) * grp + i // 2, i % 2)
        for s_ in st:
            for p in range(n_sub // 2):
                s_.update_part(p)
        for s_ in st:
            for j in range(n_sub):
                s_.products_part(j)
        for s_ in st:
            for p in range(n_sub // 2):
                s_.cross_part(p)
        for s_ in st:
            for p in range(n_sub // 2):
                s_.matmul_part(p)
        for i in range(len(st)):
            for j in range(n_sub):
                emit(i, j)
        return 0

    lax.fori_loop(0, nb // GROUP_ROWS, per_group, 0)

    o_hg = _head_rms(mix_scr[:, BRANCH:2 * BRANCH], ones_h, hgn_ref[...])
    mix_scr[:, BRANCH:2 * BRANCH] = o_hg * _silu(proj(C_HZ, BRANCH))
    o_gla = _head_rms(mix_scr[:, 2 * BRANCH:3 * BRANCH], ones_h, glan_ref[...])
    mix_scr[:, 2 * BRANCH:3 * BRANCH] = o_gla * _silu(proj(C_GZ, BRANCH))
    o_ret = _head_ln(mix_scr[:, 3 * BRANCH:4 * BRANCH], ones_h, retn_ref[...])
    mix_scr[:, 3 * BRANCH:4 * BRANCH] = o_ret * _silu(proj(C_RZ, BRANCH))
    hn = _rmsnorm_rows(xn_ref[...].reshape(rows, D_MODEL), normw_ref[...]).astype(BF16)
    for qtr in range(N_QUARTER):
        proj_scr[qtr] = jnp.dot(hn, w_quarter(qtr), preferred_element_type=F32)
    lr_scr[...] = jnp.dot(hn, win_ref[:, C_LR:N_PACK], preferred_element_type=F32)
    out = _mix_and_project(x_ref[...].reshape(rows, D_MODEL), mix_scr[...], wout_ref[...], fnw_ref[...],
                           apply_final)
    y_ref[...] = out.reshape(nb, T_CHUNK, D_MODEL)

    @pl.when(step == pl.num_programs(0) - 1)
    def _emit_states():
        for b in range(nb):
            for state, compact in ((hgs_ref, hgc_ref), (glas_ref, glac_ref), (rets_ref, retc_ref)):
                dk = compact.shape[3]
                for h in range(HEADS):
                    compact[b, h] = state[b // 2, b % 2, h * DV:(h + 1) * DV, h * dk:(h + 1) * dk]


def _dot_sel_lhs2(sel, x):
    x1 = x.astype(BF16)
    x2 = (x - x1.astype(F32)).astype(BF16)
    return (jnp.dot(sel, x1, preferred_element_type=F32) + jnp.dot(sel, x2, preferred_element_type=F32))


def _head_rms_t(o, ones_h, gain):
    ms = _dot_sel_lhs2(ones_h, o * o) * (1.0 / DV)
    return o * lax.rsqrt(ms + EPS) * gain


def _head_ln_t(o, ones_h, gain):
    c = o - _dot_sel_lhs2(ones_h, o) * (1.0 / DV)
    var = _dot_sel_lhs2(ones_h, c * c) * (1.0 / DV)
    return c * lax.rsqrt(var + EPS) * gain


def _rotary_t(t, cos, sin_signed, first_half):
    half = RET_DK // 2
    swapped = jnp.where(first_half, pltpu.roll(t, LANES - half, 0), pltpu.roll(t, half, 0))
    return t * cos + swapped * sin_signed


def _sample_kernel(x_ref, normw_ref, wt_ref, wout_ref, bbdt_ref, cbdt_ref, are_ref, aim_ref, dskip_ref,
                   wglut_ref, lbp_ref, hgn_ref, wupt_ref, bgate_ref, glan_ref, retn_ref, fnw_ref,
                   cos_ref, sin_ref, dret_ref, onesh_ref,
                   s5re_ref, s5im_ref, hg_ref, gla_ref, ret_ref,
                   y_ref, s5re_o, s5im_o, hg_o, gla_o, ret_o,
                   xs_scr, pt_scr, hk_scr, ot_scr, mixt_scr):
    layer, head = pl.program_id(0), pl.program_id(1)
    last_layer, last_head = pl.num_programs(0) - 1, pl.num_programs(1) - 1

    @pl.when((layer == 0) & (head == 0))
    def _load_x():
        xs_scr[...] = x_ref[...]

    @pl.when(head == 0)
    def _dense():
        hh = _rmsnorm_rows(xs_scr[...], normw_ref[...]).astype(BF16)
        pt_scr[...] = lax.dot_general(wt_ref[...], hh, (((1,), (1,)), ((), ())), preferred_element_type=F32)

        u = pt_scr[C_U:C_U + BRANCH, :]
        bu = _dot3(bbdt_ref[...], u)
        a_re, a_im = are_ref[...], aim_ref[...]
        s0r, s0i = s5re_ref[...], s5im_ref[...]
        s_re = a_re * s0r - a_im * s0i + bu[0:S5_N]
        s_im = a_re * s0i + a_im * s0r + bu[S5_N:2 * S5_N]
        s5re_o[...] = s_re
        s5im_o[...] = s_im
        y = _gelu_tanh(_dot3(cbdt_ref[...], jnp.concatenate([s_re, s_im], axis=0)) + u * dskip_ref[...])
        y = y * _sigmoid(jnp.dot(wglut_ref[...], y.astype(BF16), preferred_element_type=F32))
        mixt_scr[0:BRANCH, :] = y * _silu(pt_scr[C_SZ:C_SZ + BRANCH, :])

        lbp = lbp_ref[...]
        log_f, hk = _hgrn_gates(pt_scr[C_HF:C_HF + BRANCH, :], lbp[:, 0:1], lbp[:, 1:2], lbp[:, 2:3])
        pt_scr[C_HF:C_HF + BRANCH, :] = jnp.exp(log_f)
        hk_scr[...] = hk
        g_gla = _log_sigmoid(_dot3(wupt_ref[...], pt_scr[C_LR:C_LR + LANES, :]) + bgate_ref[...])
        pt_scr[C_LR:C_LR + LANES, :] = jnp.exp(g_gla * (1.0 / GLA_TAU))
        pt_scr[C_GQ:C_GQ + LANES, :] = pt_scr[C_GQ:C_GQ + LANES, :] * (GLA_DK ** -0.5)
        row = lax.broadcasted_iota(jnp.int32, (HEADS * RET_DK, LANES), 0)
        first_half = (row % RET_DK) < (RET_DK // 2)
        pt_scr[C_RQ:C_RQ + LANES, :] = _rotary_t(pt_scr[C_RQ:C_RQ + LANES, :], cos_ref[...], sin_ref[...],
                                                 first_half)
        pt_scr[C_RK:C_RK + LANES, :] = _rotary_t(pt_scr[C_RK:C_RK + LANES, :], cos_ref[...], sin_ref[...],
                                                 first_half) * (RET_DK ** -0.5)

    def head_update(s0_ref, s_out_ref, dk, dec_ref, dec_row, key_ref, key_row, q_row, v_row, out_row):
        vt = pt_scr[pl.ds(pl.multiple_of(v_row + head * DV, DV), DV), :]

        def feature(kk, acc):
            r = head * dk + kk
            bcast = lambda ref, r0: jnp.broadcast_to(ref[pl.ds(r0 + r, 1), :], (DV, LANES))
            s_new = s0_ref[kk] * bcast(dec_ref, dec_row) + bcast(key_ref, key_row) * vt
            s_out_ref[kk] = s_new
            return acc + bcast(pt_scr, q_row) * s_new

        acc = lax.fori_loop(0, dk, feature, jnp.zeros((DV, LANES), F32), unroll=4)
        ot_scr[pl.ds(pl.multiple_of(out_row + head * DV, DV), DV), :] = acc

    head_update(hg_ref, hg_o, HG_DK, pt_scr, C_HF, hk_scr, 0, C_HQ, C_HI, 0)
    head_update(gla_ref, gla_o, GLA_DK, pt_scr, C_LR, pt_scr, C_GK, C_GQ, C_GV, BRANCH)
    head_update(ret_ref, ret_o, RET_DK, dret_ref, 0, pt_scr, C_RK, C_RQ, C_RV, 2 * BRANCH)

    @pl.when(head == last_head)
    def _finish():
        ones_h = onesh_ref[...]
        o_hg = _head_rms_t(ot_scr[0:BRANCH, :], ones_h, hgn_ref[...])
        mixt_scr[BRANCH:2 * BRANCH, :] = o_hg * _silu(pt_scr[C_HZ:C_HZ + BRANCH, :])
        o_gla = _head_rms_t(ot_scr[BRANCH:2 * BRANCH, :], ones_h, glan_ref[...])
        mixt_scr[2 * BRANCH:3 * BRANCH, :] = o_gla * _silu(pt_scr[C_GZ:C_GZ + BRANCH, :])
        o_ret = _head_ln_t(ot_scr[2 * BRANCH:3 * BRANCH, :], ones_h, retn_ref[...])
        mixt_scr[3 * BRANCH:4 * BRANCH, :] = o_ret * _silu(pt_scr[C_RZ:C_RZ + BRANCH, :])
        out = xs_scr[...] + lax.dot_general(mixt_scr[...].astype(BF16), wout_ref[...], (((0,), (0,)), ((), ())),
                                            preferred_element_type=F32)
        xs_scr[...] = out

        @pl.when(layer == last_layer)
        def _emit():
            y_ref[...] = _rmsnorm_rows(out, fnw_ref[...])


def _ret_log_gamma():
    return jnp.log1p(-jnp.exp2(-5.0 - jnp.arange(HEADS, dtype=F32)))


def _constants():
    ones_h = (np.arange(BRANCH)[:, None] // DV == np.arange(BRANCH)[None, :] // DV)
    ones_g = (np.arange(HEADS * GLA_DK)[:, None] // GLA_DK == np.arange(BRANCH)[None, :] // DV)
    r = np.arange(T_CHUNK)
    tril = (r[:, None] // PAIR == r[None, :] // PAIR) & (r[None, :] <= r[:, None])
    same_group_b = np.arange(BRANCH)[:, None] // S5_CH == np.arange(S5_N)[None, :] // S5_STATE
    as_bf16 = lambda m: jnp.asarray(m.astype(np.float32), dtype=BF16)
    return dict(ones_h=as_bf16(ones_h), ones_g=as_bf16(ones_g), tril=as_bf16(tril),
                s5_mask=jnp.asarray(same_group_b.astype(np.float32)))


def _ret_tables():
    lg = _ret_log_gamma()
    idx = jnp.arange(T_CHUNK, dtype=F32)
    rel = idx[:, None] - idx[None, :]
    causal = rel >= 0
    decay = jnp.where(causal[None], jnp.exp(jnp.where(causal, rel, 0.0)[None] * lg[:, None, None]), 0.0)
    dstack = jnp.transpose(decay, (1, 0, 2)).reshape(T_CHUNK, HEADS * T_CHUNK)
    inner = jnp.repeat(jnp.exp((idx[:, None] + 1.0) * lg[None, :]), DV, axis=1)
    kdec = jnp.repeat(jnp.exp((T_CHUNK - 1.0 - idx[:, None]) * lg[None, :]), RET_DK, axis=1)
    cdec = jnp.repeat(jnp.exp(T_CHUNK * lg)[None, :], RET_DK, axis=1)
    dret = jnp.broadcast_to(jnp.repeat(jnp.exp(lg), RET_DK)[:, None], (HEADS * RET_DK, LANES))
    return dstack, inner, kdec, cdec, dret


def _rope_tables(pos):
    half = RET_DK // 2
    inv = ROPE_BASE ** (-jnp.arange(half, dtype=F32) / half)
    ang = pos.astype(F32)[:, None] * inv[None, :]
    cos, sin = jnp.cos(ang), jnp.sin(ang)
    cos_t = jnp.tile(jnp.concatenate([cos, cos], axis=1), (1, HEADS))
    sin_t = jnp.tile(jnp.concatenate([-sin, sin], axis=1), (1, HEADS))
    return cos_t, sin_t


def _pack_w_in_t(w):
    wt = jnp.swapaxes(w, 1, 2)
    offs = np.cumsum([0, 256, 256, 256, 256, 256, 256, 128, 128, 256, 16, 256, 128, 128, 256, 256])
    seg = lambda i: wt[:, int(offs[i]):int(offs[i + 1]), :]
    order = [0, 1, 2, 3, 4, 5, 6, 7, 8, 10, 11, 12, 13, 14]
    pad = jnp.zeros((w.shape[0], LANES - GLA_LOWRANK, w.shape[1]), w.dtype)
    return jnp.concatenate([seg(i) for i in order] + [seg(9), pad], axis=1).astype(BF16)


def _s5_discretize(lam_re, lam_im, log_step, b_re, b_im, c_re, c_im, mask):
    lr, li = lam_re.astype(F32), lam_im.astype(F32)
    step = jnp.exp(log_step.astype(F32))[..., None]
    mag = jnp.exp(lr * step)
    ab_re = mag * jnp.cos(li * step)
    ab_im = mag * jnp.sin(li * step)
    den = lr * lr + li * li
    nr = ab_re - 1.0
    f_re = (nr * lr + ab_im * li) / den
    f_im = (ab_im * lr - nr * li) / den
    br, bi = b_re.astype(F32), b_im.astype(F32)
    bb_re = f_re[..., None] * br - f_im[..., None] * bi
    bb_im = f_re[..., None] * bi + f_im[..., None] * br
    nl = lr.shape[0]

    def drive(bb):
        rows = jnp.transpose(bb, (0, 1, 3, 2)).reshape(nl, BRANCH, S5_STATE)
        return jnp.tile(rows, (1, 1, S5_GROUPS)) * mask

    def readout(cc):
        rows = cc.astype(F32).reshape(nl, BRANCH, S5_STATE)
        return jnp.tile(rows, (1, 1, S5_GROUPS)) * mask

    bbd = jnp.concatenate([drive(bb_re), drive(bb_im)], axis=2)
    cbd_t = jnp.concatenate([readout(c_re), -readout(c_im)], axis=2)
    return bbd, cbd_t, ab_re.reshape(nl, 1, S5_N), ab_im.reshape(nl, 1, S5_N)


def _full(shape):
    return pl.BlockSpec(shape, lambda *_: (0,) * len(shape), pipeline_mode=pl.Buffered(1))


def _of_layer(arr, layer):
    nd = arr.ndim - 1
    return pl.BlockSpec((None,) + arr.shape[1:], lambda *_: (layer,) + (0,) * nd,
                        pipeline_mode=pl.Buffered(1))


def _prompt_layer(x, layer, p, consts, tabs, rope, apply_final):
    nb, seq, _ = x.shape
    n_steps = seq // T_CHUNK
    rows = nb * T_CHUNK
    dstack, inner, kdec, cdec, _ = tabs
    cos_t, sin_t = rope
    per_layer = [p['norm_w'], p['w_pack'], p['w_out'], p['bbd'].astype(BF16), p['cbd'],
                 p['a_re'], p['a_im'],
                 p['d_skip'], p['w_glu'], p['lbp'], p['hg_norm'], p['w_up'], p['b_gate'], p['gla_norm'],
                 p['ret_norm']]
    shared = [dstack, inner, kdec, cdec, consts['ones_h'], consts['ones_g'], consts['tril'],
              p['final_norm']]
    in_specs = [pl.BlockSpec((nb, T_CHUNK, D_MODEL), lambda i: (0, i, 0)),
                pl.BlockSpec((nb, T_CHUNK, D_MODEL), lambda i: (0, jnp.minimum(i + 1, n_steps - 1), 0))]
    in_specs += [_of_layer(a, layer) for a in per_layer]
    in_specs += [pl.BlockSpec((T_CHUNK, LANES), lambda i: (i, 0)), pl.BlockSpec((T_CHUNK, LANES), lambda i: (i, 0))]
    in_specs += [_full(a.shape) for a in shared]
    out_shape = (jax.ShapeDtypeStruct((nb, seq, D_MODEL), F32),
                 jax.ShapeDtypeStruct((nb, 2 * S5_N), F32),
                 jax.ShapeDtypeStruct((nb, HEADS, DV, HG_DK), F32),
                 jax.ShapeDtypeStruct((nb, HEADS, DV, GLA_DK), F32),
                 jax.ShapeDtypeStruct((nb, HEADS, DV, RET_DK), F32))
    out_specs = (pl.BlockSpec((nb, T_CHUNK, D_MODEL), lambda i: (0, i, 0)),
                 pl.BlockSpec((nb, 2 * S5_N), lambda i: (0, 0)),
                 pl.BlockSpec((nb, HEADS, DV, HG_DK), lambda i: (0, 0, 0, 0)),
                 pl.BlockSpec((nb, HEADS, DV, GLA_DK), lambda i: (0, 0, 0, 0)),
                 pl.BlockSpec((nb, HEADS, DV, RET_DK), lambda i: (0, 0, 0, 0)))
    scratch = [pltpu.VMEM((nb // 2, 2, BRANCH, HEADS * HG_DK), F32),
               pltpu.VMEM((nb // 2, 2, BRANCH, HEADS * GLA_DK), F32),
               pltpu.VMEM((nb // 2, 2, BRANCH, HEADS * RET_DK), F32),
               pltpu.VMEM((N_QUARTER, rows, QUARTER), F32), pltpu.VMEM((rows, LANES), F32),
               pltpu.VMEM((rows, BRANCH), F32),
               pltpu.VMEM((2 * S5_N // LANES, nb * PITCH, LANES), F32),
               pltpu.VMEM((rows, D_MODEL), F32),
               ] + GROUP_ROWS * (
                   [pltpu.VMEM((TERM_ROWS, kd), F32) for kd in (HEADS * HG_DK, HEADS * GLA_DK)]
                   + [pltpu.VMEM((w // LANES, T_CHUNK, LANES), F32) for kd in (HEADS * HG_DK, HEADS * GLA_DK)
                      for w in (kd, kd, BRANCH)])
    y, s5, hgs, glas, rets = pl.pallas_call(
        functools.partial(_prompt_kernel, apply_final),
        grid=(n_steps,), in_specs=in_specs, out_specs=out_specs, out_shape=out_shape,
        scratch_shapes=scratch, name='prompt_layer',
        compiler_params=pltpu.CompilerParams(dimension_semantics=('arbitrary',),
                                             vmem_limit_bytes=VMEM_LIMIT),
    )(x, x, *per_layer, cos_t, sin_t, *shared)
    s5 = s5.reshape(nb, 2, S5_GROUPS, S5_STATE)

    to_kv = lambda st: jnp.swapaxes(st, 2, 3)
    return y, (s5[:, 0], s5[:, 1], to_kv(hgs), to_kv(glas), to_kv(rets))


def _sample_step(x, states, p, consts, tabs, rope):
    nb = x.shape[0]
    depth = p['w_t'].shape[0]
    s5re, s5im, hg, gla, ret = states
    to_lanes = lambda s: jnp.moveaxis(s, 1, -1)
    s5re_t = to_lanes(s5re).reshape(depth, S5_N, nb)
    s5im_t = to_lanes(s5im).reshape(depth, S5_N, nb)
    hg_t, gla_t, ret_t = to_lanes(hg), to_lanes(gla), to_lanes(ret)
    cos_t, sin_t = rope
    col = lambda a: jnp.swapaxes(a, -1, -2)
    per_layer = [p['norm_w'], p['w_t'], p['w_out'], col(p['bbd']), p['cbd_t'], col(p['a_re']), col(p['a_im']),
                 col(p['d_skip']), col(p['w_glu']), col(p['lbp']), col(p['hg_norm']), col(p['w_up']),
                 col(p['b_gate']), col(p['gla_norm']), col(p['ret_norm'])]
    shared = [p['final_norm'], col(cos_t), col(sin_t), tabs[4], consts['ones_h']]
    layer_spec = lambda a: pl.BlockSpec((None,) + a.shape[1:], lambda l, h: (l,) + (0,) * (a.ndim - 1))
    head_spec = lambda a: pl.BlockSpec((None, None) + a.shape[2:], lambda l, h: (l, h) + (0,) * (a.ndim - 2))
    state_specs = [layer_spec(s5re_t), layer_spec(s5im_t), head_spec(hg_t), head_spec(gla_t), head_spec(ret_t)]
    in_specs = ([_full(x.shape)] + [layer_spec(a) for a in per_layer] + [_full(a.shape) for a in shared]
                + state_specs)
    state_arrays = [s5re_t, s5im_t, hg_t, gla_t, ret_t]
    out_shape = tuple([jax.ShapeDtypeStruct(x.shape, F32)]
                      + [jax.ShapeDtypeStruct(a.shape, F32) for a in state_arrays])
    out_specs = tuple([pl.BlockSpec(x.shape, lambda l, h: (0, 0))] + state_specs)
    scratch = [pltpu.VMEM((nb, D_MODEL), F32), pltpu.VMEM((N_PACK, nb), F32), pltpu.VMEM((BRANCH, nb), F32),
               pltpu.VMEM((3 * BRANCH, nb), F32), pltpu.VMEM((4 * BRANCH, nb), F32)]
    outs = pl.pallas_call(
        _sample_kernel, grid=(depth, HEADS), in_specs=in_specs, out_specs=out_specs, out_shape=out_shape,
        scratch_shapes=scratch, name='sample_step',
        compiler_params=pltpu.CompilerParams(dimension_semantics=('arbitrary', 'arbitrary'),
                                             vmem_limit_bytes=VMEM_LIMIT),
    )(x, *per_layer, *shared, *state_arrays)
    from_lanes = lambda s: jnp.moveaxis(s, -1, 1)
    new = (from_lanes(outs[1].reshape(depth, S5_GROUPS, S5_STATE, nb)),
           from_lanes(outs[2].reshape(depth, S5_GROUPS, S5_STATE, nb)),
           from_lanes(outs[3]), from_lanes(outs[4]), from_lanes(outs[5]))
    return outs[0], new


def kernel(x_prompt, x_sample, state_s5_re, state_s5_im, state_hgrn, state_gla, state_ret, norm_w, final_norm_w, w_in, w_out, s5_lam_re, s5_lam_im, s5_log_step, s5_b_re, s5_b_im, s5_c_re, s5_c_im, s5_d, s5_w_glu, hgrn_lb_logits, hgrn_norm_w, gla_w_gate_up, gla_b_gate, gla_norm_w, ret_norm_w):
    depth = w_in.shape[0]
    seq = x_prompt.shape[1]
    consts = _constants()
    tabs = _ret_tables()
    rope_p = _rope_tables(jnp.arange(seq))
    rope_s = _rope_tables(PAST_LEN + jnp.arange(1))
    lb = jnp.cumsum(jax.nn.softmax(hgrn_lb_logits.astype(F32), axis=0), axis=0)
    lb = (lb - lb[0:1])[:, None, :]
    bbd, cbd_t, a_re, a_im = _s5_discretize(s5_lam_re, s5_lam_im, s5_log_step, s5_b_re, s5_b_im, s5_c_re,
                                            s5_c_im, consts['s5_mask'])
    w_t = lax.optimization_barrier(_pack_w_in_t(w_in))
    row = lambda a: a[:, None, :].astype(F32)
    w_up = jnp.zeros((depth, LANES, HEADS * GLA_DK), F32).at[:, :GLA_LOWRANK].set(gla_w_gate_up.astype(F32))
    p = dict(norm_w=row(norm_w),
             w_pack=jnp.swapaxes(w_t, 1, 2), w_t=w_t, w_out=w_out.astype(BF16),
             bbd=bbd, cbd=jnp.swapaxes(cbd_t, 1, 2).astype(BF16), cbd_t=cbd_t, a_re=a_re, a_im=a_im,
             d_skip=row(s5_d), w_glu=s5_w_glu.astype(BF16),
             lbp=jnp.concatenate([jnp.log(lb), jnp.log1p(-lb), 1.0 - lb, jnp.zeros((depth, 5, BRANCH), F32)],
                                 axis=1),
             hg_norm=row(hgrn_norm_w), w_up=w_up, b_gate=row(gla_b_gate), gla_norm=row(gla_norm_w),
             ret_norm=row(ret_norm_w), final_norm=final_norm_w[None, :].astype(F32))

    xp = x_prompt
    new_p = ([], [], [], [], [])
    for l in range(depth):
        xp, st_p = _prompt_layer(xp, l, p, consts, tabs, rope_p, l == depth - 1)
        for i in range(5):
            new_p[i].append(st_p[i])
    xs, new_s = _sample_step(x_sample.reshape(x_sample.shape[0], D_MODEL),
                             (state_s5_re, state_s5_im, state_hgrn, state_gla, state_ret),
                             p, consts, tabs, rope_s)
    return (xp, xs.reshape(x_sample.shape),
            jnp.stack(new_p[0]), jnp.stack(new_p[1]), jnp.stack(new_p[2]), jnp.stack(new_p[3]),
            jnp.stack(new_p[4])) + new_s
```

```python
import functools
import math

import numpy as np
import jax
import jax.numpy as jnp
from jax import lax
from jax.experimental import pallas as pl
from jax.experimental.pallas import tpu as pltpu

F32 = jnp.float32
BF16 = jnp.bfloat16

D_MODEL = 1024
BRANCH = 256
S5_CH = 16
S5_GROUPS = 16
S5_STATE = 64
S5_N = S5_GROUPS * S5_STATE
HEADS = 4
HG_DK = 64
GLA_DK = 32
RET_DK = 32
DV = 64
GLA_LOWRANK = 16
GLA_TAU = 16.0
ROPE_BASE = 10000.0
PAST_LEN = 16384
EPS = 1e-6
SUB = 16
PAIR = 2 * SUB
GROUP_ROWS = 4

LANES = 128
T_CHUNK = 64
PITCH = T_CHUNK + 8
SUB_TERM_ROWS = SUB * (SUB + SUB // 2) // 2
TERM_ROWS = (T_CHUNK // SUB) * SUB_TERM_ROWS
LOG2E = math.log2(math.e)
MASK_NEG = -1e30
VMEM_LIMIT = 60 * 1024 * 1024

C_U, C_SZ, C_HQ, C_HF, C_HI, C_HZ = 0, 256, 512, 768, 1024, 1280
C_GQ, C_GK, C_GV, C_GZ = 1536, 1664, 1792, 2048
C_RQ, C_RK, C_RV, C_RZ = 2304, 2432, 2560, 2816
C_LR = 3072
N_PACK = 3200
N_QUARTER = 4
QUARTER = C_LR // N_QUARTER


def _dot(a, b):
    return jnp.dot(a.astype(BF16), b.astype(BF16), preferred_element_type=F32)


def _dot_nt(a, b):
    return lax.dot_general(a.astype(BF16), b.astype(BF16), (((1,), (1,)), ((), ())),
                           preferred_element_type=F32)


def _dot_tn(a, b):
    return lax.dot_general(a.astype(BF16), b.astype(BF16), (((0,), (0,)), ((), ())),
                           preferred_element_type=F32)


def _split3(x):
    x1 = x.astype(BF16)
    r1 = x - x1.astype(F32)
    x2 = r1.astype(BF16)
    x3 = (r1 - x2.astype(F32)).astype(BF16)
    return x1, x2, x3


def _dot_sel_lhs(sel, x):
    x1, x2, x3 = _split3(x)
    d = lambda p: jnp.dot(sel, p, preferred_element_type=F32)
    return d(x1) + d(x2) + d(x3)


def _dot_sel_rhs2(x, sel):
    x1 = x.astype(BF16)
    x2 = (x - x1.astype(F32)).astype(BF16)
    return (jnp.dot(x1, sel, preferred_element_type=F32)
            + jnp.dot(x2, sel, preferred_element_type=F32))


def _dot3(a, b):
    a1 = a.astype(BF16)
    a2 = (a - a1.astype(F32)).astype(BF16)
    b1 = b.astype(BF16)
    b2 = (b - b1.astype(F32)).astype(BF16)
    d = lambda p, q: jnp.dot(p, q, preferred_element_type=F32)
    return d(a1, b1) + d(a1, b2) + d(a2, b1)


def _sigmoid(x):
    return 0.5 * jnp.tanh(0.5 * x) + 0.5


def _silu(x):
    return x * _sigmoid(x)


def _log_sigmoid(x):
    return jnp.minimum(x, 0.0) - jnp.log(1.0 + jnp.exp(-jnp.abs(x)))


def _gelu_tanh(x):
    return 0.5 * x * (1.0 + jnp.tanh(math.sqrt(2.0 / math.pi) * (x + 0.044715 * (x * x * x))))


def _rmsnorm_rows(x, w):
    return x * lax.rsqrt(jnp.mean(x * x, axis=-1, keepdims=True) + EPS) * w


def _head_rms(o, ones_h, gain):
    ms = _dot_sel_rhs2(o * o, ones_h) * (1.0 / DV)
    return o * lax.rsqrt(ms + EPS) * gain


def _head_ln(o, ones_h, gain):
    c = o - _dot_sel_rhs2(o, ones_h) * (1.0 / DV)
    var = _dot_sel_rhs2(c * c, ones_h) * (1.0 / DV)
    return c * lax.rsqrt(var + EPS) * gain


def _hgrn_gates(xf, loglb, log1mlb, one_m_lb):
    bterm = log1mlb + _log_sigmoid(xf)
    m = jnp.maximum(loglb, bterm)
    log_f = m + jnp.log(jnp.exp(loglb - m) + jnp.exp(bterm - m))
    return log_f, one_m_lb * _sigmoid(-xf)


def _rotary(t, cos, sin_signed, first_half):
    half = RET_DK // 2
    swapped = jnp.where(first_half, pltpu.roll(t, LANES - half, 1), pltpu.roll(t, half, 1))
    return t * cos + swapped * sin_signed


def _s5_output(y_lin, u, sz, dskip, wglu):
    y = _gelu_tanh(y_lin + u * dskip)
    y = y * _sigmoid(_dot(y, wglu))
    return y * _silu(sz)


def _mix_and_project(x, mix, wout, fnw, apply_final):
    out = x + _dot(mix, wout)
    if apply_final:
        out = _rmsnorm_rows(out, fnw)
    return out


class _Slabs:
    def __init__(self, ref, base=0):
        self.ref, self.base = ref, base

    def keep(self, value):
        for c in range(self.ref.shape[0]):
            self.ref[c, pl.ds(self.base, value.shape[0]), :] = value[:, c * LANES:(c + 1) * LANES]

    def tile(self, r0, n):
        return jnp.concatenate([self.ref[c, pl.ds(self.base + r0, n), :] for c in range(self.ref.shape[0])],
                               axis=1)

    def row(self, r, n):
        return jnp.concatenate([self.ref[c, pl.ds(self.base + r, n, stride=0), :]
                                for c in range(self.ref.shape[0])], axis=1)


class _GatedStream:
    def __init__(self, q, k, v, g, s_ref, ones_kv, mask_vk, tril, term_scr, row_scr, kdim):
        self.q, self.k, self.v, self.g = q, k, v, g
        self.bc, self.kc, self.vc = (_Slabs(r) for r in row_scr)
        self.s_ref, self.ones_kv, self.mask_vk = s_ref, ones_kv, mask_vk
        self.tril, self.term_scr, self.kdim = tril, term_scr, kdim
        self.ws, self.inter, self.att, self.cross = {}, {}, {}, {}

    def cumulate(self):
        self.bc.keep(_dot_sel_lhs(self.tril, self.g()) * LOG2E)
        self.kc.keep(self.k())
        self.vc.keep(self.v())

    def _total(self, p):
        return self.bc.tile(PAIR * (p + 1) - 1, 1)

    def update_part(self, p):
        sl = slice(PAIR * p, PAIR * (p + 1))
        self.ws[p] = _dot_tn(self.v()[sl],
                             self.k()[sl] * jnp.exp2(self._total(p) - self.bc.tile(PAIR * p, PAIR)))

    def cross_part(self, p):
        r_a, r_b = PAIR * p, PAIR * p + SUB
        edge = self.bc.row(r_b - 1, SUB)
        q_b = self.q()[r_b:r_b + SUB] * jnp.exp2(self.bc.tile(r_b, SUB) - edge)
        k_a = self.kc.tile(r_a, SUB) * jnp.exp2(edge - self.bc.tile(r_a, SUB))
        lane = lax.broadcasted_iota(jnp.int32, (SUB, self.kdim), 1) // (self.kdim // HEADS)
        q_heads = jnp.concatenate([jnp.where(lane == h, q_b, 0.0) for h in range(HEADS)], axis=0)
        scores = _dot_nt(q_heads, k_a)
        res = _dot(scores, self.vc.tile(r_a, SUB))
        lane_v = lax.broadcasted_iota(jnp.int32, (SUB, BRANCH), 1) // DV
        out = jnp.where(lane_v == 0, res[0:SUB], 0.0)
        for h in range(1, HEADS):
            out = out + jnp.where(lane_v == h, res[SUB * h:SUB * (h + 1)], 0.0)
        self.cross[p] = out

    def products_part(self, j):
        hs, term_scr, r0 = SUB // 2, self.term_scr, SUB * j
        q, bc = self.q()[r0:r0 + SUB], self.bc.tile(r0, SUB)
        trow = lax.broadcasted_iota(jnp.int32, (hs, self.kdim), 0)
        neg = [jnp.where(trow >= d, 0.0, MASK_NEG) for d in range(1, hs)]
        q_lo, q_hi, b_lo, b_hi = q[0:hs], q[hs:SUB], bc[0:hs], bc[hs:SUB]
        row = j * SUB_TERM_ROWS
        for s in range(SUB):
            ks, bs = self.kc.row(r0 + s, hs), self.bc.row(r0 + s, hs)
            d = s % hs
            q_dg, b_dg = (q_lo, b_lo) if s < hs else (q_hi, b_hi)
            e_dg = b_dg - bs if d == 0 else b_dg - bs + neg[d - 1]
            term_scr[row:row + hs, :] = q_dg * ks * jnp.exp2(e_dg)
            row += hs
            if s < hs:
                term_scr[row:row + hs, :] = q_hi * ks * jnp.exp2(b_hi - bs)
                row += hs

    def matmul_part(self, p):
        sl = slice(PAIR * p, PAIR * (p + 1))
        if p == 0:
            self.state = self.s_ref[...]
        self.inter[p] = _dot_nt(self.q()[sl] * jnp.exp2(self.bc.tile(PAIR * p, PAIR)), self.state)
        self.state = self.state * jnp.exp2(self._total(p)) + self.ws[p] * self.mask_vk
        if p == T_CHUNK // PAIR - 1:
            self.s_ref[...] = self.state
        rows = slice(2 * p * SUB_TERM_ROWS, 2 * (p + 1) * SUB_TERM_ROWS)
        self.att[p] = _dot(self.term_scr[rows, :], self.ones_kv)

    def output_part(self, j):
        hs, r0, p, second = SUB // 2, SUB * j, j // 2, j % 2
        att = self.att[p][second * SUB_TERM_ROWS:(second + 1) * SUB_TERM_ROWS]
        start = self.inter[p][second * SUB:(second + 1) * SUB]
        if second:
            start = start + self.cross[p]
        acc_lo, acc_hi = start[0:hs], start[hs:SUB]
        row = 0
        for s in range(SUB):
            vs = self.vc.row(r0 + s, hs)
            if s < hs:
                acc_lo = acc_lo + att[row:row + hs] * vs
                row += hs
            acc_hi = acc_hi + att[row:row + hs] * vs
            row += hs
        return jnp.concatenate([acc_lo, acc_hi], axis=0)


def _ret_chunk(q, k, v, cos, sin_signed, s_ref, dstack, inner, kdec, cdec, mask_vk):
    lane = lax.broadcasted_iota(jnp.int32, (T_CHUNK, HEADS * RET_DK), 1)
    first_half = (lane % RET_DK) < (RET_DK // 2)
    rq = _rotary(q, cos, sin_signed, first_half)
    rk = _rotary(k, cos, sin_signed, first_half) * (RET_DK ** -0.5)
    lane_v = lax.broadcasted_iota(jnp.int32, (T_CHUNK, HEADS * DV), 1)
    kst = jnp.concatenate([jnp.where(lane // RET_DK == h, rk, 0.0) for h in range(HEADS)], axis=0)
    vst = jnp.concatenate([jnp.where(lane_v // DV == h, v, 0.0) for h in range(HEADS)], axis=0)
    p = _dot_nt(rq, kst) * dstack
    s_t = s_ref[...]
    o = _dot(p, vst) + _dot_nt(rq, s_t) * inner
    s_ref[...] = s_t * cdec + _dot_tn(v, rk * kdec) * mask_vk
    return o


def _prompt_kernel(apply_final,
                   x_ref, xn_ref, normw_ref, win_ref, wout_ref, bbd_ref, cbd_ref, are_ref, aim_ref,
                   dskip_ref, wglu_ref, lbp_ref, hgn_ref, wup_ref, bgate_ref, glan_ref, retn_ref, cos_ref,
                   sin_ref, dstack_ref, inner_ref, kdec_ref, cdec_ref, onesh_ref, onesg_ref, tril_ref,
                   fnw_ref,
                   y_ref, s5_ref, hgc_ref, glac_ref, retc_ref,
                   hgs_ref, glas_ref, rets_ref, proj_scr, lr_scr, hk_scr, bu_scr, mix_scr, *stream_scratch):
    nb = x_ref.shape[0]
    rows = nb * T_CHUNK
    n_slab = 2 * S5_N // LANES
    half = n_slab // 2
    step = pl.program_id(0)

    def w_quarter(qtr):
        return win_ref[:, qtr * QUARTER:(qtr + 1) * QUARTER]

    def proj(col, width, rs=slice(None)):
        return proj_scr[col // QUARTER, rs, col % QUARTER:col % QUARTER + width]

    @pl.when(step == 0)
    def _init():
        s5_ref[...] = jnp.zeros_like(s5_ref)
        hgs_ref[...] = jnp.zeros_like(hgs_ref)
        glas_ref[...] = jnp.zeros_like(glas_ref)
        rets_ref[...] = jnp.zeros_like(rets_ref)
        bu_scr[...] = jnp.zeros_like(bu_scr)
        h0 = _rmsnorm_rows(x_ref[...].reshape(rows, D_MODEL), normw_ref[...]).astype(BF16)
        for qtr in range(N_QUARTER):
            proj_scr[qtr] = jnp.dot(h0, w_quarter(qtr), preferred_element_type=F32)
        lr_scr[...] = jnp.dot(h0, win_ref[:, C_LR:N_PACK], preferred_element_type=F32)

    u16 = proj(C_U, BRANCH).astype(BF16)

    def drive(c2):
        bu = jnp.dot(u16, bbd_ref[:, 2 * c2 * LANES:2 * (c2 + 1) * LANES], preferred_element_type=F32)
        for cc in range(2):
            for b in range(nb):
                bu_scr[2 * c2 + cc, b * PITCH:b * PITCH + T_CHUNK, :] = bu[b * T_CHUNK:(b + 1) * T_CHUNK,
                                                                           cc * LANES:(cc + 1) * LANES]

    lbp = lbp_ref[...]
    log_f, hk = _hgrn_gates(proj(C_HF, BRANCH), lbp[0:1, :], lbp[1:2, :], lbp[2:3, :])
    proj_scr[C_HF // QUARTER, :, C_HF % QUARTER:C_HF % QUARTER + BRANCH] = log_f
    hk_scr[...] = hk
    g_gla = _log_sigmoid(_dot3(lr_scr[...], wup_ref[...]) + bgate_ref[...])
    lr_scr[...] = g_gla * (1.0 / GLA_TAU)
    for c2 in range(n_slab // 2):
        drive(c2)
    a_re = [jnp.broadcast_to(are_ref[:, c * LANES:(c + 1) * LANES], (nb, LANES)) for c in range(half)]
    a_im = [jnp.broadcast_to(aim_ref[:, c * LANES:(c + 1) * LANES], (nb, LANES)) for c in range(half)]
    s_init = s5_ref[...]
    carry0 = tuple(s_init[:, c * LANES:(c + 1) * LANES] for c in range(n_slab))

    def scan_step(t, carry):
        new = [None] * n_slab
        for c in range(half):
            sr, si = carry[c], carry[half + c]
            br = bu_scr[c, pl.ds(t, nb, stride=PITCH), :]
            bi = bu_scr[half + c, pl.ds(t, nb, stride=PITCH), :]
            nr = a_re[c] * sr - a_im[c] * si + br
            ni = a_re[c] * si + a_im[c] * sr + bi
            bu_scr[c, pl.ds(t, nb, stride=PITCH), :] = nr
            bu_scr[half + c, pl.ds(t, nb, stride=PITCH), :] = ni
            new[c], new[half + c] = nr, ni
        return tuple(new)

    carry = lax.fori_loop(0, T_CHUNK, scan_step, carry0, unroll=8)
    s5_ref[...] = jnp.concatenate(carry, axis=1)
    s_all = jnp.concatenate([bu_scr[c] for c in range(n_slab)], axis=1)
    y_all = _dot(s_all, cbd_ref[...])
    y_lin = jnp.concatenate([y_all[b * PITCH:b * PITCH + T_CHUNK] for b in range(nb)], axis=0)
    mix_scr[:, 0:BRANCH] = _s5_output(y_lin, proj(C_U, BRANCH), proj(C_SZ, BRANCH), dskip_ref[...],
                                      wglu_ref[...])

    ones_h = onesh_ref[...]
    ones_g = onesg_ref[...]
    mask_h = ones_h.astype(F32)
    mask_g = jnp.transpose(ones_g.astype(F32))
    tril = tril_ref[...]

    def rows_of(bp, odd):
        return pl.ds(pl.multiple_of((2 * bp + odd) * T_CHUNK, T_CHUNK), T_CHUNK)

    def hgrn_stream(bp, odd, term, row_scr):
        rs = rows_of(bp, odd)
        return _GatedStream(lambda: proj(C_HQ, BRANCH, rs), lambda: hk_scr[rs, :], lambda: proj(C_HI, BRANCH, rs),
                            lambda: proj(C_HF, BRANCH, rs), hgs_ref.at[bp, odd], ones_h, mask_h, tril, term,
                            row_scr, HEADS * HG_DK)

    def gla_stream(bp, odd, term, row_scr):
        rs = rows_of(bp, odd)
        return _GatedStream(lambda: proj(C_GQ, LANES, rs) * (GLA_DK ** -0.5), lambda: proj(C_GK, LANES, rs),
                            lambda: proj(C_GV, BRANCH, rs), lambda: lr_scr[rs, :],
                            glas_ref.at[bp, odd], ones_g, mask_g, tril, term, row_scr, HEADS * GLA_DK)

    n_sub = T_CHUNK // SUB
    def retention(bp, odd):
        rs = rows_of(bp, odd)
        mix_scr[rs, 3 * BRANCH:4 * BRANCH] = _ret_chunk(
            proj(C_RQ, LANES, rs), proj(C_RK, LANES, rs), proj(C_RV, BRANCH, rs), cos_ref[...],
            sin_ref[...], rets_ref.at[bp, odd], dstack_ref[...], inner_ref[...], kdec_ref[...],
            cdec_ref[...], mask_g)

    def per_group(grp, _):
        st, slots = [], []
        for i in range(GROUP_ROWS):
            bp, odd = (GROUP_ROWS // 2) * grp + i // 2, i % 2
            scr = stream_scratch[8 * i:8 * (i + 1)]
            st += [hgrn_stream(bp, odd, scr[0], scr[2:5]), gla_stream(bp, odd, scr[1], scr[5:8])]
            slots += [(bp, odd, BRANCH), (bp, odd, 2 * BRANCH)]

        def emit(i, j):
            bp, odd, col = slots[i]
            r0 = pl.multiple_of((2 * bp + odd) * T_CHUNK + SUB * j, SUB)
            mix_scr[pl.ds(r0, SUB), col:col + BRANCH] = st[i].output_part(j)

        for s_ in st:
            s_.cumulate()
        for i in range(GROUP_ROWS):
            retention((GROUP_ROWS // 2) * grp + i // 2, i % 2)
        for s_ in st:
            for p in range(n_sub // 2):
                s_.update_part(p)
        for s_ in st:
            for j in range(n_sub):
                s_.products_part(j)
        for s_ in st:
            for p in range(n_sub // 2):
                s_.cross_part(p)
        for s_ in st:
            for p in range(n_sub // 2):
                s_.matmul_part(p)
        for i in range(len(st)):
            for j in range(n_sub):
                emit(i, j)
        return 0

    lax.fori_loop(0, nb // GROUP_ROWS, per_group, 0)

    o_hg = _head_rms(mix_scr[:, BRANCH:2 * BRANCH], ones_h, hgn_ref[...])
    mix_scr[:, BRANCH:2 * BRANCH] = o_hg * _silu(proj(C_HZ, BRANCH))
    o_gla = _head_rms(mix_scr[:, 2 * BRANCH:3 * BRANCH], ones_h, glan_ref[...])
    mix_scr[:, 2 * BRANCH:3 * BRANCH] = o_gla * _silu(proj(C_GZ, BRANCH))
    o_ret = _head_ln(mix_scr[:, 3 * BRANCH:4 * BRANCH], ones_h, retn_ref[...])
    mix_scr[:, 3 * BRANCH:4 * BRANCH] = o_ret * _silu(proj(C_RZ, BRANCH))
    hn = _rmsnorm_rows(xn_ref[...].reshape(rows, D_MODEL), normw_ref[...]).astype(BF16)
    for qtr in range(N_QUARTER):
        proj_scr[qtr] = jnp.dot(hn, w_quarter(qtr), preferred_element_type=F32)
    lr_scr[...] = jnp.dot(hn, win_ref[:, C_LR:N_PACK], preferred_element_type=F32)
    out = _mix_and_project(x_ref[...].reshape(rows, D_MODEL), mix_scr[...], wout_ref[...], fnw_ref[...],
                           apply_final)
    y_ref[...] = out.reshape(nb, T_CHUNK, D_MODEL)

    @pl.when(step == pl.num_programs(0) - 1)
    def _emit_states():
        for b in range(nb):
            for state, compact in ((hgs_ref, hgc_ref), (glas_ref, glac_ref), (rets_ref, retc_ref)):
                dk = compact.shape[3]
                for h in range(HEADS):
                    compact[b, h] = state[b // 2, b % 2, h * DV:(h + 1) * DV, h * dk:(h + 1) * dk]


def _dot_sel_lhs2(sel, x):
    x1 = x.astype(BF16)
    x2 = (x - x1.astype(F32)).astype(BF16)
    return (jnp.dot(sel, x1, preferred_element_type=F32) + jnp.dot(sel, x2, preferred_element_type=F32))


def _head_rms_t(o, ones_h, gain):
    ms = _dot_sel_lhs2(ones_h, o * o) * (1.0 / DV)
    return o * lax.rsqrt(ms + EPS) * gain


def _head_ln_t(o, ones_h, gain):
    c = o - _dot_sel_lhs2(ones_h, o) * (1.0 / DV)
    var = _dot_sel_lhs2(ones_h, c * c) * (1.0 / DV)
    return c * lax.rsqrt(var + EPS) * gain


def _rotary_t(t, cos, sin_signed, first_half):
    half = RET_DK // 2
    swapped = jnp.where(first_half, pltpu.roll(t, LANES - half, 0), pltpu.roll(t, half, 0))
    return t * cos + swapped * sin_signed


def _sample_kernel(x_ref, normw_ref, wt_ref, wout_ref, bbdt_ref, cbdt_ref, are_ref, aim_ref, dskip_ref,
                   wglut_ref, lbp_ref, hgn_ref, wupt_ref, bgate_ref, glan_ref, retn_ref, fnw_ref,
                   cos_ref, sin_ref, dret_ref, onesh_ref,
                   s5re_ref, s5im_ref, hg_ref, gla_ref, ret_ref,
                   y_ref, s5re_o, s5im_o, hg_o, gla_o, ret_o,
                   xs_scr, pt_scr, hk_scr, ot_scr, mixt_scr):
    layer, head = pl.program_id(0), pl.program_id(1)
    last_layer, last_head = pl.num_programs(0) - 1, pl.num_programs(1) - 1

    @pl.when((layer == 0) & (head == 0))
    def _load_x():
        xs_scr[...] = x_ref[...]

    @pl.when(head == 0)
    def _dense():
        hh = _rmsnorm_rows(xs_scr[...], normw_ref[...]).astype(BF16)
        pt_scr[...] = lax.dot_general(wt_ref[...], hh, (((1,), (1,)), ((), ())), preferred_element_type=F32)

        u = pt_scr[C_U:C_U + BRANCH, :]
        bu = _dot3(bbdt_ref[...], u)
        a_re, a_im = are_ref[...], aim_ref[...]
        s0r, s0i = s5re_ref[...], s5im_ref[...]
        s_re = a_re * s0r - a_im * s0i + bu[0:S5_N]
        s_im = a_re * s0i + a_im * s0r + bu[S5_N:2 * S5_N]
        s5re_o[...] = s_re
        s5im_o[...] = s_im
        y = _gelu_tanh(_dot3(cbdt_ref[...], jnp.concatenate([s_re, s_im], axis=0)) + u * dskip_ref[...])
        y = y * _sigmoid(jnp.dot(wglut_ref[...], y.astype(BF16), preferred_element_type=F32))
        mixt_scr[0:BRANCH, :] = y * _silu(pt_scr[C_SZ:C_SZ + BRANCH, :])

        lbp = lbp_ref[...]
        log_f, hk = _hgrn_gates(pt_scr[C_HF:C_HF + BRANCH, :], lbp[:, 0:1], lbp[:, 1:2], lbp[:, 2:3])
        pt_scr[C_HF:C_HF + BRANCH, :] = jnp.exp(log_f)
        hk_scr[...] = hk
        g_gla = _log_sigmoid(_dot3(wupt_ref[...], pt_scr[C_LR:C_LR + LANES, :]) + bgate_ref[...])
        pt_scr[C_LR:C_LR + LANES, :] = jnp.exp(g_gla * (1.0 / GLA_TAU))
        pt_scr[C_GQ:C_GQ + LANES, :] = pt_scr[C_GQ:C_GQ + LANES, :] * (GLA_DK ** -0.5)
        row = lax.broadcasted_iota(jnp.int32, (HEADS * RET_DK, LANES), 0)
        first_half = (row % RET_DK) < (RET_DK // 2)
        pt_scr[C_RQ:C_RQ + LANES, :] = _rotary_t(pt_scr[C_RQ:C_RQ + LANES, :], cos_ref[...], sin_ref[...],
                                                 first_half)
        pt_scr[C_RK:C_RK + LANES, :] = _rotary_t(pt_scr[C_RK:C_RK + LANES, :], cos_ref[...], sin_ref[...],
                                                 first_half) * (RET_DK ** -0.5)

    def head_update(s0_ref, s_out_ref, dk, dec_ref, dec_row, key_ref, key_row, q_row, v_row, out_row):
        vt = pt_scr[pl.ds(pl.multiple_of(v_row + head * DV, DV), DV), :]

        def feature(kk, acc):
            r = head * dk + kk
            bcast = lambda ref, r0: jnp.broadcast_to(ref[pl.ds(r0 + r, 1), :], (DV, LANES))
            s_new = s0_ref[kk] * bcast(dec_ref, dec_row) + bcast(key_ref, key_row) * vt
            s_out_ref[kk] = s_new
            return acc + bcast(pt_scr, q_row) * s_new

        acc = lax.fori_loop(0, dk, feature, jnp.zeros((DV, LANES), F32), unroll=4)
        ot_scr[pl.ds(pl.multiple_of(out_row + head * DV, DV), DV), :] = acc

    head_update(hg_ref, hg_o, HG_DK, pt_scr, C_HF, hk_scr, 0, C_HQ, C_HI, 0)
    head_update(gla_ref, gla_o, GLA_DK, pt_scr, C_LR, pt_scr, C_GK, C_GQ, C_GV, BRANCH)
    head_update(ret_ref, ret_o, RET_DK, dret_ref, 0, pt_scr, C_RK, C_RQ, C_RV, 2 * BRANCH)

    @pl.when(head == last_head)
    def _finish():
        ones_h = onesh_ref[...]
        o_hg = _head_rms_t(ot_scr[0:BRANCH, :], ones_h, hgn_ref[...])
        mixt_scr[BRANCH:2 * BRANCH, :] = o_hg * _silu(pt_scr[C_HZ:C_HZ + BRANCH, :])
        o_gla = _head_rms_t(ot_scr[BRANCH:2 * BRANCH, :], ones_h, glan_ref[...])
        mixt_scr[2 * BRANCH:3 * BRANCH, :] = o_gla * _silu(pt_scr[C_GZ:C_GZ + BRANCH, :])
        o_ret = _head_ln_t(ot_scr[2 * BRANCH:3 * BRANCH, :], ones_h, retn_ref[...])
        mixt_scr[3 * BRANCH:4 * BRANCH, :] = o_ret * _silu(pt_scr[C_RZ:C_RZ + BRANCH, :])
        out = xs_scr[...] + lax.dot_general(mixt_scr[...].astype(BF16), wout_ref[...], (((0,), (0,)), ((), ())),
                                            preferred_element_type=F32)
        xs_scr[...] = out

        @pl.when(layer == last_layer)
        def _emit():
            y_ref[...] = _rmsnorm_rows(out, fnw_ref[...])


def _ret_log_gamma():
    return jnp.log1p(-jnp.exp2(-5.0 - jnp.arange(HEADS, dtype=F32)))


def _constants():
    ones_h = (np.arange(BRANCH)[:, None] // DV == np.arange(BRANCH)[None, :] // DV)
    ones_g = (np.arange(HEADS * GLA_DK)[:, None] // GLA_DK == np.arange(BRANCH)[None, :] // DV)
    r = np.arange(T_CHUNK)
    tril = (r[:, None] // PAIR == r[None, :] // PAIR) & (r[None, :] <= r[:, None])
    same_group_b = np.arange(BRANCH)[:, None] // S5_CH == np.arange(S5_N)[None, :] // S5_STATE
    as_bf16 = lambda m: jnp.asarray(m.astype(np.float32), dtype=BF16)
    return dict(ones_h=as_bf16(ones_h), ones_g=as_bf16(ones_g), tril=as_bf16(tril),
                s5_mask=jnp.asarray(same_group_b.astype(np.float32)))


def _ret_tables():
    lg = _ret_log_gamma()
    idx = jnp.arange(T_CHUNK, dtype=F32)
    rel = idx[:, None] - idx[None, :]
    causal = rel >= 0
    decay = jnp.where(causal[None], jnp.exp(jnp.where(causal, rel, 0.0)[None] * lg[:, None, None]), 0.0)
    dstack = jnp.transpose(decay, (1, 0, 2)).reshape(T_CHUNK, HEADS * T_CHUNK)
    inner = jnp.repeat(jnp.exp((idx[:, None] + 1.0) * lg[None, :]), DV, axis=1)
    kdec = jnp.repeat(jnp.exp((T_CHUNK - 1.0 - idx[:, None]) * lg[None, :]), RET_DK, axis=1)
    cdec = jnp.repeat(jnp.exp(T_CHUNK * lg)[None, :], RET_DK, axis=1)
    dret = jnp.broadcast_to(jnp.repeat(jnp.exp(lg), RET_DK)[:, None], (HEADS * RET_DK, LANES))
    return dstack, inner, kdec, cdec, dret


def _rope_tables(pos):
    half = RET_DK // 2
    inv = ROPE_BASE ** (-jnp.arange(half, dtype=F32) / half)
    ang = pos.astype(F32)[:, None] * inv[None, :]
    cos, sin = jnp.cos(ang), jnp.sin(ang)
    cos_t = jnp.tile(jnp.concatenate([cos, cos], axis=1), (1, HEADS))
    sin_t = jnp.tile(jnp.concatenate([-sin, sin], axis=1), (1, HEADS))
    return cos_t, sin_t


def _pack_w_in_t(w):
    wt = jnp.swapaxes(w, 1, 2)
    offs = np.cumsum([0, 256, 256, 256, 256, 256, 256, 128, 128, 256, 16, 256, 128, 128, 256, 256])
    seg = lambda i: wt[:, int(offs[i]):int(offs[i + 1]), :]
    order = [0, 1, 2, 3, 4, 5, 6, 7, 8, 10, 11, 12, 13, 14]
    pad = jnp.zeros((w.shape[0], LANES - GLA_LOWRANK, w.shape[1]), w.dtype)
    return jnp.concatenate([seg(i) for i in order] + [seg(9), pad], axis=1).astype(BF16)


def _s5_discretize(lam_re, lam_im, log_step, b_re, b_im, c_re, c_im, mask):
    lr, li = lam_re.astype(F32), lam_im.astype(F32)
    step = jnp.exp(log_step.astype(F32))[..., None]
    mag = jnp.exp(lr * step)
    ab_re = mag * jnp.cos(li * step)
    ab_im = mag * jnp.sin(li * step)
    den = lr * lr + li * li
    nr = ab_re - 1.0
    f_re = (nr * lr + ab_im * li) / den
    f_im = (ab_im * lr - nr * li) / den
    br, bi = b_re.astype(F32), b_im.astype(F32)
    bb_re = f_re[..., None] * br - f_im[..., None] * bi
    bb_im = f_re[..., None] * bi + f_im[..., None] * br
    nl = lr.shape[0]

    def drive(bb):
        rows = jnp.transpose(bb, (0, 1, 3, 2)).reshape(nl, BRANCH, S5_STATE)
        return jnp.tile(rows, (1, 1, S5_GROUPS)) * mask

    def readout(cc):
        rows = cc.astype(F32).reshape(nl, BRANCH, S5_STATE)
        return jnp.tile(rows, (1, 1, S5_GROUPS)) * mask

    bbd = jnp.concatenate([drive(bb_re), drive(bb_im)], axis=2)
    cbd_t = jnp.concatenate([readout(c_re), -readout(c_im)], axis=2)
    return bbd, cbd_t, ab_re.reshape(nl, 1, S5_N), ab_im.reshape(nl, 1, S5_N)


def _full(shape):
    return pl.BlockSpec(shape, lambda *_: (0,) * len(shape), pipeline_mode=pl.Buffered(1))


def _of_layer(arr, layer):
    nd = arr.ndim - 1
    return pl.BlockSpec((None,) + arr.shape[1:], lambda *_: (layer,) + (0,) * nd,
                        pipeline_mode=pl.Buffered(1))


def _prompt_layer(x, layer, p, consts, tabs, rope, apply_final):
    nb, seq, _ = x.shape
    n_steps = seq // T_CHUNK
    rows = nb * T_CHUNK
    dstack, inner, kdec, cdec, _ = tabs
    cos_t, sin_t = rope
    per_layer = [p['norm_w'], p['w_pack'], p['w_out'], p['bbd'].astype(BF16), p['cbd'],
                 p['a_re'], p['a_im'],
                 p['d_skip'], p['w_glu'], p['lbp'], p['hg_norm'], p['w_up'], p['b_gate'], p['gla_norm'],
                 p['ret_norm']]
    shared = [dstack, inner, kdec, cdec, consts['ones_h'], consts['ones_g'], consts['tril'],
              p['final_norm']]
    in_specs = [pl.BlockSpec((nb, T_CHUNK, D_MODEL), lambda i: (0, i, 0)),
                pl.BlockSpec((nb, T_CHUNK, D_MODEL), lambda i: (0, jnp.minimum(i + 1, n_steps - 1), 0))]
    in_specs += [_of_layer(a, layer) for a in per_layer]
    in_specs += [pl.BlockSpec((T_CHUNK, LANES), lambda i: (i, 0)), pl.BlockSpec((T_CHUNK, LANES), lambda i: (i, 0))]
    in_specs += [_full(a.shape) for a in shared]
    out_shape = (jax.ShapeDtypeStruct((nb, seq, D_MODEL), F32),
                 jax.ShapeDtypeStruct((nb, 2 * S5_N), F32),
                 jax.ShapeDtypeStruct((nb, HEADS, DV, HG_DK), F32),
                 jax.ShapeDtypeStruct((nb, HEADS, DV, GLA_DK), F32),
                 jax.ShapeDtypeStruct((nb, HEADS, DV, RET_DK), F32))
    out_specs = (pl.BlockSpec((nb, T_CHUNK, D_MODEL), lambda i: (0, i, 0)),
                 pl.BlockSpec((nb, 2 * S5_N), lambda i: (0, 0)),
                 pl.BlockSpec((nb, HEADS, DV, HG_DK), lambda i: (0, 0, 0, 0)),
                 pl.BlockSpec((nb, HEADS, DV, GLA_DK), lambda i: (0, 0, 0, 0)),
                 pl.BlockSpec((nb, HEADS, DV, RET_DK), lambda i: (0, 0, 0, 0)))
    scratch = [pltpu.VMEM((nb // 2, 2, BRANCH, HEADS * HG_DK), F32),
               pltpu.VMEM((nb // 2, 2, BRANCH, HEADS * GLA_DK), F32),
               pltpu.VMEM((nb // 2, 2, BRANCH, HEADS * RET_DK), F32),
               pltpu.VMEM((N_QUARTER, rows, QUARTER), F32), pltpu.VMEM((rows, LANES), F32),
               pltpu.VMEM((rows, BRANCH), F32),
               pltpu.VMEM((2 * S5_N // LANES, nb * PITCH, LANES), F32),
               pltpu.VMEM((rows, D_MODEL), F32),
               ] + GROUP_ROWS * (
                   [pltpu.VMEM((TERM_ROWS, kd), F32) for kd in (HEADS * HG_DK, HEADS * GLA_DK)]
                   + [pltpu.VMEM((w // LANES, T_CHUNK, LANES), F32) for kd in (HEADS * HG_DK, HEADS * GLA_DK)
                      for w in (kd, kd, BRANCH)])
    y, s5, hgs, glas, rets = pl.pallas_call(
        functools.partial(_prompt_kernel, apply_final),
        grid=(n_steps,), in_specs=in_specs, out_specs=out_specs, out_shape=out_shape,
        scratch_shapes=scratch, name='prompt_layer',
        compiler_params=pltpu.CompilerParams(dimension_semantics=('arbitrary',),
                                             vmem_limit_bytes=VMEM_LIMIT),
    )(x, x, *per_layer, cos_t, sin_t, *shared)
    s5 = s5.reshape(nb, 2, S5_GROUPS, S5_STATE)

    to_kv = lambda st: jnp.swapaxes(st, 2, 3)
    return y, (s5[:, 0], s5[:, 1], to_kv(hgs), to_kv(glas), to_kv(rets))


def _sample_step(x, states, p, consts, tabs, rope):
    nb = x.shape[0]
    depth = p['w_t'].shape[0]
    s5re, s5im, hg, gla, ret = states
    to_lanes = lambda s: jnp.moveaxis(s, 1, -1)
    s5re_t = to_lanes(s5re).reshape(depth, S5_N, nb)
    s5im_t = to_lanes(s5im).reshape(depth, S5_N, nb)
    hg_t, gla_t, ret_t = to_lanes(hg), to_lanes(gla), to_lanes(ret)
    cos_t, sin_t = rope
    col = lambda a: jnp.swapaxes(a, -1, -2)
    per_layer = [p['norm_w'], p['w_t'], p['w_out'], col(p['bbd']), p['cbd_t'], col(p['a_re']), col(p['a_im']),
                 col(p['d_skip']), col(p['w_glu']), col(p['lbp']), col(p['hg_norm']), col(p['w_up']),
                 col(p['b_gate']), col(p['gla_norm']), col(p['ret_norm'])]
    shared = [p['final_norm'], col(cos_t), col(sin_t), tabs[4], consts['ones_h']]
    layer_spec = lambda a: pl.BlockSpec((None,) + a.shape[1:], lambda l, h: (l,) + (0,) * (a.ndim - 1))
    head_spec = lambda a: pl.BlockSpec((None, None) + a.shape[2:], lambda l, h: (l, h) + (0,) * (a.ndim - 2))
    state_specs = [layer_spec(s5re_t), layer_spec(s5im_t), head_spec(hg_t), head_spec(gla_t), head_spec(ret_t)]
    in_specs = ([_full(x.shape)] + [layer_spec(a) for a in per_layer] + [_full(a.shape) for a in shared]
                + state_specs)
    state_arrays = [s5re_t, s5im_t, hg_t, gla_t, ret_t]
    out_shape = tuple([jax.ShapeDtypeStruct(x.shape, F32)]
                      + [jax.ShapeDtypeStruct(a.shape, F32) for a in state_arrays])
    out_specs = tuple([pl.BlockSpec(x.shape, lambda l, h: (0, 0))] + state_specs)
    scratch = [pltpu.VMEM((nb, D_MODEL), F32), pltpu.VMEM((N_PACK, nb), F32), pltpu.VMEM((BRANCH, nb), F32),
               pltpu.VMEM((3 * BRANCH, nb), F32), pltpu.VMEM((4 * BRANCH, nb), F32)]
    outs = pl.pallas_call(
        _sample_kernel, grid=(depth, HEADS), in_specs=in_specs, out_specs=out_specs, out_shape=out_shape,
        scratch_shapes=scratch, name='sample_step',
        compiler_params=pltpu.CompilerParams(dimension_semantics=('arbitrary', 'arbitrary'),
                                             vmem_limit_bytes=VMEM_LIMIT),
    )(x, *per_layer, *shared, *state_arrays)
    from_lanes = lambda s: jnp.moveaxis(s, -1, 1)
    new = (from_lanes(outs[1].reshape(depth, S5_GROUPS, S5_STATE, nb)),
           from_lanes(outs[2].reshape(depth, S5_GROUPS, S5_STATE, nb)),
           from_lanes(outs[3]), from_lanes(outs[4]), from_lanes(outs[5]))
    return outs[0], new


def kernel(x_prompt, x_sample, state_s5_re, state_s5_im, state_hgrn, state_gla, state_ret, norm_w, final_norm_w, w_in, w_out, s5_lam_re, s5_lam_im, s5_log_step, s5_b_re, s5_b_im, s5_c_re, s5_c_im, s5_d, s5_w_glu, hgrn_lb_logits, hgrn_norm_w, gla_w_gate_up, gla_b_gate, gla_norm_w, ret_norm_w):
    depth = w_in.shape[0]
    seq = x_prompt.shape[1]
    consts = _constants()
    tabs = _ret_tables()
    rope_p = _rope_tables(jnp.arange(seq))
    rope_s = _rope_tables(PAST_LEN + jnp.arange(1))
    lb = jnp.cumsum(jax.nn.softmax(hgrn_lb_logits.astype(F32), axis=0), axis=0)
    lb = (lb - lb[0:1])[:, None, :]
    bbd, cbd_t, a_re, a_im = _s5_discretize(s5_lam_re, s5_lam_im, s5_log_step, s5_b_re, s5_b_im, s5_c_re,
                                            s5_c_im, consts['s5_mask'])
    w_t = lax.optimization_barrier(_pack_w_in_t(w_in))
    row = lambda a: a[:, None, :].astype(F32)
    w_up = jnp.zeros((depth, LANES, HEADS * GLA_DK), F32).at[:, :GLA_LOWRANK].set(gla_w_gate_up.astype(F32))
    p = dict(norm_w=row(norm_w),
             w_pack=jnp.swapaxes(w_t, 1, 2), w_t=w_t, w_out=w_out.astype(BF16),
             bbd=bbd, cbd=jnp.swapaxes(cbd_t, 1, 2).astype(BF16), cbd_t=cbd_t, a_re=a_re, a_im=a_im,
             d_skip=row(s5_d), w_glu=s5_w_glu.astype(BF16),
             lbp=jnp.concatenate([jnp.log(lb), jnp.log1p(-lb), 1.0 - lb, jnp.zeros((depth, 5, BRANCH), F32)],
                                 axis=1),
             hg_norm=row(hgrn_norm_w), w_up=w_up, b_gate=row(gla_b_gate), gla_norm=row(gla_norm_w),
             ret_norm=row(ret_norm_w), final_norm=final_norm_w[None, :].astype(F32))

    xp = x_prompt
    new_p = ([], [], [], [], [])
    for l in range(depth):
        xp, st_p = _prompt_layer(xp, l, p, consts, tabs, rope_p, l == depth - 1)
        for i in range(5):
            new_p[i].append(st_p[i])
    xs, new_s = _sample_step(x_sample.reshape(x_sample.shape[0], D_MODEL),
                             (state_s5_re, state_s5_im, state_hgrn, state_gla, state_ret),
                             p, consts, tabs, rope_s)
    return (xp, xs.reshape(x_sample.shape),
            jnp.stack(new_p[0]), jnp.stack(new_p[1]), jnp.stack(new_p[2]), jnp.stack(new_p[3]),
            jnp.stack(new_p[4])) + new_s
```

```python
import functools
import math

import numpy as np
import jax
import jax.numpy as jnp
from jax import lax
from jax.experimental import pallas as pl
from jax.experimental.pallas import tpu as pltpu

F32 = jnp.float32
BF16 = jnp.bfloat16

D_MODEL = 1024
BRANCH = 256
S5_CH = 16
S5_GROUPS = 16
S5_STATE = 64
S5_N = S5_GROUPS * S5_STATE
HEADS = 4
HG_DK = 64
GLA_DK = 32
RET_DK = 32
DV = 64
GLA_LOWRANK = 16
GLA_TAU = 16.0
ROPE_BASE = 10000.0
PAST_LEN = 16384
EPS = 1e-6
SUB = 16
PAIR = 2 * SUB
GROUP_ROWS = 4

LANES = 128
T_CHUNK = 64
PITCH = T_CHUNK + 8
SUB_TERM_ROWS = SUB * (SUB + SUB // 2) // 2
TERM_ROWS = (T_CHUNK // SUB) * SUB_TERM_ROWS
LOG2E = math.log2(math.e)
MASK_NEG = -1e30
VMEM_LIMIT = 60 * 1024 * 1024

C_U, C_SZ, C_HQ, C_HF, C_HI, C_HZ = 0, 256, 512, 768, 1024, 1280
C_GQ, C_GK, C_GV, C_GZ = 1536, 1664, 1792, 2048
C_RQ, C_RK, C_RV, C_RZ = 2304, 2432, 2560, 2816
C_LR = 3072
(V_NORM, V_ARE, V_AIM, V_DSKIP, V_LOGLB, V_LOG1MLB, V_ONEMLB, V_HGN, V_BGATE, V_GLAN, V_RETN) = range(11)
V_ROWS = 16
N_PACK = 3200
N_QUARTER = 4
QUARTER = C_LR // N_QUARTER


def _dot(a, b):
    return jnp.dot(a.astype(BF16), b.astype(BF16), preferred_element_type=F32)


def _dot_nt(a, b):
    return lax.dot_general(a.astype(BF16), b.astype(BF16), (((1,), (1,)), ((), ())),
                           preferred_element_type=F32)


def _dot_tn(a, b):
    return lax.dot_general(a.astype(BF16), b.astype(BF16), (((0,), (0,)), ((), ())),
                           preferred_element_type=F32)


def _split3(x):
    x1 = x.astype(BF16)
    r1 = x - x1.astype(F32)
    x2 = r1.astype(BF16)
    x3 = (r1 - x2.astype(F32)).astype(BF16)
    return x1, x2, x3


def _dot_sel_lhs(sel, x):
    x1, x2, x3 = _split3(x)
    d = lambda p: jnp.dot(sel, p, preferred_element_type=F32)
    return d(x1) + d(x2) + d(x3)


def _dot_sel_rhs2(x, sel):
    x1 = x.astype(BF16)
    x2 = (x - x1.astype(F32)).astype(BF16)
    return (jnp.dot(x1, sel, preferred_element_type=F32)
            + jnp.dot(x2, sel, preferred_element_type=F32))


def _dot3(a, b):
    a1 = a.astype(BF16)
    a2 = (a - a1.astype(F32)).astype(BF16)
    b1 = b.astype(BF16)
    b2 = (b - b1.astype(F32)).astype(BF16)
    d = lambda p, q: jnp.dot(p, q, preferred_element_type=F32)
    return d(a1, b1) + d(a1, b2) + d(a2, b1)


def _sigmoid(x):
    return 0.5 * jnp.tanh(0.5 * x) + 0.5


def _silu(x):
    return x * _sigmoid(x)


def _log_sigmoid(x):
    return jnp.minimum(x, 0.0) - jnp.log(1.0 + jnp.exp(-jnp.abs(x)))


def _gelu_tanh(x):
    return 0.5 * x * (1.0 + jnp.tanh(math.sqrt(2.0 / math.pi) * (x + 0.044715 * (x * x * x))))


def _rmsnorm_rows(x, w):
    return x * lax.rsqrt(jnp.mean(x * x, axis=-1, keepdims=True) + EPS) * w


def _head_rms(o, ones_h, gain):
    ms = _dot_sel_rhs2(o * o, ones_h) * (1.0 / DV)
    return o * lax.rsqrt(ms + EPS) * gain


def _head_ln(o, ones_h, gain):
    c = o - _dot_sel_rhs2(o, ones_h) * (1.0 / DV)
    var = _dot_sel_rhs2(c * c, ones_h) * (1.0 / DV)
    return c * lax.rsqrt(var + EPS) * gain


def _hgrn_gates(xf, loglb, log1mlb, one_m_lb):
    bterm = log1mlb + _log_sigmoid(xf)
    m = jnp.maximum(loglb, bterm)
    log_f = m + jnp.log(jnp.exp(loglb - m) + jnp.exp(bterm - m))
    return log_f, one_m_lb * _sigmoid(-xf)


def _rotary(t, cos, sin_signed, first_half):
    half = RET_DK // 2
    swapped = jnp.where(first_half, pltpu.roll(t, LANES - half, 1), pltpu.roll(t, half, 1))
    return t * cos + swapped * sin_signed


def _s5_output(y_lin, u, sz, dskip, wglu):
    y = _gelu_tanh(y_lin + u * dskip)
    y = y * _sigmoid(_dot(y, wglu))
    return y * _silu(sz)


def _mix_and_project(x, mix, wout, fnw, apply_final):
    out = x + _dot(mix, wout)
    if apply_final:
        out = _rmsnorm_rows(out, fnw)
    return out


class _Slabs:
    def __init__(self, ref, base=0):
        self.ref, self.base = ref, base

    def keep(self, value):
        for c in range(self.ref.shape[0]):
            self.ref[c, pl.ds(self.base, value.shape[0]), :] = value[:, c * LANES:(c + 1) * LANES]

    def tile(self, r0, n):
        return jnp.concatenate([self.ref[c, pl.ds(self.base + r0, n), :] for c in range(self.ref.shape[0])],
                               axis=1)

    def row(self, r, n):
        return jnp.concatenate([self.ref[c, pl.ds(self.base + r, n, stride=0), :]
                                for c in range(self.ref.shape[0])], axis=1)


class _GatedStream:
    def __init__(self, q, k, v, g, s_ref, ones_kv, mask_vk, tril, term_scr, row_scr, kdim):
        self.q, self.k, self.v, self.g = q, k, v, g
        self.bc, self.kc, self.vc = (_Slabs(r) for r in row_scr)
        self.s_ref, self.ones_kv, self.mask_vk = s_ref, ones_kv, mask_vk
        self.tril, self.term_scr, self.kdim = tril, term_scr, kdim
        self.ws, self.inter, self.att, self.cross = {}, {}, {}, {}

    def cumulate(self):
        self.bc.keep(_dot_sel_lhs(self.tril, self.g()) * LOG2E)
        self.kc.keep(self.k())
        self.vc.keep(self.v())

    def _total(self, p):
        return self.bc.tile(PAIR * (p + 1) - 1, 1)

    def update_part(self, p):
        sl = slice(PAIR * p, PAIR * (p + 1))
        self.ws[p] = _dot_tn(self.v()[sl],
                             self.k()[sl] * jnp.exp2(self._total(p) - self.bc.tile(PAIR * p, PAIR)))

    def cross_part(self, p):
        r_a, r_b = PAIR * p, PAIR * p + SUB
        edge = self.bc.row(r_b - 1, SUB)
        q_b = self.q()[r_b:r_b + SUB] * jnp.exp2(self.bc.tile(r_b, SUB) - edge)
        k_a = self.kc.tile(r_a, SUB) * jnp.exp2(edge - self.bc.tile(r_a, SUB))
        lane = lax.broadcasted_iota(jnp.int32, (SUB, self.kdim), 1) // (self.kdim // HEADS)
        q_heads = jnp.concatenate([jnp.where(lane == h, q_b, 0.0) for h in range(HEADS)], axis=0)
        scores = _dot_nt(q_heads, k_a)
        res = _dot(scores, self.vc.tile(r_a, SUB))
        lane_v = lax.broadcasted_iota(jnp.int32, (SUB, BRANCH), 1) // DV
        out = jnp.where(lane_v == 0, res[0:SUB], 0.0)
        for h in range(1, HEADS):
            out = out + jnp.where(lane_v == h, res[SUB * h:SUB * (h + 1)], 0.0)
        self.cross[p] = out

    def products_part(self, j):
        hs, term_scr, r0 = SUB // 2, self.term_scr, SUB * j
        q, bc = self.q()[r0:r0 + SUB], self.bc.tile(r0, SUB)
        trow = lax.broadcasted_iota(jnp.int32, (hs, self.kdim), 0)
        neg = [jnp.where(trow >= d, 0.0, MASK_NEG) for d in range(1, hs)]
        q_lo, q_hi, b_lo, b_hi = q[0:hs], q[hs:SUB], bc[0:hs], bc[hs:SUB]
        row = j * SUB_TERM_ROWS
        for s in range(SUB):
            ks, bs = self.kc.row(r0 + s, hs), self.bc.row(r0 + s, hs)
            d = s % hs
            q_dg, b_dg = (q_lo, b_lo) if s < hs else (q_hi, b_hi)
            e_dg = b_dg - bs if d == 0 else b_dg - bs + neg[d - 1]
            term_scr[row:row + hs, :] = q_dg * ks * jnp.exp2(e_dg)
            row += hs
            if s < hs:
                term_scr[row:row + hs, :] = q_hi * ks * jnp.exp2(b_hi - bs)
                row += hs

    def matmul_part(self, p):
        sl = slice(PAIR * p, PAIR * (p + 1))
        if p == 0:
            self.state = self.s_ref[...]
        self.inter[p] = _dot_nt(self.q()[sl] * jnp.exp2(self.bc.tile(PAIR * p, PAIR)), self.state)
        self.state = self.state * jnp.exp2(self._total(p)) + self.ws[p] * self.mask_vk
        if p == T_CHUNK // PAIR - 1:
            self.s_ref[...] = self.state
        rows = slice(2 * p * SUB_TERM_ROWS, 2 * (p + 1) * SUB_TERM_ROWS)
        self.att[p] = _dot(self.term_scr[rows, :], self.ones_kv)

    def output_part(self, j):
        hs, r0, p, second = SUB // 2, SUB * j, j // 2, j % 2
        att = self.att[p][second * SUB_TERM_ROWS:(second + 1) * SUB_TERM_ROWS]
        start = self.inter[p][second * SUB:(second + 1) * SUB]
        if second:
            start = start + self.cross[p]
        acc_lo, acc_hi = start[0:hs], start[hs:SUB]
        row = 0
        for s in range(SUB):
            vs = self.vc.row(r0 + s, hs)
            if s < hs:
                acc_lo = acc_lo + att[row:row + hs] * vs
                row += hs
            acc_hi = acc_hi + att[row:row + hs] * vs
            row += hs
        return jnp.concatenate([acc_lo, acc_hi], axis=0)


def _ret_chunk(q, k, v, cos, sin_signed, s_ref, dstack, inner, kdec, cdec, mask_vk):
    lane = lax.broadcasted_iota(jnp.int32, (T_CHUNK, HEADS * RET_DK), 1)
    first_half = (lane % RET_DK) < (RET_DK // 2)
    rq = _rotary(q, cos, sin_signed, first_half)
    rk = _rotary(k, cos, sin_signed, first_half) * (RET_DK ** -0.5)
    lane_v = lax.broadcasted_iota(jnp.int32, (T_CHUNK, HEADS * DV), 1)
    kst = jnp.concatenate([jnp.where(lane // RET_DK == h, rk, 0.0) for h in range(HEADS)], axis=0)
    vst = jnp.concatenate([jnp.where(lane_v // DV == h, v, 0.0) for h in range(HEADS)], axis=0)
    p = _dot_nt(rq, kst) * dstack
    s_t = s_ref[...]
    o = _dot(p, vst) + _dot_nt(rq, s_t) * inner
    s_ref[...] = s_t * cdec + _dot_tn(v, rk * kdec) * mask_vk
    return o


def _prompt_kernel(apply_final,
                   x_ref, xn_ref, vec_ref, win_ref, wout_ref, bbd_ref, cbd_ref, wglu_ref, wup_ref, cos_ref,
                   sin_ref, dstack_ref, inner_ref, kdec_ref, cdec_ref, onesh_ref, onesg_ref, tril_ref,
                   fnw_ref,
                   y_ref, s5_ref, hgc_ref, glac_ref, retc_ref,
                   hgs_ref, glas_ref, rets_ref, proj_scr, lr_scr, hk_scr, bu_scr, mix_scr, *stream_scratch):
    nb = x_ref.shape[0]
    rows = nb * T_CHUNK
    n_slab = 2 * S5_N // LANES
    half = n_slab // 2
    step = pl.program_id(0)

    def vec(row, width):
        return vec_ref[row:row + 1, 0:width]

    def w_quarter(qtr):
        return win_ref[:, qtr * QUARTER:(qtr + 1) * QUARTER]

    def proj(col, width, rs=slice(None)):
        return proj_scr[col // QUARTER, rs, col % QUARTER:col % QUARTER + width]

    @pl.when(step == 0)
    def _init():
        s5_ref[...] = jnp.zeros_like(s5_ref)
        hgs_ref[...] = jnp.zeros_like(hgs_ref)
        glas_ref[...] = jnp.zeros_like(glas_ref)
        rets_ref[...] = jnp.zeros_like(rets_ref)
        bu_scr[...] = jnp.zeros_like(bu_scr)
        h0 = _rmsnorm_rows(x_ref[...].reshape(rows, D_MODEL), vec(V_NORM, D_MODEL)).astype(BF16)
        for qtr in range(N_QUARTER):
            proj_scr[qtr] = jnp.dot(h0, w_quarter(qtr), preferred_element_type=F32)
        lr_scr[...] = jnp.dot(h0, win_ref[:, C_LR:N_PACK], preferred_element_type=F32)

    u16 = proj(C_U, BRANCH).astype(BF16)

    def drive(c2):
        bu = jnp.dot(u16, bbd_ref[:, 2 * c2 * LANES:2 * (c2 + 1) * LANES], preferred_element_type=F32)
        for cc in range(2):
            for b in range(nb):
                bu_scr[2 * c2 + cc, b * PITCH:b * PITCH + T_CHUNK, :] = bu[b * T_CHUNK:(b + 1) * T_CHUNK,
                                                                           cc * LANES:(cc + 1) * LANES]

    log_f, hk = _hgrn_gates(proj(C_HF, BRANCH), vec(V_LOGLB, BRANCH), vec(V_LOG1MLB, BRANCH),
                            vec(V_ONEMLB, BRANCH))
    proj_scr[C_HF // QUARTER, :, C_HF % QUARTER:C_HF % QUARTER + BRANCH] = log_f
    hk_scr[...] = hk
    g_gla = _log_sigmoid(_dot3(lr_scr[...], wup_ref[...]) + vec(V_BGATE, LANES))
    lr_scr[...] = g_gla * (1.0 / GLA_TAU)
    for c2 in range(n_slab // 2):
        drive(c2)
    a_re = [jnp.broadcast_to(vec_ref[V_ARE:V_ARE + 1, c * LANES:(c + 1) * LANES], (nb, LANES)) for c in range(half)]
    a_im = [jnp.broadcast_to(vec_ref[V_AIM:V_AIM + 1, c * LANES:(c + 1) * LANES], (nb, LANES)) for c in range(half)]
    s_init = s5_ref[...]
    carry0 = tuple(s_init[:, c * LANES:(c + 1) * LANES] for c in range(n_slab))

    def scan_step(t, carry):
        new = [None] * n_slab
        for c in range(half):
            sr, si = carry[c], carry[half + c]
            br = bu_scr[c, pl.ds(t, nb, stride=PITCH), :]
            bi = bu_scr[half + c, pl.ds(t, nb, stride=PITCH), :]
            nr = a_re[c] * sr - a_im[c] * si + br
            ni = a_re[c] * si + a_im[c] * sr + bi
            bu_scr[c, pl.ds(t, nb, stride=PITCH), :] = nr
            bu_scr[half + c, pl.ds(t, nb, stride=PITCH), :] = ni
            new[c], new[half + c] = nr, ni
        return tuple(new)

    carry = lax.fori_loop(0, T_CHUNK, scan_step, carry0, unroll=8)
    s5_ref[...] = jnp.concatenate(carry, axis=1)
    s_all = jnp.concatenate([bu_scr[c] for c in range(n_slab)], axis=1)
    y_all = _dot(s_all, cbd_ref[...])
    y_lin = jnp.concatenate([y_all[b * PITCH:b * PITCH + T_CHUNK] for b in range(nb)], axis=0)
    mix_scr[:, 0:BRANCH] = _s5_output(y_lin, proj(C_U, BRANCH), proj(C_SZ, BRANCH), vec(V_DSKIP, BRANCH),
                                      wglu_ref[...])

    ones_h = onesh_ref[...]
    ones_g = onesg_ref[...]
    mask_h = ones_h.astype(F32)
    mask_g = jnp.transpose(ones_g.astype(F32))
    tril = tril_ref[...]

    def rows_of(bp, odd):
        return pl.ds(pl.multiple_of((2 * bp + odd) * T_CHUNK, T_CHUNK), T_CHUNK)

    def hgrn_stream(bp, odd, term, row_scr):
        rs = rows_of(bp, odd)
        return _GatedStream(lambda: proj(C_HQ, BRANCH, rs), lambda: hk_scr[rs, :], lambda: proj(C_HI, BRANCH, rs),
                            lambda: proj(C_HF, BRANCH, rs), hgs_ref.at[bp, odd], ones_h, mask_h, tril, term,
                            row_scr, HEADS * HG_DK)

    def gla_stream(bp, odd, term, row_scr):
        rs = rows_of(bp, odd)
        return _GatedStream(lambda: proj(C_GQ, LANES, rs) * (GLA_DK ** -0.5), lambda: proj(C_GK, LANES, rs),
                            lambda: proj(C_GV, BRANCH, rs), lambda: lr_scr[rs, :],
                            glas_ref.at[bp, odd], ones_g, mask_g, tril, term, row_scr, HEADS * GLA_DK)

    n_sub = T_CHUNK // SUB

    def retention(bp, odd):
        rs = rows_of(bp, odd)
        mix_scr[rs, 3 * BRANCH:4 * BRANCH] = _ret_chunk(
            proj(C_RQ, LANES, rs), proj(C_RK, LANES, rs), proj(C_RV, BRANCH, rs), cos_ref[...],
            sin_ref[...], rets_ref.at[bp, odd], dstack_ref[...], inner_ref[...], kdec_ref[...],
            cdec_ref[...], mask_g)

    def per_group(grp, _):
        st, slots = [], []
        for i in range(GROUP_ROWS):
            bp, odd = (GROUP_ROWS // 2) * grp + i // 2, i % 2
            scr = stream_scratch[8 * i:8 * (i + 1)]
            st += [hgrn_stream(bp, odd, scr[0], scr[2:5]), gla_stream(bp, odd, scr[1], scr[5:8])]
            slots += [(bp, odd, BRANCH), (bp, odd, 2 * BRANCH)]

        def emit(i, j):
            bp, odd, col = slots[i]
            r0 = pl.multiple_of((2 * bp + odd) * T_CHUNK + SUB * j, SUB)
            mix_scr[pl.ds(r0, SUB), col:col + BRANCH] = st[i].output_part(j)

        for s_ in st:
            s_.cumulate()
        for i in range(GROUP_ROWS):
            retention((GROUP_ROWS // 2) * grp + i // 2, i % 2)
        for s_ in st:
            for p in range(n_sub // 2):
                s_.update_part(p)
        for s_ in st:
            for j in range(n_sub):
                s_.products_part(j)
        for s_ in st:
            for p in range(n_sub // 2):
                s_.cross_part(p)
        for s_ in st:
            for p in range(n_sub // 2):
                s_.matmul_part(p)
        for i in range(len(st)):
            for j in range(n_sub):
                emit(i, j)
        return 0

    lax.fori_loop(0, nb // GROUP_ROWS, per_group, 0)

    o_hg = _head_rms(mix_scr[:, BRANCH:2 * BRANCH], ones_h, vec(V_HGN, BRANCH))
    mix_scr[:, BRANCH:2 * BRANCH] = o_hg * _silu(proj(C_HZ, BRANCH))
    o_gla = _head_rms(mix_scr[:, 2 * BRANCH:3 * BRANCH], ones_h, vec(V_GLAN, BRANCH))
    mix_scr[:, 2 * BRANCH:3 * BRANCH] = o_gla * _silu(proj(C_GZ, BRANCH))
    o_ret = _head_ln(mix_scr[:, 3 * BRANCH:4 * BRANCH], ones_h, vec(V_RETN, BRANCH))
    mix_scr[:, 3 * BRANCH:4 * BRANCH] = o_ret * _silu(proj(C_RZ, BRANCH))
    hn = _rmsnorm_rows(xn_ref[...].reshape(rows, D_MODEL), vec(V_NORM, D_MODEL)).astype(BF16)
    for qtr in range(N_QUARTER):
        proj_scr[qtr] = jnp.dot(hn, w_quarter(qtr), preferred_element_type=F32)
    lr_scr[...] = jnp.dot(hn, win_ref[:, C_LR:N_PACK], preferred_element_type=F32)
    out = _mix_and_project(x_ref[...].reshape(rows, D_MODEL), mix_scr[...], wout_ref[...], fnw_ref[...],
                           apply_final)
    y_ref[...] = out.reshape(nb, T_CHUNK, D_MODEL)

    @pl.when(step == pl.num_programs(0) - 1)
    def _emit_states():
        for b in range(nb):
            for state, compact in ((hgs_ref, hgc_ref), (glas_ref, glac_ref), (rets_ref, retc_ref)):
                dk = compact.shape[3]
                for h in range(HEADS):
                    compact[b, h] = state[b // 2, b % 2, h * DV:(h + 1) * DV, h * dk:(h + 1) * dk]


def _dot_sel_lhs2(sel, x):
    x1 = x.astype(BF16)
    x2 = (x - x1.astype(F32)).astype(BF16)
    return (jnp.dot(sel, x1, preferred_element_type=F32) + jnp.dot(sel, x2, preferred_element_type=F32))


def _head_rms_t(o, ones_h, gain):
    ms = _dot_sel_lhs2(ones_h, o * o) * (1.0 / DV)
    return o * lax.rsqrt(ms + EPS) * gain


def _head_ln_t(o, ones_h, gain):
    c = o - _dot_sel_lhs2(ones_h, o) * (1.0 / DV)
    var = _dot_sel_lhs2(ones_h, c * c) * (1.0 / DV)
    return c * lax.rsqrt(var + EPS) * gain


def _rotary_t(t, cos, sin_signed, first_half):
    half = RET_DK // 2
    swapped = jnp.where(first_half, pltpu.roll(t, LANES - half, 0), pltpu.roll(t, half, 0))
    return t * cos + swapped * sin_signed


def _sample_kernel(x_ref, vec_ref, vect_ref, wt_ref, wout_ref, bbdt_ref, cbdt_ref, wglut_ref, wupt_ref, fnw_ref,
                   cos_ref, sin_ref, dret_ref, onesh_ref,
                   s5re_ref, s5im_ref, hg_ref, gla_ref, ret_ref,
                   y_ref, s5re_o, s5im_o, hg_o, gla_o, ret_o,
                   xs_scr, pt_scr, hk_scr, ot_scr, mixt_scr):
    layer, head = pl.program_id(0), pl.program_id(1)

    def col(row, height):
        return vect_ref[0:height, row:row + 1]

    last_layer, last_head = pl.num_programs(0) - 1, pl.num_programs(1) - 1

    @pl.when((layer == 0) & (head == 0))
    def _load_x():
        xs_scr[...] = x_ref[...]

    @pl.when(head == 0)
    def _dense():
        hh = _rmsnorm_rows(xs_scr[...], vec_ref[V_NORM:V_NORM + 1, :]).astype(BF16)
        pt_scr[...] = lax.dot_general(wt_ref[...], hh, (((1,), (1,)), ((), ())), preferred_element_type=F32)

        u = pt_scr[C_U:C_U + BRANCH, :]
        bu = _dot3(bbdt_ref[...], u)
        a_re, a_im = col(V_ARE, S5_N), col(V_AIM, S5_N)
        s0r, s0i = s5re_ref[...], s5im_ref[...]
        s_re = a_re * s0r - a_im * s0i + bu[0:S5_N]
        s_im = a_re * s0i + a_im * s0r + bu[S5_N:2 * S5_N]
        s5re_o[...] = s_re
        s5im_o[...] = s_im
        y = _gelu_tanh(_dot3(cbdt_ref[...], jnp.concatenate([s_re, s_im], axis=0)) + u * col(V_DSKIP, BRANCH))
        y = y * _sigmoid(jnp.dot(wglut_ref[...], y.astype(BF16), preferred_element_type=F32))
        mixt_scr[0:BRANCH, :] = y * _silu(pt_scr[C_SZ:C_SZ + BRANCH, :])

        log_f, hk = _hgrn_gates(pt_scr[C_HF:C_HF + BRANCH, :], col(V_LOGLB, BRANCH), col(V_LOG1MLB, BRANCH),
                                col(V_ONEMLB, BRANCH))
        pt_scr[C_HF:C_HF + BRANCH, :] = jnp.exp(log_f)
        hk_scr[...] = hk
        g_gla = _log_sigmoid(_dot3(wupt_ref[...], pt_scr[C_LR:C_LR + LANES, :]) + col(V_BGATE, LANES))
        pt_scr[C_LR:C_LR + LANES, :] = jnp.exp(g_gla * (1.0 / GLA_TAU))
        pt_scr[C_GQ:C_GQ + LANES, :] = pt_scr[C_GQ:C_GQ + LANES, :] * (GLA_DK ** -0.5)
        row = lax.broadcasted_iota(jnp.int32, (HEADS * RET_DK, LANES), 0)
        first_half = (row % RET_DK) < (RET_DK // 2)
        pt_scr[C_RQ:C_RQ + LANES, :] = _rotary_t(pt_scr[C_RQ:C_RQ + LANES, :], cos_ref[...], sin_ref[...],
                                                 first_half)
        pt_scr[C_RK:C_RK + LANES, :] = _rotary_t(pt_scr[C_RK:C_RK + LANES, :], cos_ref[...], sin_ref[...],
                                                 first_half) * (RET_DK ** -0.5)

    def head_update(s0_ref, s_out_ref, dk, dec_ref, dec_row, key_ref, key_row, q_row, v_row, out_row):
        vt = pt_scr[pl.ds(pl.multiple_of(v_row + head * DV, DV), DV), :]

        def feature(kk, acc):
            r = head * dk + kk
            bcast = lambda ref, r0: jnp.broadcast_to(ref[pl.ds(r0 + r, 1), :], (DV, LANES))
            s_new = s0_ref[kk] * bcast(dec_ref, dec_row) + bcast(key_ref, key_row) * vt
            s_out_ref[kk] = s_new
            return acc + bcast(pt_scr, q_row) * s_new

        acc = lax.fori_loop(0, dk, feature, jnp.zeros((DV, LANES), F32), unroll=4)
        ot_scr[pl.ds(pl.multiple_of(out_row + head * DV, DV), DV), :] = acc

    head_update(hg_ref, hg_o, HG_DK, pt_scr, C_HF, hk_scr, 0, C_HQ, C_HI, 0)
    head_update(gla_ref, gla_o, GLA_DK, pt_scr, C_LR, pt_scr, C_GK, C_GQ, C_GV, BRANCH)
    head_update(ret_ref, ret_o, RET_DK, dret_ref, 0, pt_scr, C_RK, C_RQ, C_RV, 2 * BRANCH)

    @pl.when(head == last_head)
    def _finish():
        ones_h = onesh_ref[...]
        o_hg = _head_rms_t(ot_scr[0:BRANCH, :], ones_h, col(V_HGN, BRANCH))
        mixt_scr[BRANCH:2 * BRANCH, :] = o_hg * _silu(pt_scr[C_HZ:C_HZ + BRANCH, :])
        o_gla = _head_rms_t(ot_scr[BRANCH:2 * BRANCH, :], ones_h, col(V_GLAN, BRANCH))
        mixt_scr[2 * BRANCH:3 * BRANCH, :] = o_gla * _silu(pt_scr[C_GZ:C_GZ + BRANCH, :])
        o_ret = _head_ln_t(ot_scr[2 * BRANCH:3 * BRANCH, :], ones_h, col(V_RETN, BRANCH))
        mixt_scr[3 * BRANCH:4 * BRANCH, :] = o_ret * _silu(pt_scr[C_RZ:C_RZ + BRANCH, :])
        out = xs_scr[...] + lax.dot_general(mixt_scr[...].astype(BF16), wout_ref[...], (((0,), (0,)), ((), ())),
                                            preferred_element_type=F32)
        xs_scr[...] = out

        @pl.when(layer == last_layer)
        def _emit():
            y_ref[...] = _rmsnorm_rows(out, fnw_ref[...])


def _ret_log_gamma():
    return jnp.log1p(-jnp.exp2(-5.0 - jnp.arange(HEADS, dtype=F32)))


def _constants():
    ones_h = (np.arange(BRANCH)[:, None] // DV == np.arange(BRANCH)[None, :] // DV)
    ones_g = (np.arange(HEADS * GLA_DK)[:, None] // GLA_DK == np.arange(BRANCH)[None, :] // DV)
    r = np.arange(T_CHUNK)
    tril = (r[:, None] // PAIR == r[None, :] // PAIR) & (r[None, :] <= r[:, None])
    same_group_b = np.arange(BRANCH)[:, None] // S5_CH == np.arange(S5_N)[None, :] // S5_STATE
    as_bf16 = lambda m: jnp.asarray(m.astype(np.float32), dtype=BF16)
    return dict(ones_h=as_bf16(ones_h), ones_g=as_bf16(ones_g), tril=as_bf16(tril),
                s5_mask=jnp.asarray(same_group_b.astype(np.float32)))


def _ret_tables():
    lg = _ret_log_gamma()
    idx = jnp.arange(T_CHUNK, dtype=F32)
    rel = idx[:, None] - idx[None, :]
    causal = rel >= 0
    decay = jnp.where(causal[None], jnp.exp(jnp.where(causal, rel, 0.0)[None] * lg[:, None, None]), 0.0)
    dstack = jnp.transpose(decay, (1, 0, 2)).reshape(T_CHUNK, HEADS * T_CHUNK)
    inner = jnp.repeat(jnp.exp((idx[:, None] + 1.0) * lg[None, :]), DV, axis=1)
    kdec = jnp.repeat(jnp.exp((T_CHUNK - 1.0 - idx[:, None]) * lg[None, :]), RET_DK, axis=1)
    cdec = jnp.repeat(jnp.exp(T_CHUNK * lg)[None, :], RET_DK, axis=1)
    dret = jnp.broadcast_to(jnp.repeat(jnp.exp(lg), RET_DK)[:, None], (HEADS * RET_DK, LANES))
    return dstack, inner, kdec, cdec, dret


def _rope_tables(pos):
    half = RET_DK // 2
    inv = ROPE_BASE ** (-jnp.arange(half, dtype=F32) / half)
    ang = pos.astype(F32)[:, None] * inv[None, :]
    cos, sin = jnp.cos(ang), jnp.sin(ang)
    cos_t = jnp.tile(jnp.concatenate([cos, cos], axis=1), (1, HEADS))
    sin_t = jnp.tile(jnp.concatenate([-sin, sin], axis=1), (1, HEADS))
    return cos_t, sin_t


def _pack_w_in_t(w):
    wt = jnp.swapaxes(w, 1, 2)
    offs = np.cumsum([0, 256, 256, 256, 256, 256, 256, 128, 128, 256, 16, 256, 128, 128, 256, 256])
    seg = lambda i: wt[:, int(offs[i]):int(offs[i + 1]), :]
    order = [0, 1, 2, 3, 4, 5, 6, 7, 8, 10, 11, 12, 13, 14]
    pad = jnp.zeros((w.shape[0], LANES - GLA_LOWRANK, w.shape[1]), w.dtype)
    return jnp.concatenate([seg(i) for i in order] + [seg(9), pad], axis=1).astype(BF16)


def _s5_discretize(lam_re, lam_im, log_step, b_re, b_im, c_re, c_im, mask):
    lr, li = lam_re.astype(F32), lam_im.astype(F32)
    step = jnp.exp(log_step.astype(F32))[..., None]
    mag = jnp.exp(lr * step)
    ab_re = mag * jnp.cos(li * step)
    ab_im = mag * jnp.sin(li * step)
    den = lr * lr + li * li
    nr = ab_re - 1.0
    f_re = (nr * lr + ab_im * li) / den
    f_im = (ab_im * lr - nr * li) / den
    br, bi = b_re.astype(F32), b_im.astype(F32)
    bb_re = f_re[..., None] * br - f_im[..., None] * bi
    bb_im = f_re[..., None] * bi + f_im[..., None] * br
    nl = lr.shape[0]

    def drive(bb):
        rows = jnp.transpose(bb, (0, 1, 3, 2)).reshape(nl, BRANCH, S5_STATE)
        return jnp.tile(rows, (1, 1, S5_GROUPS)) * mask

    def readout(cc):
        rows = cc.astype(F32).reshape(nl, BRANCH, S5_STATE)
        return jnp.tile(rows, (1, 1, S5_GROUPS)) * mask

    bbd = jnp.concatenate([drive(bb_re), drive(bb_im)], axis=2)
    cbd_t = jnp.concatenate([readout(c_re), -readout(c_im)], axis=2)
    return bbd, cbd_t, ab_re.reshape(nl, S5_N), ab_im.reshape(nl, S5_N)


def _full(shape):
    return pl.BlockSpec(shape, lambda *_: (0,) * len(shape), pipeline_mode=pl.Buffered(1))


def _of_layer(arr, layer):
    nd = arr.ndim - 1
    return pl.BlockSpec((None,) + arr.shape[1:], lambda *_: (layer,) + (0,) * nd,
                        pipeline_mode=pl.Buffered(1))


def _prompt_layer(x, layer, p, consts, tabs, rope, apply_final):
    nb, seq, _ = x.shape
    n_steps = seq // T_CHUNK
    rows = nb * T_CHUNK
    dstack, inner, kdec, cdec, _ = tabs
    cos_t, sin_t = rope
    per_layer = [p['vec'], p['w_pack'], p['w_out'], p['bbd'].astype(BF16), p['cbd'], p['w_glu'], p['w_up']]
    shared = [dstack, inner, kdec, cdec, consts['ones_h'], consts['ones_g'], consts['tril'],
              p['final_norm']]
    in_specs = [pl.BlockSpec((nb, T_CHUNK, D_MODEL), lambda i: (0, i, 0)),
                pl.BlockSpec((nb, T_CHUNK, D_MODEL), lambda i: (0, jnp.minimum(i + 1, n_steps - 1), 0))]
    in_specs += [_of_layer(a, layer) for a in per_layer]
    in_specs += [pl.BlockSpec((T_CHUNK, LANES), lambda i: (i, 0)), pl.BlockSpec((T_CHUNK, LANES), lambda i: (i, 0))]
    in_specs += [_full(a.shape) for a in shared]
    out_shape = (jax.ShapeDtypeStruct((nb, seq, D_MODEL), F32),
                 jax.ShapeDtypeStruct((nb, 2 * S5_N), F32),
                 jax.ShapeDtypeStruct((nb, HEADS, DV, HG_DK), F32),
                 jax.ShapeDtypeStruct((nb, HEADS, DV, GLA_DK), F32),
                 jax.ShapeDtypeStruct((nb, HEADS, DV, RET_DK), F32))
    out_specs = (pl.BlockSpec((nb, T_CHUNK, D_MODEL), lambda i: (0, i, 0)),
                 pl.BlockSpec((nb, 2 * S5_N), lambda i: (0, 0)),
                 pl.BlockSpec((nb, HEADS, DV, HG_DK), lambda i: (0, 0, 0, 0)),
                 pl.BlockSpec((nb, HEADS, DV, GLA_DK), lambda i: (0, 0, 0, 0)),
                 pl.BlockSpec((nb, HEADS, DV, RET_DK), lambda i: (0, 0, 0, 0)))
    scratch = [pltpu.VMEM((nb // 2, 2, BRANCH, HEADS * HG_DK), F32),
               pltpu.VMEM((nb // 2, 2, BRANCH, HEADS * GLA_DK), F32),
               pltpu.VMEM((nb // 2, 2, BRANCH, HEADS * RET_DK), F32),
               pltpu.VMEM((N_QUARTER, rows, QUARTER), F32), pltpu.VMEM((rows, LANES), F32),
               pltpu.VMEM((rows, BRANCH), F32),
               pltpu.VMEM((2 * S5_N // LANES, nb * PITCH, LANES), F32),
               pltpu.VMEM((rows, D_MODEL), F32),
               ] + GROUP_ROWS * (
                   [pltpu.VMEM((TERM_ROWS, kd), F32) for kd in (HEADS * HG_DK, HEADS * GLA_DK)]
                   + [pltpu.VMEM((w // LANES, T_CHUNK, LANES), F32) for kd in (HEADS * HG_DK, HEADS * GLA_DK)
                      for w in (kd, kd, BRANCH)])
    y, s5, hgs, glas, rets = pl.pallas_call(
        functools.partial(_prompt_kernel, apply_final),
        grid=(n_steps,), in_specs=in_specs, out_specs=out_specs, out_shape=out_shape,
        scratch_shapes=scratch, name='prompt_layer',
        compiler_params=pltpu.CompilerParams(dimension_semantics=('arbitrary',),
                                             vmem_limit_bytes=VMEM_LIMIT),
    )(x, x, *per_layer, cos_t, sin_t, *shared)
    s5 = s5.reshape(nb, 2, S5_GROUPS, S5_STATE)

    to_kv = lambda st: jnp.swapaxes(st, 2, 3)
    return y, (s5[:, 0], s5[:, 1], to_kv(hgs), to_kv(glas), to_kv(rets))


def _sample_step(x, states, p, consts, tabs, rope):
    nb = x.shape[0]
    depth = p['w_t'].shape[0]
    s5re, s5im, hg, gla, ret = states
    to_lanes = lambda s: jnp.moveaxis(s, 1, -1)
    s5re_t = to_lanes(s5re).reshape(depth, S5_N, nb)
    s5im_t = to_lanes(s5im).reshape(depth, S5_N, nb)
    hg_t, gla_t, ret_t = to_lanes(hg), to_lanes(gla), to_lanes(ret)
    cos_t, sin_t = rope
    col = lambda a: jnp.swapaxes(a, -1, -2)
    per_layer = [p['vec'], col(p['vec']), p['w_t'], p['w_out'], col(p['bbd']), p['cbd_t'], col(p['w_glu']),
                 col(p['w_up'])]
    shared = [p['final_norm'], col(cos_t), col(sin_t), tabs[4], consts['ones_h']]
    layer_spec = lambda a: pl.BlockSpec((None,) + a.shape[1:], lambda l, h: (l,) + (0,) * (a.ndim - 1))
    head_spec = lambda a: pl.BlockSpec((None, None) + a.shape[2:], lambda l, h: (l, h) + (0,) * (a.ndim - 2))
    state_specs = [layer_spec(s5re_t), layer_spec(s5im_t), head_spec(hg_t), head_spec(gla_t), head_spec(ret_t)]
    in_specs = ([_full(x.shape)] + [layer_spec(a) for a in per_layer] + [_full(a.shape) for a in shared]
                + state_specs)
    state_arrays = [s5re_t, s5im_t, hg_t, gla_t, ret_t]
    out_shape = tuple([jax.ShapeDtypeStruct(x.shape, F32)]
                      + [jax.ShapeDtypeStruct(a.shape, F32) for a in state_arrays])
    out_specs = tuple([pl.BlockSpec(x.shape, lambda l, h: (0, 0))] + state_specs)
    scratch = [pltpu.VMEM((nb, D_MODEL), F32), pltpu.VMEM((N_PACK, nb), F32), pltpu.VMEM((BRANCH, nb), F32),
               pltpu.VMEM((3 * BRANCH, nb), F32), pltpu.VMEM((4 * BRANCH, nb), F32)]
    outs = pl.pallas_call(
        _sample_kernel, grid=(depth, HEADS), in_specs=in_specs, out_specs=out_specs, out_shape=out_shape,
        scratch_shapes=scratch, name='sample_step',
        compiler_params=pltpu.CompilerParams(dimension_semantics=('arbitrary', 'arbitrary'),
                                             vmem_limit_bytes=VMEM_LIMIT),
    )(x, *per_layer, *shared, *state_arrays)
    from_lanes = lambda s: jnp.moveaxis(s, -1, 1)
    new = (from_lanes(outs[1].reshape(depth, S5_GROUPS, S5_STATE, nb)),
           from_lanes(outs[2].reshape(depth, S5_GROUPS, S5_STATE, nb)),
           from_lanes(outs[3]), from_lanes(outs[4]), from_lanes(outs[5]))
    return outs[0], new


def kernel(x_prompt, x_sample, state_s5_re, state_s5_im, state_hgrn, state_gla, state_ret, norm_w, final_norm_w, w_in, w_out, s5_lam_re, s5_lam_im, s5_log_step, s5_b_re, s5_b_im, s5_c_re, s5_c_im, s5_d, s5_w_glu, hgrn_lb_logits, hgrn_norm_w, gla_w_gate_up, gla_b_gate, gla_norm_w, ret_norm_w):
    depth = w_in.shape[0]
    seq = x_prompt.shape[1]
    consts = _constants()
    tabs = _ret_tables()
    rope_p = _rope_tables(jnp.arange(seq))
    rope_s = _rope_tables(PAST_LEN + jnp.arange(1))
    lb = jnp.cumsum(jax.nn.softmax(hgrn_lb_logits.astype(F32), axis=0), axis=0)
    lb = lb - lb[0:1]
    bbd, cbd_t, a_re, a_im = _s5_discretize(s5_lam_re, s5_lam_im, s5_log_step, s5_b_re, s5_b_im, s5_c_re,
                                            s5_c_im, consts['s5_mask'])
    w_t = lax.optimization_barrier(_pack_w_in_t(w_in))
    w_up = jnp.zeros((depth, LANES, HEADS * GLA_DK), F32).at[:, :GLA_LOWRANK].set(gla_w_gate_up.astype(F32))
    vectors = [norm_w, a_re, a_im, s5_d, jnp.log(lb), jnp.log1p(-lb), 1.0 - lb, hgrn_norm_w, gla_b_gate,
               gla_norm_w, ret_norm_w]
    rows = [jnp.pad(v.astype(F32), ((0, 0), (0, D_MODEL - v.shape[1]))) for v in vectors]
    rows += [jnp.zeros((depth, D_MODEL), F32)] * (V_ROWS - len(rows))
    p = dict(vec=jnp.stack(rows, axis=1),
             w_pack=jnp.swapaxes(w_t, 1, 2), w_t=w_t, w_out=w_out.astype(BF16),
             bbd=bbd, cbd=jnp.swapaxes(cbd_t, 1, 2).astype(BF16), cbd_t=cbd_t,
             w_glu=s5_w_glu.astype(BF16), w_up=w_up, final_norm=final_norm_w[None, :].astype(F32))

    xp = x_prompt
    new_p = ([], [], [], [], [])
    for l in range(depth):
        xp, st_p = _prompt_layer(xp, l, p, consts, tabs, rope_p, l == depth - 1)
        for i in range(5):
            new_p[i].append(st_p[i])
    xs, new_s = _sample_step(x_sample.reshape(x_sample.shape[0], D_MODEL),
                             (state_s5_re, state_s5_im, state_hgrn, state_gla, state_ret),
                             p, consts, tabs, rope_s)
    return (xp, xs.reshape(x_sample.shape),
            jnp.stack(new_p[0]), jnp.stack(new_p[1]), jnp.stack(new_p[2]), jnp.stack(new_p[3]),
            jnp.stack(new_p[4])) + new_s
```

```python
import functools
import math

import numpy as np
import jax
import jax.numpy as jnp
from jax import lax
from jax.experimental import pallas as pl
from jax.experimental.pallas import tpu as pltpu

F32 = jnp.float32
BF16 = jnp.bfloat16

D_MODEL = 1024
BRANCH = 256
S5_CH = 16
S5_GROUPS = 16
S5_STATE = 64
S5_N = S5_GROUPS * S5_STATE
HEADS = 4
HG_DK = 64
GLA_DK = 32
RET_DK = 32
DV = 64
GLA_LOWRANK = 16
GLA_TAU = 16.0
ROPE_BASE = 10000.0
PAST_LEN = 16384
EPS = 1e-6
SUB = 16
PAIR = 2 * SUB
GROUP_ROWS = 4

LANES = 128
T_CHUNK = 64
PITCH = T_CHUNK + 8
SUB_TERM_ROWS = SUB * (SUB + SUB // 2) // 2
TERM_ROWS = (T_CHUNK // SUB) * SUB_TERM_ROWS
LOG2E = math.log2(math.e)
MASK_NEG = -1e30
VMEM_LIMIT = 60 * 1024 * 1024

C_U, C_SZ, C_HQ, C_HF, C_HI, C_HZ = 0, 256, 512, 768, 1024, 1280
C_GQ, C_GK, C_GV, C_GZ = 1536, 1664, 1792, 2048
C_RQ, C_RK, C_RV, C_RZ = 2304, 2432, 2560, 2816
C_LR = 3072
N_PACK = 3200
N_QUARTER = 4
QUARTER = C_LR // N_QUARTER


def _dot(a, b):
    return jnp.dot(a.astype(BF16), b.astype(BF16), preferred_element_type=F32)


def _dot_nt(a, b):
    return lax.dot_general(a.astype(BF16), b.astype(BF16), (((1,), (1,)), ((), ())),
                           preferred_element_type=F32)


def _dot_tn(a, b):
    return lax.dot_general(a.astype(BF16), b.astype(BF16), (((0,), (0,)), ((), ())),
                           preferred_element_type=F32)


def _split3(x):
    x1 = x.astype(BF16)
    r1 = x - x1.astype(F32)
    x2 = r1.astype(BF16)
    x3 = (r1 - x2.astype(F32)).astype(BF16)
    return x1, x2, x3


def _dot_sel_lhs(sel, x):
    x1, x2, x3 = _split3(x)
    d = lambda p: jnp.dot(sel, p, preferred_element_type=F32)
    return d(x1) + d(x2) + d(x3)


def _dot_sel_rhs2(x, sel):
    x1 = x.astype(BF16)
    x2 = (x - x1.astype(F32)).astype(BF16)
    return (jnp.dot(x1, sel, preferred_element_type=F32)
            + jnp.dot(x2, sel, preferred_element_type=F32))


def _dot3(a, b):
    a1 = a.astype(BF16)
    a2 = (a - a1.astype(F32)).astype(BF16)
    b1 = b.astype(BF16)
    b2 = (b - b1.astype(F32)).astype(BF16)
    d = lambda p, q: jnp.dot(p, q, preferred_element_type=F32)
    return d(a1, b1) + d(a1, b2) + d(a2, b1)


def _sigmoid(x):
    return 0.5 * jnp.tanh(0.5 * x) + 0.5


def _silu(x):
    return x * _sigmoid(x)


def _log_sigmoid(x):
    return jnp.minimum(x, 0.0) - jnp.log(1.0 + jnp.exp(-jnp.abs(x)))


def _gelu_tanh(x):
    return 0.5 * x * (1.0 + jnp.tanh(math.sqrt(2.0 / math.pi) * (x + 0.044715 * (x * x * x))))


def _rmsnorm_rows(x, w):
    return x * lax.rsqrt(jnp.mean(x * x, axis=-1, keepdims=True) + EPS) * w


def _head_rms(o, ones_h, gain):
    ms = _dot_sel_rhs2(o * o, ones_h) * (1.0 / DV)
    return o * lax.rsqrt(ms + EPS) * gain


def _head_ln(o, ones_h, gain):
    c = o - _dot_sel_rhs2(o, ones_h) * (1.0 / DV)
    var = _dot_sel_rhs2(c * c, ones_h) * (1.0 / DV)
    return c * lax.rsqrt(var + EPS) * gain


def _hgrn_gates(xf, loglb, log1mlb, one_m_lb):
    bterm = log1mlb + _log_sigmoid(xf)
    m = jnp.maximum(loglb, bterm)
    log_f = m + jnp.log(jnp.exp(loglb - m) + jnp.exp(bterm - m))
    return log_f, one_m_lb * _sigmoid(-xf)


def _rotary(t, cos, sin_signed, first_half):
    half = RET_DK // 2
    swapped = jnp.where(first_half, pltpu.roll(t, LANES - half, 1), pltpu.roll(t, half, 1))
    return t * cos + swapped * sin_signed


def _s5_output(y_lin, u, sz, dskip, wglu):
    y = _gelu_tanh(y_lin + u * dskip)
    y = y * _sigmoid(_dot(y, wglu))
    return y * _silu(sz)


def _mix_and_project(x, mix, wout, fnw, apply_final):
    out = x + _dot(mix, wout)
    if apply_final:
        out = _rmsnorm_rows(out, fnw)
    return out


class _Slabs:
    def __init__(self, ref, base=0):
        self.ref, self.base = ref, base

    def keep(self, value):
        for c in range(self.ref.shape[0]):
            self.ref[c, pl.ds(self.base, value.shape[0]), :] = value[:, c * LANES:(c + 1) * LANES]

    def tile(self, r0, n):
        return jnp.concatenate([self.ref[c, pl.ds(self.base + r0, n), :] for c in range(self.ref.shape[0])],
                               axis=1)

    def row(self, r, n):
        return jnp.concatenate([self.ref[c, pl.ds(self.base + r, n, stride=0), :]
                                for c in range(self.ref.shape[0])], axis=1)


class _GatedStream:
    def __init__(self, q, k, v, g, s_ref, ones_kv, mask_vk, tril, term_scr, row_scr, kdim):
        self.q, self.k, self.v, self.g = q, k, v, g
        self.bc, self.kc, self.vc = (_Slabs(r) for r in row_scr)
        self.s_ref, self.ones_kv, self.mask_vk = s_ref, ones_kv, mask_vk
        self.tril, self.term_scr, self.kdim = tril, term_scr, kdim
        self.ws, self.inter, self.att, self.cross = {}, {}, {}, {}

    def cumulate(self):
        self.bc.keep(_dot_sel_lhs(self.tril, self.g()) * LOG2E)
        self.kc.keep(self.k())
        self.vc.keep(self.v())

    def _total(self, p):
        return self.bc.tile(PAIR * (p + 1) - 1, 1)

    def update_part(self, p):
        sl = slice(PAIR * p, PAIR * (p + 1))
        self.ws[p] = _dot_tn(self.v()[sl],
                             self.k()[sl] * jnp.exp2(self._total(p) - self.bc.tile(PAIR * p, PAIR)))

    def cross_part(self, p):
        r_a, r_b = PAIR * p, PAIR * p + SUB
        edge = self.bc.row(r_b - 1, SUB)
        q_b = self.q()[r_b:r_b + SUB] * jnp.exp2(self.bc.tile(r_b, SUB) - edge)
        k_a = self.kc.tile(r_a, SUB) * jnp.exp2(edge - self.bc.tile(r_a, SUB))
        lane = lax.broadcasted_iota(jnp.int32, (SUB, self.kdim), 1) // (self.kdim // HEADS)
        q_heads = jnp.concatenate([jnp.where(lane == h, q_b, 0.0) for h in range(HEADS)], axis=0)
        scores = _dot_nt(q_heads, k_a)
        res = _dot(scores, self.vc.tile(r_a, SUB))
        lane_v = lax.broadcasted_iota(jnp.int32, (SUB, BRANCH), 1) // DV
        out = jnp.where(lane_v == 0, res[0:SUB], 0.0)
        for h in range(1, HEADS):
            out = out + jnp.where(lane_v == h, res[SUB * h:SUB * (h + 1)], 0.0)
        self.cross[p] = out

    def products_part(self, j):
        hs, term_scr, r0 = SUB // 2, self.term_scr, SUB * j
        q, bc = self.q()[r0:r0 + SUB], self.bc.tile(r0, SUB)
        trow = lax.broadcasted_iota(jnp.int32, (hs, self.kdim), 0)
        neg = [jnp.where(trow >= d, 0.0, MASK_NEG) for d in range(1, hs)]
        q_lo, q_hi, b_lo, b_hi = q[0:hs], q[hs:SUB], bc[0:hs], bc[hs:SUB]
        row = j * SUB_TERM_ROWS
        for s in range(SUB):
            ks, bs = self.kc.row(r0 + s, hs), self.bc.row(r0 + s, hs)
            d = s % hs
            q_dg, b_dg = (q_lo, b_lo) if s < hs else (q_hi, b_hi)
            e_dg = b_dg - bs if d == 0 else b_dg - bs + neg[d - 1]
            term_scr[row:row + hs, :] = q_dg * ks * jnp.exp2(e_dg)
            row += hs
            if s < hs:
                term_scr[row:row + hs, :] = q_hi * ks * jnp.exp2(b_hi - bs)
                row += hs

    def matmul_part(self, p):
        sl = slice(PAIR * p, PAIR * (p + 1))
        if p == 0:
            self.state = self.s_ref[...]
        self.inter[p] = _dot_nt(self.q()[sl] * jnp.exp2(self.bc.tile(PAIR * p, PAIR)), self.state)
        self.state = self.state * jnp.exp2(self._total(p)) + self.ws[p] * self.mask_vk
        if p == T_CHUNK // PAIR - 1:
            self.s_ref[...] = self.state
        rows = slice(2 * p * SUB_TERM_ROWS, 2 * (p + 1) * SUB_TERM_ROWS)
        self.att[p] = _dot(self.term_scr[rows, :], self.ones_kv)

    def output_part(self, j):
        hs, r0, p, second = SUB // 2, SUB * j, j // 2, j % 2
        att = self.att[p][second * SUB_TERM_ROWS:(second + 1) * SUB_TERM_ROWS]
        start = self.inter[p][second * SUB:(second + 1) * SUB]
        if second:
            start = start + self.cross[p]
        acc_lo, acc_hi = start[0:hs], start[hs:SUB]
        row = 0
        for s in range(SUB):
            vs = self.vc.row(r0 + s, hs)
            if s < hs:
                acc_lo = acc_lo + att[row:row + hs] * vs
                row += hs
            acc_hi = acc_hi + att[row:row + hs] * vs
            row += hs
        return jnp.concatenate([acc_lo, acc_hi], axis=0)


def _ret_chunk(q, k, v, cos, sin_signed, s_ref, dstack, inner, kdec, cdec, mask_vk):
    lane = lax.broadcasted_iota(jnp.int32, (T_CHUNK, HEADS * RET_DK), 1)
    first_half = (lane % RET_DK) < (RET_DK // 2)
    rq = _rotary(q, cos, sin_signed, first_half)
    rk = _rotary(k, cos, sin_signed, first_half) * (RET_DK ** -0.5)
    lane_v = lax.broadcasted_iota(jnp.int32, (T_CHUNK, HEADS * DV), 1)
    kst = jnp.concatenate([jnp.where(lane // RET_DK == h, rk, 0.0) for h in range(HEADS)], axis=0)
    vst = jnp.concatenate([jnp.where(lane_v // DV == h, v, 0.0) for h in range(HEADS)], axis=0)
    p = _dot_nt(rq, kst) * dstack
    s_t = s_ref[...]
    o = _dot(p, vst) + _dot_nt(rq, s_t) * inner
    s_ref[...] = s_t * cdec + _dot_tn(v, rk * kdec) * mask_vk
    return o


def _prompt_kernel(apply_final,
                   x_ref, xn_ref, normw_ref, win_ref, wout_ref, bbd_ref, cbd_ref, are_ref, aim_ref,
                   dskip_ref, wglu_ref, lbp_ref, hgn_ref, wup_ref, bgate_ref, glan_ref, retn_ref, cos_ref,
                   sin_ref, dstack_ref, inner_ref, kdec_ref, cdec_ref, onesh_ref, onesg_ref, tril_ref,
                   fnw_ref,
                   y_ref, s5_ref, hgc_ref, glac_ref, retc_ref,
                   hgs_ref, glas_ref, rets_ref, proj_scr, lr_scr, hk_scr, bu_scr, mix_scr, *stream_scratch):
    nb = x_ref.shape[0]
    rows = nb * T_CHUNK
    n_slab = 2 * S5_N // LANES
    half = n_slab // 2
    step = pl.program_id(0)

    def w_quarter(qtr):
        return win_ref[:, qtr * QUARTER:(qtr + 1) * QUARTER]

    def proj(col, width, rs=slice(None)):
        return proj_scr[col // QUARTER, rs, col % QUARTER:col % QUARTER + width]

    @pl.when(step == 0)
    def _init():
        s5_ref[...] = jnp.zeros_like(s5_ref)
        hgs_ref[...] = jnp.zeros_like(hgs_ref)
        glas_ref[...] = jnp.zeros_like(glas_ref)
        rets_ref[...] = jnp.zeros_like(rets_ref)
        bu_scr[...] = jnp.zeros_like(bu_scr)
        h0 = _rmsnorm_rows(x_ref[...].reshape(rows, D_MODEL), normw_ref[...]).astype(BF16)
        for qtr in range(N_QUARTER):
            proj_scr[qtr] = jnp.dot(h0, w_quarter(qtr), preferred_element_type=F32)
        lr_scr[...] = jnp.dot(h0, win_ref[:, C_LR:N_PACK], preferred_element_type=F32)

    u16 = proj(C_U, BRANCH).astype(BF16)

    def drive(c2):
        bu = jnp.dot(u16, bbd_ref[:, 2 * c2 * LANES:2 * (c2 + 1) * LANES], preferred_element_type=F32)
        for cc in range(2):
            for b in range(nb):
                bu_scr[2 * c2 + cc, b * PITCH:b * PITCH + T_CHUNK, :] = bu[b * T_CHUNK:(b + 1) * T_CHUNK,
                                                                           cc * LANES:(cc + 1) * LANES]

    lbp = lbp_ref[...]
    log_f, hk = _hgrn_gates(proj(C_HF, BRANCH), lbp[0:1, :], lbp[1:2, :], lbp[2:3, :])
    proj_scr[C_HF // QUARTER, :, C_HF % QUARTER:C_HF % QUARTER + BRANCH] = log_f
    hk_scr[...] = hk
    g_gla = _log_sigmoid(_dot3(lr_scr[...], wup_ref[...]) + bgate_ref[...])
    lr_scr[...] = g_gla * (1.0 / GLA_TAU)
    for c2 in range(n_slab // 2):
        drive(c2)
    a_re = [jnp.broadcast_to(are_ref[:, c * LANES:(c + 1) * LANES], (nb, LANES)) for c in range(half)]
    a_im = [jnp.broadcast_to(aim_ref[:, c * LANES:(c + 1) * LANES], (nb, LANES)) for c in range(half)]
    s_init = s5_ref[...]
    carry0 = tuple(s_init[:, c * LANES:(c + 1) * LANES] for c in range(n_slab))

    def scan_step(t, carry):
        new = [None] * n_slab
        for c in range(half):
            sr, si = carry[c], carry[half + c]
            br = bu_scr[c, pl.ds(t, nb, stride=PITCH), :]
            bi = bu_scr[half + c, pl.ds(t, nb, stride=PITCH), :]
            nr = a_re[c] * sr - a_im[c] * si + br
            ni = a_re[c] * si + a_im[c] * sr + bi
            bu_scr[c, pl.ds(t, nb, stride=PITCH), :] = nr
            bu_scr[half + c, pl.ds(t, nb, stride=PITCH), :] = ni
            new[c], new[half + c] = nr, ni
        return tuple(new)

    carry = lax.fori_loop(0, T_CHUNK, scan_step, carry0, unroll=8)
    s5_ref[...] = jnp.concatenate(carry, axis=1)
    s_all = jnp.concatenate([bu_scr[c] for c in range(n_slab)], axis=1)
    y_all = _dot(s_all, cbd_ref[...])
    y_lin = jnp.concatenate([y_all[b * PITCH:b * PITCH + T_CHUNK] for b in range(nb)], axis=0)
    mix_scr[:, 0:BRANCH] = _s5_output(y_lin, proj(C_U, BRANCH), proj(C_SZ, BRANCH), dskip_ref[...],
                                      wglu_ref[...])

    ones_h = onesh_ref[...]
    ones_g = onesg_ref[...]
    mask_h = ones_h.astype(F32)
    mask_g = jnp.transpose(ones_g.astype(F32))
    tril = tril_ref[...]

    def rows_of(bp, odd):
        return pl.ds(pl.multiple_of((2 * bp + odd) * T_CHUNK, T_CHUNK), T_CHUNK)

    def hgrn_stream(bp, odd, term, row_scr):
        rs = rows_of(bp, odd)
        return _GatedStream(lambda: proj(C_HQ, BRANCH, rs), lambda: hk_scr[rs, :], lambda: proj(C_HI, BRANCH, rs),
                            lambda: proj(C_HF, BRANCH, rs), hgs_ref.at[bp, odd], ones_h, mask_h, tril, term,
                            row_scr, HEADS * HG_DK)

    def gla_stream(bp, odd, term, row_scr):
        rs = rows_of(bp, odd)
        return _GatedStream(lambda: proj(C_GQ, LANES, rs) * (GLA_DK ** -0.5), lambda: proj(C_GK, LANES, rs),
                            lambda: proj(C_GV, BRANCH, rs), lambda: lr_scr[rs, :],
                            glas_ref.at[bp, odd], ones_g, mask_g, tril, term, row_scr, HEADS * GLA_DK)

    n_sub = T_CHUNK // SUB

    def retention(bp, odd):
        rs = rows_of(bp, odd)
        mix_scr[rs, 3 * BRANCH:4 * BRANCH] = _ret_chunk(
            proj(C_RQ, LANES, rs), proj(C_RK, LANES, rs), proj(C_RV, BRANCH, rs), cos_ref[...],
            sin_ref[...], rets_ref.at[bp, odd], dstack_ref[...], inner_ref[...], kdec_ref[...],
            cdec_ref[...], mask_g)

    def per_group(grp, _):
        st, slots = [], []
        for i in range(GROUP_ROWS):
            bp, odd = (GROUP_ROWS // 2) * grp + i // 2, i % 2
            scr = stream_scratch[8 * i:8 * (i + 1)]
            st += [hgrn_stream(bp, odd, scr[0], scr[2:5]), gla_stream(bp, odd, scr[1], scr[5:8])]
            slots += [(bp, odd, BRANCH), (bp, odd, 2 * BRANCH)]

        def emit(i, j):
            bp, odd, col = slots[i]
            r0 = pl.multiple_of((2 * bp + odd) * T_CHUNK + SUB * j, SUB)
            mix_scr[pl.ds(r0, SUB), col:col + BRANCH] = st[i].output_part(j)

        for s_ in st:
            s_.cumulate()
        for i in range(GROUP_ROWS):
            retention((GROUP_ROWS // 2) * grp + i // 2, i % 2)
        for s_ in st:
            for p in range(n_sub // 2):
                s_.update_part(p)
        for s_ in st:
            for j in range(n_sub):
                s_.products_part(j)
        for s_ in st:
            for p in range(n_sub // 2):
                s_.cross_part(p)
        for s_ in st:
            for p in range(n_sub // 2):
                s_.matmul_part(p)
        for i in range(len(st)):
            for j in range(n_sub):
                emit(i, j)
        return 0

    lax.fori_loop(0, nb // GROUP_ROWS, per_group, 0)

    o_hg = _head_rms(mix_scr[:, BRANCH:2 * BRANCH], ones_h, hgn_ref[...])
    mix_scr[:, BRANCH:2 * BRANCH] = o_hg * _silu(proj(C_HZ, BRANCH))
    o_gla = _head_rms(mix_scr[:, 2 * BRANCH:3 * BRANCH], ones_h, glan_ref[...])
    mix_scr[:, 2 * BRANCH:3 * BRANCH] = o_gla * _silu(proj(C_GZ, BRANCH))
    o_ret = _head_ln(mix_scr[:, 3 * BRANCH:4 * BRANCH], ones_h, retn_ref[...])
    mix_scr[:, 3 * BRANCH:4 * BRANCH] = o_ret * _silu(proj(C_RZ, BRANCH))
    hn = _rmsnorm_rows(xn_ref[...].reshape(rows, D_MODEL), normw_ref[...]).astype(BF16)
    for qtr in range(N_QUARTER):
        proj_scr[qtr] = jnp.dot(hn, w_quarter(qtr), preferred_element_type=F32)
    lr_scr[...] = jnp.dot(hn, win_ref[:, C_LR:N_PACK], preferred_element_type=F32)
    out = _mix_and_project(x_ref[...].reshape(rows, D_MODEL), mix_scr[...], wout_ref[...], fnw_ref[...],
                           apply_final)
    y_ref[...] = out.reshape(nb, T_CHUNK, D_MODEL)

    @pl.when(step == pl.num_programs(0) - 1)
    def _emit_states():
        for b in range(nb):
            for state, compact in ((hgs_ref, hgc_ref), (glas_ref, glac_ref), (rets_ref, retc_ref)):
                dk = compact.shape[3]
                for h in range(HEADS):
                    compact[b, h] = state[b // 2, b % 2, h * DV:(h + 1) * DV, h * dk:(h + 1) * dk]


def _dot_sel_lhs2(sel, x):
    x1 = x.astype(BF16)
    x2 = (x - x1.astype(F32)).astype(BF16)
    return (jnp.dot(sel, x1, preferred_element_type=F32) + jnp.dot(sel, x2, preferred_element_type=F32))


def _head_rms_t(o, ones_h, gain):
    ms = _dot_sel_lhs2(ones_h, o * o) * (1.0 / DV)
    return o * lax.rsqrt(ms + EPS) * gain


def _head_ln_t(o, ones_h, gain):
    c = o - _dot_sel_lhs2(ones_h, o) * (1.0 / DV)
    var = _dot_sel_lhs2(ones_h, c * c) * (1.0 / DV)
    return c * lax.rsqrt(var + EPS) * gain


def _rotary_t(t, cos, sin_signed, first_half):
    half = RET_DK // 2
    swapped = jnp.where(first_half, pltpu.roll(t, LANES - half, 0), pltpu.roll(t, half, 0))
    return t * cos + swapped * sin_signed


def _sample_kernel(x_ref, normw_ref, wt_ref, wout_ref, bbdt_ref, cbdt_ref, are_ref, aim_ref, dskip_ref,
                   wglut_ref, lbp_ref, hgn_ref, wupt_ref, bgate_ref, glan_ref, retn_ref, fnw_ref,
                   cos_ref, sin_ref, dret_ref, onesh_ref,
                   s5re_ref, s5im_ref, hg_ref, gla_ref, ret_ref,
                   y_ref, s5re_o, s5im_o, hg_o, gla_o, ret_o,
                   xs_scr, pt_scr, hk_scr, ot_scr, mixt_scr):
    layer, head = pl.program_id(0), pl.program_id(1)
    last_layer, last_head = pl.num_programs(0) - 1, pl.num_programs(1) - 1

    @pl.when((layer == 0) & (head == 0))
    def _load_x():
        xs_scr[...] = x_ref[...]

    @pl.when(head == 0)
    def _dense():
        hh = _rmsnorm_rows(xs_scr[...], normw_ref[...]).astype(BF16)
        pt_scr[...] = lax.dot_general(wt_ref[...], hh, (((1,), (1,)), ((), ())), preferred_element_type=F32)

        u = pt_scr[C_U:C_U + BRANCH, :]
        bu = _dot3(bbdt_ref[...], u)
        a_re, a_im = are_ref[...], aim_ref[...]
        s0r, s0i = s5re_ref[...], s5im_ref[...]
        s_re = a_re * s0r - a_im * s0i + bu[0:S5_N]
        s_im = a_re * s0i + a_im * s0r + bu[S5_N:2 * S5_N]
        s5re_o[...] = s_re
        s5im_o[...] = s_im
        y = _gelu_tanh(_dot3(cbdt_ref[...], jnp.concatenate([s_re, s_im], axis=0)) + u * dskip_ref[...])
        y = y * _sigmoid(jnp.dot(wglut_ref[...], y.astype(BF16), preferred_element_type=F32))
        mixt_scr[0:BRANCH, :] = y * _silu(pt_scr[C_SZ:C_SZ + BRANCH, :])

        lbp = lbp_ref[...]
        log_f, hk = _hgrn_gates(pt_scr[C_HF:C_HF + BRANCH, :], lbp[:, 0:1], lbp[:, 1:2], lbp[:, 2:3])
        pt_scr[C_HF:C_HF + BRANCH, :] = jnp.exp(log_f)
        hk_scr[...] = hk
        g_gla = _log_sigmoid(_dot3(wupt_ref[...], pt_scr[C_LR:C_LR + LANES, :]) + bgate_ref[...])
        pt_scr[C_LR:C_LR + LANES, :] = jnp.exp(g_gla * (1.0 / GLA_TAU))
        pt_scr[C_GQ:C_GQ + LANES, :] = pt_scr[C_GQ:C_GQ + LANES, :] * (GLA_DK ** -0.5)
        row = lax.broadcasted_iota(jnp.int32, (HEADS * RET_DK, LANES), 0)
        first_half = (row % RET_DK) < (RET_DK // 2)
        pt_scr[C_RQ:C_RQ + LANES, :] = _rotary_t(pt_scr[C_RQ:C_RQ + LANES, :], cos_ref[...], sin_ref[...],
                                                 first_half)
        pt_scr[C_RK:C_RK + LANES, :] = _rotary_t(pt_scr[C_RK:C_RK + LANES, :], cos_ref[...], sin_ref[...],
                                                 first_half) * (RET_DK ** -0.5)

    def head_update(s0_ref, s_out_ref, dk, dec_ref, dec_row, key_ref, key_row, q_row, v_row, out_row):
        vt = pt_scr[pl.ds(pl.multiple_of(v_row + head * DV, DV), DV), :]

        def feature(kk, acc):
            r = head * dk + kk
            bcast = lambda ref, r0: jnp.broadcast_to(ref[pl.ds(r0 + r, 1), :], (DV, LANES))
            s_new = s0_ref[kk] * bcast(dec_ref, dec_row) + bcast(key_ref, key_row) * vt
            s_out_ref[kk] = s_new
            return acc + bcast(pt_scr, q_row) * s_new

        acc = lax.fori_loop(0, dk, feature, jnp.zeros((DV, LANES), F32), unroll=4)
        ot_scr[pl.ds(pl.multiple_of(out_row + head * DV, DV), DV), :] = acc

    head_update(hg_ref, hg_o, HG_DK, pt_scr, C_HF, hk_scr, 0, C_HQ, C_HI, 0)
    head_update(gla_ref, gla_o, GLA_DK, pt_scr, C_LR, pt_scr, C_GK, C_GQ, C_GV, BRANCH)
    head_update(ret_ref, ret_o, RET_DK, dret_ref, 0, pt_scr, C_RK, C_RQ, C_RV, 2 * BRANCH)

    @pl.when(head == last_head)
    def _finish():
        ones_h = onesh_ref[...]
        o_hg = _head_rms_t(ot_scr[0:BRANCH, :], ones_h, hgn_ref[...])
        mixt_scr[BRANCH:2 * BRANCH, :] = o_hg * _silu(pt_scr[C_HZ:C_HZ + BRANCH, :])
        o_gla = _head_rms_t(ot_scr[BRANCH:2 * BRANCH, :], ones_h, glan_ref[...])
        mixt_scr[2 * BRANCH:3 * BRANCH, :] = o_gla * _silu(pt_scr[C_GZ:C_GZ + BRANCH, :])
        o_ret = _head_ln_t(ot_scr[2 * BRANCH:3 * BRANCH, :], ones_h, retn_ref[...])
        mixt_scr[3 * BRANCH:4 * BRANCH, :] = o_ret * _silu(pt_scr[C_RZ:C_RZ + BRANCH, :])
        out = xs_scr[...] + lax.dot_general(mixt_scr[...].astype(BF16), wout_ref[...], (((0,), (0,)), ((), ())),
                                            preferred_element_type=F32)
        xs_scr[...] = out

        @pl.when(layer == last_layer)
        def _emit():
            y_ref[...] = _rmsnorm_rows(out, fnw_ref[...])


def _ret_log_gamma():
    return jnp.log1p(-jnp.exp2(-5.0 - jnp.arange(HEADS, dtype=F32)))


def _constants():
    ones_h = (np.arange(BRANCH)[:, None] // DV == np.arange(BRANCH)[None, :] // DV)
    ones_g = (np.arange(HEADS * GLA_DK)[:, None] // GLA_DK == np.arange(BRANCH)[None, :] // DV)
    r = np.arange(T_CHUNK)
    tril = (r[:, None] // PAIR == r[None, :] // PAIR) & (r[None, :] <= r[:, None])
    same_group_b = np.arange(BRANCH)[:, None] // S5_CH == np.arange(S5_N)[None, :] // S5_STATE
    as_bf16 = lambda m: jnp.asarray(m.astype(np.float32), dtype=BF16)
    return dict(ones_h=as_bf16(ones_h), ones_g=as_bf16(ones_g), tril=as_bf16(tril),
                s5_mask=jnp.asarray(same_group_b.astype(np.float32)))


def _ret_tables():
    lg = _ret_log_gamma()
    idx = jnp.arange(T_CHUNK, dtype=F32)
    rel = idx[:, None] - idx[None, :]
    causal = rel >= 0
    decay = jnp.where(causal[None], jnp.exp(jnp.where(causal, rel, 0.0)[None] * lg[:, None, None]), 0.0)
    dstack = jnp.transpose(decay, (1, 0, 2)).reshape(T_CHUNK, HEADS * T_CHUNK)
    inner = jnp.repeat(jnp.exp((idx[:, None] + 1.0) * lg[None, :]), DV, axis=1)
    kdec = jnp.repeat(jnp.exp((T_CHUNK - 1.0 - idx[:, None]) * lg[None, :]), RET_DK, axis=1)
    cdec = jnp.repeat(jnp.exp(T_CHUNK * lg)[None, :], RET_DK, axis=1)
    dret = jnp.broadcast_to(jnp.repeat(jnp.exp(lg), RET_DK)[:, None], (HEADS * RET_DK, LANES))
    return dstack, inner, kdec, cdec, dret


def _rope_tables(pos):
    half = RET_DK // 2
    inv = ROPE_BASE ** (-jnp.arange(half, dtype=F32) / half)
    ang = pos.astype(F32)[:, None] * inv[None, :]
    cos, sin = jnp.cos(ang), jnp.sin(ang)
    cos_t = jnp.tile(jnp.concatenate([cos, cos], axis=1), (1, HEADS))
    sin_t = jnp.tile(jnp.concatenate([-sin, sin], axis=1), (1, HEADS))
    return cos_t, sin_t


def _pack_w_in_t(w):
    wt = jnp.swapaxes(w, 1, 2)
    offs = np.cumsum([0, 256, 256, 256, 256, 256, 256, 128, 128, 256, 16, 256, 128, 128, 256, 256])
    seg = lambda i: wt[:, int(offs[i]):int(offs[i + 1]), :]
    order = [0, 1, 2, 3, 4, 5, 6, 7, 8, 10, 11, 12, 13, 14]
    pad = jnp.zeros((w.shape[0], LANES - GLA_LOWRANK, w.shape[1]), w.dtype)
    return jnp.concatenate([seg(i) for i in order] + [seg(9), pad], axis=1).astype(BF16)


def _s5_discretize(lam_re, lam_im, log_step, b_re, b_im, c_re, c_im, mask):
    lr, li = lam_re.astype(F32), lam_im.astype(F32)
    step = jnp.exp(log_step.astype(F32))[..., None]
    mag = jnp.exp(lr * step)
    ab_re = mag * jnp.cos(li * step)
    ab_im = mag * jnp.sin(li * step)
    den = lr * lr + li * li
    nr = ab_re - 1.0
    f_re = (nr * lr + ab_im * li) / den
    f_im = (ab_im * lr - nr * li) / den
    br, bi = b_re.astype(F32), b_im.astype(F32)
    bb_re = f_re[..., None] * br - f_im[..., None] * bi
    bb_im = f_re[..., None] * bi + f_im[..., None] * br
    nl = lr.shape[0]

    def drive(bb):
        rows = jnp.transpose(bb, (0, 1, 3, 2)).reshape(nl, BRANCH, S5_STATE)
        return jnp.tile(rows, (1, 1, S5_GROUPS)) * mask

    def readout(cc):
        rows = cc.astype(F32).reshape(nl, BRANCH, S5_STATE)
        return jnp.tile(rows, (1, 1, S5_GROUPS)) * mask

    bbd = jnp.concatenate([drive(bb_re), drive(bb_im)], axis=2)
    cbd_t = jnp.concatenate([readout(c_re), -readout(c_im)], axis=2)
    return bbd, cbd_t, ab_re.reshape(nl, 1, S5_N), ab_im.reshape(nl, 1, S5_N)


def _full(shape):
    return pl.BlockSpec(shape, lambda *_: (0,) * len(shape), pipeline_mode=pl.Buffered(1))


def _of_layer(arr, layer):
    nd = arr.ndim - 1
    return pl.BlockSpec((None,) + arr.shape[1:], lambda *_: (layer,) + (0,) * nd,
                        pipeline_mode=pl.Buffered(1))


def _prompt_layer(x, layer, p, consts, tabs, rope, apply_final):
    nb, seq, _ = x.shape
    n_steps = seq // T_CHUNK
    rows = nb * T_CHUNK
    dstack, inner, kdec, cdec, _ = tabs
    cos_t, sin_t = rope
    per_layer = [p['norm_w'], p['w_pack'], p['w_out'], p['bbd'].astype(BF16), p['cbd'],
                 p['a_re'], p['a_im'],
                 p['d_skip'], p['w_glu'], p['lbp'], p['hg_norm'], p['w_up'], p['b_gate'], p['gla_norm'],
                 p['ret_norm']]
    shared = [dstack, inner, kdec, cdec, consts['ones_h'], consts['ones_g'], consts['tril'],
              p['final_norm']]
    in_specs = [pl.BlockSpec((nb, T_CHUNK, D_MODEL), lambda i: (0, i, 0)),
                pl.BlockSpec((nb, T_CHUNK, D_MODEL), lambda i: (0, jnp.minimum(i + 1, n_steps - 1), 0))]
    in_specs += [_of_layer(a, layer) for a in per_layer]
    in_specs += [pl.BlockSpec((T_CHUNK, LANES), lambda i: (i, 0)), pl.BlockSpec((T_CHUNK, LANES), lambda i: (i, 0))]
    in_specs += [_full(a.shape) for a in shared]
    out_shape = (jax.ShapeDtypeStruct((nb, seq, D_MODEL), F32),
                 jax.ShapeDtypeStruct((nb, 2 * S5_N), F32),
                 jax.ShapeDtypeStruct((nb, HEADS, DV, HG_DK), F32),
                 jax.ShapeDtypeStruct((nb, HEADS, DV, GLA_DK), F32),
                 jax.ShapeDtypeStruct((nb, HEADS, DV, RET_DK), F32))
    out_specs = (pl.BlockSpec((nb, T_CHUNK, D_MODEL), lambda i: (0, i, 0)),
                 pl.BlockSpec((nb, 2 * S5_N), lambda i: (0, 0)),
                 pl.BlockSpec((nb, HEADS, DV, HG_DK), lambda i: (0, 0, 0, 0)),
                 pl.BlockSpec((nb, HEADS, DV, GLA_DK), lambda i: (0, 0, 0, 0)),
                 pl.BlockSpec((nb, HEADS, DV, RET_DK), lambda i: (0, 0, 0, 0)))
    scratch = [pltpu.VMEM((nb // 2, 2, BRANCH, HEADS * HG_DK), F32),
               pltpu.VMEM((nb // 2, 2, BRANCH, HEADS * GLA_DK), F32),
               pltpu.VMEM((nb // 2, 2, BRANCH, HEADS * RET_DK), F32),
               pltpu.VMEM((N_QUARTER, rows, QUARTER), F32), pltpu.VMEM((rows, LANES), F32),
               pltpu.VMEM((rows, BRANCH), F32),
               pltpu.VMEM((2 * S5_N // LANES, nb * PITCH, LANES), F32),
               pltpu.VMEM((rows, D_MODEL), F32),
               ] + GROUP_ROWS * (
                   [pltpu.VMEM((TERM_ROWS, kd), F32) for kd in (HEADS * HG_DK, HEADS * GLA_DK)]
                   + [pltpu.VMEM((w // LANES, T_CHUNK, LANES), F32) for kd in (HEADS * HG_DK, HEADS * GLA_DK)
                      for w in (kd, kd, BRANCH)])
    y, s5, hgs, glas, rets = pl.pallas_call(
        functools.partial(_prompt_kernel, apply_final),
        grid=(n_steps,), in_specs=in_specs, out_specs=out_specs, out_shape=out_shape,
        scratch_shapes=scratch, name='prompt_layer',
        compiler_params=pltpu.CompilerParams(dimension_semantics=('arbitrary',),
                                             vmem_limit_bytes=VMEM_LIMIT),
    )(x, x, *per_layer, cos_t, sin_t, *shared)
    s5 = s5.reshape(nb, 2, S5_GROUPS, S5_STATE)

    to_kv = lambda st: jnp.swapaxes(st, 2, 3)
    return y, (s5[:, 0], s5[:, 1], to_kv(hgs), to_kv(glas), to_kv(rets))


def _sample_step(x, states, p, consts, tabs, rope):
    nb = x.shape[0]
    depth = p['w_t'].shape[0]
    s5re, s5im, hg, gla, ret = states
    to_lanes = lambda s: jnp.moveaxis(s, 1, -1)
    s5re_t = to_lanes(s5re).reshape(depth, S5_N, nb)
    s5im_t = to_lanes(s5im).reshape(depth, S5_N, nb)
    hg_t, gla_t, ret_t = to_lanes(hg), to_lanes(gla), to_lanes(ret)
    cos_t, sin_t = rope
    col = lambda a: jnp.swapaxes(a, -1, -2)
    per_layer = [p['norm_w'], p['w_t'], p['w_out'], col(p['bbd']), p['cbd_t'], col(p['a_re']), col(p['a_im']),
                 col(p['d_skip']), col(p['w_glu']), col(p['lbp']), col(p['hg_norm']), col(p['w_up']),
                 col(p['b_gate']), col(p['gla_norm']), col(p['ret_norm'])]
    shared = [p['final_norm'], col(cos_t), col(sin_t), tabs[4], consts['ones_h']]
    layer_spec = lambda a: pl.BlockSpec((None,) + a.shape[1:], lambda l, h: (l,) + (0,) * (a.ndim - 1))
    head_spec = lambda a: pl.BlockSpec((None, None) + a.shape[2:], lambda l, h: (l, h) + (0,) * (a.ndim - 2))
    state_specs = [layer_spec(s5re_t), layer_spec(s5im_t), head_spec(hg_t), head_spec(gla_t), head_spec(ret_t)]
    in_specs = ([_full(x.shape)] + [layer_spec(a) for a in per_layer] + [_full(a.shape) for a in shared]
                + state_specs)
    state_arrays = [s5re_t, s5im_t, hg_t, gla_t, ret_t]
    out_shape = tuple([jax.ShapeDtypeStruct(x.shape, F32)]
                      + [jax.ShapeDtypeStruct(a.shape, F32) for a in state_arrays])
    out_specs = tuple([pl.BlockSpec(x.shape, lambda l, h: (0, 0))] + state_specs)
    scratch = [pltpu.VMEM((nb, D_MODEL), F32), pltpu.VMEM((N_PACK, nb), F32), pltpu.VMEM((BRANCH, nb), F32),
               pltpu.VMEM((3 * BRANCH, nb), F32), pltpu.VMEM((4 * BRANCH, nb), F32)]
    outs = pl.pallas_call(
        _sample_kernel, grid=(depth, HEADS), in_specs=in_specs, out_specs=out_specs, out_shape=out_shape,
        scratch_shapes=scratch, name='sample_step',
        compiler_params=pltpu.CompilerParams(dimension_semantics=('arbitrary', 'arbitrary'),
                                             vmem_limit_bytes=VMEM_LIMIT),
    )(x, *per_layer, *shared, *state_arrays)
    from_lanes = lambda s: jnp.moveaxis(s, -1, 1)
    new = (from_lanes(outs[1].reshape(depth, S5_GROUPS, S5_STATE, nb)),
           from_lanes(outs[2].reshape(depth, S5_GROUPS, S5_STATE, nb)),
           from_lanes(outs[3]), from_lanes(outs[4]), from_lanes(outs[5]))
    return outs[0], new


def kernel(x_prompt, x_sample, state_s5_re, state_s5_im, state_hgrn, state_gla, state_ret, norm_w, final_norm_w, w_in, w_out, s5_lam_re, s5_lam_im, s5_log_step, s5_b_re, s5_b_im, s5_c_re, s5_c_im, s5_d, s5_w_glu, hgrn_lb_logits, hgrn_norm_w, gla_w_gate_up, gla_b_gate, gla_norm_w, ret_norm_w):
    depth = w_in.shape[0]
    seq = x_prompt.shape[1]
    consts = _constants()
    tabs = _ret_tables()
    rope_p = _rope_tables(jnp.arange(seq))
    rope_s = _rope_tables(PAST_LEN + jnp.arange(1))
    lb = jnp.cumsum(jax.nn.softmax(hgrn_lb_logits.astype(F32), axis=0), axis=0)
    lb = (lb - lb[0:1])[:, None, :]
    bbd, cbd_t, a_re, a_im = _s5_discretize(s5_lam_re, s5_lam_im, s5_log_step, s5_b_re, s5_b_im, s5_c_re,
                                            s5_c_im, consts['s5_mask'])
    w_t = lax.optimization_barrier(_pack_w_in_t(w_in))
    row = lambda a: a[:, None, :].astype(F32)
    w_up = jnp.zeros((depth, LANES, HEADS * GLA_DK), F32).at[:, :GLA_LOWRANK].set(gla_w_gate_up.astype(F32))
    p = dict(norm_w=row(norm_w),
             w_pack=jnp.swapaxes(w_t, 1, 2), w_t=w_t, w_out=w_out.astype(BF16),
             bbd=bbd, cbd=jnp.swapaxes(cbd_t, 1, 2).astype(BF16), cbd_t=cbd_t, a_re=a_re, a_im=a_im,
             d_skip=row(s5_d), w_glu=s5_w_glu.astype(BF16),
             lbp=jnp.concatenate([jnp.log(lb), jnp.log1p(-lb), 1.0 - lb, jnp.zeros((depth, 5, BRANCH), F32)],
                                 axis=1),
             hg_norm=row(hgrn_norm_w), w_up=w_up, b_gate=row(gla_b_gate), gla_norm=row(gla_norm_w),
             ret_norm=row(ret_norm_w), final_norm=final_norm_w[None, :].astype(F32))

    xp = x_prompt
    new_p = ([], [], [], [], [])
    for l in range(depth):
        xp, st_p = _prompt_layer(xp, l, p, consts, tabs, rope_p, l == depth - 1)
        for i in range(5):
            new_p[i].append(st_p[i])
    xs, new_s = _sample_step(x_sample.reshape(x_sample.shape[0], D_MODEL),
                             (state_s5_re, state_s5_im, state_hgrn, state_gla, state_ret),
                             p, consts, tabs, rope_s)
    return (xp, xs.reshape(x_sample.shape),
            jnp.stack(new_p[0]), jnp.stack(new_p[1]), jnp.stack(new_p[2]), jnp.stack(new_p[3]),
            jnp.stack(new_p[4])) + new_s
```

```python
import functools
import math

import numpy as np
import jax
import jax.numpy as jnp
from jax import lax
from jax.experimental import pallas as pl
from jax.experimental.pallas import tpu as pltpu

F32 = jnp.float32
BF16 = jnp.bfloat16

D_MODEL = 1024
BRANCH = 256
S5_CH = 16
S5_GROUPS = 16
S5_STATE = 64
S5_N = S5_GROUPS * S5_STATE
HEADS = 4
HG_DK = 64
GLA_DK = 32
RET_DK = 32
DV = 64
GLA_LOWRANK = 16
GLA_TAU = 16.0
ROPE_BASE = 10000.0
PAST_LEN = 16384
EPS = 1e-6
SUB = 16
PAIR = 2 * SUB
GROUP_ROWS = 4

LANES = 128
T_CHUNK = 64
PITCH = T_CHUNK + 8
SUB_TERM_ROWS = SUB * (SUB + SUB // 2) // 2
TERM_ROWS = (T_CHUNK // SUB) * SUB_TERM_ROWS
LOG2E = math.log2(math.e)
MASK_NEG = -1e30
VMEM_LIMIT = 60 * 1024 * 1024

C_U, C_SZ, C_HQ, C_HF, C_HI, C_HZ = 0, 256, 512, 768, 1024, 1280
C_GQ, C_GK, C_GV, C_GZ = 1536, 1664, 1792, 2048
C_RQ, C_RK, C_RV, C_RZ = 2304, 2432, 2560, 2816
C_LR = 3072
N_PACK = 3200
N_QUARTER = 4
QUARTER = C_LR // N_QUARTER


def _dot(a, b):
    return jnp.dot(a.astype(BF16), b.astype(BF16), preferred_element_type=F32)


def _dot_nt(a, b):
    return lax.dot_general(a.astype(BF16), b.astype(BF16), (((1,), (1,)), ((), ())),
                           preferred_element_type=F32)


def _dot_tn(a, b):
    return lax.dot_general(a.astype(BF16), b.astype(BF16), (((0,), (0,)), ((), ())),
                           preferred_element_type=F32)


def _split3(x):
    x1 = x.astype(BF16)
    r1 = x - x1.astype(F32)
    x2 = r1.astype(BF16)
    x3 = (r1 - x2.astype(F32)).astype(BF16)
    return x1, x2, x3


def _dot_sel_lhs(sel, x):
    x1, x2, x3 = _split3(x)
    d = lambda p: jnp.dot(sel, p, preferred_element_type=F32)
    return d(x1) + d(x2) + d(x3)


def _dot_sel_rhs2(x, sel):
    x1 = x.astype(BF16)
    x2 = (x - x1.astype(F32)).astype(BF16)
    return (jnp.dot(x1, sel, preferred_element_type=F32)
            + jnp.dot(x2, sel, preferred_element_type=F32))


def _dot3(a, b):
    a1 = a.astype(BF16)
    a2 = (a - a1.astype(F32)).astype(BF16)
    b1 = b.astype(BF16)
    b2 = (b - b1.astype(F32)).astype(BF16)
    d = lambda p, q: jnp.dot(p, q, preferred_element_type=F32)
    return d(a1, b1) + d(a1, b2) + d(a2, b1)


def _sigmoid(x):
    return 0.5 * jnp.tanh(0.5 * x) + 0.5


def _silu(x):
    return x * _sigmoid(x)


def _log_sigmoid(x):
    return jnp.minimum(x, 0.0) - jnp.log(1.0 + jnp.exp(-jnp.abs(x)))


def _gelu_tanh(x):
    return 0.5 * x * (1.0 + jnp.tanh(math.sqrt(2.0 / math.pi) * (x + 0.044715 * (x * x * x))))


def _rmsnorm_rows(x, w):
    return x * lax.rsqrt(jnp.mean(x * x, axis=-1, keepdims=True) + EPS) * w


def _head_rms(o, ones_h, gain):
    ms = _dot_sel_rhs2(o * o, ones_h) * (1.0 / DV)
    return o * lax.rsqrt(ms + EPS) * gain


def _head_ln(o, ones_h, gain):
    c = o - _dot_sel_rhs2(o, ones_h) * (1.0 / DV)
    var = _dot_sel_rhs2(c * c, ones_h) * (1.0 / DV)
    return c * lax.rsqrt(var + EPS) * gain


def _hgrn_gates(xf, loglb, log1mlb, one_m_lb):
    bterm = log1mlb + _log_sigmoid(xf)
    m = jnp.maximum(loglb, bterm)
    log_f = m + jnp.log(jnp.exp(loglb - m) + jnp.exp(bterm - m))
    return log_f, one_m_lb * _sigmoid(-xf)


def _rotary(t, cos, sin_signed, first_half):
    half = RET_DK // 2
    swapped = jnp.where(first_half, pltpu.roll(t, LANES - half, 1), pltpu.roll(t, half, 1))
    return t * cos + swapped * sin_signed


def _s5_output(y_lin, u, sz, dskip, wglu):
    y = _gelu_tanh(y_lin + u * dskip)
    y = y * _sigmoid(_dot(y, wglu))
    return y * _silu(sz)


def _mix_and_project(x, mix, wout, fnw, apply_final):
    out = x + _dot(mix, wout)
    if apply_final:
        out = _rmsnorm_rows(out, fnw)
    return out


class _Slabs:
    def __init__(self, ref, base=0):
        self.ref, self.base = ref, base

    def keep(self, value):
        for c in range(self.ref.shape[0]):
            self.ref[c, pl.ds(self.base, value.shape[0]), :] = value[:, c * LANES:(c + 1) * LANES]

    def tile(self, r0, n):
        return jnp.concatenate([self.ref[c, pl.ds(self.base + r0, n), :] for c in range(self.ref.shape[0])],
                               axis=1)

    def row(self, r, n):
        return jnp.concatenate([self.ref[c, pl.ds(self.base + r, n, stride=0), :]
                                for c in range(self.ref.shape[0])], axis=1)


class _GatedStream:
    def __init__(self, q, k, v, g, s_ref, ones_kv, mask_vk, tril, term_scr, row_scr, kdim):
        self.q, self.k, self.v, self.g = q, k, v, g
        self.bc, self.kc, self.vc = (_Slabs(r) for r in row_scr)
        self.s_ref, self.ones_kv, self.mask_vk = s_ref, ones_kv, mask_vk
        self.tril, self.term_scr, self.kdim = tril, term_scr, kdim
        self.ws, self.inter, self.att, self.cross = {}, {}, {}, {}

    def cumulate(self):
        self.bc.keep(_dot_sel_lhs(self.tril, self.g()) * LOG2E)
        self.kc.keep(self.k())
        self.vc.keep(self.v())

    def _total(self, p):
        return self.bc.tile(PAIR * (p + 1) - 1, 1)

    def update_part(self, p):
        sl = slice(PAIR * p, PAIR * (p + 1))
        self.ws[p] = _dot_tn(self.v()[sl],
                             self.k()[sl] * jnp.exp2(self._total(p) - self.bc.tile(PAIR * p, PAIR)))

    def cross_part(self, p):
        r_a, r_b = PAIR * p, PAIR * p + SUB
        edge = self.bc.row(r_b - 1, SUB)
        q_b = self.q()[r_b:r_b + SUB] * jnp.exp2(self.bc.tile(r_b, SUB) - edge)
        k_a = self.kc.tile(r_a, SUB) * jnp.exp2(edge - self.bc.tile(r_a, SUB))
        lane = lax.broadcasted_iota(jnp.int32, (SUB, self.kdim), 1) // (self.kdim // HEADS)
        q_heads = jnp.concatenate([jnp.where(lane == h, q_b, 0.0) for h in range(HEADS)], axis=0)
        scores = _dot_nt(q_heads, k_a)
        res = _dot(scores, self.vc.tile(r_a, SUB))
        lane_v = lax.broadcasted_iota(jnp.int32, (SUB, BRANCH), 1) // DV
        out = jnp.where(lane_v == 0, res[0:SUB], 0.0)
        for h in range(1, HEADS):
            out = out + jnp.where(lane_v == h, res[SUB * h:SUB * (h + 1)], 0.0)
        self.cross[p] = out

    def products_part(self, j):
        hs, term_scr, r0 = SUB // 2, self.term_scr, SUB * j
        q, bc = self.q()[r0:r0 + SUB], self.bc.tile(r0, SUB)
        trow = lax.broadcasted_iota(jnp.int32, (hs, self.kdim), 0)
        neg = [jnp.where(trow >= d, 0.0, MASK_NEG) for d in range(1, hs)]
        q_lo, q_hi, b_lo, b_hi = q[0:hs], q[hs:SUB], bc[0:hs], bc[hs:SUB]
        row = j * SUB_TERM_ROWS
        for s in range(SUB):
            ks, bs = self.kc.row(r0 + s, hs), self.bc.row(r0 + s, hs)
            d = s % hs
            q_dg, b_dg = (q_lo, b_lo) if s < hs else (q_hi, b_hi)
            e_dg = b_dg - bs if d == 0 else b_dg - bs + neg[d - 1]
            term_scr[row:row + hs, :] = q_dg * ks * jnp.exp2(e_dg)
            row += hs
            if s < hs:
                term_scr[row:row + hs, :] = q_hi * ks * jnp.exp2(b_hi - bs)
                row += hs

    def matmul_part(self, p):
        sl = slice(PAIR * p, PAIR * (p + 1))
        if p == 0:
            self.state = self.s_ref[...]
        self.inter[p] = _dot_nt(self.q()[sl] * jnp.exp2(self.bc.tile(PAIR * p, PAIR)), self.state)
        self.state = self.state * jnp.exp2(self._total(p)) + self.ws[p] * self.mask_vk
        if p == T_CHUNK // PAIR - 1:
            self.s_ref[...] = self.state
        rows = slice(2 * p * SUB_TERM_ROWS, 2 * (p + 1) * SUB_TERM_ROWS)
        self.att[p] = _dot(self.term_scr[rows, :], self.ones_kv)

    def output_part(self, j):
        hs, r0, p, second = SUB // 2, SUB * j, j // 2, j % 2
        att = self.att[p][second * SUB_TERM_ROWS:(second + 1) * SUB_TERM_ROWS]
        start = self.inter[p][second * SUB:(second + 1) * SUB]
        if second:
            start = start + self.cross[p]
        acc_lo, acc_hi = start[0:hs], start[hs:SUB]
        row = 0
        for s in range(SUB):
            vs = self.vc.row(r0 + s, hs)
            if s < hs:
                acc_lo = acc_lo + att[row:row + hs] * vs
                row += hs
            acc_hi = acc_hi + att[row:row + hs] * vs
            row += hs
        return jnp.concatenate([acc_lo, acc_hi], axis=0)


def _ret_chunk(q, k, v, cos, sin_signed, s_ref, dstack, inner, kdec, cdec, mask_vk):
    lane = lax.broadcasted_iota(jnp.int32, (T_CHUNK, HEADS * RET_DK), 1)
    first_half = (lane % RET_DK) < (RET_DK // 2)
    rq = _rotary(q, cos, sin_signed, first_half)
    rk = _rotary(k, cos, sin_signed, first_half) * (RET_DK ** -0.5)
    lane_v = lax.broadcasted_iota(jnp.int32, (T_CHUNK, HEADS * DV), 1)
    kst = jnp.concatenate([jnp.where(lane // RET_DK == h, rk, 0.0) for h in range(HEADS)], axis=0)
    vst = jnp.concatenate([jnp.where(lane_v // DV == h, v, 0.0) for h in range(HEADS)], axis=0)
    p = _dot_nt(rq, kst) * dstack
    s_t = s_ref[...]
    o = _dot(p, vst) + _dot_nt(rq, s_t) * inner
    s_ref[...] = s_t * cdec + _dot_tn(v, rk * kdec) * mask_vk
    return o


def _prompt_kernel(apply_final,
                   x_ref, xn_ref, normw_ref, wt_ref, wout_ref, bbd_ref, cbd_ref, are_ref, aim_ref,
                   dskip_ref, wglu_ref, lbp_ref, hgn_ref, wup_ref, bgate_ref, glan_ref, retn_ref, cos_ref,
                   sin_ref, dstack_ref, inner_ref, kdec_ref, cdec_ref, onesh_ref, onesg_ref, tril_ref,
                   fnw_ref,
                   y_ref, s5_ref, hgc_ref, glac_ref, retc_ref,
                   hgs_ref, glas_ref, rets_ref, win_scr, proj_scr, lr_scr, hk_scr, bu_scr, mix_scr, *stream_scratch):
    nb = x_ref.shape[0]
    rows = nb * T_CHUNK
    n_slab = 2 * S5_N // LANES
    half = n_slab // 2
    step = pl.program_id(0)

    def w_quarter(qtr):
        return win_scr[:, qtr * QUARTER:(qtr + 1) * QUARTER]

    def proj(col, width, rs=slice(None)):
        return proj_scr[col // QUARTER, rs, col % QUARTER:col % QUARTER + width]

    @pl.when(step == 0)
    def _init():
        s5_ref[...] = jnp.zeros_like(s5_ref)
        hgs_ref[...] = jnp.zeros_like(hgs_ref)
        glas_ref[...] = jnp.zeros_like(glas_ref)
        rets_ref[...] = jnp.zeros_like(rets_ref)
        bu_scr[...] = jnp.zeros_like(bu_scr)
        for c in range(N_PACK // LANES):
            win_scr[:, c * LANES:(c + 1) * LANES] = jnp.transpose(wt_ref[c * LANES:(c + 1) * LANES, :])
        h0 = _rmsnorm_rows(x_ref[...].reshape(rows, D_MODEL), normw_ref[...]).astype(BF16)
        for qtr in range(N_QUARTER):
            proj_scr[qtr] = jnp.dot(h0, w_quarter(qtr), preferred_element_type=F32)
        lr_scr[...] = jnp.dot(h0, win_scr[:, C_LR:N_PACK], preferred_element_type=F32)

    u16 = proj(C_U, BRANCH).astype(BF16)

    def drive(c2):
        bu = jnp.dot(u16, bbd_ref[:, 2 * c2 * LANES:2 * (c2 + 1) * LANES], preferred_element_type=F32)
        for cc in range(2):
            for b in range(nb):
                bu_scr[2 * c2 + cc, b * PITCH:b * PITCH + T_CHUNK, :] = bu[b * T_CHUNK:(b + 1) * T_CHUNK,
                                                                           cc * LANES:(cc + 1) * LANES]

    lbp = lbp_ref[...]
    log_f, hk = _hgrn_gates(proj(C_HF, BRANCH), lbp[0:1, :], lbp[1:2, :], lbp[2:3, :])
    proj_scr[C_HF // QUARTER, :, C_HF % QUARTER:C_HF % QUARTER + BRANCH] = log_f
    hk_scr[...] = hk
    g_gla = _log_sigmoid(_dot3(lr_scr[...], wup_ref[...]) + bgate_ref[...])
    lr_scr[...] = g_gla * (1.0 / GLA_TAU)
    for c2 in range(n_slab // 2):
        drive(c2)
    a_re = [jnp.broadcast_to(are_ref[:, c * LANES:(c + 1) * LANES], (nb, LANES)) for c in range(half)]
    a_im = [jnp.broadcast_to(aim_ref[:, c * LANES:(c + 1) * LANES], (nb, LANES)) for c in range(half)]
    s_init = s5_ref[...]
    carry0 = tuple(s_init[:, c * LANES:(c + 1) * LANES] for c in range(n_slab))

    def scan_step(t, carry):
        new = [None] * n_slab
        for c in range(half):
            sr, si = carry[c], carry[half + c]
            br = bu_scr[c, pl.ds(t, nb, stride=PITCH), :]
            bi = bu_scr[half + c, pl.ds(t, nb, stride=PITCH), :]
            nr = a_re[c] * sr - a_im[c] * si + br
            ni = a_re[c] * si + a_im[c] * sr + bi
            bu_scr[c, pl.ds(t, nb, stride=PITCH), :] = nr
            bu_scr[half + c, pl.ds(t, nb, stride=PITCH), :] = ni
            new[c], new[half + c] = nr, ni
        return tuple(new)

    carry = lax.fori_loop(0, T_CHUNK, scan_step, carry0, unroll=8)
    s5_ref[...] = jnp.concatenate(carry, axis=1)
    s_all = jnp.concatenate([bu_scr[c] for c in range(n_slab)], axis=1)
    y_all = _dot(s_all, cbd_ref[...])
    y_lin = jnp.concatenate([y_all[b * PITCH:b * PITCH + T_CHUNK] for b in range(nb)], axis=0)
    mix_scr[:, 0:BRANCH] = _s5_output(y_lin, proj(C_U, BRANCH), proj(C_SZ, BRANCH), dskip_ref[...],
                                      wglu_ref[...])

    ones_h = onesh_ref[...]
    ones_g = onesg_ref[...]
    mask_h = ones_h.astype(F32)
    mask_g = jnp.transpose(ones_g.astype(F32))
    tril = tril_ref[...]

    def rows_of(bp, odd):
        return pl.ds(pl.multiple_of((2 * bp + odd) * T_CHUNK, T_CHUNK), T_CHUNK)

    def hgrn_stream(bp, odd, term, row_scr):
        rs = rows_of(bp, odd)
        return _GatedStream(lambda: proj(C_HQ, BRANCH, rs), lambda: hk_scr[rs, :], lambda: proj(C_HI, BRANCH, rs),
                            lambda: proj(C_HF, BRANCH, rs), hgs_ref.at[bp, odd], ones_h, mask_h, tril, term,
                            row_scr, HEADS * HG_DK)

    def gla_stream(bp, odd, term, row_scr):
        rs = rows_of(bp, odd)
        return _GatedStream(lambda: proj(C_GQ, LANES, rs) * (GLA_DK ** -0.5), lambda: proj(C_GK, LANES, rs),
                            lambda: proj(C_GV, BRANCH, rs), lambda: lr_scr[rs, :],
                            glas_ref.at[bp, odd], ones_g, mask_g, tril, term, row_scr, HEADS * GLA_DK)

    n_sub = T_CHUNK // SUB

    def retention(bp, odd):
        rs = rows_of(bp, odd)
        mix_scr[rs, 3 * BRANCH:4 * BRANCH] = _ret_chunk(
            proj(C_RQ, LANES, rs), proj(C_RK, LANES, rs), proj(C_RV, BRANCH, rs), cos_ref[...],
            sin_ref[...], rets_ref.at[bp, odd], dstack_ref[...], inner_ref[...], kdec_ref[...],
            cdec_ref[...], mask_g)

    def per_group(grp, _):
        st, slots = [], []
        for i in range(GROUP_ROWS):
            bp, odd = (GROUP_ROWS // 2) * grp + i // 2, i % 2
            scr = stream_scratch[8 * i:8 * (i + 1)]
            st += [hgrn_stream(bp, odd, scr[0], scr[2:5]), gla_stream(bp, odd, scr[1], scr[5:8])]
            slots += [(bp, odd, BRANCH), (bp, odd, 2 * BRANCH)]

        def emit(i, j):
            bp, odd, col = slots[i]
            r0 = pl.multiple_of((2 * bp + odd) * T_CHUNK + SUB * j, SUB)
            mix_scr[pl.ds(r0, SUB), col:col + BRANCH] = st[i].output_part(j)

        for s_ in st:
            s_.cumulate()
        for i in range(GROUP_ROWS):
            retention((GROUP_ROWS // 2) * grp + i // 2, i % 2)
        for s_ in st:
            for p in range(n_sub // 2):
                s_.update_part(p)
        for s_ in st:
            for j in range(n_sub):
                s_.products_part(j)
        for s_ in st:
            for p in range(n_sub // 2):
                s_.cross_part(p)
        for s_ in st:
            for p in range(n_sub // 2):
                s_.matmul_part(p)
        for i in range(len(st)):
            for j in range(n_sub):
                emit(i, j)
        return 0

    lax.fori_loop(0, nb // GROUP_ROWS, per_group, 0)

    o_hg = _head_rms(mix_scr[:, BRANCH:2 * BRANCH], ones_h, hgn_ref[...])
    mix_scr[:, BRANCH:2 * BRANCH] = o_hg * _silu(proj(C_HZ, BRANCH))
    o_gla = _head_rms(mix_scr[:, 2 * BRANCH:3 * BRANCH], ones_h, glan_ref[...])
    mix_scr[:, 2 * BRANCH:3 * BRANCH] = o_gla * _silu(proj(C_GZ, BRANCH))
    o_ret = _head_ln(mix_scr[:, 3 * BRANCH:4 * BRANCH], ones_h, retn_ref[...])
    mix_scr[:, 3 * BRANCH:4 * BRANCH] = o_ret * _silu(proj(C_RZ, BRANCH))
    hn = _rmsnorm_rows(xn_ref[...].reshape(rows, D_MODEL), normw_ref[...]).astype(BF16)
    for qtr in range(N_QUARTER):
        proj_scr[qtr] = jnp.dot(hn, w_quarter(qtr), preferred_element_type=F32)
    lr_scr[...] = jnp.dot(hn, win_scr[:, C_LR:N_PACK], preferred_element_type=F32)
    out = _mix_and_project(x_ref[...].reshape(rows, D_MODEL), mix_scr[...], wout_ref[...], fnw_ref[...],
                           apply_final)
    y_ref[...] = out.reshape(nb, T_CHUNK, D_MODEL)

    @pl.when(step == pl.num_programs(0) - 1)
    def _emit_states():
        for b in range(nb):
            for state, compact in ((hgs_ref, hgc_ref), (glas_ref, glac_ref), (rets_ref, retc_ref)):
                dk = compact.shape[3]
                for h in range(HEADS):
                    compact[b, h] = state[b // 2, b % 2, h * DV:(h + 1) * DV, h * dk:(h + 1) * dk]


def _dot_sel_lhs2(sel, x):
    x1 = x.astype(BF16)
    x2 = (x - x1.astype(F32)).astype(BF16)
    return (jnp.dot(sel, x1, preferred_element_type=F32) + jnp.dot(sel, x2, preferred_element_type=F32))


def _head_rms_t(o, ones_h, gain):
    ms = _dot_sel_lhs2(ones_h, o * o) * (1.0 / DV)
    return o * lax.rsqrt(ms + EPS) * gain


def _head_ln_t(o, ones_h, gain):
    c = o - _dot_sel_lhs2(ones_h, o) * (1.0 / DV)
    var = _dot_sel_lhs2(ones_h, c * c) * (1.0 / DV)
    return c * lax.rsqrt(var + EPS) * gain


def _rotary_t(t, cos, sin_signed, first_half):
    half = RET_DK // 2
    swapped = jnp.where(first_half, pltpu.roll(t, LANES - half, 0), pltpu.roll(t, half, 0))
    return t * cos + swapped * sin_signed


def _sample_kernel(x_ref, normw_ref, wt_ref, wout_ref, bbdt_ref, cbdt_ref, are_ref, aim_ref, dskip_ref,
                   wglut_ref, lbp_ref, hgn_ref, wupt_ref, bgate_ref, glan_ref, retn_ref, fnw_ref,
                   cos_ref, sin_ref, dret_ref, onesh_ref,
                   s5re_ref, s5im_ref, hg_ref, gla_ref, ret_ref,
                   y_ref, s5re_o, s5im_o, hg_o, gla_o, ret_o,
                   xs_scr, pt_scr, hk_scr, ot_scr, mixt_scr):
    layer, head = pl.program_id(0), pl.program_id(1)
    last_layer, last_head = pl.num_programs(0) - 1, pl.num_programs(1) - 1

    @pl.when((layer == 0) & (head == 0))
    def _load_x():
        xs_scr[...] = x_ref[...]

    @pl.when(head == 0)
    def _dense():
        hh = _rmsnorm_rows(xs_scr[...], normw_ref[...]).astype(BF16)
        pt_scr[...] = lax.dot_general(wt_ref[...], hh, (((1,), (1,)), ((), ())), preferred_element_type=F32)

        u = pt_scr[C_U:C_U + BRANCH, :]
        bu = _dot3(bbdt_ref[...], u)
        a_re, a_im = are_ref[...], aim_ref[...]
        s0r, s0i = s5re_ref[...], s5im_ref[...]
        s_re = a_re * s0r - a_im * s0i + bu[0:S5_N]
        s_im = a_re * s0i + a_im * s0r + bu[S5_N:2 * S5_N]
        s5re_o[...] = s_re
        s5im_o[...] = s_im
        y = _gelu_tanh(_dot3(cbdt_ref[...], jnp.concatenate([s_re, s_im], axis=0)) + u * dskip_ref[...])
        y = y * _sigmoid(jnp.dot(wglut_ref[...], y.astype(BF16), preferred_element_type=F32))
        mixt_scr[0:BRANCH, :] = y * _silu(pt_scr[C_SZ:C_SZ + BRANCH, :])

        lbp = lbp_ref[...]
        log_f, hk = _hgrn_gates(pt_scr[C_HF:C_HF + BRANCH, :], lbp[:, 0:1], lbp[:, 1:2], lbp[:, 2:3])
        pt_scr[C_HF:C_HF + BRANCH, :] = jnp.exp(log_f)
        hk_scr[...] = hk
        g_gla = _log_sigmoid(_dot3(wupt_ref[...], pt_scr[C_LR:C_LR + LANES, :]) + bgate_ref[...])
        pt_scr[C_LR:C_LR + LANES, :] = jnp.exp(g_gla * (1.0 / GLA_TAU))
        pt_scr[C_GQ:C_GQ + LANES, :] = pt_scr[C_GQ:C_GQ + LANES, :] * (GLA_DK ** -0.5)
        row = lax.broadcasted_iota(jnp.int32, (HEADS * RET_DK, LANES), 0)
        first_half = (row % RET_DK) < (RET_DK // 2)
        pt_scr[C_RQ:C_RQ + LANES, :] = _rotary_t(pt_scr[C_RQ:C_RQ + LANES, :], cos_ref[...], sin_ref[...],
                                                 first_half)
        pt_scr[C_RK:C_RK + LANES, :] = _rotary_t(pt_scr[C_RK:C_RK + LANES, :], cos_ref[...], sin_ref[...],
                                                 first_half) * (RET_DK ** -0.5)

    def head_update(s0_ref, s_out_ref, dk, dec_ref, dec_row, key_ref, key_row, q_row, v_row, out_row):
        vt = pt_scr[pl.ds(pl.multiple_of(v_row + head * DV, DV), DV), :]

        def feature(kk, acc):
            r = head * dk + kk
            bcast = lambda ref, r0: jnp.broadcast_to(ref[pl.ds(r0 + r, 1), :], (DV, LANES))
            s_new = s0_ref[kk] * bcast(dec_ref, dec_row) + bcast(key_ref, key_row) * vt
            s_out_ref[kk] = s_new
            return acc + bcast(pt_scr, q_row) * s_new

        acc = lax.fori_loop(0, dk, feature, jnp.zeros((DV, LANES), F32), unroll=4)
        ot_scr[pl.ds(pl.multiple_of(out_row + head * DV, DV), DV), :] = acc

    head_update(hg_ref, hg_o, HG_DK, pt_scr, C_HF, hk_scr, 0, C_HQ, C_HI, 0)
    head_update(gla_ref, gla_o, GLA_DK, pt_scr, C_LR, pt_scr, C_GK, C_GQ, C_GV, BRANCH)
    head_update(ret_ref, ret_o, RET_DK, dret_ref, 0, pt_scr, C_RK, C_RQ, C_RV, 2 * BRANCH)

    @pl.when(head == last_head)
    def _finish():
        ones_h = onesh_ref[...]
        o_hg = _head_rms_t(ot_scr[0:BRANCH, :], ones_h, hgn_ref[...])
        mixt_scr[BRANCH:2 * BRANCH, :] = o_hg * _silu(pt_scr[C_HZ:C_HZ + BRANCH, :])
        o_gla = _head_rms_t(ot_scr[BRANCH:2 * BRANCH, :], ones_h, glan_ref[...])
        mixt_scr[2 * BRANCH:3 * BRANCH, :] = o_gla * _silu(pt_scr[C_GZ:C_GZ + BRANCH, :])
        o_ret = _head_ln_t(ot_scr[2 * BRANCH:3 * BRANCH, :], ones_h, retn_ref[...])
        mixt_scr[3 * BRANCH:4 * BRANCH, :] = o_ret * _silu(pt_scr[C_RZ:C_RZ + BRANCH, :])
        out = xs_scr[...] + lax.dot_general(mixt_scr[...].astype(BF16), wout_ref[...], (((0,), (0,)), ((), ())),
                                            preferred_element_type=F32)
        xs_scr[...] = out

        @pl.when(layer == last_layer)
        def _emit():
            y_ref[...] = _rmsnorm_rows(out, fnw_ref[...])


def _ret_log_gamma():
    return jnp.log1p(-jnp.exp2(-5.0 - jnp.arange(HEADS, dtype=F32)))


def _constants():
    ones_h = (np.arange(BRANCH)[:, None] // DV == np.arange(BRANCH)[None, :] // DV)
    ones_g = (np.arange(HEADS * GLA_DK)[:, None] // GLA_DK == np.arange(BRANCH)[None, :] // DV)
    r = np.arange(T_CHUNK)
    tril = (r[:, None] // PAIR == r[None, :] // PAIR) & (r[None, :] <= r[:, None])
    same_group_b = np.arange(BRANCH)[:, None] // S5_CH == np.arange(S5_N)[None, :] // S5_STATE
    as_bf16 = lambda m: jnp.asarray(m.astype(np.float32), dtype=BF16)
    return dict(ones_h=as_bf16(ones_h), ones_g=as_bf16(ones_g), tril=as_bf16(tril),
                s5_mask=jnp.asarray(same_group_b.astype(np.float32)))


def _ret_tables():
    lg = _ret_log_gamma()
    idx = jnp.arange(T_CHUNK, dtype=F32)
    rel = idx[:, None] - idx[None, :]
    causal = rel >= 0
    decay = jnp.where(causal[None], jnp.exp(jnp.where(causal, rel, 0.0)[None] * lg[:, None, None]), 0.0)
    dstack = jnp.transpose(decay, (1, 0, 2)).reshape(T_CHUNK, HEADS * T_CHUNK)
    inner = jnp.repeat(jnp.exp((idx[:, None] + 1.0) * lg[None, :]), DV, axis=1)
    kdec = jnp.repeat(jnp.exp((T_CHUNK - 1.0 - idx[:, None]) * lg[None, :]), RET_DK, axis=1)
    cdec = jnp.repeat(jnp.exp(T_CHUNK * lg)[None, :], RET_DK, axis=1)
    dret = jnp.broadcast_to(jnp.repeat(jnp.exp(lg), RET_DK)[:, None], (HEADS * RET_DK, LANES))
    return dstack, inner, kdec, cdec, dret


def _rope_tables(pos):
    half = RET_DK // 2
    inv = ROPE_BASE ** (-jnp.arange(half, dtype=F32) / half)
    ang = pos.astype(F32)[:, None] * inv[None, :]
    cos, sin = jnp.cos(ang), jnp.sin(ang)
    cos_t = jnp.tile(jnp.concatenate([cos, cos], axis=1), (1, HEADS))
    sin_t = jnp.tile(jnp.concatenate([-sin, sin], axis=1), (1, HEADS))
    return cos_t, sin_t


def _pack_w_in_t(w):
    wt = jnp.swapaxes(w, 1, 2)
    offs = np.cumsum([0, 256, 256, 256, 256, 256, 256, 128, 128, 256, 16, 256, 128, 128, 256, 256])
    seg = lambda i: wt[:, int(offs[i]):int(offs[i + 1]), :]
    order = [0, 1, 2, 3, 4, 5, 6, 7, 8, 10, 11, 12, 13, 14]
    pad = jnp.zeros((w.shape[0], LANES - GLA_LOWRANK, w.shape[1]), w.dtype)
    return jnp.concatenate([seg(i) for i in order] + [seg(9), pad], axis=1).astype(BF16)


def _s5_discretize(lam_re, lam_im, log_step, b_re, b_im, c_re, c_im, mask):
    lr, li = lam_re.astype(F32), lam_im.astype(F32)
    step = jnp.exp(log_step.astype(F32))[..., None]
    mag = jnp.exp(lr * step)
    ab_re = mag * jnp.cos(li * step)
    ab_im = mag * jnp.sin(li * step)
    den = lr * lr + li * li
    nr = ab_re - 1.0
    f_re = (nr * lr + ab_im * li) / den
    f_im = (ab_im * lr - nr * li) / den
    br, bi = b_re.astype(F32), b_im.astype(F32)
    bb_re = f_re[..., None] * br - f_im[..., None] * bi
    bb_im = f_re[..., None] * bi + f_im[..., None] * br
    nl = lr.shape[0]

    def drive(bb):
        rows = jnp.transpose(bb, (0, 1, 3, 2)).reshape(nl, BRANCH, S5_STATE)
        return jnp.tile(rows, (1, 1, S5_GROUPS)) * mask

    def readout(cc):
        rows = cc.astype(F32).reshape(nl, BRANCH, S5_STATE)
        return jnp.tile(rows, (1, 1, S5_GROUPS)) * mask

    bbd = jnp.concatenate([drive(bb_re), drive(bb_im)], axis=2)
    cbd_t = jnp.concatenate([readout(c_re), -readout(c_im)], axis=2)
    return bbd, cbd_t, ab_re.reshape(nl, 1, S5_N), ab_im.reshape(nl, 1, S5_N)


def _full(shape):
    return pl.BlockSpec(shape, lambda *_: (0,) * len(shape), pipeline_mode=pl.Buffered(1))


def _of_layer(arr, layer):
    nd = arr.ndim - 1
    return pl.BlockSpec((None,) + arr.shape[1:], lambda *_: (layer,) + (0,) * nd,
                        pipeline_mode=pl.Buffered(1))


def _prompt_layer(x, layer, p, consts, tabs, rope, apply_final):
    nb, seq, _ = x.shape
    n_steps = seq // T_CHUNK
    rows = nb * T_CHUNK
    dstack, inner, kdec, cdec, _ = tabs
    cos_t, sin_t = rope
    per_layer = [p['norm_w'], p['w_t'], p['w_out'], p['bbd'].astype(BF16), p['cbd'],
                 p['a_re'], p['a_im'],
                 p['d_skip'], p['w_glu'], p['lbp'], p['hg_norm'], p['w_up'], p['b_gate'], p['gla_norm'],
                 p['ret_norm']]
    shared = [dstack, inner, kdec, cdec, consts['ones_h'], consts['ones_g'], consts['tril'],
              p['final_norm']]
    in_specs = [pl.BlockSpec((nb, T_CHUNK, D_MODEL), lambda i: (0, i, 0)),
                pl.BlockSpec((nb, T_CHUNK, D_MODEL), lambda i: (0, jnp.minimum(i + 1, n_steps - 1), 0))]
    in_specs += [_of_layer(a, layer) for a in per_layer]
    in_specs += [pl.BlockSpec((T_CHUNK, LANES), lambda i: (i, 0)), pl.BlockSpec((T_CHUNK, LANES), lambda i: (i, 0))]
    in_specs += [_full(a.shape) for a in shared]
    out_shape = (jax.ShapeDtypeStruct((nb, seq, D_MODEL), F32),
                 jax.ShapeDtypeStruct((nb, 2 * S5_N), F32),
                 jax.ShapeDtypeStruct((nb, HEADS, DV, HG_DK), F32),
                 jax.ShapeDtypeStruct((nb, HEADS, DV, GLA_DK), F32),
                 jax.ShapeDtypeStruct((nb, HEADS, DV, RET_DK), F32))
    out_specs = (pl.BlockSpec((nb, T_CHUNK, D_MODEL), lambda i: (0, i, 0)),
                 pl.BlockSpec((nb, 2 * S5_N), lambda i: (0, 0)),
                 pl.BlockSpec((nb, HEADS, DV, HG_DK), lambda i: (0, 0, 0, 0)),
                 pl.BlockSpec((nb, HEADS, DV, GLA_DK), lambda i: (0, 0, 0, 0)),
                 pl.BlockSpec((nb, HEADS, DV, RET_DK), lambda i: (0, 0, 0, 0)))
    scratch = [pltpu.VMEM((nb // 2, 2, BRANCH, HEADS * HG_DK), F32),
               pltpu.VMEM((nb // 2, 2, BRANCH, HEADS * GLA_DK), F32),
               pltpu.VMEM((nb // 2, 2, BRANCH, HEADS * RET_DK), F32),
               pltpu.VMEM((D_MODEL, N_PACK), BF16),
               pltpu.VMEM((N_QUARTER, rows, QUARTER), F32), pltpu.VMEM((rows, LANES), F32),
               pltpu.VMEM((rows, BRANCH), F32),
               pltpu.VMEM((2 * S5_N // LANES, nb * PITCH, LANES), F32),
               pltpu.VMEM((rows, D_MODEL), F32),
               ] + GROUP_ROWS * (
                   [pltpu.VMEM((TERM_ROWS, kd), F32) for kd in (HEADS * HG_DK, HEADS * GLA_DK)]
                   + [pltpu.VMEM((w // LANES, T_CHUNK, LANES), F32) for kd in (HEADS * HG_DK, HEADS * GLA_DK)
                      for w in (kd, kd, BRANCH)])
    y, s5, hgs, glas, rets = pl.pallas_call(
        functools.partial(_prompt_kernel, apply_final),
        grid=(n_steps,), in_specs=in_specs, out_specs=out_specs, out_shape=out_shape,
        scratch_shapes=scratch, name='prompt_layer',
        compiler_params=pltpu.CompilerParams(dimension_semantics=('arbitrary',),
                                             vmem_limit_bytes=VMEM_LIMIT),
    )(x, x, *per_layer, cos_t, sin_t, *shared)
    s5 = s5.reshape(nb, 2, S5_GROUPS, S5_STATE)

    to_kv = lambda st: jnp.swapaxes(st, 2, 3)
    return y, (s5[:, 0], s5[:, 1], to_kv(hgs), to_kv(glas), to_kv(rets))


def _sample_step(x, states, p, consts, tabs, rope):
    nb = x.shape[0]
    depth = p['w_t'].shape[0]
    s5re, s5im, hg, gla, ret = states
    to_lanes = lambda s: jnp.moveaxis(s, 1, -1)
    s5re_t = to_lanes(s5re).reshape(depth, S5_N, nb)
    s5im_t = to_lanes(s5im).reshape(depth, S5_N, nb)
    hg_t, gla_t, ret_t = to_lanes(hg), to_lanes(gla), to_lanes(ret)
    cos_t, sin_t = rope
    col = lambda a: jnp.swapaxes(a, -1, -2)
    per_layer = [p['norm_w'], p['w_t'], p['w_out'], col(p['bbd']), p['cbd_t'], col(p['a_re']), col(p['a_im']),
                 col(p['d_skip']), col(p['w_glu']), col(p['lbp']), col(p['hg_norm']), col(p['w_up']),
                 col(p['b_gate']), col(p['gla_norm']), col(p['ret_norm'])]
    shared = [p['final_norm'], col(cos_t), col(sin_t), tabs[4], consts['ones_h']]
    layer_spec = lambda a: pl.BlockSpec((None,) + a.shape[1:], lambda l, h: (l,) + (0,) * (a.ndim - 1))
    head_spec = lambda a: pl.BlockSpec((None, None) + a.shape[2:], lambda l, h: (l, h) + (0,) * (a.ndim - 2))
    state_specs = [layer_spec(s5re_t), layer_spec(s5im_t), head_spec(hg_t), head_spec(gla_t), head_spec(ret_t)]
    in_specs = ([_full(x.shape)] + [layer_spec(a) for a in per_layer] + [_full(a.shape) for a in shared]
                + state_specs)
    state_arrays = [s5re_t, s5im_t, hg_t, gla_t, ret_t]
    out_shape = tuple([jax.ShapeDtypeStruct(x.shape, F32)]
                      + [jax.ShapeDtypeStruct(a.shape, F32) for a in state_arrays])
    out_specs = tuple([pl.BlockSpec(x.shape, lambda l, h: (0, 0))] + state_specs)
    scratch = [pltpu.VMEM((nb, D_MODEL), F32), pltpu.VMEM((N_PACK, nb), F32), pltpu.VMEM((BRANCH, nb), F32),
               pltpu.VMEM((3 * BRANCH, nb), F32), pltpu.VMEM((4 * BRANCH, nb), F32)]
    outs = pl.pallas_call(
        _sample_kernel, grid=(depth, HEADS), in_specs=in_specs, out_specs=out_specs, out_shape=out_shape,
        scratch_shapes=scratch, name='sample_step',
        compiler_params=pltpu.CompilerParams(dimension_semantics=('arbitrary', 'arbitrary'),
                                             vmem_limit_bytes=VMEM_LIMIT),
    )(x, *per_layer, *shared, *state_arrays)
    from_lanes = lambda s: jnp.moveaxis(s, -1, 1)
    new = (from_lanes(outs[1].reshape(depth, S5_GROUPS, S5_STATE, nb)),
           from_lanes(outs[2].reshape(depth, S5_GROUPS, S5_STATE, nb)),
           from_lanes(outs[3]), from_lanes(outs[4]), from_lanes(outs[5]))
    return outs[0], new


def kernel(x_prompt, x_sample, state_s5_re, state_s5_im, state_hgrn, state_gla, state_ret, norm_w, final_norm_w, w_in, w_out, s5_lam_re, s5_lam_im, s5_log_step, s5_b_re, s5_b_im, s5_c_re, s5_c_im, s5_d, s5_w_glu, hgrn_lb_logits, hgrn_norm_w, gla_w_gate_up, gla_b_gate, gla_norm_w, ret_norm_w):
    depth = w_in.shape[0]
    seq = x_prompt.shape[1]
    consts = _constants()
    tabs = _ret_tables()
    rope_p = _rope_tables(jnp.arange(seq))
    rope_s = _rope_tables(PAST_LEN + jnp.arange(1))
    lb = jnp.cumsum(jax.nn.softmax(hgrn_lb_logits.astype(F32), axis=0), axis=0)
    lb = (lb - lb[0:1])[:, None, :]
    bbd, cbd_t, a_re, a_im = _s5_discretize(s5_lam_re, s5_lam_im, s5_log_step, s5_b_re, s5_b_im, s5_c_re,
                                            s5_c_im, consts['s5_mask'])
    w_t = lax.optimization_barrier(_pack_w_in_t(w_in))
    row = lambda a: a[:, None, :].astype(F32)
    w_up = jnp.zeros((depth, LANES, HEADS * GLA_DK), F32).at[:, :GLA_LOWRANK].set(gla_w_gate_up.astype(F32))
    p = dict(norm_w=row(norm_w),
             w_t=w_t, w_out=w_out.astype(BF16),
             bbd=bbd, cbd=jnp.swapaxes(cbd_t, 1, 2).astype(BF16), cbd_t=cbd_t, a_re=a_re, a_im=a_im,
             d_skip=row(s5_d), w_glu=s5_w_glu.astype(BF16),
             lbp=jnp.concatenate([jnp.log(lb), jnp.log1p(-lb), 1.0 - lb, jnp.zeros((depth, 5, BRANCH), F32)],
                                 axis=1),
             hg_norm=row(hgrn_norm_w), w_up=w_up, b_gate=row(gla_b_gate), gla_norm=row(gla_norm_w),
             ret_norm=row(ret_norm_w), final_norm=final_norm_w[None, :].astype(F32))

    xp = x_prompt
    new_p = ([], [], [], [], [])
    for l in range(depth):
        xp, st_p = _prompt_layer(xp, l, p, consts, tabs, rope_p, l == depth - 1)
        for i in range(5):
            new_p[i].append(st_p[i])
    xs, new_s = _sample_step(x_sample.reshape(x_sample.shape[0], D_MODEL),
                             (state_s5_re, state_s5_im, state_hgrn, state_gla, state_ret),
                             p, consts, tabs, rope_s)
    return (xp, xs.reshape(x_sample.shape),
            jnp.stack(new_p[0]), jnp.stack(new_p[1]), jnp.stack(new_p[2]), jnp.stack(new_p[3]),
            jnp.stack(new_p[4])) + new_s
```

```python
import functools
import math

import numpy as np
import jax
import jax.numpy as jnp
from jax import lax
from jax.experimental import pallas as pl
from jax.experimental.pallas import tpu as pltpu

F32 = jnp.float32
BF16 = jnp.bfloat16

D_MODEL = 1024
BRANCH = 256
S5_CH = 16
S5_GROUPS = 16
S5_STATE = 64
S5_N = S5_GROUPS * S5_STATE
HEADS = 4
HG_DK = 64
GLA_DK = 32
RET_DK = 32
DV = 64
GLA_LOWRANK = 16
GLA_TAU = 16.0
ROPE_BASE = 10000.0
PAST_LEN = 16384
EPS = 1e-6
SUB = 16
PAIR = 2 * SUB
GROUP_ROWS = 4

LANES = 128
T_CHUNK = 64
PITCH = T_CHUNK + 8
SUB_TERM_ROWS = SUB * (SUB + SUB // 2) // 2
TERM_ROWS = (T_CHUNK // SUB) * SUB_TERM_ROWS
LOG2E = math.log2(math.e)
MASK_NEG = -1e30
VMEM_LIMIT = 60 * 1024 * 1024

C_U, C_SZ, C_HQ, C_HF, C_HI, C_HZ = 0, 256, 512, 768, 1024, 1280
C_GQ, C_GK, C_GV, C_GZ = 1536, 1664, 1792, 2048
C_RQ, C_RK, C_RV, C_RZ = 2304, 2432, 2560, 2816
C_LR = 3072
N_PACK = 3200
N_QUARTER = 4
QUARTER = C_LR // N_QUARTER


def _dot(a, b):
    return jnp.dot(a.astype(BF16), b.astype(BF16), preferred_element_type=F32)


def _dot_nt(a, b):
    return lax.dot_general(a.astype(BF16), b.astype(BF16), (((1,), (1,)), ((), ())),
                           preferred_element_type=F32)


def _dot_tn(a, b):
    return lax.dot_general(a.astype(BF16), b.astype(BF16), (((0,), (0,)), ((), ())),
                           preferred_element_type=F32)


def _split3(x):
    x1 = x.astype(BF16)
    r1 = x - x1.astype(F32)
    x2 = r1.astype(BF16)
    x3 = (r1 - x2.astype(F32)).astype(BF16)
    return x1, x2, x3


def _dot_sel_lhs(sel, x):
    x1, x2, x3 = _split3(x)
    d = lambda p: jnp.dot(sel, p, preferred_element_type=F32)
    return d(x1) + d(x2) + d(x3)


def _dot_sel_rhs2(x, sel):
    x1 = x.astype(BF16)
    x2 = (x - x1.astype(F32)).astype(BF16)
    return (jnp.dot(x1, sel, preferred_element_type=F32)
            + jnp.dot(x2, sel, preferred_element_type=F32))


def _dot3(a, b):
    a1 = a.astype(BF16)
    a2 = (a - a1.astype(F32)).astype(BF16)
    b1 = b.astype(BF16)
    b2 = (b - b1.astype(F32)).astype(BF16)
    d = lambda p, q: jnp.dot(p, q, preferred_element_type=F32)
    return d(a1, b1) + d(a1, b2) + d(a2, b1)


def _sigmoid(x):
    return 0.5 * jnp.tanh(0.5 * x) + 0.5


def _silu(x):
    return x * _sigmoid(x)


def _log_sigmoid(x):
    return jnp.minimum(x, 0.0) - jnp.log(1.0 + jnp.exp(-jnp.abs(x)))


def _gelu_tanh(x):
    return 0.5 * x * (1.0 + jnp.tanh(math.sqrt(2.0 / math.pi) * (x + 0.044715 * (x * x * x))))


def _rmsnorm_rows(x, w):
    return x * lax.rsqrt(jnp.mean(x * x, axis=-1, keepdims=True) + EPS) * w


def _head_rms(o, ones_h, gain):
    ms = _dot_sel_rhs2(o * o, ones_h) * (1.0 / DV)
    return o * lax.rsqrt(ms + EPS) * gain


def _head_ln(o, ones_h, gain):
    c = o - _dot_sel_rhs2(o, ones_h) * (1.0 / DV)
    var = _dot_sel_rhs2(c * c, ones_h) * (1.0 / DV)
    return c * lax.rsqrt(var + EPS) * gain


def _hgrn_gates(xf, loglb, log1mlb, one_m_lb):
    bterm = log1mlb + _log_sigmoid(xf)
    m = jnp.maximum(loglb, bterm)
    log_f = m + jnp.log(jnp.exp(loglb - m) + jnp.exp(bterm - m))
    return log_f, one_m_lb * _sigmoid(-xf)


def _rotary(t, cos, sin_signed, first_half):
    half = RET_DK // 2
    swapped = jnp.where(first_half, pltpu.roll(t, LANES - half, 1), pltpu.roll(t, half, 1))
    return t * cos + swapped * sin_signed


def _s5_output(y_lin, u, sz, dskip, wglu):
    y = _gelu_tanh(y_lin + u * dskip)
    y = y * _sigmoid(_dot(y, wglu))
    return y * _silu(sz)


def _mix_and_project(x, mix, wout, fnw, apply_final):
    out = x + _dot(mix, wout)
    if apply_final:
        out = _rmsnorm_rows(out, fnw)
    return out


class _Slabs:
    def __init__(self, ref, base=0):
        self.ref, self.base = ref, base

    def keep(self, value):
        for c in range(self.ref.shape[0]):
            self.ref[c, pl.ds(self.base, value.shape[0]), :] = value[:, c * LANES:(c + 1) * LANES]

    def tile(self, r0, n):
        return jnp.concatenate([self.ref[c, pl.ds(self.base + r0, n), :] for c in range(self.ref.shape[0])],
                               axis=1)

    def row(self, r, n):
        return jnp.concatenate([self.ref[c, pl.ds(self.base + r, n, stride=0), :]
                                for c in range(self.ref.shape[0])], axis=1)


class _GatedStream:
    def __init__(self, q, k, v, g, s_ref, ones_kv, mask_vk, tril, term_scr, row_scr, kdim):
        self.q, self.k, self.v, self.g = q, k, v, g
        self.bc, self.kc, self.vc = (_Slabs(r) for r in row_scr)
        self.s_ref, self.ones_kv, self.mask_vk = s_ref, ones_kv, mask_vk
        self.tril, self.term_scr, self.kdim = tril, term_scr, kdim
        self.ws, self.inter, self.att, self.cross = {}, {}, {}, {}

    def cumulate(self):
        self.bc.keep(_dot_sel_lhs(self.tril, self.g()) * LOG2E)
        self.kc.keep(self.k())
        self.vc.keep(self.v())

    def _total(self, p):
        return self.bc.tile(PAIR * (p + 1) - 1, 1)

    def update_part(self, p):
        sl = slice(PAIR * p, PAIR * (p + 1))
        self.ws[p] = _dot_tn(self.v()[sl],
                             self.k()[sl] * jnp.exp2(self._total(p) - self.bc.tile(PAIR * p, PAIR)))

    def cross_part(self, p):
        r_a, r_b = PAIR * p, PAIR * p + SUB
        edge = self.bc.row(r_b - 1, SUB)
        q_b = self.q()[r_b:r_b + SUB] * jnp.exp2(self.bc.tile(r_b, SUB) - edge)
        k_a = self.kc.tile(r_a, SUB) * jnp.exp2(edge - self.bc.tile(r_a, SUB))
        lane = lax.broadcasted_iota(jnp.int32, (SUB, self.kdim), 1) // (self.kdim // HEADS)
        q_heads = jnp.concatenate([jnp.where(lane == h, q_b, 0.0) for h in range(HEADS)], axis=0)
        scores = _dot_nt(q_heads, k_a)
        res = _dot(scores, self.vc.tile(r_a, SUB))
        lane_v = lax.broadcasted_iota(jnp.int32, (SUB, BRANCH), 1) // DV
        out = jnp.where(lane_v == 0, res[0:SUB], 0.0)
        for h in range(1, HEADS):
            out = out + jnp.where(lane_v == h, res[SUB * h:SUB * (h + 1)], 0.0)
        self.cross[p] = out

    def products_part(self, j):
        hs, term_scr, r0 = SUB // 2, self.term_scr, SUB * j
        q, bc = self.q()[r0:r0 + SUB], self.bc.tile(r0, SUB)
        trow = lax.broadcasted_iota(jnp.int32, (hs, self.kdim), 0)
        neg = [jnp.where(trow >= d, 0.0, MASK_NEG) for d in range(1, hs)]
        q_lo, q_hi, b_lo, b_hi = q[0:hs], q[hs:SUB], bc[0:hs], bc[hs:SUB]
        row = j * SUB_TERM_ROWS
        for s in range(SUB):
            ks, bs = self.kc.row(r0 + s, hs), self.bc.row(r0 + s, hs)
            d = s % hs
            q_dg, b_dg = (q_lo, b_lo) if s < hs else (q_hi, b_hi)
            e_dg = b_dg - bs if d == 0 else b_dg - bs + neg[d - 1]
            term_scr[row:row + hs, :] = q_dg * ks * jnp.exp2(e_dg)
            row += hs
            if s < hs:
                term_scr[row:row + hs, :] = q_hi * ks * jnp.exp2(b_hi - bs)
                row += hs

    def matmul_part(self, p):
        sl = slice(PAIR * p, PAIR * (p + 1))
        if p == 0:
            self.state = self.s_ref[...]
        self.inter[p] = _dot_nt(self.q()[sl] * jnp.exp2(self.bc.tile(PAIR * p, PAIR)), self.state)
        self.state = self.state * jnp.exp2(self._total(p)) + self.ws[p] * self.mask_vk
        if p == T_CHUNK // PAIR - 1:
            self.s_ref[...] = self.state
        rows = slice(2 * p * SUB_TERM_ROWS, 2 * (p + 1) * SUB_TERM_ROWS)
        self.att[p] = _dot(self.term_scr[rows, :], self.ones_kv)

    def output_part(self, j):
        hs, r0, p, second = SUB // 2, SUB * j, j // 2, j % 2
        att = self.att[p][second * SUB_TERM_ROWS:(second + 1) * SUB_TERM_ROWS]
        start = self.inter[p][second * SUB:(second + 1) * SUB]
        if second:
            start = start + self.cross[p]
        acc_lo, acc_hi = start[0:hs], start[hs:SUB]
        row = 0
        for s in range(SUB):
            vs = self.vc.row(r0 + s, hs)
            if s < hs:
                acc_lo = acc_lo + att[row:row + hs] * vs
                row += hs
            acc_hi = acc_hi + att[row:row + hs] * vs
            row += hs
        return jnp.concatenate([acc_lo, acc_hi], axis=0)


def _ret_chunk(q, k, v, cos, sin_signed, s_ref, dstack, inner, kdec, cdec, mask_vk):
    lane = lax.broadcasted_iota(jnp.int32, (T_CHUNK, HEADS * RET_DK), 1)
    first_half = (lane % RET_DK) < (RET_DK // 2)
    rq = _rotary(q, cos, sin_signed, first_half)
    rk = _rotary(k, cos, sin_signed, first_half) * (RET_DK ** -0.5)
    lane_v = lax.broadcasted_iota(jnp.int32, (T_CHUNK, HEADS * DV), 1)
    kst = jnp.concatenate([jnp.where(lane // RET_DK == h, rk, 0.0) for h in range(HEADS)], axis=0)
    vst = jnp.concatenate([jnp.where(lane_v // DV == h, v, 0.0) for h in range(HEADS)], axis=0)
    p = _dot_nt(rq, kst) * dstack
    s_t = s_ref[...]
    o = _dot(p, vst) + _dot_nt(rq, s_t) * inner
    s_ref[...] = s_t * cdec + _dot_tn(v, rk * kdec) * mask_vk
    return o


def _prompt_kernel(apply_final,
                   x_ref, xn_ref, normw_ref, wt_ref, wout_ref, bbd_ref, cbd_ref, are_ref, aim_ref,
                   dskip_ref, wglu_ref, lbp_ref, hgn_ref, wup_ref, bgate_ref, glan_ref, retn_ref, cos_ref,
                   sin_ref, dstack_ref, inner_ref, kdec_ref, cdec_ref, onesh_ref, onesg_ref, tril_ref,
                   fnw_ref,
                   y_ref, s5_ref, hgc_ref, glac_ref, retc_ref,
                   hgs_ref, glas_ref, rets_ref, win_scr, proj_scr, lr_scr, hk_scr, bu_scr, mix_scr, *stream_scratch):
    nb = x_ref.shape[0]
    rows = nb * T_CHUNK
    n_slab = 2 * S5_N // LANES
    half = n_slab // 2
    step = pl.program_id(0)

    def w_quarter(qtr):
        return win_scr[:, qtr * QUARTER:(qtr + 1) * QUARTER]

    def proj(col, width, rs=slice(None)):
        return proj_scr[col // QUARTER, rs, col % QUARTER:col % QUARTER + width]

    @pl.when(step == 0)
    def _init():
        s5_ref[...] = jnp.zeros_like(s5_ref)
        hgs_ref[...] = jnp.zeros_like(hgs_ref)
        glas_ref[...] = jnp.zeros_like(glas_ref)
        rets_ref[...] = jnp.zeros_like(rets_ref)
        bu_scr[...] = jnp.zeros_like(bu_scr)
        for c in range(N_PACK // LANES):
            win_scr[:, c * LANES:(c + 1) * LANES] = jnp.transpose(wt_ref[c * LANES:(c + 1) * LANES, :])
        h0 = _rmsnorm_rows(x_ref[...].reshape(rows, D_MODEL), normw_ref[...]).astype(BF16)
        for qtr in range(N_QUARTER):
            proj_scr[qtr] = jnp.dot(h0, w_quarter(qtr), preferred_element_type=F32)
        lr_scr[...] = jnp.dot(h0, win_scr[:, C_LR:N_PACK], preferred_element_type=F32)

    u16 = proj(C_U, BRANCH).astype(BF16)

    def drive(c2):
        bu = jnp.dot(u16, bbd_ref[:, 2 * c2 * LANES:2 * (c2 + 1) * LANES], preferred_element_type=F32)
        for cc in range(2):
            for b in range(nb):
                bu_scr[2 * c2 + cc, b * PITCH:b * PITCH + T_CHUNK, :] = bu[b * T_CHUNK:(b + 1) * T_CHUNK,
                                                                           cc * LANES:(cc + 1) * LANES]

    lbp = lbp_ref[...]
    log_f, hk = _hgrn_gates(proj(C_HF, BRANCH), lbp[0:1, :], lbp[1:2, :], lbp[2:3, :])
    proj_scr[C_HF // QUARTER, :, C_HF % QUARTER:C_HF % QUARTER + BRANCH] = log_f
    hk_scr[...] = hk
    g_gla = _log_sigmoid(_dot3(lr_scr[...], wup_ref[...]) + bgate_ref[...])
    lr_scr[...] = g_gla * (1.0 / GLA_TAU)
    for c2 in range(n_slab // 2):
        drive(c2)
    a_re = [jnp.broadcast_to(are_ref[:, c * LANES:(c + 1) * LANES], (nb, LANES)) for c in range(half)]
    a_im = [jnp.broadcast_to(aim_ref[:, c * LANES:(c + 1) * LANES], (nb, LANES)) for c in range(half)]
    s_init = s5_ref[...]
    carry0 = tuple(s_init[:, c * LANES:(c + 1) * LANES] for c in range(n_slab))

    def scan_step(t, carry):
        new = [None] * n_slab
        for c in range(half):
            sr, si = carry[c], carry[half + c]
            br = bu_scr[c, pl.ds(t, nb, stride=PITCH), :]
            bi = bu_scr[half + c, pl.ds(t, nb, stride=PITCH), :]
            nr = a_re[c] * sr - a_im[c] * si + br
            ni = a_re[c] * si + a_im[c] * sr + bi
            bu_scr[c, pl.ds(t, nb, stride=PITCH), :] = nr
            bu_scr[half + c, pl.ds(t, nb, stride=PITCH), :] = ni
            new[c], new[half + c] = nr, ni
        return tuple(new)

    carry = lax.fori_loop(0, T_CHUNK, scan_step, carry0, unroll=8)
    s5_ref[...] = jnp.concatenate(carry, axis=1)
    s_all = jnp.concatenate([bu_scr[c] for c in range(n_slab)], axis=1)
    y_all = _dot(s_all, cbd_ref[...])
    y_lin = jnp.concatenate([y_all[b * PITCH:b * PITCH + T_CHUNK] for b in range(nb)], axis=0)
    mix_scr[:, 0:BRANCH] = _s5_output(y_lin, proj(C_U, BRANCH), proj(C_SZ, BRANCH), dskip_ref[...],
                                      wglu_ref[...])

    ones_h = onesh_ref[...]
    ones_g = onesg_ref[...]
    mask_h = ones_h.astype(F32)
    mask_g = jnp.transpose(ones_g.astype(F32))
    tril = tril_ref[...]

    def rows_of(bp, odd):
        return pl.ds(pl.multiple_of((2 * bp + odd) * T_CHUNK, T_CHUNK), T_CHUNK)

    def hgrn_stream(bp, odd, term, row_scr):
        rs = rows_of(bp, odd)
        return _GatedStream(lambda: proj(C_HQ, BRANCH, rs), lambda: hk_scr[rs, :], lambda: proj(C_HI, BRANCH, rs),
                            lambda: proj(C_HF, BRANCH, rs), hgs_ref.at[bp, odd], ones_h, mask_h, tril, term,
                            row_scr, HEADS * HG_DK)

    def gla_stream(bp, odd, term, row_scr):
        rs = rows_of(bp, odd)
        return _GatedStream(lambda: proj(C_GQ, LANES, rs) * (GLA_DK ** -0.5), lambda: proj(C_GK, LANES, rs),
                            lambda: proj(C_GV, BRANCH, rs), lambda: lr_scr[rs, :],
                            glas_ref.at[bp, odd], ones_g, mask_g, tril, term, row_scr, HEADS * GLA_DK)

    n_sub = T_CHUNK // SUB

    def retention(bp, odd):
        rs = rows_of(bp, odd)
        mix_scr[rs, 3 * BRANCH:4 * BRANCH] = _ret_chunk(
            proj(C_RQ, LANES, rs), proj(C_RK, LANES, rs), proj(C_RV, BRANCH, rs), cos_ref[...],
            sin_ref[...], rets_ref.at[bp, odd], dstack_ref[...], inner_ref[...], kdec_ref[...],
            cdec_ref[...], mask_g)

    def per_group(grp, _):
        st, slots = [], []
        for i in range(GROUP_ROWS):
            bp, odd = (GROUP_ROWS // 2) * grp + i // 2, i % 2
            scr = stream_scratch[8 * i:8 * (i + 1)]
            st += [hgrn_stream(bp, odd, scr[0], scr[2:5]), gla_stream(bp, odd, scr[1], scr[5:8])]
            slots += [(bp, odd, BRANCH), (bp, odd, 2 * BRANCH)]

        def emit(i, j):
            bp, odd, col = slots[i]
            r0 = pl.multiple_of((2 * bp + odd) * T_CHUNK + SUB * j, SUB)
            mix_scr[pl.ds(r0, SUB), col:col + BRANCH] = st[i].output_part(j)

        for s_ in st:
            s_.cumulate()
        for i in range(GROUP_ROWS):
            retention((GROUP_ROWS // 2) * grp + i // 2, i % 2)
        for s_ in st:
            for p in range(n_sub // 2):
                s_.update_part(p)
        for s_ in st:
            for j in range(n_sub):
                s_.products_part(j)
        for s_ in st:
            for p in range(n_sub // 2):
                s_.cross_part(p)
        for s_ in st:
            for p in range(n_sub // 2):
                s_.matmul_part(p)
        for i in range(len(st)):
            for j in range(n_sub):
                emit(i, j)
        return 0

    lax.fori_loop(0, nb // GROUP_ROWS, per_group, 0)

    o_hg = _head_rms(mix_scr[:, BRANCH:2 * BRANCH], ones_h, hgn_ref[...])
    mix_scr[:, BRANCH:2 * BRANCH] = o_hg * _silu(proj(C_HZ, BRANCH))
    o_gla = _head_rms(mix_scr[:, 2 * BRANCH:3 * BRANCH], ones_h, glan_ref[...])
    mix_scr[:, 2 * BRANCH:3 * BRANCH] = o_gla * _silu(proj(C_GZ, BRANCH))
    o_ret = _head_ln(mix_scr[:, 3 * BRANCH:4 * BRANCH], ones_h, retn_ref[...])
    mix_scr[:, 3 * BRANCH:4 * BRANCH] = o_ret * _silu(proj(C_RZ, BRANCH))
    hn = _rmsnorm_rows(xn_ref[...].reshape(rows, D_MODEL), normw_ref[...]).astype(BF16)
    for qtr in range(N_QUARTER):
        proj_scr[qtr] = jnp.dot(hn, w_quarter(qtr), preferred_element_type=F32)
    lr_scr[...] = jnp.dot(hn, win_scr[:, C_LR:N_PACK], preferred_element_type=F32)
    out = _mix_and_project(x_ref[...].reshape(rows, D_MODEL), mix_scr[...], wout_ref[...], fnw_ref[...],
                           apply_final)
    y_ref[...] = out.reshape(nb, T_CHUNK, D_MODEL)

    @pl.when(step == pl.num_programs(0) - 1)
    def _emit_states():
        for b in range(nb):
            for state, compact in ((hgs_ref, hgc_ref), (glas_ref, glac_ref), (rets_ref, retc_ref)):
                dk = compact.shape[3]
                for h in range(HEADS):
                    compact[b, h] = state[b // 2, b % 2, h * DV:(h + 1) * DV, h * dk:(h + 1) * dk]


def _dot_sel_lhs2(sel, x):
    x1 = x.astype(BF16)
    x2 = (x - x1.astype(F32)).astype(BF16)
    return (jnp.dot(sel, x1, preferred_element_type=F32) + jnp.dot(sel, x2, preferred_element_type=F32))


def _head_rms_t(o, ones_h, gain):
    ms = _dot_sel_lhs2(ones_h, o * o) * (1.0 / DV)
    return o * lax.rsqrt(ms + EPS) * gain


def _head_ln_t(o, ones_h, gain):
    c = o - _dot_sel_lhs2(ones_h, o) * (1.0 / DV)
    var = _dot_sel_lhs2(ones_h, c * c) * (1.0 / DV)
    return c * lax.rsqrt(var + EPS) * gain


def _rotary_t(t, cos, sin_signed, first_half):
    half = RET_DK // 2
    swapped = jnp.where(first_half, pltpu.roll(t, LANES - half, 0), pltpu.roll(t, half, 0))
    return t * cos + swapped * sin_signed


def _sample_kernel(x_ref, normw_ref, wt_ref, wout_ref, bbdt_ref, cbdt_ref, are_ref, aim_ref, dskip_ref,
                   wglut_ref, lbp_ref, hgn_ref, wupt_ref, bgate_ref, glan_ref, retn_ref, fnw_ref,
                   cos_ref, sin_ref, dret_ref, onesh_ref,
                   s5re_ref, s5im_ref, hg_ref, gla_ref, ret_ref,
                   y_ref, s5re_o, s5im_o, hg_o, gla_o, ret_o,
                   xs_scr, pt_scr, hk_scr, ot_scr, mixt_scr):
    layer, head = pl.program_id(0), pl.program_id(1)
    last_layer, last_head = pl.num_programs(0) - 1, pl.num_programs(1) - 1

    @pl.when((layer == 0) & (head == 0))
    def _load_x():
        xs_scr[...] = x_ref[...]

    @pl.when(head == 0)
    def _dense():
        hh = _rmsnorm_rows(xs_scr[...], normw_ref[...]).astype(BF16)
        pt_scr[...] = lax.dot_general(wt_ref[...], hh, (((1,), (1,)), ((), ())), preferred_element_type=F32)

        u = pt_scr[C_U:C_U + BRANCH, :]
        bu = _dot3(jnp.transpose(bbdt_ref[...]), u)
        a_re, a_im = are_ref[...], aim_ref[...]
        s0r, s0i = s5re_ref[...], s5im_ref[...]
        s_re = a_re * s0r - a_im * s0i + bu[0:S5_N]
        s_im = a_re * s0i + a_im * s0r + bu[S5_N:2 * S5_N]
        s5re_o[...] = s_re
        s5im_o[...] = s_im
        y = _gelu_tanh(_dot3(cbdt_ref[...], jnp.concatenate([s_re, s_im], axis=0)) + u * dskip_ref[...])
        y = y * _sigmoid(jnp.dot(wglut_ref[...], y.astype(BF16), preferred_element_type=F32))
        mixt_scr[0:BRANCH, :] = y * _silu(pt_scr[C_SZ:C_SZ + BRANCH, :])

        lbp = lbp_ref[...]
        log_f, hk = _hgrn_gates(pt_scr[C_HF:C_HF + BRANCH, :], lbp[:, 0:1], lbp[:, 1:2], lbp[:, 2:3])
        pt_scr[C_HF:C_HF + BRANCH, :] = jnp.exp(log_f)
        hk_scr[...] = hk
        g_gla = _log_sigmoid(_dot3(wupt_ref[...], pt_scr[C_LR:C_LR + LANES, :]) + bgate_ref[...])
        pt_scr[C_LR:C_LR + LANES, :] = jnp.exp(g_gla * (1.0 / GLA_TAU))
        pt_scr[C_GQ:C_GQ + LANES, :] = pt_scr[C_GQ:C_GQ + LANES, :] * (GLA_DK ** -0.5)
        row = lax.broadcasted_iota(jnp.int32, (HEADS * RET_DK, LANES), 0)
        first_half = (row % RET_DK) < (RET_DK // 2)
        pt_scr[C_RQ:C_RQ + LANES, :] = _rotary_t(pt_scr[C_RQ:C_RQ + LANES, :], cos_ref[...], sin_ref[...],
                                                 first_half)
        pt_scr[C_RK:C_RK + LANES, :] = _rotary_t(pt_scr[C_RK:C_RK + LANES, :], cos_ref[...], sin_ref[...],
                                                 first_half) * (RET_DK ** -0.5)

    def head_update(s0_ref, s_out_ref, dk, dec_ref, dec_row, key_ref, key_row, q_row, v_row, out_row):
        vt = pt_scr[pl.ds(pl.multiple_of(v_row + head * DV, DV), DV), :]

        def feature(kk, acc):
            r = head * dk + kk
            bcast = lambda ref, r0: jnp.broadcast_to(ref[pl.ds(r0 + r, 1), :], (DV, LANES))
            s_new = s0_ref[kk] * bcast(dec_ref, dec_row) + bcast(key_ref, key_row) * vt
            s_out_ref[kk] = s_new
            return acc + bcast(pt_scr, q_row) * s_new

        acc = lax.fori_loop(0, dk, feature, jnp.zeros((DV, LANES), F32), unroll=4)
        ot_scr[pl.ds(pl.multiple_of(out_row + head * DV, DV), DV), :] = acc

    head_update(hg_ref, hg_o, HG_DK, pt_scr, C_HF, hk_scr, 0, C_HQ, C_HI, 0)
    head_update(gla_ref, gla_o, GLA_DK, pt_scr, C_LR, pt_scr, C_GK, C_GQ, C_GV, BRANCH)
    head_update(ret_ref, ret_o, RET_DK, dret_ref, 0, pt_scr, C_RK, C_RQ, C_RV, 2 * BRANCH)

    @pl.when(head == last_head)
    def _finish():
        ones_h = onesh_ref[...]
        o_hg = _head_rms_t(ot_scr[0:BRANCH, :], ones_h, hgn_ref[...])
        mixt_scr[BRANCH:2 * BRANCH, :] = o_hg * _silu(pt_scr[C_HZ:C_HZ + BRANCH, :])
        o_gla = _head_rms_t(ot_scr[BRANCH:2 * BRANCH, :], ones_h, glan_ref[...])
        mixt_scr[2 * BRANCH:3 * BRANCH, :] = o_gla * _silu(pt_scr[C_GZ:C_GZ + BRANCH, :])
        o_ret = _head_ln_t(ot_scr[2 * BRANCH:3 * BRANCH, :], ones_h, retn_ref[...])
        mixt_scr[3 * BRANCH:4 * BRANCH, :] = o_ret * _silu(pt_scr[C_RZ:C_RZ + BRANCH, :])
        out = xs_scr[...] + lax.dot_general(mixt_scr[...].astype(BF16), wout_ref[...], (((0,), (0,)), ((), ())),
                                            preferred_element_type=F32)
        xs_scr[...] = out

        @pl.when(layer == last_layer)
        def _emit():
            y_ref[...] = _rmsnorm_rows(out, fnw_ref[...])


def _ret_log_gamma():
    return jnp.log1p(-jnp.exp2(-5.0 - jnp.arange(HEADS, dtype=F32)))


def _constants():
    ones_h = (np.arange(BRANCH)[:, None] // DV == np.arange(BRANCH)[None, :] // DV)
    ones_g = (np.arange(HEADS * GLA_DK)[:, None] // GLA_DK == np.arange(BRANCH)[None, :] // DV)
    r = np.arange(T_CHUNK)
    tril = (r[:, None] // PAIR == r[None, :] // PAIR) & (r[None, :] <= r[:, None])
    same_group_b = np.arange(BRANCH)[:, None] // S5_CH == np.arange(S5_N)[None, :] // S5_STATE
    as_bf16 = lambda m: jnp.asarray(m.astype(np.float32), dtype=BF16)
    return dict(ones_h=as_bf16(ones_h), ones_g=as_bf16(ones_g), tril=as_bf16(tril),
                s5_mask=jnp.asarray(same_group_b.astype(np.float32)))


def _ret_tables():
    lg = _ret_log_gamma()
    idx = jnp.arange(T_CHUNK, dtype=F32)
    rel = idx[:, None] - idx[None, :]
    causal = rel >= 0
    decay = jnp.where(causal[None], jnp.exp(jnp.where(causal, rel, 0.0)[None] * lg[:, None, None]), 0.0)
    dstack = jnp.transpose(decay, (1, 0, 2)).reshape(T_CHUNK, HEADS * T_CHUNK)
    inner = jnp.repeat(jnp.exp((idx[:, None] + 1.0) * lg[None, :]), DV, axis=1)
    kdec = jnp.repeat(jnp.exp((T_CHUNK - 1.0 - idx[:, None]) * lg[None, :]), RET_DK, axis=1)
    cdec = jnp.repeat(jnp.exp(T_CHUNK * lg)[None, :], RET_DK, axis=1)
    dret = jnp.broadcast_to(jnp.repeat(jnp.exp(lg), RET_DK)[:, None], (HEADS * RET_DK, LANES))
    return dstack, inner, kdec, cdec, dret


def _rope_tables(pos):
    half = RET_DK // 2
    inv = ROPE_BASE ** (-jnp.arange(half, dtype=F32) / half)
    ang = pos.astype(F32)[:, None] * inv[None, :]
    cos, sin = jnp.cos(ang), jnp.sin(ang)
    cos_t = jnp.tile(jnp.concatenate([cos, cos], axis=1), (1, HEADS))
    sin_t = jnp.tile(jnp.concatenate([-sin, sin], axis=1), (1, HEADS))
    return cos_t, sin_t


def _pack_w_in_t(w):
    wt = jnp.swapaxes(w, 1, 2)
    offs = np.cumsum([0, 256, 256, 256, 256, 256, 256, 128, 128, 256, 16, 256, 128, 128, 256, 256])
    seg = lambda i: wt[:, int(offs[i]):int(offs[i + 1]), :]
    order = [0, 1, 2, 3, 4, 5, 6, 7, 8, 10, 11, 12, 13, 14]
    pad = jnp.zeros((w.shape[0], LANES - GLA_LOWRANK, w.shape[1]), w.dtype)
    return jnp.concatenate([seg(i) for i in order] + [seg(9), pad], axis=1).astype(BF16)


def _s5_discretize(lam_re, lam_im, log_step, b_re, b_im, c_re, c_im, mask):
    lr, li = lam_re.astype(F32), lam_im.astype(F32)
    step = jnp.exp(log_step.astype(F32))[..., None]
    mag = jnp.exp(lr * step)
    ab_re = mag * jnp.cos(li * step)
    ab_im = mag * jnp.sin(li * step)
    den = lr * lr + li * li
    nr = ab_re - 1.0
    f_re = (nr * lr + ab_im * li) / den
    f_im = (ab_im * lr - nr * li) / den
    br, bi = b_re.astype(F32), b_im.astype(F32)
    bb_re = f_re[..., None] * br - f_im[..., None] * bi
    bb_im = f_re[..., None] * bi + f_im[..., None] * br
    nl = lr.shape[0]

    def drive(bb):
        rows = jnp.transpose(bb, (0, 1, 3, 2)).reshape(nl, BRANCH, S5_STATE)
        return jnp.tile(rows, (1, 1, S5_GROUPS)) * mask

    def readout(cc):
        rows = cc.astype(F32).reshape(nl, BRANCH, S5_STATE)
        return jnp.tile(rows, (1, 1, S5_GROUPS)) * mask

    bbd = jnp.concatenate([drive(bb_re), drive(bb_im)], axis=2)
    cbd_t = jnp.concatenate([readout(c_re), -readout(c_im)], axis=2)
    return bbd, cbd_t, ab_re.reshape(nl, 1, S5_N), ab_im.reshape(nl, 1, S5_N)


def _full(shape):
    return pl.BlockSpec(shape, lambda *_: (0,) * len(shape), pipeline_mode=pl.Buffered(1))


def _of_layer(arr, layer):
    nd = arr.ndim - 1
    return pl.BlockSpec((None,) + arr.shape[1:], lambda *_: (layer,) + (0,) * nd,
                        pipeline_mode=pl.Buffered(1))


def _prompt_layer(x, layer, p, consts, tabs, rope, apply_final):
    nb, seq, _ = x.shape
    n_steps = seq // T_CHUNK
    rows = nb * T_CHUNK
    dstack, inner, kdec, cdec, _ = tabs
    cos_t, sin_t = rope
    per_layer = [p['norm_w'], p['w_t'], p['w_out'], p['bbd'].astype(BF16), p['cbd'],
                 p['a_re'], p['a_im'],
                 p['d_skip'], p['w_glu'], p['lbp'], p['hg_norm'], p['w_up'], p['b_gate'], p['gla_norm'],
                 p['ret_norm']]
    shared = [dstack, inner, kdec, cdec, consts['ones_h'], consts['ones_g'], consts['tril'],
              p['final_norm']]
    in_specs = [pl.BlockSpec((nb, T_CHUNK, D_MODEL), lambda i: (0, i, 0)),
                pl.BlockSpec((nb, T_CHUNK, D_MODEL), lambda i: (0, jnp.minimum(i + 1, n_steps - 1), 0))]
    in_specs += [_of_layer(a, layer) for a in per_layer]
    in_specs += [pl.BlockSpec((T_CHUNK, LANES), lambda i: (i, 0)), pl.BlockSpec((T_CHUNK, LANES), lambda i: (i, 0))]
    in_specs += [_full(a.shape) for a in shared]
    out_shape = (jax.ShapeDtypeStruct((nb, seq, D_MODEL), F32),
                 jax.ShapeDtypeStruct((nb, 2 * S5_N), F32),
                 jax.ShapeDtypeStruct((nb, HEADS, DV, HG_DK), F32),
                 jax.ShapeDtypeStruct((nb, HEADS, DV, GLA_DK), F32),
                 jax.ShapeDtypeStruct((nb, HEADS, DV, RET_DK), F32))
    out_specs = (pl.BlockSpec((nb, T_CHUNK, D_MODEL), lambda i: (0, i, 0)),
                 pl.BlockSpec((nb, 2 * S5_N), lambda i: (0, 0)),
                 pl.BlockSpec((nb, HEADS, DV, HG_DK), lambda i: (0, 0, 0, 0)),
                 pl.BlockSpec((nb, HEADS, DV, GLA_DK), lambda i: (0, 0, 0, 0)),
                 pl.BlockSpec((nb, HEADS, DV, RET_DK), lambda i: (0, 0, 0, 0)))
    scratch = [pltpu.VMEM((nb // 2, 2, BRANCH, HEADS * HG_DK), F32),
               pltpu.VMEM((nb // 2, 2, BRANCH, HEADS * GLA_DK), F32),
               pltpu.VMEM((nb // 2, 2, BRANCH, HEADS * RET_DK), F32),
               pltpu.VMEM((D_MODEL, N_PACK), BF16),
               pltpu.VMEM((N_QUARTER, rows, QUARTER), F32), pltpu.VMEM((rows, LANES), F32),
               pltpu.VMEM((rows, BRANCH), F32),
               pltpu.VMEM((2 * S5_N // LANES, nb * PITCH, LANES), F32),
               pltpu.VMEM((rows, D_MODEL), F32),
               ] + GROUP_ROWS * (
                   [pltpu.VMEM((TERM_ROWS, kd), F32) for kd in (HEADS * HG_DK, HEADS * GLA_DK)]
                   + [pltpu.VMEM((w // LANES, T_CHUNK, LANES), F32) for kd in (HEADS * HG_DK, HEADS * GLA_DK)
                      for w in (kd, kd, BRANCH)])
    y, s5, hgs, glas, rets = pl.pallas_call(
        functools.partial(_prompt_kernel, apply_final),
        grid=(n_steps,), in_specs=in_specs, out_specs=out_specs, out_shape=out_shape,
        scratch_shapes=scratch, name='prompt_layer',
        compiler_params=pltpu.CompilerParams(dimension_semantics=('arbitrary',),
                                             vmem_limit_bytes=VMEM_LIMIT),
    )(x, x, *per_layer, cos_t, sin_t, *shared)
    s5 = s5.reshape(nb, 2, S5_GROUPS, S5_STATE)

    to_kv = lambda st: jnp.swapaxes(st, 2, 3)
    return y, (s5[:, 0], s5[:, 1], to_kv(hgs), to_kv(glas), to_kv(rets))


def _sample_step(x, states, p, consts, tabs, rope):
    nb = x.shape[0]
    depth = p['w_t'].shape[0]
    s5re, s5im, hg, gla, ret = states
    to_lanes = lambda s: jnp.moveaxis(s, 1, -1)
    s5re_t = to_lanes(s5re).reshape(depth, S5_N, nb)
    s5im_t = to_lanes(s5im).reshape(depth, S5_N, nb)
    hg_t, gla_t, ret_t = to_lanes(hg), to_lanes(gla), to_lanes(ret)
    cos_t, sin_t = rope
    col = lambda a: jnp.swapaxes(a, -1, -2)
    per_layer = [p['norm_w'], p['w_t'], p['w_out'], p['bbd'], p['cbd_t'], col(p['a_re']), col(p['a_im']),
                 col(p['d_skip']), col(p['w_glu']), col(p['lbp']), col(p['hg_norm']), col(p['w_up']),
                 col(p['b_gate']), col(p['gla_norm']), col(p['ret_norm'])]
    shared = [p['final_norm'], col(cos_t), col(sin_t), tabs[4], consts['ones_h']]
    layer_spec = lambda a: pl.BlockSpec((None,) + a.shape[1:], lambda l, h: (l,) + (0,) * (a.ndim - 1))
    head_spec = lambda a: pl.BlockSpec((None, None) + a.shape[2:], lambda l, h: (l, h) + (0,) * (a.ndim - 2))
    state_specs = [layer_spec(s5re_t), layer_spec(s5im_t), head_spec(hg_t), head_spec(gla_t), head_spec(ret_t)]
    in_specs = ([_full(x.shape)] + [layer_spec(a) for a in per_layer] + [_full(a.shape) for a in shared]
                + state_specs)
    state_arrays = [s5re_t, s5im_t, hg_t, gla_t, ret_t]
    out_shape = tuple([jax.ShapeDtypeStruct(x.shape, F32)]
                      + [jax.ShapeDtypeStruct(a.shape, F32) for a in state_arrays])
    out_specs = tuple([pl.BlockSpec(x.shape, lambda l, h: (0, 0))] + state_specs)
    scratch = [pltpu.VMEM((nb, D_MODEL), F32), pltpu.VMEM((N_PACK, nb), F32), pltpu.VMEM((BRANCH, nb), F32),
               pltpu.VMEM((3 * BRANCH, nb), F32), pltpu.VMEM((4 * BRANCH, nb), F32)]
    outs = pl.pallas_call(
        _sample_kernel, grid=(depth, HEADS), in_specs=in_specs, out_specs=out_specs, out_shape=out_shape,
        scratch_shapes=scratch, name='sample_step',
        compiler_params=pltpu.CompilerParams(dimension_semantics=('arbitrary', 'arbitrary'),
                                             vmem_limit_bytes=VMEM_LIMIT),
    )(x, *per_layer, *shared, *state_arrays)
    from_lanes = lambda s: jnp.moveaxis(s, -1, 1)
    new = (from_lanes(outs[1].reshape(depth, S5_GROUPS, S5_STATE, nb)),
           from_lanes(outs[2].reshape(depth, S5_GROUPS, S5_STATE, nb)),
           from_lanes(outs[3]), from_lanes(outs[4]), from_lanes(outs[5]))
    return outs[0], new


def kernel(x_prompt, x_sample, state_s5_re, state_s5_im, state_hgrn, state_gla, state_ret, norm_w, final_norm_w, w_in, w_out, s5_lam_re, s5_lam_im, s5_log_step, s5_b_re, s5_b_im, s5_c_re, s5_c_im, s5_d, s5_w_glu, hgrn_lb_logits, hgrn_norm_w, gla_w_gate_up, gla_b_gate, gla_norm_w, ret_norm_w):
    depth = w_in.shape[0]
    seq = x_prompt.shape[1]
    consts = _constants()
    tabs = _ret_tables()
    rope_p = _rope_tables(jnp.arange(seq))
    rope_s = _rope_tables(PAST_LEN + jnp.arange(1))
    lb = jnp.cumsum(jax.nn.softmax(hgrn_lb_logits.astype(F32), axis=0), axis=0)
    lb = (lb - lb[0:1])[:, None, :]
    bbd, cbd_t, a_re, a_im = _s5_discretize(s5_lam_re, s5_lam_im, s5_log_step, s5_b_re, s5_b_im, s5_c_re,
                                            s5_c_im, consts['s5_mask'])
    w_t = lax.optimization_barrier(_pack_w_in_t(w_in))
    row = lambda a: a[:, None, :].astype(F32)
    w_up = jnp.zeros((depth, LANES, HEADS * GLA_DK), F32).at[:, :GLA_LOWRANK].set(gla_w_gate_up.astype(F32))
    p = dict(norm_w=row(norm_w),
             w_t=w_t, w_out=w_out.astype(BF16),
             bbd=bbd, cbd=jnp.swapaxes(cbd_t, 1, 2).astype(BF16), cbd_t=cbd_t, a_re=a_re, a_im=a_im,
             d_skip=row(s5_d), w_glu=s5_w_glu.astype(BF16),
             lbp=jnp.concatenate([jnp.log(lb), jnp.log1p(-lb), 1.0 - lb, jnp.zeros((depth, 5, BRANCH), F32)],
                                 axis=1),
             hg_norm=row(hgrn_norm_w), w_up=w_up, b_gate=row(gla_b_gate), gla_norm=row(gla_norm_w),
             ret_norm=row(ret_norm_w), final_norm=final_norm_w[None, :].astype(F32))

    xp = x_prompt
    new_p = ([], [], [], [], [])
    for l in range(depth):
        xp, st_p = _prompt_layer(xp, l, p, consts, tabs, rope_p, l == depth - 1)
        for i in range(5):
            new_p[i].append(st_p[i])
    xs, new_s = _sample_step(x_sample.reshape(x_sample.shape[0], D_MODEL),
                             (state_s5_re, state_s5_im, state_hgrn, state_gla, state_ret),
                             p, consts, tabs, rope_s)
    return (xp, xs.reshape(x_sample.shape),
            jnp.stack(new_p[0]), jnp.stack(new_p[1]), jnp.stack(new_p[2]), jnp.stack(new_p[3]),
            jnp.stack(new_p[4])) + new_s
```
